```python
import math
import jax, jax.numpy as jnp
from jax import lax
import numpy as np

D_MODEL = 1024
BATCH = 8
SEQ = 2048
DEPTH = 1

CHUNK = 64
Q_BLOCK = 128
MEM_LEN = 256
HEAD_DIM = 64
N_DIFF_HEADS = 4
DIFF_V_DIM = 2 * HEAD_DIM
DIFF_QK_WIDTH = N_DIFF_HEADS * 2 * HEAD_DIM
DIFF_WIDTH = N_DIFF_HEADS * DIFF_V_DIM
N_FOX_HEADS = 8
FOX_WIDTH = N_FOX_HEADS * HEAD_DIM
MIX_WIDTH = DIFF_WIDTH + FOX_WIDTH
ROPE_DIM = HEAD_DIM // 4
ROPE_THETA = 500000.0
N_MEM_HEADS = 4
MEM_HEAD_DIM = D_MODEL // N_MEM_HEADS
D_FF = 4 * D_MODEL
EPS = 1e-6
NEG_INF = -1e30

IN_SIZES = [DIFF_QK_WIDTH, DIFF_QK_WIDTH, DIFF_WIDTH, FOX_WIDTH, FOX_WIDTH, FOX_WIDTH, N_FOX_HEADS]
IN_OFFSETS = [int(o) for o in np.cumsum(IN_SIZES)[:-1]]
IN_COLS = int(sum(IN_SIZES))

kernel_name = "hybrid_diffattn_fox_memxattn_block"


def rms_norm(t, w):
    tf = t.astype(jnp.float32)
    tf = tf * lax.rsqrt(jnp.mean(tf * tf, axis=-1, keepdims=True) + EPS)
    return (tf * w.astype(jnp.float32)).astype(t.dtype)


def to_heads(t, n_heads):
    b, s, w = t.shape
    return t.reshape(b, s, n_heads, w // n_heads).transpose(0, 2, 1, 3)


def from_heads(t):
    b, n, s, d = t.shape
    return t.transpose(0, 2, 1, 3).reshape(b, s, n * d)


def rope_tables(positions):
    inv_freq = ROPE_THETA ** (-jnp.arange(0, ROPE_DIM, 2, dtype=jnp.float32) / ROPE_DIM)
    ang = positions.astype(jnp.float32)[..., None] * inv_freq
    return jnp.cos(ang)[:, None], jnp.sin(ang)[:, None]


def apply_partial_rope(t, cos, sin):
    tf = t.astype(jnp.float32)
    half = ROPE_DIM // 2
    t1, t2, rest = tf[..., :half], tf[..., half:ROPE_DIM], tf[..., ROPE_DIM:]
    out = jnp.concatenate([t1 * cos - t2 * sin, t2 * cos + t1 * sin, rest], axis=-1)
    return out.astype(t.dtype)


def chunk_causal_mask(q0, n_q, n_k):
    q_chunk = (q0 + jnp.arange(n_q)) // CHUNK
    k_chunk = jnp.arange(n_k) // CHUNK
    return k_chunk[None, :] <= q_chunk[:, None]


def frame_causal_mask(q0, n_q, n_k):
    return jnp.arange(n_k)[None, :] <= (q0 + jnp.arange(n_q))[:, None]


def masked_softmax(scores, mask):
    return jax.nn.softmax(jnp.where(mask, scores, NEG_INF), axis=-1)


def differential_attention(q1, q2, k1, k2, v, lam):
    seq = q1.shape[2]
    scale = HEAD_DIM ** -0.5
    outs = []
    for q0 in range(0, seq, Q_BLOCK):
        q_end = q0 + Q_BLOCK
        mask = chunk_causal_mask(q0, Q_BLOCK, q_end)
        s1 = jnp.einsum('bhqd,bhkd->bhqk', q1[:, :, q0:q_end], k1[:, :, :q_end]).astype(jnp.float32) * scale
        s2 = jnp.einsum('bhqd,bhkd->bhqk', q2[:, :, q0:q_end], k2[:, :, :q_end]).astype(jnp.float32) * scale
        a = masked_softmax(s1, mask) - lam * masked_softmax(s2, mask)
        outs.append(jnp.einsum('bhqk,bhkd->bhqd', a.astype(v.dtype), v[:, :, :q_end]))
    return jnp.concatenate(outs, axis=2)


def forgetting_attention(q, k, v, cum_logf):
    seq = q.shape[2]
    scale = HEAD_DIM ** -0.5
    outs = []
    for q0 in range(0, seq, Q_BLOCK):
        q_end = q0 + Q_BLOCK
        mask = frame_causal_mask(q0, Q_BLOCK, q_end)
        s = jnp.einsum('bhqd,bhkd->bhqk', q[:, :, q0:q_end], k[:, :, :q_end]).astype(jnp.float32) * scale
        s = s + (cum_logf[:, :, q0:q_end, None] - cum_logf[:, :, None, :q_end])
        p = masked_softmax(s, mask)
        outs.append(jnp.einsum('bhqk,bhkd->bhqd', p.astype(v.dtype), v[:, :, :q_end]))
    return jnp.concatenate(outs, axis=2)


def setup_inputs(seed: int = 0) -> dict:
    key = jax.random.key(seed)
    ks = jax.random.split(key, 32)
    f32 = jnp.float32

    def normal(k, shape, scale):
        return jax.random.normal(k, shape, f32) * scale

    def gain(k, shape):
        return 1.0 + 0.02 * jax.random.normal(k, shape, f32)

    D = D_MODEL
    return {
        "x": normal(ks[0], (BATCH, SEQ, D), 1.0),
        "mem": normal(ks[1], (BATCH, MEM_LEN, D), 1.0),
        "positions": jnp.arange(SEQ, dtype=jnp.int32)[None, :]
        + CHUNK * jax.random.randint(ks[2], (BATCH, 1), 0, 512, dtype=jnp.int32),
        "norm_mix_w": gain(ks[3], (DEPTH, D)),
        "w_in": normal(ks[4], (DEPTH, D, IN_COLS), D ** -0.5),
        "b_forget": jax.random.uniform(ks[5], (DEPTH, N_FOX_HEADS), f32, 2.0, 4.0),
        "diff_q_norm_w": gain(ks[6], (DEPTH, HEAD_DIM)),
        "diff_k_norm_w": gain(ks[7], (DEPTH, HEAD_DIM)),
        "lambda_q1": normal(ks[8], (DEPTH, HEAD_DIM), 0.1),
        "lambda_k1": normal(ks[9], (DEPTH, HEAD_DIM), 0.1),
        "lambda_q2": normal(ks[10], (DEPTH, HEAD_DIM), 0.1),
        "lambda_k2": normal(ks[11], (DEPTH, HEAD_DIM), 0.1),
        "diff_subln_w": gain(ks[12], (DEPTH, DIFF_V_DIM)),
        "fox_q_norm_w": gain(ks[13], (DEPTH, HEAD_DIM)),
        "fox_k_norm_w": gain(ks[14], (DEPTH, HEAD_DIM)),
        "w_out": normal(ks[15], (DEPTH, MIX_WIDTH, D), MIX_WIDTH ** -0.5),
        "norm_mem_q_w": gain(ks[16], (DEPTH, D)),
        "norm_mem_kv_w": gain(ks[17], (DEPTH, D)),
        "w_mem_q": normal(ks[18], (DEPTH, D, N_MEM_HEADS * MEM_HEAD_DIM), D ** -0.5),
        "w_mem_kv": normal(ks[19], (DEPTH, D, 2 * N_MEM_HEADS * MEM_HEAD_DIM), D ** -0.5),
        "mem_q_norm_w": gain(ks[20], (DEPTH, MEM_HEAD_DIM)),
        "mem_k_norm_w": gain(ks[21], (DEPTH, MEM_HEAD_DIM)),
        "w_mem_o": normal(ks[22], (DEPTH, N_MEM_HEADS * MEM_HEAD_DIM, D), (N_MEM_HEADS * MEM_HEAD_DIM) ** -0.5),
        "norm_mlp_w": gain(ks[23], (DEPTH, D)),
        "w_up": normal(ks[24], (DEPTH, D, D_FF), D ** -0.5),
        "w_down": normal(ks[25], (DEPTH, D_FF, D), D_FF ** -0.5),
    }


def reference(x, mem, positions, norm_mix_w, w_in, b_forget, diff_q_norm_w, diff_k_norm_w,
              lambda_q1, lambda_k1, lambda_q2, lambda_k2, diff_subln_w, fox_q_norm_w, fox_k_norm_w,
              w_out, norm_mem_q_w, norm_mem_kv_w, w_mem_q, w_mem_kv, mem_q_norm_w, mem_k_norm_w,
              w_mem_o, norm_mlp_w, w_up, w_down):
    cos, sin = rope_tables(positions)
    for l in range(DEPTH):
        lam_init = 0.8 - 0.6 * math.exp(-0.3 * l)

        h = rms_norm(x, norm_mix_w[l])
        proj = h @ w_in[l]
        dq, dk, dv, fq, fk, fv, f_logit = jnp.split(proj, IN_OFFSETS, axis=-1)

        dq = to_heads(dq, N_DIFF_HEADS)
        dk = to_heads(dk, N_DIFF_HEADS)
        dv = to_heads(dv, N_DIFF_HEADS)
        q1 = apply_partial_rope(rms_norm(dq[..., :HEAD_DIM], diff_q_norm_w[l]), cos, sin)
        q2 = apply_partial_rope(rms_norm(dq[..., HEAD_DIM:], diff_q_norm_w[l]), cos, sin)
        k1 = apply_partial_rope(rms_norm(dk[..., :HEAD_DIM], diff_k_norm_w[l]), cos, sin)
        k2 = apply_partial_rope(rms_norm(dk[..., HEAD_DIM:], diff_k_norm_w[l]), cos, sin)
        lam = (jnp.exp(jnp.sum(lambda_q1[l].astype(jnp.float32) * lambda_k1[l].astype(jnp.float32)))
               - jnp.exp(jnp.sum(lambda_q2[l].astype(jnp.float32) * lambda_k2[l].astype(jnp.float32)))
               + lam_init)
        o_diff = differential_attention(q1, q2, k1, k2, dv, lam)
        o_diff = rms_norm(o_diff, diff_subln_w[l]) * (1.0 - lam_init)

        log_f = jax.nn.log_sigmoid((f_logit + b_forget[l]).astype(jnp.float32))
        cum_logf = jnp.cumsum(log_f, axis=1).transpose(0, 2, 1)
        fq = rms_norm(to_heads(fq, N_FOX_HEADS), fox_q_norm_w[l])
        fk = rms_norm(to_heads(fk, N_FOX_HEADS), fox_k_norm_w[l])
        fv = to_heads(fv, N_FOX_HEADS)
        o_fox = forgetting_attention(fq, fk, fv, cum_logf)

        mixed = jnp.concatenate([from_heads(o_diff), from_heads(o_fox)], axis=-1)
        x = x + mixed @ w_out[l]

        hq = rms_norm(x, norm_mem_q_w[l])
        hm = rms_norm(mem, norm_mem_kv_w[l])
        mq = rms_norm(to_heads(hq @ w_mem_q[l], N_MEM_HEADS), mem_q_norm_w[l])
        mk, mv = jnp.split(hm @ w_mem_kv[l], 2, axis=-1)
        mk = rms_norm(to_heads(mk, N_MEM_HEADS), mem_k_norm_w[l])
        mv = to_heads(mv, N_MEM_HEADS)
        ms = jnp.einsum('bhqd,bhkd->bhqk', mq, mk).astype(jnp.float32) * (MEM_HEAD_DIM ** -0.5)
        mp = jax.nn.softmax(ms, axis=-1)
        mo = jnp.einsum('bhqk,bhkd->bhqd', mp.astype(mv.dtype), mv)
        x = x + from_heads(mo) @ w_mem_o[l]

        h = rms_norm(x, norm_mlp_w[l])
        x = x + jnp.square(jax.nn.relu(h @ w_up[l])) @ w_down[l]
    return x
```

```python
import functools
import math

import jax
import jax.numpy as jnp
from jax import lax
from jax.experimental import pallas as pl
from jax.experimental.pallas import tpu as pltpu

F32 = jnp.float32
BF16 = jnp.bfloat16

D_MODEL = 1024
CHUNK = 64
HEAD_DIM = 64
N_DIFF_HEADS = 4
DIFF_V_DIM = 2 * HEAD_DIM
N_FOX_HEADS = 8
GROUP_W = 512
ROPE_DIM = HEAD_DIM // 4
ROPE_THETA = 500000.0
N_MEM_HEADS = 4
MEM_HEAD_DIM = D_MODEL // N_MEM_HEADS
D_FF = 4 * D_MODEL
EPS = 1e-6
NEG_INF = -1e30

LANES = 128
MXU_DIM = 256
VMEM_LIMIT = 56 * 1024 * 1024

TOK_TILE = 512
ATT_TILE = 256


def _params(sem):
    return pltpu.CompilerParams(dimension_semantics=sem, vmem_limit_bytes=VMEM_LIMIT)


def _rms(x, w):
    ms = jnp.mean(x * x, axis=-1, keepdims=True)
    return (x * lax.rsqrt(ms + EPS)) * w


def _dot(a, b):
    return jnp.dot(a, b, preferred_element_type=F32)


def _dot_nt(a, b):
    return lax.dot_general(a, b, (((1,), (1,)), ((), ())), preferred_element_type=F32)


def _same_group_matrix(n, group):
    r = lax.broadcasted_iota(jnp.int32, (n, n), 0) // group
    c = lax.broadcasted_iota(jnp.int32, (n, n), 1) // group
    return jnp.where(r == c, 1.0, 0.0).astype(BF16)


def _group_rms(t, w, group):
    g = _same_group_matrix(MXU_DIM, group)
    sq = (t * t).astype(BF16)
    parts = [_dot(sq[:, c:c + MXU_DIM], g) for c in range(0, t.shape[1], MXU_DIM)]
    ss = jnp.concatenate(parts, axis=-1)
    return (t * lax.rsqrt(ss * (1.0 / group) + EPS)) * w


def _rope_body(pos_ref, freq_ref, cos_ref, sin_ref, nsin_ref):
    ang = pos_ref[...] * freq_ref[...]
    s = jnp.sin(ang)
    cos_ref[...] = jnp.cos(ang)
    sin_ref[...] = s
    nsin_ref[...] = -s


def _rope_tables(positions):
    n_tok = positions.size
    half = ROPE_DIM // 2
    inv_freq = ROPE_THETA ** (-jnp.arange(0, ROPE_DIM, 2, dtype=F32) / ROPE_DIM)
    rows = n_tok * half // LANES
    pos = jnp.repeat(positions.reshape(-1).astype(F32), half).reshape(rows, LANES)
    freq = jnp.tile(inv_freq, LANES // half).reshape(1, LANES)
    out = jax.ShapeDtypeStruct((rows, LANES), F32)
    cos, sin, nsin = pl.pallas_call(
        _rope_body, out_shape=(out, out, out), name="rope_tables")(pos, freq)
    cos, sin, nsin = (t.reshape(n_tok, half) for t in (cos, sin, nsin))
    rest = HEAD_DIM - ROPE_DIM
    ctab = jnp.concatenate([cos, cos, jnp.ones((n_tok, rest), F32)], axis=-1)
    stab = jnp.concatenate([nsin, sin, jnp.zeros((n_tok, rest), F32)], axis=-1)
    reps = LANES // HEAD_DIM
    return jnp.tile(ctab, (1, reps)), jnp.tile(stab, (1, reps))


def _rope(t, ctab, stab):
    half = ROPE_DIM // 2
    lane = lax.broadcasted_iota(jnp.int32, (1, LANES), 1) % HEAD_DIM
    first_half = lane < half
    outs = []
    for c in range(0, t.shape[1], LANES):
        tb = t[:, c:c + LANES]
        from_above = pltpu.roll(tb, LANES - half, axis=1)
        from_below = pltpu.roll(tb, half, axis=1)
        partner = jnp.where(first_half, from_above, from_below)
        outs.append(tb * ctab + partner * stab)
    return jnp.concatenate(outs, axis=-1)


def _log_sigmoid(x):
    return jnp.minimum(x, 0.0) - jnp.log1p(jnp.exp(-jnp.abs(x)))


def _in_proj_body(x_ref, nw_ref, w_ref, ctab_ref, stab_ref, dqw_ref, dkw_ref, fqw_ref,
                  fkw_ref, bf_ref, dq_ref, dk_ref, dv_ref, fq_ref, fk_ref, fv_ref,
                  cum_ref, carry_ref):
    @pl.when(pl.program_id(1) == 0)
    def _():
        carry_ref[...] = jnp.zeros_like(carry_ref)

    h = _rms(x_ref[...], nw_ref[...]).astype(BF16)
    ctab = ctab_ref[...]
    stab = stab_ref[...]

    def proj(group, width=GROUP_W):
        c0 = group * GROUP_W
        return _dot(h, w_ref[:, c0:c0 + width])

    dq_ref[...] = _rope(_group_rms(proj(0), dqw_ref[...], HEAD_DIM), ctab, stab).astype(BF16)
    dk_ref[...] = _rope(_group_rms(proj(1), dkw_ref[...], HEAD_DIM), ctab, stab).astype(BF16)
    dv_ref[...] = proj(2).astype(BF16)
    fq_ref[...] = _group_rms(proj(3), fqw_ref[...], HEAD_DIM).astype(BF16)
    fk_ref[...] = _group_rms(proj(4), fkw_ref[...], HEAD_DIM).astype(BF16)
    fv_ref[...] = proj(5).astype(BF16)

    log_f = _log_sigmoid(proj(6, LANES) + bf_ref[...])
    r = lax.broadcasted_iota(jnp.int32, (MXU_DIM, MXU_DIM), 0)
    c = lax.broadcasted_iota(jnp.int32, (MXU_DIM, MXU_DIM), 1)
    tril = jnp.where(c <= r, 1.0, 0.0).astype(BF16)
    carry = carry_ref[0:1, :]
    for r0 in range(0, TOK_TILE, MXU_DIM):
        blk = log_f[r0:r0 + MXU_DIM]
        hi = blk.astype(BF16)
        rem = blk - hi.astype(F32)
        mid = rem.astype(BF16)
        lo = (rem - mid.astype(F32)).astype(BF16)
        cum = (_dot(tril, hi) + _dot(tril, mid)) + _dot(tril, lo) + carry
        cum_ref[r0:r0 + MXU_DIM, :] = cum
        carry = cum[MXU_DIM - 1:MXU_DIM, :]
    carry_ref[0:1, :] = carry


def _in_proj(x2d, batch, norm_w, w_in, ctab, stab, dqw, dkw, fqw, fkw, b_forget):
    n_tok = x2d.shape[0]
    tiles = n_tok // batch // TOK_TILE
    n_groups = 6
    w_cols = n_groups * GROUP_W + LANES
    w = jnp.pad(w_in, ((0, 0), (0, w_cols - w_in.shape[1]))).astype(BF16)
    bf = jnp.pad(b_forget, (0, LANES - b_forget.shape[0])).reshape(1, LANES)

    def tile_w(v, scale=1.0):
        return jnp.tile(v * scale, GROUP_W // HEAD_DIM).reshape(1, GROUP_W)

    row = lambda b, i: (b * tiles + i, 0)
    fixed = lambda b, i: (0, 0)
    tok = lambda width: pl.BlockSpec((TOK_TILE, width), row)
    vec = lambda width: pl.BlockSpec((1, width), fixed)
    qkv = jax.ShapeDtypeStruct((n_tok, GROUP_W), BF16)
    scale = HEAD_DIM ** -0.5
    return pl.pallas_call(
        _in_proj_body,
        grid=(batch, tiles),
        in_specs=[tok(D_MODEL), vec(D_MODEL), pl.BlockSpec((D_MODEL, w_cols), fixed),
                  tok(LANES), tok(LANES), vec(GROUP_W), vec(GROUP_W), vec(GROUP_W),
                  vec(GROUP_W), vec(LANES)],
        out_specs=[tok(GROUP_W)] * n_groups + [tok(LANES)],
        out_shape=[qkv] * n_groups + [jax.ShapeDtypeStruct((n_tok, LANES), F32)],
        scratch_shapes=[pltpu.VMEM((8, LANES), F32)],
        compiler_params=_params(("arbitrary", "arbitrary")),
        name="in_proj",
    )(x2d, norm_w.reshape(1, -1), w, ctab, stab, tile_w(dqw, scale), tile_w(dkw),
      tile_w(fqw, scale), tile_w(fkw), bf)


def _softmax_step(s, v, m, l, acc):
    m_new = jnp.maximum(m, jnp.max(s, axis=-1, keepdims=True))
    alpha = jnp.exp(m - m_new)
    p = jnp.exp(s - m_new)
    l_new = alpha * l + jnp.sum(p, axis=-1, keepdims=True)
    acc_new = alpha * acc + _dot(p.astype(BF16), v)
    return m_new, l_new, acc_new


def _block_ids(div):
    q_id = lax.broadcasted_iota(jnp.int32, (ATT_TILE, ATT_TILE), 0) // div
    k_id = lax.broadcasted_iota(jnp.int32, (ATT_TILE, ATT_TILE), 1) // div
    return k_id <= q_id


def _init_state(width):
    return (jnp.full((ATT_TILE, 1), NEG_INF, F32), jnp.zeros((ATT_TILE, 1), F32),
            jnp.zeros((ATT_TILE, width), F32))


def _diff_attn_body(q_ref, k_ref, v_ref, lq1_ref, lk1_ref, lq2_ref, lk2_ref, subw_ref,
                    o_ref, *, lam_init):
    i = pl.program_id(2)
    q = q_ref[...]
    q1, q2 = q[:, :HEAD_DIM], q[:, HEAD_DIM:]

    def step(j, state, mask):
        start = pl.multiple_of(j * ATT_TILE, ATT_TILE)
        k = k_ref[pl.ds(start, ATT_TILE), :]
        v = v_ref[pl.ds(start, ATT_TILE), :]
        s1 = _dot_nt(q1, k[:, :HEAD_DIM])
        s2 = _dot_nt(q2, k[:, HEAD_DIM:])
        if mask is not None:
            s1 = jnp.where(mask, s1, NEG_INF)
            s2 = jnp.where(mask, s2, NEG_INF)
        return _softmax_step(s1, v, *state[:3]) + _softmax_step(s2, v, *state[3:])

    state = _init_state(DIFF_V_DIM) + _init_state(DIFF_V_DIM)
    state = lax.fori_loop(0, i, lambda j, st: step(j, st, None), state)
    m1, l1, acc1, m2, l2, acc2 = step(i, state, _block_ids(CHUNK))

    lam = (jnp.exp(jnp.sum(lq1_ref[...] * lk1_ref[...], axis=-1, keepdims=True))
           - jnp.exp(jnp.sum(lq2_ref[...] * lk2_ref[...], axis=-1, keepdims=True))
           + lam_init)
    o = acc1 / l1 - lam * (acc2 / l2)
    o_ref[...] = (_rms(o, subw_ref[...]) * (1.0 - lam_init)).astype(BF16)


def _diff_attn(dq, dk, dv, batch, lq1, lk1, lq2, lk2, subw, lam_init):
    n_tok = dq.shape[0]
    seq = n_tok // batch
    tiles = seq // ATT_TILE
    q_spec = pl.BlockSpec((ATT_TILE, LANES), lambda b, h, i: (b * tiles + i, h))
    kv_spec = pl.BlockSpec((seq, LANES), lambda b, h, i: (b, h))
    vec = lambda width: pl.BlockSpec((1, width), lambda b, h, i: (0, 0))
    return pl.pallas_call(
        functools.partial(_diff_attn_body, lam_init=lam_init),
        grid=(batch, N_DIFF_HEADS, tiles),
        in_specs=[q_spec, kv_spec, kv_spec, vec(HEAD_DIM), vec(HEAD_DIM), vec(HEAD_DIM),
                  vec(HEAD_DIM), vec(DIFF_V_DIM)],
        out_specs=q_spec,
        out_shape=jax.ShapeDtypeStruct((n_tok, GROUP_W), BF16),
        compiler_params=_params(("arbitrary", "arbitrary", "arbitrary")),
        name="diff_attn",
    )(dq, dk, dv, lq1.reshape(1, -1), lk1.reshape(1, -1), lq2.reshape(1, -1),
      lk2.reshape(1, -1), subw.reshape(1, -1))


def _fox_attn_body(q_ref, k_ref, v_ref, ccol_ref, crow_ref, o_ref, *, tiles):
    pair = pl.program_id(1)
    i = pl.program_id(2)
    q = q_ref[...]
    ccol = ccol_ref[...]
    lane = lax.broadcasted_iota(jnp.int32, (1, LANES), 1)
    heads_per_pair = LANES // HEAD_DIM

    for a in range(heads_per_pair):
        head = pair * heads_per_pair + a
        lo = a * HEAD_DIM
        qa = q[:, lo:lo + HEAD_DIM]
        c_q = jnp.sum(jnp.where(lane == head, ccol, 0.0), axis=-1, keepdims=True)

        def step(j, state, mask):
            start = pl.multiple_of(j * ATT_TILE, ATT_TILE)
            k = k_ref[pl.ds(start, ATT_TILE), lo:lo + HEAD_DIM]
            v = v_ref[pl.ds(start, ATT_TILE), lo:lo + HEAD_DIM]
            c_k = crow_ref[head * tiles + j]
            s = _dot_nt(qa, k) + (c_q - c_k)
            if mask is not None:
                s = jnp.where(mask, s, NEG_INF)
            return _softmax_step(s, v, *state)

        state = lax.fori_loop(0, i, lambda j, st: step(j, st, None), _init_state(HEAD_DIM))
        _, l, acc = step(i, state, _block_ids(1))
        o_ref[:, lo:lo + HEAD_DIM] = (acc / l).astype(BF16)


def _fox_attn(fq, fk, fv, cum, batch):
    n_tok = fq.shape[0]
    seq = n_tok // batch
    tiles = seq // ATT_TILE
    pairs = GROUP_W // LANES
    crow = cum[:, :N_FOX_HEADS].reshape(batch, seq, N_FOX_HEADS).transpose(0, 2, 1)
    crow = crow.reshape(batch * N_FOX_HEADS * tiles, 1, ATT_TILE)
    q_spec = pl.BlockSpec((ATT_TILE, LANES), lambda b, p, i: (b * tiles + i, p))
    kv_spec = pl.BlockSpec((seq, LANES), lambda b, p, i: (b, p))
    return pl.pallas_call(
        functools.partial(_fox_attn_body, tiles=tiles),
        grid=(batch, pairs, tiles),
        in_specs=[q_spec, kv_spec, kv_spec,
                  pl.BlockSpec((ATT_TILE, LANES), lambda b, p, i: (b * tiles + i, 0)),
                  pl.BlockSpec((N_FOX_HEADS * tiles, 1, ATT_TILE), lambda b, p, i: (b, 0, 0))],
        out_specs=q_spec,
        out_shape=jax.ShapeDtypeStruct((n_tok, GROUP_W), BF16),
        compiler_params=_params(("arbitrary", "arbitrary", "arbitrary")),
        name="fox_attn",
    )(fq, fk, fv, cum, crow)


def _out_proj_body(x_ref, od_ref, of_ref, wo_ref, nw_ref, wq_ref, qw_ref, x1_ref, mq_ref):
    x1 = x_ref[...] + (_dot(od_ref[...], wo_ref[:GROUP_W, :])
                       + _dot(of_ref[...], wo_ref[GROUP_W:, :]))
    x1_ref[...] = x1
    hq = _rms(x1, nw_ref[...]).astype(BF16)
    mq = _dot(hq, wq_ref[...])
    qw = qw_ref[...]
    for c in range(0, D_MODEL, MEM_HEAD_DIM):
        mq_ref[:, c:c + MEM_HEAD_DIM] = _rms(mq[:, c:c + MEM_HEAD_DIM], qw).astype(BF16)


def _out_proj(x2d, od, of, w_out, norm_w, w_mem_q, mem_q_norm_w):
    n_tok = x2d.shape[0]
    tok = lambda width: pl.BlockSpec((TOK_TILE, width), lambda i: (i, 0))
    fixed = lambda r, c: pl.BlockSpec((r, c), lambda i: (0, 0))
    scale = MEM_HEAD_DIM ** -0.5
    return pl.pallas_call(
        _out_proj_body,
        grid=(n_tok // TOK_TILE,),
        in_specs=[tok(D_MODEL), tok(GROUP_W), tok(GROUP_W), fixed(D_MODEL, D_MODEL),
                  fixed(1, D_MODEL), fixed(D_MODEL, D_MODEL), fixed(1, MEM_HEAD_DIM)],
        out_specs=[tok(D_MODEL), tok(D_MODEL)],
        out_shape=[jax.ShapeDtypeStruct((n_tok, D_MODEL), F32),
                   jax.ShapeDtypeStruct((n_tok, D_MODEL), BF16)],
        compiler_params=_params(("arbitrary",)),
        name="out_proj",
    )(x2d, od, of, w_out.astype(BF16), norm_w.reshape(1, -1), w_mem_q.astype(BF16),
      (mem_q_norm_w * scale).reshape(1, -1))


def _mem_kv_body(m_ref, nw_ref, w_ref, kw_ref, mk_ref, mv_ref):
    hm = _rms(m_ref[...], nw_ref[...]).astype(BF16)
    kv = _dot(hm, w_ref[...])
    kw = kw_ref[...]
    for c in range(0, D_MODEL, MEM_HEAD_DIM):
        mk_ref[:, c:c + MEM_HEAD_DIM] = _rms(kv[:, c:c + MEM_HEAD_DIM], kw).astype(BF16)
    mv_ref[...] = kv[:, D_MODEL:].astype(BF16)


def _mem_kv(mem2d, norm_w, w_mem_kv, mem_k_norm_w):
    n_mem = mem2d.shape[0]
    tok = pl.BlockSpec((TOK_TILE, D_MODEL), lambda i: (i, 0))
    fixed = lambda r, c: pl.BlockSpec((r, c), lambda i: (0, 0))
    out = jax.ShapeDtypeStruct((n_mem, D_MODEL), BF16)
    return pl.pallas_call(
        _mem_kv_body,
        grid=(n_mem // TOK_TILE,),
        in_specs=[tok, fixed(1, D_MODEL), fixed(D_MODEL, 2 * D_MODEL), fixed(1, MEM_HEAD_DIM)],
        out_specs=[tok, tok],
        out_shape=[out, out],
        compiler_params=_params(("arbitrary",)),
        name="mem_kv",
    )(mem2d, norm_w.reshape(1, -1), w_mem_kv.astype(BF16), mem_k_norm_w.reshape(1, -1))


def _mem_attn_body(x1_ref, mq_ref, mk_ref, mv_ref, wo_ref, x2_ref):
    mq = mq_ref[...]
    heads = []
    for c in range(0, D_MODEL, MEM_HEAD_DIM):
        s = _dot_nt(mq[:, c:c + MEM_HEAD_DIM], mk_ref[:, c:c + MEM_HEAD_DIM])
        p = jnp.exp(s - jnp.max(s, axis=-1, keepdims=True))
        l = jnp.sum(p, axis=-1, keepdims=True)
        heads.append((_dot(p.astype(BF16), mv_ref[:, c:c + MEM_HEAD_DIM]) / l).astype(BF16))
    mo = jnp.concatenate(heads, axis=-1)
    x2_ref[...] = x1_ref[...] + _dot(mo, wo_ref[...])


def _mem_attn(x1, mq, mk, mv, w_mem_o, batch):
    n_tok = x1.shape[0]
    tiles = n_tok // batch // TOK_TILE
    mem_len = mk.shape[0] // batch
    tok = pl.BlockSpec((TOK_TILE, D_MODEL), lambda b, i: (b * tiles + i, 0))
    mem = pl.BlockSpec((mem_len, D_MODEL), lambda b, i: (b, 0))
    return pl.pallas_call(
        _mem_attn_body,
        grid=(batch, tiles),
        in_specs=[tok, tok, mem, mem, pl.BlockSpec((D_MODEL, D_MODEL), lambda b, i: (0, 0))],
        out_specs=tok,
        out_shape=jax.ShapeDtypeStruct((n_tok, D_MODEL), F32),
        compiler_params=_params(("arbitrary", "arbitrary")),
        name="mem_attn",
    )(x1, mq, mk, mv, w_mem_o.astype(BF16))


FF_CHUNK = 1024


def _mlp_body(x_ref, nw_ref, wu_ref, wd_ref, o_ref):
    x = x_ref[...]
    h = _rms(x, nw_ref[...]).astype(BF16)
    acc = x
    for c in range(0, D_FF, FF_CHUNK):
        u = jnp.maximum(_dot(h, wu_ref[:, c:c + FF_CHUNK]), 0.0)
        acc = acc + _dot((u * u).astype(BF16), wd_ref[c:c + FF_CHUNK, :])
    o_ref[...] = acc


def _mlp(x2, norm_w, w_up, w_down):
    n_tok = x2.shape[0]
    tok = pl.BlockSpec((TOK_TILE, D_MODEL), lambda i: (i, 0))
    fixed = lambda r, c: pl.BlockSpec((r, c), lambda i: (0, 0))
    return pl.pallas_call(
        _mlp_body,
        grid=(n_tok // TOK_TILE,),
        in_specs=[tok, fixed(1, D_MODEL), fixed(D_MODEL, D_FF), fixed(D_FF, D_MODEL)],
        out_specs=tok,
        out_shape=jax.ShapeDtypeStruct((n_tok, D_MODEL), F32),
        compiler_params=_params(("arbitrary",)),
        name="mlp",
    )(x2, norm_w.reshape(1, -1), w_up.astype(BF16), w_down.astype(BF16))


def kernel(x, mem, positions, norm_mix_w, w_in, b_forget, diff_q_norm_w, diff_k_norm_w,
           lambda_q1, lambda_k1, lambda_q2, lambda_k2, diff_subln_w, fox_q_norm_w,
           fox_k_norm_w, w_out, norm_mem_q_w, norm_mem_kv_w, w_mem_q, w_mem_kv,
           mem_q_norm_w, mem_k_norm_w, w_mem_o, norm_mlp_w, w_up, w_down):
    batch, seq, d = x.shape
    depth = w_in.shape[0]
    assert d == D_MODEL and seq % TOK_TILE == 0 and seq % ATT_TILE == 0
    assert (batch * mem.shape[1]) % TOK_TILE == 0

    ctab, stab = _rope_tables(positions)
    xc = x.reshape(batch * seq, d)
    mem2d = mem.reshape(-1, d)
    for l in range(depth):
        lam_init = 0.8 - 0.6 * math.exp(-0.3 * l)
        dq, dk, dv, fq, fk, fv, cum = _in_proj(
            xc, batch, norm_mix_w[l], w_in[l], ctab, stab, diff_q_norm_w[l],
            diff_k_norm_w[l], fox_q_norm_w[l], fox_k_norm_w[l], b_forget[l])
        od = _diff_attn(dq, dk, dv, batch, lambda_q1[l], lambda_k1[l], lambda_q2[l],
                        lambda_k2[l], diff_subln_w[l], lam_init)
        of = _fox_attn(fq, fk, fv, cum, batch)
        x1, mq = _out_proj(xc, od, of, w_out[l], norm_mem_q_w[l], w_mem_q[l], mem_q_norm_w[l])
        mk, mv = _mem_kv(mem2d, norm_mem_kv_w[l], w_mem_kv[l], mem_k_norm_w[l])
        x2 = _mem_attn(x1, mq, mk, mv, w_mem_o[l], batch)
        xc = _mlp(x2, norm_mlp_w[l], w_up[l], w_down[l])
    return xc.reshape(batch, seq, d)
```

```python
import functools
import math

import jax
import jax.numpy as jnp
from jax import lax
from jax.experimental import pallas as pl
from jax.experimental.pallas import tpu as pltpu

F32 = jnp.float32
BF16 = jnp.bfloat16

D_MODEL = 1024
CHUNK = 64
HEAD_DIM = 64
N_DIFF_HEADS = 4
DIFF_V_DIM = 2 * HEAD_DIM
N_FOX_HEADS = 8
GROUP_W = 512
ROPE_DIM = HEAD_DIM // 4
ROPE_THETA = 500000.0
N_MEM_HEADS = 4
MEM_HEAD_DIM = D_MODEL // N_MEM_HEADS
D_FF = 4 * D_MODEL
EPS = 1e-6
NEG_INF = -1e30
LOG2E = math.log2(math.e)

LANES = 128
MXU_DIM = 256
VMEM_LIMIT = 56 * 1024 * 1024

TOK_TILE = 512
ATT_TILE = 256


def _params(sem):
    return pltpu.CompilerParams(dimension_semantics=sem, vmem_limit_bytes=VMEM_LIMIT)


def _rms(x, w):
    ms = jnp.mean(x * x, axis=-1, keepdims=True)
    return (x * lax.rsqrt(ms + EPS)) * w


def _dot(a, b):
    return jnp.dot(a, b, preferred_element_type=F32)


def _dot_nt(a, b):
    return lax.dot_general(a, b, (((1,), (1,)), ((), ())), preferred_element_type=F32)


def _same_group_matrix(n, group):
    r = lax.broadcasted_iota(jnp.int32, (n, n), 0) // group
    c = lax.broadcasted_iota(jnp.int32, (n, n), 1) // group
    return jnp.where(r == c, 1.0, 0.0).astype(BF16)


def _group_rms(t, w, group):
    g = _same_group_matrix(MXU_DIM, group)
    sq = (t * t).astype(BF16)
    parts = [_dot(sq[:, c:c + MXU_DIM], g) for c in range(0, t.shape[1], MXU_DIM)]
    ss = jnp.concatenate(parts, axis=-1)
    return (t * lax.rsqrt(ss * (1.0 / group) + EPS)) * w


def _rope_body(pos_ref, freq_ref, cos_ref, sin_ref, nsin_ref):
    ang = pos_ref[...] * freq_ref[...]
    s = jnp.sin(ang)
    cos_ref[...] = jnp.cos(ang)
    sin_ref[...] = s
    nsin_ref[...] = -s


def _rope_tables(positions):
    n_tok = positions.size
    half = ROPE_DIM // 2
    inv_freq = ROPE_THETA ** (-jnp.arange(0, ROPE_DIM, 2, dtype=F32) / ROPE_DIM)
    rows = n_tok * half // LANES
    pos = jnp.repeat(positions.reshape(-1).astype(F32), half).reshape(rows, LANES)
    freq = jnp.tile(inv_freq, LANES // half).reshape(1, LANES)
    out = jax.ShapeDtypeStruct((rows, LANES), F32)
    cos, sin, nsin = pl.pallas_call(
        _rope_body, out_shape=(out, out, out), name="rope_tables")(pos, freq)
    cos, sin, nsin = (t.reshape(n_tok, half) for t in (cos, sin, nsin))
    rest = HEAD_DIM - ROPE_DIM
    ctab = jnp.concatenate([cos, cos, jnp.ones((n_tok, rest), F32)], axis=-1)
    stab = jnp.concatenate([nsin, sin, jnp.zeros((n_tok, rest), F32)], axis=-1)
    reps = LANES // HEAD_DIM
    return jnp.tile(ctab, (1, reps)), jnp.tile(stab, (1, reps))


def _rope(t, ctab, stab):
    half = ROPE_DIM // 2
    lane = lax.broadcasted_iota(jnp.int32, (1, LANES), 1) % HEAD_DIM
    first_half = lane < half
    outs = []
    for c in range(0, t.shape[1], LANES):
        tb = t[:, c:c + LANES]
        from_above = pltpu.roll(tb, LANES - half, axis=1)
        from_below = pltpu.roll(tb, half, axis=1)
        partner = jnp.where(first_half, from_above, from_below)
        outs.append(tb * ctab + partner * stab)
    return jnp.concatenate(outs, axis=-1)


def _log_sigmoid(x):
    return jnp.minimum(x, 0.0) - jnp.log1p(jnp.exp(-jnp.abs(x)))


def _split3(x):
    hi = x.astype(BF16)
    rem = x - hi.astype(F32)
    mid = rem.astype(BF16)
    lo = (rem - mid.astype(F32)).astype(BF16)
    return hi, mid, lo


BIAS_LANES = 6 * N_FOX_HEADS


def _in_proj_body(x_ref, nw_ref, w_ref, wvt_ref, ctab_ref, stab_ref, dqw_ref, dkw_ref,
                  fqw_ref, fkw_ref, bf_ref, dq_ref, dk_ref, dvt_ref, fq_ref, fk_ref,
                  fvt_ref, qaug_ref, kaug_ref, carry_ref):
    @pl.when(pl.program_id(1) == 0)
    def _():
        carry_ref[...] = jnp.zeros_like(carry_ref)

    h = _rms(x_ref[...], nw_ref[...]).astype(BF16)
    ctab = ctab_ref[...]
    stab = stab_ref[...]

    def proj(group, width=GROUP_W):
        c0 = group * GROUP_W
        return _dot(h, w_ref[:, c0:c0 + width])

    dq_ref[...] = _rope(_group_rms(proj(0), dqw_ref[...], HEAD_DIM), ctab, stab).astype(BF16)
    dk_ref[...] = _rope(_group_rms(proj(1), dkw_ref[...], HEAD_DIM), ctab, stab).astype(BF16)
    fq_ref[...] = _group_rms(proj(2), fqw_ref[...], HEAD_DIM).astype(BF16)
    fk_ref[...] = _group_rms(proj(3), fkw_ref[...], HEAD_DIM).astype(BF16)

    for vt_ref, r0 in ((dvt_ref, 0), (fvt_ref, GROUP_W)):
        vt = _dot_nt(wvt_ref[r0:r0 + GROUP_W, :], h).astype(BF16)
        for t in range(TOK_TILE // ATT_TILE):
            vt_ref[t] = vt[:, t * ATT_TILE:(t + 1) * ATT_TILE]

    log_f = _log_sigmoid(proj(4, LANES) + bf_ref[...])
    r = lax.broadcasted_iota(jnp.int32, (MXU_DIM, MXU_DIM), 0)
    c = lax.broadcasted_iota(jnp.int32, (MXU_DIM, MXU_DIM), 1)
    tril = jnp.where(c <= r, 1.0, 0.0).astype(BF16)
    lane = lax.broadcasted_iota(jnp.int32, (MXU_DIM, LANES), 1)
    carry = carry_ref[0:1, :]
    for r0 in range(0, TOK_TILE, MXU_DIM):
        hi, mid, lo = _split3(log_f[r0:r0 + MXU_DIM])
        cum = (_dot(tril, hi) + _dot(tril, mid)) + _dot(tril, lo) + carry
        carry = cum[MXU_DIM - 1:MXU_DIM, :]
        hi, mid, lo = (t.astype(F32) for t in _split3(cum * LOG2E))
        kaug_ref[r0:r0 + MXU_DIM, :] = jnp.where(
            lane < 8, -hi, jnp.where(lane < 16, -mid, jnp.where(
                lane < 24, -lo, jnp.where(lane < BIAS_LANES, 1.0, 0.0)))).astype(BF16)
        qaug_ref[r0:r0 + MXU_DIM, :] = jnp.where(
            lane < 24, 1.0, jnp.where(lane < 32, hi, jnp.where(
                lane < 40, mid, jnp.where(lane < BIAS_LANES, lo, 0.0)))).astype(BF16)
    carry_ref[0:1, :] = carry


def _in_proj(x2d, batch, norm_w, w_in, ctab, stab, dqw, dkw, fqw, fkw, b_forget):
    n_tok = x2d.shape[0]
    tiles = n_tok // batch // TOK_TILE
    g = GROUP_W
    dq, dk, dv, fq, fk, fv, fl = (w_in[:, s:e] for s, e in
                                   ((0, g), (g, 2 * g), (2 * g, 3 * g), (3 * g, 4 * g),
                                    (4 * g, 5 * g), (5 * g, 6 * g), (6 * g, 6 * g + N_FOX_HEADS)))
    reps = BIAS_LANES // N_FOX_HEADS
    fl = jnp.pad(jnp.tile(fl, (1, reps)), ((0, 0), (0, LANES - BIAS_LANES)))
    w = jnp.concatenate([dq, dk, fq, fk, fl], axis=1).astype(BF16)
    wvt = jnp.concatenate([dv, fv], axis=1).T.astype(BF16)
    bf = jnp.pad(jnp.tile(b_forget, reps), (0, LANES - BIAS_LANES)).reshape(1, LANES)

    def tile_w(v, scale=1.0):
        return jnp.tile(v * scale, GROUP_W // HEAD_DIM).reshape(1, GROUP_W)

    row = lambda b, i: (b * tiles + i, 0)
    fixed = lambda b, i: (0, 0)
    tok = lambda width: pl.BlockSpec((TOK_TILE, width), row)
    vec = lambda width: pl.BlockSpec((1, width), fixed)
    slabs = TOK_TILE // ATT_TILE
    vt_spec = pl.BlockSpec((slabs, GROUP_W, ATT_TILE), lambda b, i: (b * tiles + i, 0, 0))
    qk = jax.ShapeDtypeStruct((n_tok, GROUP_W), BF16)
    vt = jax.ShapeDtypeStruct((n_tok // ATT_TILE, GROUP_W, ATT_TILE), BF16)
    aug = jax.ShapeDtypeStruct((n_tok, LANES), BF16)
    qscale = HEAD_DIM ** -0.5 * LOG2E
    return pl.pallas_call(
        _in_proj_body,
        grid=(batch, tiles),
        in_specs=[tok(D_MODEL), vec(D_MODEL), pl.BlockSpec(w.shape, fixed),
                  pl.BlockSpec(wvt.shape, fixed), tok(LANES), tok(LANES), vec(GROUP_W),
                  vec(GROUP_W), vec(GROUP_W), vec(GROUP_W), vec(LANES)],
        out_specs=[tok(g), tok(g), vt_spec, tok(g), tok(g), vt_spec, tok(LANES), tok(LANES)],
        out_shape=[qk, qk, vt, qk, qk, vt, aug, aug],
        scratch_shapes=[pltpu.VMEM((8, LANES), F32)],
        compiler_params=_params(("arbitrary", "arbitrary")),
        name="in_proj",
    )(x2d, norm_w.reshape(1, -1), w, wvt, ctab, stab, tile_w(dqw, qscale), tile_w(dkw),
      tile_w(fqw, qscale), tile_w(fkw), bf)


N_CHAINS = 2 * (GROUP_W // LANES)


def _attn_body(q_ref, k_ref, vt_ref, *rest, fox, lam_init):
    if fox:
        qaug_ref, kaug_ref, o_ref, rhs_scr, m_scr, l_scr, acc_scr = rest
    else:
        lq1_ref, lk1_ref, lq2_ref, lk2_ref, subw_ref, o_ref, rhs_scr, m_scr, l_scr, acc_scr = rest
    i = pl.program_id(1)
    width = acc_scr.shape[1]
    lane = lax.broadcasted_iota(jnp.int32, (ATT_TILE, LANES), 1)

    if fox:
        qaug = qaug_ref[...].astype(F32)
    for g in range(GROUP_W // LANES):
        qg = q_ref[:, g * LANES:(g + 1) * LANES].astype(F32)
        for a in range(2):
            c = 2 * g + a
            keep = lane < HEAD_DIM if a == 0 else lane >= HEAD_DIM
            rhs_scr[c, :, :LANES] = jnp.where(keep, qg, 0.0).astype(BF16)
            if fox:
                rhs_scr[c, :, LANES:] = jnp.where(lane % N_FOX_HEADS == c, qaug, 0.0).astype(BF16)
    m_scr[...] = jnp.full(m_scr.shape, NEG_INF, F32)
    l_scr[...] = jnp.zeros(l_scr.shape, F32)
    acc_scr[...] = jnp.zeros(acc_scr.shape, F32)

    def block(j, mask):
        start = pl.multiple_of(j * ATT_TILE, ATT_TILE)
        for g in range(GROUP_W // LANES):
            lhs = k_ref[pl.ds(start, ATT_TILE), g * LANES:(g + 1) * LANES]
            if fox:
                lhs = jnp.concatenate([lhs, kaug_ref[pl.ds(start, ATT_TILE), :]], axis=-1)
            for a in range(2):
                c = 2 * g + a
                s = _dot_nt(lhs, rhs_scr[c])
                if mask is not None:
                    s = jnp.where(mask, s, NEG_INF)
                m_old = m_scr[c:c + 1, :]
                m_new = jnp.maximum(m_old, jnp.max(s, axis=0, keepdims=True))
                alpha = jnp.exp2(m_old - m_new)
                p = jnp.exp2(s - m_new)
                l_scr[c:c + 1, :] = alpha * l_scr[c:c + 1, :] + jnp.sum(p, axis=0, keepdims=True)
                m_scr[c:c + 1, :] = m_new
                r0 = c * HEAD_DIM if fox else g * LANES
                vt = vt_ref[j, r0:r0 + width, :]
                acc_scr[c] = alpha * acc_scr[c] + _dot(vt, p.astype(BF16))

    def loop_body(j, carry):
        block(j, None)
        return carry

    lax.fori_loop(0, i, loop_body, 0)
    div = 1 if fox else CHUNK
    k_id = lax.broadcasted_iota(jnp.int32, (ATT_TILE, ATT_TILE), 0) // div
    q_id = lax.broadcasted_iota(jnp.int32, (ATT_TILE, ATT_TILE), 1) // div
    block(i, k_id <= q_id)

    if not fox:
        lam = (jnp.exp(jnp.sum(lq1_ref[...] * lk1_ref[...], axis=-1, keepdims=True))
               - jnp.exp(jnp.sum(lq2_ref[...] * lk2_ref[...], axis=-1, keepdims=True))
               + lam_init)
    for g in range(GROUP_W // LANES):
        o1 = acc_scr[2 * g] / l_scr[2 * g:2 * g + 1, :]
        o2 = acc_scr[2 * g + 1] / l_scr[2 * g + 1:2 * g + 2, :]
        if fox:
            o = jnp.concatenate([o1, o2], axis=0).T
        else:
            o = _rms((o1 - lam * o2).T, subw_ref[...]) * (1.0 - lam_init)
        o_ref[:, g * LANES:(g + 1) * LANES] = o.astype(BF16)


def _attention(q, k, vt, batch, extra, *, fox, lam_init=0.0):
    n_tok = q.shape[0]
    seq = n_tok // batch
    tiles = seq // ATT_TILE
    width = HEAD_DIM if fox else DIFF_V_DIM
    q_spec = lambda w: pl.BlockSpec((ATT_TILE, w), lambda b, i: (b * tiles + i, 0))
    kv_spec = lambda w: pl.BlockSpec((seq, w), lambda b, i: (b, 0))
    vt_spec = pl.BlockSpec((tiles, GROUP_W, ATT_TILE), lambda b, i: (b, 0, 0))
    if fox:
        extra_specs = [q_spec(LANES), kv_spec(LANES)]
    else:
        extra_specs = [pl.BlockSpec(e.shape, lambda b, i: (0, 0)) for e in extra]
    rhs_w = 2 * LANES if fox else LANES
    return pl.pallas_call(
        functools.partial(_attn_body, fox=fox, lam_init=lam_init),
        grid=(batch, tiles),
        in_specs=[q_spec(GROUP_W), kv_spec(GROUP_W), vt_spec] + extra_specs,
        out_specs=q_spec(GROUP_W),
        out_shape=jax.ShapeDtypeStruct((n_tok, GROUP_W), BF16),
        scratch_shapes=[pltpu.VMEM((N_CHAINS, ATT_TILE, rhs_w), BF16),
                        pltpu.VMEM((N_CHAINS, ATT_TILE), F32),
                        pltpu.VMEM((N_CHAINS, ATT_TILE), F32),
                        pltpu.VMEM((N_CHAINS, width, ATT_TILE), F32)],
        compiler_params=_params(("arbitrary", "arbitrary")),
        name="fox_attn" if fox else "diff_attn",
    )(q, k, vt, *extra)


def _out_proj_body(x_ref, od_ref, of_ref, wo_ref, nw_ref, wq_ref, qw_ref, x1_ref, mq_ref):
    x1 = x_ref[...] + (_dot(od_ref[...], wo_ref[:GROUP_W, :])
                       + _dot(of_ref[...], wo_ref[GROUP_W:, :]))
    x1_ref[...] = x1
    hq = _rms(x1, nw_ref[...]).astype(BF16)
    mq = _dot(hq, wq_ref[...])
    qw = qw_ref[...]
    for c in range(0, D_MODEL, MEM_HEAD_DIM):
        mq_ref[:, c:c + MEM_HEAD_DIM] = _rms(mq[:, c:c + MEM_HEAD_DIM], qw).astype(BF16)


def _out_proj(x2d, od, of, w_out, norm_w, w_mem_q, mem_q_norm_w):
    n_tok = x2d.shape[0]
    tok = lambda width: pl.BlockSpec((TOK_TILE, width), lambda i: (i, 0))
    fixed = lambda r, c: pl.BlockSpec((r, c), lambda i: (0, 0))
    scale = MEM_HEAD_DIM ** -0.5
    return pl.pallas_call(
        _out_proj_body,
        grid=(n_tok // TOK_TILE,),
        in_specs=[tok(D_MODEL), tok(GROUP_W), tok(GROUP_W), fixed(D_MODEL, D_MODEL),
                  fixed(1, D_MODEL), fixed(D_MODEL, D_MODEL), fixed(1, MEM_HEAD_DIM)],
        out_specs=[tok(D_MODEL), tok(D_MODEL)],
        out_shape=[jax.ShapeDtypeStruct((n_tok, D_MODEL), F32),
                   jax.ShapeDtypeStruct((n_tok, D_MODEL), BF16)],
        compiler_params=_params(("arbitrary",)),
        name="out_proj",
    )(x2d, od, of, w_out.astype(BF16), norm_w.reshape(1, -1), w_mem_q.astype(BF16),
      (mem_q_norm_w * scale).reshape(1, -1))


def _mem_kv_body(m_ref, nw_ref, w_ref, kw_ref, mk_ref, mv_ref):
    hm = _rms(m_ref[...], nw_ref[...]).astype(BF16)
    kv = _dot(hm, w_ref[...])
    kw = kw_ref[...]
    for c in range(0, D_MODEL, MEM_HEAD_DIM):
        mk_ref[:, c:c + MEM_HEAD_DIM] = _rms(kv[:, c:c + MEM_HEAD_DIM], kw).astype(BF16)
    mv_ref[...] = kv[:, D_MODEL:].astype(BF16)


def _mem_kv(mem2d, norm_w, w_mem_kv, mem_k_norm_w):
    n_mem = mem2d.shape[0]
    tok = pl.BlockSpec((TOK_TILE, D_MODEL), lambda i: (i, 0))
    fixed = lambda r, c: pl.BlockSpec((r, c), lambda i: (0, 0))
    out = jax.ShapeDtypeStruct((n_mem, D_MODEL), BF16)
    return pl.pallas_call(
        _mem_kv_body,
        grid=(n_mem // TOK_TILE,),
        in_specs=[tok, fixed(1, D_MODEL), fixed(D_MODEL, 2 * D_MODEL), fixed(1, MEM_HEAD_DIM)],
        out_specs=[tok, tok],
        out_shape=[out, out],
        compiler_params=_params(("arbitrary",)),
        name="mem_kv",
    )(mem2d, norm_w.reshape(1, -1), w_mem_kv.astype(BF16), mem_k_norm_w.reshape(1, -1))


def _mem_attn_body(x1_ref, mq_ref, mk_ref, mv_ref, wo_ref, x2_ref):
    mq = mq_ref[...]
    heads = []
    for c in range(0, D_MODEL, MEM_HEAD_DIM):
        s = _dot_nt(mq[:, c:c + MEM_HEAD_DIM], mk_ref[:, c:c + MEM_HEAD_DIM])
        p = jnp.exp(s - jnp.max(s, axis=-1, keepdims=True))
        l = jnp.sum(p, axis=-1, keepdims=True)
        heads.append((_dot(p.astype(BF16), mv_ref[:, c:c + MEM_HEAD_DIM]) / l).astype(BF16))
    mo = jnp.concatenate(heads, axis=-1)
    x2_ref[...] = x1_ref[...] + _dot(mo, wo_ref[...])


def _mem_attn(x1, mq, mk, mv, w_mem_o, batch):
    n_tok = x1.shape[0]
    tiles = n_tok // batch // TOK_TILE
    mem_len = mk.shape[0] // batch
    tok = pl.BlockSpec((TOK_TILE, D_MODEL), lambda b, i: (b * tiles + i, 0))
    mem = pl.BlockSpec((mem_len, D_MODEL), lambda b, i: (b, 0))
    return pl.pallas_call(
        _mem_attn_body,
        grid=(batch, tiles),
        in_specs=[tok, tok, mem, mem, pl.BlockSpec((D_MODEL, D_MODEL), lambda b, i: (0, 0))],
        out_specs=tok,
        out_shape=jax.ShapeDtypeStruct((n_tok, D_MODEL), F32),
        compiler_params=_params(("arbitrary", "arbitrary")),
        name="mem_attn",
    )(x1, mq, mk, mv, w_mem_o.astype(BF16))


FF_CHUNK = 1024


def _mlp_body(x_ref, nw_ref, wu_ref, wd_ref, o_ref):
    x = x_ref[...]
    h = _rms(x, nw_ref[...]).astype(BF16)
    acc = x
    for c in range(0, D_FF, FF_CHUNK):
        u = jnp.maximum(_dot(h, wu_ref[:, c:c + FF_CHUNK]), 0.0)
        acc = acc + _dot((u * u).astype(BF16), wd_ref[c:c + FF_CHUNK, :])
    o_ref[...] = acc


def _mlp(x2, norm_w, w_up, w_down):
    n_tok = x2.shape[0]
    tok = pl.BlockSpec((TOK_TILE, D_MODEL), lambda i: (i, 0))
    fixed = lambda r, c: pl.BlockSpec((r, c), lambda i: (0, 0))
    return pl.pallas_call(
        _mlp_body,
        grid=(n_tok // TOK_TILE,),
        in_specs=[tok, fixed(1, D_MODEL), fixed(D_MODEL, D_FF), fixed(D_FF, D_MODEL)],
        out_specs=tok,
        out_shape=jax.ShapeDtypeStruct((n_tok, D_MODEL), F32),
        compiler_params=_params(("arbitrary",)),
        name="mlp",
    )(x2, norm_w.reshape(1, -1), w_up.astype(BF16), w_down.astype(BF16))


def kernel(x, mem, positions, norm_mix_w, w_in, b_forget, diff_q_norm_w, diff_k_norm_w,
           lambda_q1, lambda_k1, lambda_q2, lambda_k2, diff_subln_w, fox_q_norm_w,
           fox_k_norm_w, w_out, norm_mem_q_w, norm_mem_kv_w, w_mem_q, w_mem_kv,
           mem_q_norm_w, mem_k_norm_w, w_mem_o, norm_mlp_w, w_up, w_down):
    batch, seq, d = x.shape
    depth = w_in.shape[0]
    assert d == D_MODEL and seq % TOK_TILE == 0 and seq % ATT_TILE == 0
    assert (batch * mem.shape[1]) % TOK_TILE == 0

    ctab, stab = _rope_tables(positions)
    xc = x.reshape(batch * seq, d)
    mem2d = mem.reshape(-1, d)
    for l in range(depth):
        lam_init = 0.8 - 0.6 * math.exp(-0.3 * l)
        dq, dk, dvt, fq, fk, fvt, qaug, kaug = _in_proj(
            xc, batch, norm_mix_w[l], w_in[l], ctab, stab, diff_q_norm_w[l],
            diff_k_norm_w[l], fox_q_norm_w[l], fox_k_norm_w[l], b_forget[l])
        lam_params = [p[l].reshape(1, -1) for p in
                      (lambda_q1, lambda_k1, lambda_q2, lambda_k2, diff_subln_w)]
        od = _attention(dq, dk, dvt, batch, lam_params, fox=False, lam_init=lam_init)
        of = _attention(fq, fk, fvt, batch, [qaug, kaug], fox=True)
        x1, mq = _out_proj(xc, od, of, w_out[l], norm_mem_q_w[l], w_mem_q[l], mem_q_norm_w[l])
        mk, mv = _mem_kv(mem2d, norm_mem_kv_w[l], w_mem_kv[l], mem_k_norm_w[l])
        x2 = _mem_attn(x1, mq, mk, mv, w_mem_o[l], batch)
        xc = _mlp(x2, norm_mlp_w[l], w_up[l], w_down[l])
    return xc.reshape(batch, seq, d)
```

```python
import functools
import math

import jax
import jax.numpy as jnp
from jax import lax
from jax.experimental import pallas as pl
from jax.experimental.pallas import tpu as pltpu

F32 = jnp.float32
BF16 = jnp.bfloat16

D_MODEL = 1024
CHUNK = 64
HEAD_DIM = 64
N_DIFF_HEADS = 4
DIFF_V_DIM = 2 * HEAD_DIM
N_FOX_HEADS = 8
GROUP_W = 512
ROPE_DIM = HEAD_DIM // 4
ROPE_THETA = 500000.0
N_MEM_HEADS = 4
MEM_HEAD_DIM = D_MODEL // N_MEM_HEADS
D_FF = 4 * D_MODEL
EPS = 1e-6
NEG_INF = -1e30
LOG2E = math.log2(math.e)

LANES = 128
MXU_DIM = 256
VMEM_LIMIT = 56 * 1024 * 1024

TOK_TILE = 512
ATT_TILE = 256


def _params(sem):
    return pltpu.CompilerParams(dimension_semantics=sem, vmem_limit_bytes=VMEM_LIMIT)


def _rms(x, w):
    ms = jnp.mean(x * x, axis=-1, keepdims=True)
    return (x * lax.rsqrt(ms + EPS)) * w


def _dot(a, b):
    return jnp.dot(a, b, preferred_element_type=F32)


def _dot_nt(a, b):
    return lax.dot_general(a, b, (((1,), (1,)), ((), ())), preferred_element_type=F32)


def _same_group_matrix(n, group):
    r = lax.broadcasted_iota(jnp.int32, (n, n), 0) // group
    c = lax.broadcasted_iota(jnp.int32, (n, n), 1) // group
    return jnp.where(r == c, 1.0, 0.0).astype(BF16)


def _group_rms(t, w, group):
    g = _same_group_matrix(MXU_DIM, group)
    sq = (t * t).astype(BF16)
    parts = [_dot(sq[:, c:c + MXU_DIM], g) for c in range(0, t.shape[1], MXU_DIM)]
    ss = jnp.concatenate(parts, axis=-1)
    return (t * lax.rsqrt(ss * (1.0 / group) + EPS)) * w


def _rope_body(pos_ref, freq_ref, cos_ref, sin_ref, nsin_ref):
    ang = pos_ref[...] * freq_ref[...]
    s = jnp.sin(ang)
    cos_ref[...] = jnp.cos(ang)
    sin_ref[...] = s
    nsin_ref[...] = -s


def _rope_tables(positions):
    n_tok = positions.size
    half = ROPE_DIM // 2
    inv_freq = ROPE_THETA ** (-jnp.arange(0, ROPE_DIM, 2, dtype=F32) / ROPE_DIM)
    rows = n_tok * half // LANES
    pos = jnp.repeat(positions.reshape(-1).astype(F32), half).reshape(rows, LANES)
    freq = jnp.tile(inv_freq, LANES // half).reshape(1, LANES)
    out = jax.ShapeDtypeStruct((rows, LANES), F32)
    cos, sin, nsin = pl.pallas_call(
        _rope_body, out_shape=(out, out, out), name="rope_tables")(pos, freq)
    cos, sin, nsin = (t.reshape(n_tok, half) for t in (cos, sin, nsin))
    rest = HEAD_DIM - ROPE_DIM
    ctab = jnp.concatenate([cos, cos, jnp.ones((n_tok, rest), F32)], axis=-1)
    stab = jnp.concatenate([nsin, sin, jnp.zeros((n_tok, rest), F32)], axis=-1)
    reps = LANES // HEAD_DIM
    return jnp.tile(ctab, (1, reps)), jnp.tile(stab, (1, reps))


def _rope(t, ctab, stab):
    half = ROPE_DIM // 2
    lane = lax.broadcasted_iota(jnp.int32, (1, LANES), 1) % HEAD_DIM
    first_half = lane < half
    outs = []
    for c in range(0, t.shape[1], LANES):
        tb = t[:, c:c + LANES]
        from_above = pltpu.roll(tb, LANES - half, axis=1)
        from_below = pltpu.roll(tb, half, axis=1)
        partner = jnp.where(first_half, from_above, from_below)
        outs.append(tb * ctab + partner * stab)
    return jnp.concatenate(outs, axis=-1)


def _log_sigmoid(x):
    return jnp.minimum(x, 0.0) - jnp.log1p(jnp.exp(-jnp.abs(x)))


def _split3(x):
    hi = x.astype(BF16)
    rem = x - hi.astype(F32)
    mid = rem.astype(BF16)
    lo = (rem - mid.astype(F32)).astype(BF16)
    return hi, mid, lo


BIAS_LANES = 6 * N_FOX_HEADS


def _in_proj_body(x_ref, nw_ref, w_ref, wvt_ref, ctab_ref, stab_ref, dqw_ref, dkw_ref,
                  fqw_ref, fkw_ref, bf_ref, dq_ref, dk_ref, dvt_ref, fq_ref, fk_ref,
                  fvt_ref, qaug_ref, kaug_ref, carry_ref):
    @pl.when(pl.program_id(1) == 0)
    def _():
        carry_ref[...] = jnp.zeros_like(carry_ref)

    h = _rms(x_ref[...], nw_ref[...]).astype(BF16)
    ctab = ctab_ref[...]
    stab = stab_ref[...]

    def proj(group, width=GROUP_W):
        c0 = group * GROUP_W
        return _dot(h, w_ref[:, c0:c0 + width])

    dq_ref[...] = _rope(_group_rms(proj(0), dqw_ref[...], HEAD_DIM), ctab, stab).astype(BF16)
    dk_ref[...] = _rope(_group_rms(proj(1), dkw_ref[...], HEAD_DIM), ctab, stab).astype(BF16)
    fq_ref[...] = _group_rms(proj(2), fqw_ref[...], HEAD_DIM).astype(BF16)
    fk_ref[...] = _group_rms(proj(3), fkw_ref[...], HEAD_DIM).astype(BF16)

    for vt_ref, r0 in ((dvt_ref, 0), (fvt_ref, GROUP_W)):
        vt = _dot_nt(wvt_ref[r0:r0 + GROUP_W, :], h).astype(BF16)
        for t in range(TOK_TILE // ATT_TILE):
            vt_ref[t] = vt[:, t * ATT_TILE:(t + 1) * ATT_TILE]

    log_f = _log_sigmoid(proj(4, LANES) + bf_ref[...])
    r = lax.broadcasted_iota(jnp.int32, (MXU_DIM, MXU_DIM), 0)
    c = lax.broadcasted_iota(jnp.int32, (MXU_DIM, MXU_DIM), 1)
    tril = jnp.where(c <= r, 1.0, 0.0).astype(BF16)
    lane = lax.broadcasted_iota(jnp.int32, (MXU_DIM, LANES), 1)
    carry = carry_ref[0:1, :]
    for r0 in range(0, TOK_TILE, MXU_DIM):
        hi, mid, lo = _split3(log_f[r0:r0 + MXU_DIM])
        cum = (_dot(tril, hi) + _dot(tril, mid)) + _dot(tril, lo) + carry
        carry = cum[MXU_DIM - 1:MXU_DIM, :]
        hi, mid, lo = (t.astype(F32) for t in _split3(cum * LOG2E))
        kaug_ref[r0:r0 + MXU_DIM, :] = jnp.where(
            lane < 8, -hi, jnp.where(lane < 16, -mid, jnp.where(
                lane < 24, -lo, jnp.where(lane < BIAS_LANES, 1.0, 0.0)))).astype(BF16)
        qaug_ref[r0:r0 + MXU_DIM, :] = jnp.where(
            lane < 24, 1.0, jnp.where(lane < 32, hi, jnp.where(
                lane < 40, mid, jnp.where(lane < BIAS_LANES, lo, 0.0)))).astype(BF16)
    carry_ref[0:1, :] = carry


def _in_proj(x2d, batch, norm_w, w_in, ctab, stab, dqw, dkw, fqw, fkw, b_forget):
    n_tok = x2d.shape[0]
    tiles = n_tok // batch // TOK_TILE
    g = GROUP_W
    dq, dk, dv, fq, fk, fv, fl = (w_in[:, s:e] for s, e in
                                   ((0, g), (g, 2 * g), (2 * g, 3 * g), (3 * g, 4 * g),
                                    (4 * g, 5 * g), (5 * g, 6 * g), (6 * g, 6 * g + N_FOX_HEADS)))
    reps = BIAS_LANES // N_FOX_HEADS
    fl = jnp.pad(jnp.tile(fl, (1, reps)), ((0, 0), (0, LANES - BIAS_LANES)))
    w = jnp.concatenate([dq, dk, fq, fk, fl], axis=1).astype(BF16)
    wvt = jnp.concatenate([dv, fv], axis=1).T.astype(BF16)
    bf = jnp.pad(jnp.tile(b_forget, reps), (0, LANES - BIAS_LANES)).reshape(1, LANES)

    def tile_w(v, scale=1.0):
        return jnp.tile(v * scale, GROUP_W // HEAD_DIM).reshape(1, GROUP_W)

    row = lambda b, i: (b * tiles + i, 0)
    fixed = lambda b, i: (0, 0)
    tok = lambda width: pl.BlockSpec((TOK_TILE, width), row)
    vec = lambda width: pl.BlockSpec((1, width), fixed)
    slabs = TOK_TILE // ATT_TILE
    vt_spec = pl.BlockSpec((slabs, GROUP_W, ATT_TILE), lambda b, i: (b * tiles + i, 0, 0))
    qk = jax.ShapeDtypeStruct((n_tok, GROUP_W), BF16)
    vt = jax.ShapeDtypeStruct((n_tok // ATT_TILE, GROUP_W, ATT_TILE), BF16)
    aug = jax.ShapeDtypeStruct((n_tok, LANES), BF16)
    qscale = HEAD_DIM ** -0.5 * LOG2E
    return pl.pallas_call(
        _in_proj_body,
        grid=(batch, tiles),
        in_specs=[tok(D_MODEL), vec(D_MODEL), pl.BlockSpec(w.shape, fixed),
                  pl.BlockSpec(wvt.shape, fixed), tok(LANES), tok(LANES), vec(GROUP_W),
                  vec(GROUP_W), vec(GROUP_W), vec(GROUP_W), vec(LANES)],
        out_specs=[tok(g), tok(g), vt_spec, tok(g), tok(g), vt_spec, tok(LANES), tok(LANES)],
        out_shape=[qk, qk, vt, qk, qk, vt, aug, aug],
        scratch_shapes=[pltpu.VMEM((8, LANES), F32)],
        compiler_params=_params(("arbitrary", "arbitrary")),
        name="in_proj",
    )(x2d, norm_w.reshape(1, -1), w, wvt, ctab, stab, tile_w(dqw, qscale), tile_w(dkw),
      tile_w(fqw, qscale), tile_w(fkw), bf)


N_CHAINS = 2 * (GROUP_W // LANES)


def _attn_body(q_ref, k_ref, vt_ref, *rest, fox, lam_init):
    if fox:
        qaug_ref, kaug_ref, o_ref, rhs_scr, s0_scr, s1_scr, m_scr, l_scr, acc_scr = rest
    else:
        (lq1_ref, lk1_ref, lq2_ref, lk2_ref, subw_ref, o_ref, rhs_scr, s0_scr, s1_scr, m_scr,
         l_scr, acc_scr) = rest
    s_scr = (s0_scr, s1_scr)
    i = pl.program_id(1)
    width = acc_scr.shape[1]

    half_zero = jnp.zeros((HEAD_DIM, ATT_TILE), BF16)
    if fox:
        aug_t = qaug_ref[...].astype(F32).T
        aug_row = lax.broadcasted_iota(jnp.int32, aug_t.shape, 0)
    for g in range(GROUP_W // LANES):
        q_t = q_ref[:, g * LANES:(g + 1) * LANES].astype(F32).T.astype(BF16)
        rhs_scr[2 * g, :HEAD_DIM, :] = q_t[:HEAD_DIM]
        rhs_scr[2 * g, HEAD_DIM:LANES, :] = half_zero
        rhs_scr[2 * g + 1, :HEAD_DIM, :] = half_zero
        rhs_scr[2 * g + 1, HEAD_DIM:LANES, :] = q_t[HEAD_DIM:]
        if fox:
            for c in (2 * g, 2 * g + 1):
                rhs_scr[c, LANES:, :] = jnp.where(
                    aug_row % N_FOX_HEADS == c, aug_t, 0.0).astype(BF16)
    m_scr[...] = jnp.full(m_scr.shape, NEG_INF, F32)
    l_scr[...] = jnp.zeros(l_scr.shape, F32)
    acc_scr[...] = jnp.zeros(acc_scr.shape, F32)

    def scores(j, slot):
        start = pl.multiple_of(j * ATT_TILE, ATT_TILE)
        for g in range(GROUP_W // LANES):
            lhs = k_ref[pl.ds(start, ATT_TILE), g * LANES:(g + 1) * LANES]
            if fox:
                lhs = jnp.concatenate([lhs, kaug_ref[pl.ds(start, ATT_TILE), :]], axis=-1)
            for c in (2 * g, 2 * g + 1):
                s_scr[slot][c] = _dot(lhs, rhs_scr[c])

    def softmax_pv(j, slot, mask):
        for c in range(N_CHAINS):
            s = s_scr[slot][c]
            if mask is not None:
                s = jnp.where(mask, s, NEG_INF)
            m_old = m_scr[c:c + 1, :]
            m_new = jnp.maximum(m_old, jnp.max(s, axis=0, keepdims=True))
            alpha = jnp.exp2(m_old - m_new)
            p = jnp.exp2(s - m_new)
            l_scr[c:c + 1, :] = alpha * l_scr[c:c + 1, :] + jnp.sum(p, axis=0, keepdims=True)
            m_scr[c:c + 1, :] = m_new
            r0 = c * HEAD_DIM if fox else (c // 2) * LANES
            vt = vt_ref[j, r0:r0 + width, :]
            acc_scr[c] = alpha * acc_scr[c] + _dot(vt, p.astype(BF16))

    def pair(u, carry):
        t = 2 * u
        scores(t + 1, 1)
        softmax_pv(t, 0, None)
        scores(t + 2, 0)
        softmax_pv(t + 1, 1, None)
        return carry

    scores(0, 0)
    lax.fori_loop(0, i // 2, pair, 0)
    div = 1 if fox else CHUNK
    k_id = lax.broadcasted_iota(jnp.int32, (ATT_TILE, ATT_TILE), 0) // div
    q_id = lax.broadcasted_iota(jnp.int32, (ATT_TILE, ATT_TILE), 1) // div
    mask = k_id <= q_id

    @pl.when(i % 2 == 0)
    def _():
        softmax_pv(i, 0, mask)

    @pl.when(i % 2 == 1)
    def _():
        scores(i, 1)
        softmax_pv(i - 1, 0, None)
        softmax_pv(i, 1, mask)

    if not fox:
        lam = (jnp.exp(jnp.sum(lq1_ref[...] * lk1_ref[...], axis=-1, keepdims=True))
               - jnp.exp(jnp.sum(lq2_ref[...] * lk2_ref[...], axis=-1, keepdims=True))
               + lam_init)
    for g in range(GROUP_W // LANES):
        o1 = acc_scr[2 * g] / l_scr[2 * g:2 * g + 1, :]
        o2 = acc_scr[2 * g + 1] / l_scr[2 * g + 1:2 * g + 2, :]
        if fox:
            o = jnp.concatenate([o1, o2], axis=0).T
        else:
            o = _rms((o1 - lam * o2).T, subw_ref[...]) * (1.0 - lam_init)
        o_ref[:, g * LANES:(g + 1) * LANES] = o.astype(BF16)


def _attention(q, k, vt, batch, extra, *, fox, lam_init=0.0):
    n_tok = q.shape[0]
    seq = n_tok // batch
    tiles = seq // ATT_TILE
    width = HEAD_DIM if fox else DIFF_V_DIM
    q_spec = lambda w: pl.BlockSpec((ATT_TILE, w), lambda b, i: (b * tiles + i, 0))
    kv_spec = lambda w: pl.BlockSpec((seq, w), lambda b, i: (b, 0))
    vt_spec = pl.BlockSpec((tiles, GROUP_W, ATT_TILE), lambda b, i: (b, 0, 0))
    if fox:
        extra_specs = [q_spec(LANES), kv_spec(LANES)]
    else:
        extra_specs = [pl.BlockSpec(e.shape, lambda b, i: (0, 0)) for e in extra]
    rhs_w = 2 * LANES if fox else LANES
    return pl.pallas_call(
        functools.partial(_attn_body, fox=fox, lam_init=lam_init),
        grid=(batch, tiles),
        in_specs=[q_spec(GROUP_W), kv_spec(GROUP_W), vt_spec] + extra_specs,
        out_specs=q_spec(GROUP_W),
        out_shape=jax.ShapeDtypeStruct((n_tok, GROUP_W), BF16),
        scratch_shapes=[pltpu.VMEM((N_CHAINS, rhs_w, ATT_TILE), BF16),
                        pltpu.VMEM((N_CHAINS, ATT_TILE, ATT_TILE), F32),
                        pltpu.VMEM((N_CHAINS, ATT_TILE, ATT_TILE), F32),
                        pltpu.VMEM((N_CHAINS, ATT_TILE), F32),
                        pltpu.VMEM((N_CHAINS, ATT_TILE), F32),
                        pltpu.VMEM((N_CHAINS, width, ATT_TILE), F32)],
        compiler_params=_params(("arbitrary", "arbitrary")),
        name="fox_attn" if fox else "diff_attn",
    )(q, k, vt, *extra)


def _out_proj_body(x_ref, od_ref, of_ref, wo_ref, nw_ref, wq_ref, qw_ref, x1_ref, mq_ref):
    x1 = x_ref[...] + (_dot(od_ref[...], wo_ref[:GROUP_W, :])
                       + _dot(of_ref[...], wo_ref[GROUP_W:, :]))
    x1_ref[...] = x1
    hq = _rms(x1, nw_ref[...]).astype(BF16)
    mq = _dot(hq, wq_ref[...])
    qw = qw_ref[...]
    for c in range(0, D_MODEL, MEM_HEAD_DIM):
        mq_ref[:, c:c + MEM_HEAD_DIM] = _rms(mq[:, c:c + MEM_HEAD_DIM], qw).astype(BF16)


def _out_proj(x2d, od, of, w_out, norm_w, w_mem_q, mem_q_norm_w):
    n_tok = x2d.shape[0]
    tok = lambda width: pl.BlockSpec((TOK_TILE, width), lambda i: (i, 0))
    fixed = lambda r, c: pl.BlockSpec((r, c), lambda i: (0, 0))
    scale = MEM_HEAD_DIM ** -0.5
    return pl.pallas_call(
        _out_proj_body,
        grid=(n_tok // TOK_TILE,),
        in_specs=[tok(D_MODEL), tok(GROUP_W), tok(GROUP_W), fixed(D_MODEL, D_MODEL),
                  fixed(1, D_MODEL), fixed(D_MODEL, D_MODEL), fixed(1, MEM_HEAD_DIM)],
        out_specs=[tok(D_MODEL), tok(D_MODEL)],
        out_shape=[jax.ShapeDtypeStruct((n_tok, D_MODEL), F32),
                   jax.ShapeDtypeStruct((n_tok, D_MODEL), BF16)],
        compiler_params=_params(("arbitrary",)),
        name="out_proj",
    )(x2d, od, of, w_out.astype(BF16), norm_w.reshape(1, -1), w_mem_q.astype(BF16),
      (mem_q_norm_w * scale).reshape(1, -1))


def _mem_kv_body(m_ref, nw_ref, w_ref, kw_ref, mk_ref, mv_ref):
    hm = _rms(m_ref[...], nw_ref[...]).astype(BF16)
    kv = _dot(hm, w_ref[...])
    kw = kw_ref[...]
    for c in range(0, D_MODEL, MEM_HEAD_DIM):
        mk_ref[:, c:c + MEM_HEAD_DIM] = _rms(kv[:, c:c + MEM_HEAD_DIM], kw).astype(BF16)
    mv_ref[...] = kv[:, D_MODEL:].astype(BF16)


def _mem_kv(mem2d, norm_w, w_mem_kv, mem_k_norm_w):
    n_mem = mem2d.shape[0]
    tok = pl.BlockSpec((TOK_TILE, D_MODEL), lambda i: (i, 0))
    fixed = lambda r, c: pl.BlockSpec((r, c), lambda i: (0, 0))
    out = jax.ShapeDtypeStruct((n_mem, D_MODEL), BF16)
    return pl.pallas_call(
        _mem_kv_body,
        grid=(n_mem // TOK_TILE,),
        in_specs=[tok, fixed(1, D_MODEL), fixed(D_MODEL, 2 * D_MODEL), fixed(1, MEM_HEAD_DIM)],
        out_specs=[tok, tok],
        out_shape=[out, out],
        compiler_params=_params(("arbitrary",)),
        name="mem_kv",
    )(mem2d, norm_w.reshape(1, -1), w_mem_kv.astype(BF16), mem_k_norm_w.reshape(1, -1))


def _mem_attn_body(x1_ref, mq_ref, mk_ref, mv_ref, wo_ref, x2_ref):
    mq = mq_ref[...]
    heads = []
    for c in range(0, D_MODEL, MEM_HEAD_DIM):
        s = _dot_nt(mq[:, c:c + MEM_HEAD_DIM], mk_ref[:, c:c + MEM_HEAD_DIM])
        p = jnp.exp(s - jnp.max(s, axis=-1, keepdims=True))
        l = jnp.sum(p, axis=-1, keepdims=True)
        heads.append((_dot(p.astype(BF16), mv_ref[:, c:c + MEM_HEAD_DIM]) / l).astype(BF16))
    mo = jnp.concatenate(heads, axis=-1)
    x2_ref[...] = x1_ref[...] + _dot(mo, wo_ref[...])


def _mem_attn(x1, mq, mk, mv, w_mem_o, batch):
    n_tok = x1.shape[0]
    tiles = n_tok // batch // TOK_TILE
    mem_len = mk.shape[0] // batch
    tok = pl.BlockSpec((TOK_TILE, D_MODEL), lambda b, i: (b * tiles + i, 0))
    mem = pl.BlockSpec((mem_len, D_MODEL), lambda b, i: (b, 0))
    return pl.pallas_call(
        _mem_attn_body,
        grid=(batch, tiles),
        in_specs=[tok, tok, mem, mem, pl.BlockSpec((D_MODEL, D_MODEL), lambda b, i: (0, 0))],
        out_specs=tok,
        out_shape=jax.ShapeDtypeStruct((n_tok, D_MODEL), F32),
        compiler_params=_params(("arbitrary", "arbitrary")),
        name="mem_attn",
    )(x1, mq, mk, mv, w_mem_o.astype(BF16))


FF_CHUNK = 1024


def _mlp_body(x_ref, nw_ref, wu_ref, wd_ref, o_ref):
    x = x_ref[...]
    h = _rms(x, nw_ref[...]).astype(BF16)
    acc = x
    for c in range(0, D_FF, FF_CHUNK):
        u = jnp.maximum(_dot(h, wu_ref[:, c:c + FF_CHUNK]), 0.0)
        acc = acc + _dot((u * u).astype(BF16), wd_ref[c:c + FF_CHUNK, :])
    o_ref[...] = acc


def _mlp(x2, norm_w, w_up, w_down):
    n_tok = x2.shape[0]
    tok = pl.BlockSpec((TOK_TILE, D_MODEL), lambda i: (i, 0))
    fixed = lambda r, c: pl.BlockSpec((r, c), lambda i: (0, 0))
    return pl.pallas_call(
        _mlp_body,
        grid=(n_tok // TOK_TILE,),
        in_specs=[tok, fixed(1, D_MODEL), fixed(D_MODEL, D_FF), fixed(D_FF, D_MODEL)],
        out_specs=tok,
        out_shape=jax.ShapeDtypeStruct((n_tok, D_MODEL), F32),
        compiler_params=_params(("arbitrary",)),
        name="mlp",
    )(x2, norm_w.reshape(1, -1), w_up.astype(BF16), w_down.astype(BF16))


def kernel(x, mem, positions, norm_mix_w, w_in, b_forget, diff_q_norm_w, diff_k_norm_w,
           lambda_q1, lambda_k1, lambda_q2, lambda_k2, diff_subln_w, fox_q_norm_w,
           fox_k_norm_w, w_out, norm_mem_q_w, norm_mem_kv_w, w_mem_q, w_mem_kv,
           mem_q_norm_w, mem_k_norm_w, w_mem_o, norm_mlp_w, w_up, w_down):
    batch, seq, d = x.shape
    depth = w_in.shape[0]
    assert d == D_MODEL and seq % TOK_TILE == 0 and seq % ATT_TILE == 0
    assert (batch * mem.shape[1]) % TOK_TILE == 0

    ctab, stab = _rope_tables(positions)
    xc = x.reshape(batch * seq, d)
    mem2d = mem.reshape(-1, d)
    for l in range(depth):
        lam_init = 0.8 - 0.6 * math.exp(-0.3 * l)
        dq, dk, dvt, fq, fk, fvt, qaug, kaug = _in_proj(
            xc, batch, norm_mix_w[l], w_in[l], ctab, stab, diff_q_norm_w[l],
            diff_k_norm_w[l], fox_q_norm_w[l], fox_k_norm_w[l], b_forget[l])
        lam_params = [p[l].reshape(1, -1) for p in
                      (lambda_q1, lambda_k1, lambda_q2, lambda_k2, diff_subln_w)]
        od = _attention(dq, dk, dvt, batch, lam_params, fox=False, lam_init=lam_init)
        of = _attention(fq, fk, fvt, batch, [qaug, kaug], fox=True)
        x1, mq = _out_proj(xc, od, of, w_out[l], norm_mem_q_w[l], w_mem_q[l], mem_q_norm_w[l])
        mk, mv = _mem_kv(mem2d, norm_mem_kv_w[l], w_mem_kv[l], mem_k_norm_w[l])
        x2 = _mem_attn(x1, mq, mk, mv, w_mem_o[l], batch)
        xc = _mlp(x2, norm_mlp_w[l], w_up[l], w_down[l])
    return xc.reshape(batch, seq, d)
```

```python
import functools
import math

import jax
import jax.numpy as jnp
from jax import lax
from jax.experimental import pallas as pl
from jax.experimental.pallas import tpu as pltpu

F32 = jnp.float32
BF16 = jnp.bfloat16

D_MODEL = 1024
CHUNK = 64
HEAD_DIM = 64
N_DIFF_HEADS = 4
DIFF_V_DIM = 2 * HEAD_DIM
N_FOX_HEADS = 8
GROUP_W = 512
ROPE_DIM = HEAD_DIM // 4
ROPE_THETA = 500000.0
N_MEM_HEADS = 4
MEM_HEAD_DIM = D_MODEL // N_MEM_HEADS
D_FF = 4 * D_MODEL
EPS = 1e-6
NEG_INF = -1e30
LOG2E = math.log2(math.e)

LANES = 128
MXU_DIM = 256
VMEM_LIMIT = 56 * 1024 * 1024

TOK_TILE = 512
ATT_TILE = 256


def _params(sem, flags=None):
    return pltpu.CompilerParams(dimension_semantics=sem, vmem_limit_bytes=VMEM_LIMIT, flags=flags)


def _rms(x, w):
    ms = jnp.mean(x * x, axis=-1, keepdims=True)
    return (x * lax.rsqrt(ms + EPS)) * w


def _dot(a, b):
    return jnp.dot(a, b, preferred_element_type=F32)


def _dot_nt(a, b):
    return lax.dot_general(a, b, (((1,), (1,)), ((), ())), preferred_element_type=F32)


def _same_group_matrix(n, group):
    r = lax.broadcasted_iota(jnp.int32, (n, n), 0) // group
    c = lax.broadcasted_iota(jnp.int32, (n, n), 1) // group
    return jnp.where(r == c, 1.0, 0.0).astype(BF16)


def _group_rms(t, w, group):
    g = _same_group_matrix(MXU_DIM, group)
    sq = (t * t).astype(BF16)
    parts = [_dot(sq[:, c:c + MXU_DIM], g) for c in range(0, t.shape[1], MXU_DIM)]
    ss = jnp.concatenate(parts, axis=-1)
    return (t * lax.rsqrt(ss * (1.0 / group) + EPS)) * w


def _rope_body(pos_ref, freq_ref, cos_ref, sin_ref, nsin_ref):
    ang = pos_ref[...] * freq_ref[...]
    s = jnp.sin(ang)
    cos_ref[...] = jnp.cos(ang)
    sin_ref[...] = s
    nsin_ref[...] = -s


def _rope_tables(positions):
    n_tok = positions.size
    half = ROPE_DIM // 2
    inv_freq = ROPE_THETA ** (-jnp.arange(0, ROPE_DIM, 2, dtype=F32) / ROPE_DIM)
    rows = n_tok * half // LANES
    pos = jnp.repeat(positions.reshape(-1).astype(F32), half).reshape(rows, LANES)
    freq = jnp.tile(inv_freq, LANES // half).reshape(1, LANES)
    out = jax.ShapeDtypeStruct((rows, LANES), F32)
    cos, sin, nsin = pl.pallas_call(
        _rope_body, out_shape=(out, out, out), name="rope_tables")(pos, freq)
    cos, sin, nsin = (t.reshape(n_tok, half) for t in (cos, sin, nsin))
    rest = HEAD_DIM - ROPE_DIM
    ctab = jnp.concatenate([cos, cos, jnp.ones((n_tok, rest), F32)], axis=-1)
    stab = jnp.concatenate([nsin, sin, jnp.zeros((n_tok, rest), F32)], axis=-1)
    reps = LANES // HEAD_DIM
    return jnp.tile(ctab, (1, reps)), jnp.tile(stab, (1, reps))


def _rope(t, ctab, stab):
    half = ROPE_DIM // 2
    lane = lax.broadcasted_iota(jnp.int32, (1, LANES), 1) % HEAD_DIM
    first_half = lane < half
    outs = []
    for c in range(0, t.shape[1], LANES):
        tb = t[:, c:c + LANES]
        from_above = pltpu.roll(tb, LANES - half, axis=1)
        from_below = pltpu.roll(tb, half, axis=1)
        partner = jnp.where(first_half, from_above, from_below)
        outs.append(tb * ctab + partner * stab)
    return jnp.concatenate(outs, axis=-1)


def _log_sigmoid(x):
    return jnp.minimum(x, 0.0) - jnp.log1p(jnp.exp(-jnp.abs(x)))


def _split3(x):
    hi = x.astype(BF16)
    rem = x - hi.astype(F32)
    mid = rem.astype(BF16)
    lo = (rem - mid.astype(F32)).astype(BF16)
    return hi, mid, lo


BIAS_LANES = 6 * N_FOX_HEADS


def _in_proj_body(x_ref, nw_ref, w_ref, wvt_ref, ctab_ref, stab_ref, dqw_ref, dkw_ref,
                  fqw_ref, fkw_ref, bf_ref, dq_ref, dk_ref, dvt_ref, fq_ref, fk_ref,
                  fvt_ref, qaug_ref, kaug_ref, carry_ref):
    @pl.when(pl.program_id(1) == 0)
    def _():
        carry_ref[...] = jnp.zeros_like(carry_ref)

    h = _rms(x_ref[...], nw_ref[...]).astype(BF16)
    ctab = ctab_ref[...]
    stab = stab_ref[...]

    def proj(group, width=GROUP_W):
        c0 = group * GROUP_W
        return _dot(h, w_ref[:, c0:c0 + width])

    dq_ref[...] = _rope(_group_rms(proj(0), dqw_ref[...], HEAD_DIM), ctab, stab).astype(BF16)
    dk_ref[...] = _rope(_group_rms(proj(1), dkw_ref[...], HEAD_DIM), ctab, stab).astype(BF16)
    fq_ref[...] = _group_rms(proj(2), fqw_ref[...], HEAD_DIM).astype(BF16)
    fk_ref[...] = _group_rms(proj(3), fkw_ref[...], HEAD_DIM).astype(BF16)

    for vt_ref, r0 in ((dvt_ref, 0), (fvt_ref, GROUP_W)):
        vt = _dot_nt(wvt_ref[r0:r0 + GROUP_W, :], h).astype(BF16)
        for t in range(TOK_TILE // ATT_TILE):
            vt_ref[t] = vt[:, t * ATT_TILE:(t + 1) * ATT_TILE]

    log_f = _log_sigmoid(proj(4, LANES) + bf_ref[...])
    r = lax.broadcasted_iota(jnp.int32, (MXU_DIM, MXU_DIM), 0)
    c = lax.broadcasted_iota(jnp.int32, (MXU_DIM, MXU_DIM), 1)
    tril = jnp.where(c <= r, 1.0, 0.0).astype(BF16)
    lane = lax.broadcasted_iota(jnp.int32, (MXU_DIM, LANES), 1)
    carry = carry_ref[0:1, :]
    for r0 in range(0, TOK_TILE, MXU_DIM):
        hi, mid, lo = _split3(log_f[r0:r0 + MXU_DIM])
        cum = (_dot(tril, hi) + _dot(tril, mid)) + _dot(tril, lo) + carry
        carry = cum[MXU_DIM - 1:MXU_DIM, :]
        hi, mid, lo = (t.astype(F32) for t in _split3(cum * LOG2E))
        kaug_ref[r0:r0 + MXU_DIM, :] = jnp.where(
            lane < 8, -hi, jnp.where(lane < 16, -mid, jnp.where(
                lane < 24, -lo, jnp.where(lane < BIAS_LANES, 1.0, 0.0)))).astype(BF16)
        qaug_ref[r0:r0 + MXU_DIM, :] = jnp.where(
            lane < 24, 1.0, jnp.where(lane < 32, hi, jnp.where(
                lane < 40, mid, jnp.where(lane < BIAS_LANES, lo, 0.0)))).astype(BF16)
    carry_ref[0:1, :] = carry


def _in_proj(x2d, batch, norm_w, w_in, ctab, stab, dqw, dkw, fqw, fkw, b_forget):
    n_tok = x2d.shape[0]
    tiles = n_tok // batch // TOK_TILE
    g = GROUP_W
    dq, dk, dv, fq, fk, fv, fl = (w_in[:, s:e] for s, e in
                                   ((0, g), (g, 2 * g), (2 * g, 3 * g), (3 * g, 4 * g),
                                    (4 * g, 5 * g), (5 * g, 6 * g), (6 * g, 6 * g + N_FOX_HEADS)))
    reps = BIAS_LANES // N_FOX_HEADS
    fl = jnp.pad(jnp.tile(fl, (1, reps)), ((0, 0), (0, LANES - BIAS_LANES)))
    w = jnp.concatenate([dq, dk, fq, fk, fl], axis=1).astype(BF16)
    wvt = jnp.concatenate([dv, fv], axis=1).T.astype(BF16)
    bf = jnp.pad(jnp.tile(b_forget, reps), (0, LANES - BIAS_LANES)).reshape(1, LANES)

    def tile_w(v, scale=1.0):
        return jnp.tile(v * scale, GROUP_W // HEAD_DIM).reshape(1, GROUP_W)

    row = lambda b, i: (b * tiles + i, 0)
    fixed = lambda b, i: (0, 0)
    tok = lambda width: pl.BlockSpec((TOK_TILE, width), row)
    vec = lambda width: pl.BlockSpec((1, width), fixed)
    slabs = TOK_TILE // ATT_TILE
    vt_spec = pl.BlockSpec((slabs, GROUP_W, ATT_TILE), lambda b, i: (b * tiles + i, 0, 0))
    qk = jax.ShapeDtypeStruct((n_tok, GROUP_W), BF16)
    vt = jax.ShapeDtypeStruct((n_tok // ATT_TILE, GROUP_W, ATT_TILE), BF16)
    aug = jax.ShapeDtypeStruct((n_tok, LANES), BF16)
    qscale = HEAD_DIM ** -0.5 * LOG2E
    return pl.pallas_call(
        _in_proj_body,
        grid=(batch, tiles),
        in_specs=[tok(D_MODEL), vec(D_MODEL), pl.BlockSpec(w.shape, fixed),
                  pl.BlockSpec(wvt.shape, fixed), tok(LANES), tok(LANES), vec(GROUP_W),
                  vec(GROUP_W), vec(GROUP_W), vec(GROUP_W), vec(LANES)],
        out_specs=[tok(g), tok(g), vt_spec, tok(g), tok(g), vt_spec, tok(LANES), tok(LANES)],
        out_shape=[qk, qk, vt, qk, qk, vt, aug, aug],
        scratch_shapes=[pltpu.VMEM((8, LANES), F32)],
        compiler_params=_params(("arbitrary", "arbitrary")),
        name="in_proj",
    )(x2d, norm_w.reshape(1, -1), w, wvt, ctab, stab, tile_w(dqw, qscale), tile_w(dkw),
      tile_w(fqw, qscale), tile_w(fkw), bf)


N_CHAINS = 2 * (GROUP_W // LANES)
SUM_ROWS = 16


def _attn_body(q_ref, k_ref, vt_ref, *rest, fox, lam_init):
    if fox:
        qaug_ref, kaug_ref, o_ref, rhs_scr, s0_scr, s1_scr, m_scr, acc_scr = rest
    else:
        (lq1_ref, lk1_ref, lq2_ref, lk2_ref, subw_ref, o_ref, rhs_scr, s0_scr, s1_scr, m_scr,
         acc_scr) = rest
    s_scr = (s0_scr, s1_scr)
    i = pl.program_id(1)
    width = acc_scr.shape[1] - SUM_ROWS
    ones_rows = jnp.ones((SUM_ROWS, ATT_TILE), BF16)

    half_zero = jnp.zeros((HEAD_DIM, ATT_TILE), BF16)
    if fox:
        aug_t = qaug_ref[...].astype(F32).T
        aug_row = lax.broadcasted_iota(jnp.int32, aug_t.shape, 0)
    for g in range(GROUP_W // LANES):
        q_t = q_ref[:, g * LANES:(g + 1) * LANES].astype(F32).T.astype(BF16)
        rhs_scr[2 * g, :HEAD_DIM, :] = q_t[:HEAD_DIM]
        rhs_scr[2 * g, HEAD_DIM:LANES, :] = half_zero
        rhs_scr[2 * g + 1, :HEAD_DIM, :] = half_zero
        rhs_scr[2 * g + 1, HEAD_DIM:LANES, :] = q_t[HEAD_DIM:]
        if fox:
            for c in (2 * g, 2 * g + 1):
                rhs_scr[c, LANES:, :] = jnp.where(
                    aug_row % N_FOX_HEADS == c, aug_t, 0.0).astype(BF16)
    m_scr[...] = jnp.full(m_scr.shape, NEG_INF, F32)
    acc_scr[...] = jnp.zeros(acc_scr.shape, F32)

    def scores(j, slot, g):
        start = pl.multiple_of(j * ATT_TILE, ATT_TILE)
        lhs = k_ref[pl.ds(start, ATT_TILE), g * LANES:(g + 1) * LANES]
        if fox:
            lhs = jnp.concatenate([lhs, kaug_ref[pl.ds(start, ATT_TILE), :]], axis=-1)
        for c in (2 * g, 2 * g + 1):
            s_scr[slot][c] = _dot(lhs, rhs_scr[c])

    def softmax_pv(j, slot, c, mask):
        s = s_scr[slot][c]
        if mask is not None:
            s = jnp.where(mask, s, NEG_INF)
        m_old = m_scr[c:c + 1, :]
        m_new = jnp.maximum(m_old, jnp.max(s, axis=0, keepdims=True))
        alpha = jnp.exp2(m_old - m_new)
        p = jnp.exp2(s - m_new)
        m_scr[c:c + 1, :] = m_new
        r0 = c * HEAD_DIM if fox else (c // 2) * LANES
        vt = jnp.concatenate([vt_ref[j, r0:r0 + width, :], ones_rows], axis=0)
        acc_scr[c] = alpha * acc_scr[c] + _dot(vt, p.astype(BF16))

    def step(score_args, soft_args):
        for g in range(GROUP_W // LANES):
            if score_args is not None:
                scores(*score_args, g)
            if soft_args is not None:
                j, slot, mask = soft_args
                softmax_pv(j, slot, 2 * g, mask)
                softmax_pv(j, slot, 2 * g + 1, mask)

    def pair(u, carry):
        t = 2 * u
        step((t + 1, 1), (t, 0, None))
        step((t + 2, 0), (t + 1, 1, None))
        return carry

    step((0, 0), None)
    lax.fori_loop(0, i // 2, pair, 0)
    div = 1 if fox else CHUNK
    k_id = lax.broadcasted_iota(jnp.int32, (ATT_TILE, ATT_TILE), 0) // div
    q_id = lax.broadcasted_iota(jnp.int32, (ATT_TILE, ATT_TILE), 1) // div
    mask = k_id <= q_id

    @pl.when(i % 2 == 0)
    def _():
        step(None, (i, 0, mask))

    @pl.when(i % 2 == 1)
    def _():
        step((i, 1), (i - 1, 0, None))
        step(None, (i, 1, mask))

    if not fox:
        lam = (jnp.exp(jnp.sum(lq1_ref[...] * lk1_ref[...], axis=-1, keepdims=True))
               - jnp.exp(jnp.sum(lq2_ref[...] * lk2_ref[...], axis=-1, keepdims=True))
               + lam_init)
    for g in range(GROUP_W // LANES):
        o1 = acc_scr[2 * g, :width, :] / acc_scr[2 * g, width:width + 1, :]
        o2 = acc_scr[2 * g + 1, :width, :] / acc_scr[2 * g + 1, width:width + 1, :]
        if fox:
            o = jnp.concatenate([o1, o2], axis=0).T
        else:
            o = _rms((o1 - lam * o2).T, subw_ref[...]) * (1.0 - lam_init)
        o_ref[:, g * LANES:(g + 1) * LANES] = o.astype(BF16)


def _attention(q, k, vt, batch, extra, *, fox, lam_init=0.0):
    n_tok = q.shape[0]
    seq = n_tok // batch
    tiles = seq // ATT_TILE
    width = HEAD_DIM if fox else DIFF_V_DIM
    q_spec = lambda w: pl.BlockSpec((ATT_TILE, w), lambda b, i: (b * tiles + i, 0))
    kv_spec = lambda w: pl.BlockSpec((seq, w), lambda b, i: (b, 0))
    vt_spec = pl.BlockSpec((tiles, GROUP_W, ATT_TILE), lambda b, i: (b, 0, 0))
    if fox:
        extra_specs = [q_spec(LANES), kv_spec(LANES)]
    else:
        extra_specs = [pl.BlockSpec(e.shape, lambda b, i: (0, 0)) for e in extra]
    rhs_w = 2 * LANES if fox else LANES
    return pl.pallas_call(
        functools.partial(_attn_body, fox=fox, lam_init=lam_init),
        grid=(batch, tiles),
        in_specs=[q_spec(GROUP_W), kv_spec(GROUP_W), vt_spec] + extra_specs,
        out_specs=q_spec(GROUP_W),
        out_shape=jax.ShapeDtypeStruct((n_tok, GROUP_W), BF16),
        scratch_shapes=[pltpu.VMEM((N_CHAINS, rhs_w, ATT_TILE), BF16),
                        pltpu.VMEM((N_CHAINS, ATT_TILE, ATT_TILE), F32),
                        pltpu.VMEM((N_CHAINS, ATT_TILE, ATT_TILE), F32),
                        pltpu.VMEM((N_CHAINS, ATT_TILE), F32),
                        pltpu.VMEM((N_CHAINS, width + SUM_ROWS, ATT_TILE), F32)],
        compiler_params=_params(("arbitrary", "arbitrary")),
        name="fox_attn" if fox else "diff_attn",
    )(q, k, vt, *extra)


def _out_proj_body(x_ref, od_ref, of_ref, wo_ref, nw_ref, wq_ref, qw_ref, x1_ref, mq_ref):
    x1 = x_ref[...] + (_dot(od_ref[...], wo_ref[:GROUP_W, :])
                       + _dot(of_ref[...], wo_ref[GROUP_W:, :]))
    x1_ref[...] = x1
    hq = _rms(x1, nw_ref[...]).astype(BF16)
    mq = _dot(hq, wq_ref[...])
    qw = qw_ref[...]
    for c in range(0, D_MODEL, MEM_HEAD_DIM):
        mq_ref[:, c:c + MEM_HEAD_DIM] = _rms(mq[:, c:c + MEM_HEAD_DIM], qw).astype(BF16)


def _out_proj(x2d, od, of, w_out, norm_w, w_mem_q, mem_q_norm_w):
    n_tok = x2d.shape[0]
    tok = lambda width: pl.BlockSpec((TOK_TILE, width), lambda i: (i, 0))
    fixed = lambda r, c: pl.BlockSpec((r, c), lambda i: (0, 0))
    scale = MEM_HEAD_DIM ** -0.5
    return pl.pallas_call(
        _out_proj_body,
        grid=(n_tok // TOK_TILE,),
        in_specs=[tok(D_MODEL), tok(GROUP_W), tok(GROUP_W), fixed(D_MODEL, D_MODEL),
                  fixed(1, D_MODEL), fixed(D_MODEL, D_MODEL), fixed(1, MEM_HEAD_DIM)],
        out_specs=[tok(D_MODEL), tok(D_MODEL)],
        out_shape=[jax.ShapeDtypeStruct((n_tok, D_MODEL), F32),
                   jax.ShapeDtypeStruct((n_tok, D_MODEL), BF16)],
        compiler_params=_params(("arbitrary",)),
        name="out_proj",
    )(x2d, od, of, w_out.astype(BF16), norm_w.reshape(1, -1), w_mem_q.astype(BF16),
      (mem_q_norm_w * scale).reshape(1, -1))


def _mem_kv_body(m_ref, nw_ref, w_ref, kw_ref, mk_ref, mv_ref):
    hm = _rms(m_ref[...], nw_ref[...]).astype(BF16)
    kv = _dot(hm, w_ref[...])
    kw = kw_ref[...]
    for c in range(0, D_MODEL, MEM_HEAD_DIM):
        mk_ref[:, c:c + MEM_HEAD_DIM] = _rms(kv[:, c:c + MEM_HEAD_DIM], kw).astype(BF16)
    mv_ref[...] = kv[:, D_MODEL:].astype(BF16)


def _mem_kv(mem2d, norm_w, w_mem_kv, mem_k_norm_w):
    n_mem = mem2d.shape[0]
    tok = pl.BlockSpec((TOK_TILE, D_MODEL), lambda i: (i, 0))
    fixed = lambda r, c: pl.BlockSpec((r, c), lambda i: (0, 0))
    out = jax.ShapeDtypeStruct((n_mem, D_MODEL), BF16)
    return pl.pallas_call(
        _mem_kv_body,
        grid=(n_mem // TOK_TILE,),
        in_specs=[tok, fixed(1, D_MODEL), fixed(D_MODEL, 2 * D_MODEL), fixed(1, MEM_HEAD_DIM)],
        out_specs=[tok, tok],
        out_shape=[out, out],
        compiler_params=_params(("arbitrary",)),
        name="mem_kv",
    )(mem2d, norm_w.reshape(1, -1), w_mem_kv.astype(BF16), mem_k_norm_w.reshape(1, -1))


def _mem_attn_body(x1_ref, mq_ref, mk_ref, mv_ref, wo_ref, x2_ref):
    mq = mq_ref[...]
    heads = []
    for c in range(0, D_MODEL, MEM_HEAD_DIM):
        s = _dot_nt(mq[:, c:c + MEM_HEAD_DIM], mk_ref[:, c:c + MEM_HEAD_DIM])
        p = jnp.exp(s - jnp.max(s, axis=-1, keepdims=True))
        l = jnp.sum(p, axis=-1, keepdims=True)
        heads.append((_dot(p.astype(BF16), mv_ref[:, c:c + MEM_HEAD_DIM]) / l).astype(BF16))
    mo = jnp.concatenate(heads, axis=-1)
    x2_ref[...] = x1_ref[...] + _dot(mo, wo_ref[...])


def _mem_attn(x1, mq, mk, mv, w_mem_o, batch):
    n_tok = x1.shape[0]
    tiles = n_tok // batch // TOK_TILE
    mem_len = mk.shape[0] // batch
    tok = pl.BlockSpec((TOK_TILE, D_MODEL), lambda b, i: (b * tiles + i, 0))
    mem = pl.BlockSpec((mem_len, D_MODEL), lambda b, i: (b, 0))
    return pl.pallas_call(
        _mem_attn_body,
        grid=(batch, tiles),
        in_specs=[tok, tok, mem, mem, pl.BlockSpec((D_MODEL, D_MODEL), lambda b, i: (0, 0))],
        out_specs=tok,
        out_shape=jax.ShapeDtypeStruct((n_tok, D_MODEL), F32),
        compiler_params=_params(("arbitrary", "arbitrary")),
        name="mem_attn",
    )(x1, mq, mk, mv, w_mem_o.astype(BF16))


FF_CHUNK = 1024


def _mlp_body(x_ref, nw_ref, wu_ref, wd_ref, o_ref):
    x = x_ref[...]
    h = _rms(x, nw_ref[...]).astype(BF16)
    acc = x
    for c in range(0, D_FF, FF_CHUNK):
        u = jnp.maximum(_dot(h, wu_ref[:, c:c + FF_CHUNK]), 0.0)
        acc = acc + _dot((u * u).astype(BF16), wd_ref[c:c + FF_CHUNK, :])
    o_ref[...] = acc


def _mlp(x2, norm_w, w_up, w_down):
    n_tok = x2.shape[0]
    tok = pl.BlockSpec((TOK_TILE, D_MODEL), lambda i: (i, 0))
    fixed = lambda r, c: pl.BlockSpec((r, c), lambda i: (0, 0))
    return pl.pallas_call(
        _mlp_body,
        grid=(n_tok // TOK_TILE,),
        in_specs=[tok, fixed(1, D_MODEL), fixed(D_MODEL, D_FF), fixed(D_FF, D_MODEL)],
        out_specs=tok,
        out_shape=jax.ShapeDtypeStruct((n_tok, D_MODEL), F32),
        compiler_params=_params(("arbitrary",)),
        name="mlp",
    )(x2, norm_w.reshape(1, -1), w_up.astype(BF16), w_down.astype(BF16))


def kernel(x, mem, positions, norm_mix_w, w_in, b_forget, diff_q_norm_w, diff_k_norm_w,
           lambda_q1, lambda_k1, lambda_q2, lambda_k2, diff_subln_w, fox_q_norm_w,
           fox_k_norm_w, w_out, norm_mem_q_w, norm_mem_kv_w, w_mem_q, w_mem_kv,
           mem_q_norm_w, mem_k_norm_w, w_mem_o, norm_mlp_w, w_up, w_down):
    batch, seq, d = x.shape
    depth = w_in.shape[0]
    assert d == D_MODEL and seq % TOK_TILE == 0 and seq % ATT_TILE == 0
    assert (batch * mem.shape[1]) % TOK_TILE == 0

    ctab, stab = _rope_tables(positions)
    xc = x.reshape(batch * seq, d)
    mem2d = mem.reshape(-1, d)
    for l in range(depth):
        lam_init = 0.8 - 0.6 * math.exp(-0.3 * l)
        dq, dk, dvt, fq, fk, fvt, qaug, kaug = _in_proj(
            xc, batch, norm_mix_w[l], w_in[l], ctab, stab, diff_q_norm_w[l],
            diff_k_norm_w[l], fox_q_norm_w[l], fox_k_norm_w[l], b_forget[l])
        lam_params = [p[l].reshape(1, -1) for p in
                      (lambda_q1, lambda_k1, lambda_q2, lambda_k2, diff_subln_w)]
        od = _attention(dq, dk, dvt, batch, lam_params, fox=False, lam_init=lam_init)
        of = _attention(fq, fk, fvt, batch, [qaug, kaug], fox=True)
        x1, mq = _out_proj(xc, od, of, w_out[l], norm_mem_q_w[l], w_mem_q[l], mem_q_norm_w[l])
        mk, mv = _mem_kv(mem2d, norm_mem_kv_w[l], w_mem_kv[l], mem_k_norm_w[l])
        x2 = _mem_attn(x1, mq, mk, mv, w_mem_o[l], batch)
        xc = _mlp(x2, norm_mlp_w[l], w_up[l], w_down[l])
    return xc.reshape(batch, seq, d)
```

```python
import functools
import math

import jax
import jax.numpy as jnp
from jax import lax
from jax.experimental import pallas as pl
from jax.experimental.pallas import tpu as pltpu

F32 = jnp.float32
BF16 = jnp.bfloat16

D_MODEL = 1024
CHUNK = 64
HEAD_DIM = 64
N_DIFF_HEADS = 4
DIFF_V_DIM = 2 * HEAD_DIM
N_FOX_HEADS = 8
GROUP_W = 512
ROPE_DIM = HEAD_DIM // 4
ROPE_THETA = 500000.0
N_MEM_HEADS = 4
MEM_HEAD_DIM = D_MODEL // N_MEM_HEADS
D_FF = 4 * D_MODEL
EPS = 1e-6
NEG_INF = -1e30
LOG2E = math.log2(math.e)

LANES = 128
MXU_DIM = 256
VMEM_LIMIT = 56 * 1024 * 1024

TOK_TILE = 512
ATT_TILE = 256


def _params(sem):
    return pltpu.CompilerParams(dimension_semantics=sem, vmem_limit_bytes=VMEM_LIMIT)


def _rms(x, w):
    ms = jnp.mean(x * x, axis=-1, keepdims=True)
    return (x * lax.rsqrt(ms + EPS)) * w


def _dot(a, b):
    return jnp.dot(a, b, preferred_element_type=F32)


def _dot_nt(a, b):
    return lax.dot_general(a, b, (((1,), (1,)), ((), ())), preferred_element_type=F32)


def _rope_body(pos_ref, freq_ref, cos_ref, sin_ref):
    ang = pos_ref[...] * freq_ref[...]
    cos_ref[...] = jnp.cos(ang)
    sin_ref[...] = jnp.sin(ang)


def _rope_tables(positions):
    n_tok = positions.size
    half = ROPE_DIM // 2
    inv_freq = ROPE_THETA ** (-jnp.arange(0, ROPE_DIM, 2, dtype=F32) / ROPE_DIM)
    out = jax.ShapeDtypeStruct((half, n_tok), F32)
    return pl.pallas_call(_rope_body, out_shape=(out, out), name="rope_tables")(
        positions.reshape(1, n_tok).astype(F32), inv_freq.reshape(half, 1))


def _log_sigmoid(x):
    return jnp.minimum(x, 0.0) - jnp.log1p(jnp.exp(-jnp.abs(x)))


def _split3(x):
    hi = x.astype(BF16)
    rem = x - hi.astype(F32)
    mid = rem.astype(BF16)
    lo = (rem - mid.astype(F32)).astype(BF16)
    return hi, mid, lo


BIAS_ROWS = 6 * N_FOX_HEADS


def _head_norm_t(t, gain, rope):
    half = ROPE_DIM // 2
    outs = []
    for r0 in range(0, t.shape[0], HEAD_DIM):
        th = t[r0:r0 + HEAD_DIM]
        ms = jnp.sum(th * th, axis=0, keepdims=True) * (1.0 / HEAD_DIM)
        th = (th * lax.rsqrt(ms + EPS)) * gain[r0:r0 + HEAD_DIM]
        if rope is not None:
            cos, sin = rope
            t1, t2 = th[:half], th[half:ROPE_DIM]
            th = jnp.concatenate(
                [t1 * cos - t2 * sin, t2 * cos + t1 * sin, th[ROPE_DIM:]], axis=0)
        outs.append(th)
    return jnp.concatenate(outs, axis=0)


def _in_proj_body(x_ref, nw_ref, wt_ref, cos_ref, sin_ref, gain_ref, bf_ref, dqt_ref, dk_ref,
                  dvt_ref, fqt_ref, fk_ref, fvt_ref, qaugt_ref, kaug_ref, carry_ref):
    @pl.when(pl.program_id(1) == 0)
    def _():
        carry_ref[...] = jnp.zeros_like(carry_ref)

    h = _rms(x_ref[...], nw_ref[...]).astype(BF16)
    rope = (cos_ref[...], sin_ref[...])
    g = GROUP_W

    def proj_t(group, rows=GROUP_W):
        r0 = group * GROUP_W
        return _dot_nt(wt_ref[r0:r0 + rows, :], h)

    def store_keys(k_ref, kt):
        for c in range(0, g, LANES):
            k_ref[:, c:c + LANES] = kt[c:c + LANES].T.astype(BF16)

    def store_values(vt_ref, vt):
        for t in range(TOK_TILE // ATT_TILE):
            vt_ref[t] = vt[:, t * ATT_TILE:(t + 1) * ATT_TILE].astype(BF16)

    dqt_ref[...] = _head_norm_t(proj_t(0), gain_ref[0:g], rope).astype(BF16)
    store_keys(dk_ref, _head_norm_t(proj_t(1), gain_ref[g:2 * g], rope))
    store_values(dvt_ref, proj_t(2))
    fqt_ref[...] = _head_norm_t(proj_t(3), gain_ref[2 * g:3 * g], None).astype(BF16)
    store_keys(fk_ref, _head_norm_t(proj_t(4), gain_ref[3 * g:4 * g], None))
    store_values(fvt_ref, proj_t(5))

    log_f = _log_sigmoid(proj_t(6, LANES) + bf_ref[...])
    r = lax.broadcasted_iota(jnp.int32, (MXU_DIM, MXU_DIM), 0)
    c = lax.broadcasted_iota(jnp.int32, (MXU_DIM, MXU_DIM), 1)
    triu = jnp.where(r <= c, 1.0, 0.0).astype(BF16)
    row = lax.broadcasted_iota(jnp.int32, (LANES, MXU_DIM), 0)
    carry = carry_ref[:, 0:1]
    for c0 in range(0, TOK_TILE, MXU_DIM):
        hi, mid, lo = _split3(log_f[:, c0:c0 + MXU_DIM])
        cum = (_dot(hi, triu) + _dot(mid, triu)) + _dot(lo, triu) + carry
        carry = cum[:, MXU_DIM - 1:MXU_DIM]
        hi, mid, lo = (t.astype(F32) for t in _split3(cum * LOG2E))
        kaug_t = jnp.where(row < 8, -hi, jnp.where(row < 16, -mid, jnp.where(
            row < 24, -lo, jnp.where(row < BIAS_ROWS, 1.0, 0.0))))
        qaug_t = jnp.where(row < 24, 1.0, jnp.where(row < 32, hi, jnp.where(
            row < 40, mid, jnp.where(row < BIAS_ROWS, lo, 0.0))))
        kaug_ref[c0:c0 + MXU_DIM, :] = kaug_t.T.astype(BF16)
        qaugt_ref[:, c0:c0 + MXU_DIM] = qaug_t.astype(BF16)
    carry_ref[...] = jnp.broadcast_to(carry, carry_ref.shape)


def _in_proj(x2d, batch, norm_w, w_in, cos_t, sin_t, dqw, dkw, fqw, fkw, b_forget):
    n_tok = x2d.shape[0]
    tiles = n_tok // batch // TOK_TILE
    g = GROUP_W
    reps = BIAS_ROWS // N_FOX_HEADS
    fl = jnp.pad(jnp.tile(w_in[:, 6 * g:6 * g + N_FOX_HEADS], (1, reps)),
                 ((0, 0), (0, LANES - BIAS_ROWS)))
    wt = jnp.concatenate([w_in[:, :6 * g], fl], axis=1).T.astype(BF16)
    bf = jnp.pad(jnp.tile(b_forget, reps), (0, LANES - BIAS_ROWS)).reshape(LANES, 1)
    qscale = HEAD_DIM ** -0.5 * LOG2E
    gains = jnp.concatenate([jnp.tile(v, g // HEAD_DIM) for v in
                             (dqw * qscale, dkw, fqw * qscale, fkw)]).reshape(4 * g, 1)

    row = lambda b, i: (b * tiles + i, 0)
    col = lambda b, i: (0, b * tiles + i)
    fixed = lambda b, i: (0, 0)
    tok = lambda width: pl.BlockSpec((TOK_TILE, width), row)
    tok_t = lambda rows: pl.BlockSpec((rows, TOK_TILE), col)
    slabs = TOK_TILE // ATT_TILE
    vt_spec = pl.BlockSpec((slabs, g, ATT_TILE), lambda b, i: (b * tiles + i, 0, 0))
    k_shape = jax.ShapeDtypeStruct((n_tok, g), BF16)
    qt_shape = jax.ShapeDtypeStruct((g, n_tok), BF16)
    vt_shape = jax.ShapeDtypeStruct((n_tok // ATT_TILE, g, ATT_TILE), BF16)
    half = ROPE_DIM // 2
    return pl.pallas_call(
        _in_proj_body,
        grid=(batch, tiles),
        in_specs=[tok(D_MODEL), pl.BlockSpec((1, D_MODEL), fixed), pl.BlockSpec(wt.shape, fixed),
                  tok_t(half), tok_t(half), pl.BlockSpec(gains.shape, fixed),
                  pl.BlockSpec(bf.shape, fixed)],
        out_specs=[tok_t(g), tok(g), vt_spec, tok_t(g), tok(g), vt_spec, tok_t(LANES),
                   tok(LANES)],
        out_shape=[qt_shape, k_shape, vt_shape, qt_shape, k_shape, vt_shape,
                   jax.ShapeDtypeStruct((LANES, n_tok), BF16),
                   jax.ShapeDtypeStruct((n_tok, LANES), BF16)],
        scratch_shapes=[pltpu.VMEM((LANES, LANES), F32)],
        compiler_params=_params(("arbitrary", "arbitrary")),
        name="in_proj",
    )(x2d, norm_w.reshape(1, -1), wt, cos_t, sin_t, gains, bf)


N_CHAINS = 2 * (GROUP_W // LANES)
SUM_ROWS = 16


def _attn_body(qt_ref, k_ref, vt_ref, *rest, fox, lam_init):
    if fox:
        qaugt_ref, kaug_ref, o_ref, rhs_scr, s0_scr, s1_scr, m_scr, acc_scr = rest
    else:
        (lq1_ref, lk1_ref, lq2_ref, lk2_ref, subw_ref, o_ref, rhs_scr, s0_scr, s1_scr, m_scr,
         acc_scr) = rest
    s_scr = (s0_scr, s1_scr)
    i = pl.program_id(1)
    width = acc_scr.shape[1] - SUM_ROWS
    ones_rows = jnp.ones((SUM_ROWS, ATT_TILE), BF16)

    half_zero = jnp.zeros((HEAD_DIM, ATT_TILE), BF16)
    if fox:
        aug_t = qaugt_ref[...].astype(F32)
        aug_row = lax.broadcasted_iota(jnp.int32, aug_t.shape, 0)
    for g in range(GROUP_W // LANES):
        r0 = g * LANES
        rhs_scr[2 * g, :HEAD_DIM, :] = qt_ref[r0:r0 + HEAD_DIM, :]
        rhs_scr[2 * g, HEAD_DIM:LANES, :] = half_zero
        rhs_scr[2 * g + 1, :HEAD_DIM, :] = half_zero
        rhs_scr[2 * g + 1, HEAD_DIM:LANES, :] = qt_ref[r0 + HEAD_DIM:r0 + LANES, :]
        if fox:
            for c in (2 * g, 2 * g + 1):
                rhs_scr[c, LANES:, :] = jnp.where(
                    aug_row % N_FOX_HEADS == c, aug_t, 0.0).astype(BF16)
    m_scr[...] = jnp.full(m_scr.shape, NEG_INF, F32)
    acc_scr[...] = jnp.zeros(acc_scr.shape, F32)

    def scores(j, slot, g):
        start = pl.multiple_of(j * ATT_TILE, ATT_TILE)
        lhs = k_ref[pl.ds(start, ATT_TILE), g * LANES:(g + 1) * LANES]
        if fox:
            lhs = jnp.concatenate([lhs, kaug_ref[pl.ds(start, ATT_TILE), :]], axis=-1)
        for c in (2 * g, 2 * g + 1):
            s_scr[slot][c] = _dot(lhs, rhs_scr[c])

    def softmax_pv(j, slot, c, mask):
        s = s_scr[slot][c]
        if mask is not None:
            s = jnp.where(mask, s, NEG_INF)
        m_old = m_scr[c:c + 1, :]
        m_new = jnp.maximum(m_old, jnp.max(s, axis=0, keepdims=True))
        alpha = jnp.exp2(m_old - m_new)
        p = jnp.exp2(s - m_new)
        m_scr[c:c + 1, :] = m_new
        r0 = c * HEAD_DIM if fox else (c // 2) * LANES
        vt = jnp.concatenate([vt_ref[j, r0:r0 + width, :], ones_rows], axis=0)
        acc_scr[c] = alpha * acc_scr[c] + _dot(vt, p.astype(BF16))

    def step(score_args, soft_args):
        for g in range(GROUP_W // LANES):
            if score_args is not None:
                scores(*score_args, g)
            if soft_args is not None:
                j, slot, mask = soft_args
                softmax_pv(j, slot, 2 * g, mask)
                softmax_pv(j, slot, 2 * g + 1, mask)

    def pair(u, carry):
        t = 2 * u
        step((t + 1, 1), (t, 0, None))
        step((t + 2, 0), (t + 1, 1, None))
        return carry

    step((0, 0), None)
    lax.fori_loop(0, i // 2, pair, 0)
    div = 1 if fox else CHUNK
    k_id = lax.broadcasted_iota(jnp.int32, (ATT_TILE, ATT_TILE), 0) // div
    q_id = lax.broadcasted_iota(jnp.int32, (ATT_TILE, ATT_TILE), 1) // div
    mask = k_id <= q_id

    @pl.when(i % 2 == 0)
    def _():
        step(None, (i, 0, mask))

    @pl.when(i % 2 == 1)
    def _():
        step((i, 1), (i - 1, 0, None))
        step(None, (i, 1, mask))

    if not fox:
        lam = (jnp.exp(jnp.sum(lq1_ref[...] * lk1_ref[...], axis=-1, keepdims=True))
               - jnp.exp(jnp.sum(lq2_ref[...] * lk2_ref[...], axis=-1, keepdims=True))
               + lam_init)
    for g in range(GROUP_W // LANES):
        o1 = acc_scr[2 * g, :width, :] / acc_scr[2 * g, width:width + 1, :]
        o2 = acc_scr[2 * g + 1, :width, :] / acc_scr[2 * g + 1, width:width + 1, :]
        if fox:
            o = jnp.concatenate([o1, o2], axis=0).T
        else:
            o = _rms((o1 - lam * o2).T, subw_ref[...]) * (1.0 - lam_init)
        o_ref[:, g * LANES:(g + 1) * LANES] = o.astype(BF16)


def _attention(qt, k, vt, batch, extra, *, fox, lam_init=0.0):
    n_tok = k.shape[0]
    seq = n_tok // batch
    tiles = seq // ATT_TILE
    width = HEAD_DIM if fox else DIFF_V_DIM
    q_spec = lambda w: pl.BlockSpec((ATT_TILE, w), lambda b, i: (b * tiles + i, 0))
    qt_spec = lambda r: pl.BlockSpec((r, ATT_TILE), lambda b, i: (0, b * tiles + i))
    kv_spec = lambda w: pl.BlockSpec((seq, w), lambda b, i: (b, 0))
    vt_spec = pl.BlockSpec((tiles, GROUP_W, ATT_TILE), lambda b, i: (b, 0, 0))
    if fox:
        extra_specs = [qt_spec(LANES), kv_spec(LANES)]
    else:
        extra_specs = [pl.BlockSpec(e.shape, lambda b, i: (0, 0)) for e in extra]
    rhs_w = 2 * LANES if fox else LANES
    return pl.pallas_call(
        functools.partial(_attn_body, fox=fox, lam_init=lam_init),
        grid=(batch, tiles),
        in_specs=[qt_spec(GROUP_W), kv_spec(GROUP_W), vt_spec] + extra_specs,
        out_specs=q_spec(GROUP_W),
        out_shape=jax.ShapeDtypeStruct((n_tok, GROUP_W), BF16),
        scratch_shapes=[pltpu.VMEM((N_CHAINS, rhs_w, ATT_TILE), BF16),
                        pltpu.VMEM((N_CHAINS, ATT_TILE, ATT_TILE), F32),
                        pltpu.VMEM((N_CHAINS, ATT_TILE, ATT_TILE), F32),
                        pltpu.VMEM((N_CHAINS, ATT_TILE), F32),
                        pltpu.VMEM((N_CHAINS, width + SUM_ROWS, ATT_TILE), F32)],
        compiler_params=_params(("arbitrary", "arbitrary")),
        name="fox_attn" if fox else "diff_attn",
    )(qt, k, vt, *extra)


def _out_proj_body(x_ref, od_ref, of_ref, wo_ref, nw_ref, wq_ref, qw_ref, x1_ref, mq_ref):
    x1 = x_ref[...] + (_dot(od_ref[...], wo_ref[:GROUP_W, :])
                       + _dot(of_ref[...], wo_ref[GROUP_W:, :]))
    x1_ref[...] = x1
    hq = _rms(x1, nw_ref[...]).astype(BF16)
    mq = _dot(hq, wq_ref[...])
    qw = qw_ref[...]
    for c in range(0, D_MODEL, MEM_HEAD_DIM):
        mq_ref[:, c:c + MEM_HEAD_DIM] = _rms(mq[:, c:c + MEM_HEAD_DIM], qw).astype(BF16)


def _out_proj(x2d, od, of, w_out, norm_w, w_mem_q, mem_q_norm_w):
    n_tok = x2d.shape[0]
    tok = lambda width: pl.BlockSpec((TOK_TILE, width), lambda i: (i, 0))
    fixed = lambda r, c: pl.BlockSpec((r, c), lambda i: (0, 0))
    scale = MEM_HEAD_DIM ** -0.5
    return pl.pallas_call(
        _out_proj_body,
        grid=(n_tok // TOK_TILE,),
        in_specs=[tok(D_MODEL), tok(GROUP_W), tok(GROUP_W), fixed(D_MODEL, D_MODEL),
                  fixed(1, D_MODEL), fixed(D_MODEL, D_MODEL), fixed(1, MEM_HEAD_DIM)],
        out_specs=[tok(D_MODEL), tok(D_MODEL)],
        out_shape=[jax.ShapeDtypeStruct((n_tok, D_MODEL), F32),
                   jax.ShapeDtypeStruct((n_tok, D_MODEL), BF16)],
        compiler_params=_params(("arbitrary",)),
        name="out_proj",
    )(x2d, od, of, w_out.astype(BF16), norm_w.reshape(1, -1), w_mem_q.astype(BF16),
      (mem_q_norm_w * scale).reshape(1, -1))


def _mem_kv_body(m_ref, nw_ref, w_ref, kw_ref, mk_ref, mv_ref):
    hm = _rms(m_ref[...], nw_ref[...]).astype(BF16)
    kv = _dot(hm, w_ref[...])
    kw = kw_ref[...]
    for c in range(0, D_MODEL, MEM_HEAD_DIM):
        mk_ref[:, c:c + MEM_HEAD_DIM] = _rms(kv[:, c:c + MEM_HEAD_DIM], kw).astype(BF16)
    mv_ref[...] = kv[:, D_MODEL:].astype(BF16)


def _mem_kv(mem2d, norm_w, w_mem_kv, mem_k_norm_w):
    n_mem = mem2d.shape[0]
    tok = pl.BlockSpec((TOK_TILE, D_MODEL), lambda i: (i, 0))
    fixed = lambda r, c: pl.BlockSpec((r, c), lambda i: (0, 0))
    out = jax.ShapeDtypeStruct((n_mem, D_MODEL), BF16)
    return pl.pallas_call(
        _mem_kv_body,
        grid=(n_mem // TOK_TILE,),
        in_specs=[tok, fixed(1, D_MODEL), fixed(D_MODEL, 2 * D_MODEL), fixed(1, MEM_HEAD_DIM)],
        out_specs=[tok, tok],
        out_shape=[out, out],
        compiler_params=_params(("arbitrary",)),
        name="mem_kv",
    )(mem2d, norm_w.reshape(1, -1), w_mem_kv.astype(BF16), mem_k_norm_w.reshape(1, -1))


def _mem_attn_body(x1_ref, mq_ref, mk_ref, mv_ref, wo_ref, x2_ref):
    mq = mq_ref[...]
    heads = []
    for c in range(0, D_MODEL, MEM_HEAD_DIM):
        s = _dot_nt(mq[:, c:c + MEM_HEAD_DIM], mk_ref[:, c:c + MEM_HEAD_DIM])
        p = jnp.exp(s - jnp.max(s, axis=-1, keepdims=True))
        l = jnp.sum(p, axis=-1, keepdims=True)
        heads.append((_dot(p.astype(BF16), mv_ref[:, c:c + MEM_HEAD_DIM]) / l).astype(BF16))
    mo = jnp.concatenate(heads, axis=-1)
    x2_ref[...] = x1_ref[...] + _dot(mo, wo_ref[...])


def _mem_attn(x1, mq, mk, mv, w_mem_o, batch):
    n_tok = x1.shape[0]
    tiles = n_tok // batch // TOK_TILE
    mem_len = mk.shape[0] // batch
    tok = pl.BlockSpec((TOK_TILE, D_MODEL), lambda b, i: (b * tiles + i, 0))
    mem = pl.BlockSpec((mem_len, D_MODEL), lambda b, i: (b, 0))
    return pl.pallas_call(
        _mem_attn_body,
        grid=(batch, tiles),
        in_specs=[tok, tok, mem, mem, pl.BlockSpec((D_MODEL, D_MODEL), lambda b, i: (0, 0))],
        out_specs=tok,
        out_shape=jax.ShapeDtypeStruct((n_tok, D_MODEL), F32),
        compiler_params=_params(("arbitrary", "arbitrary")),
        name="mem_attn",
    )(x1, mq, mk, mv, w_mem_o.astype(BF16))


FF_CHUNK = 1024


def _mlp_body(x_ref, nw_ref, wu_ref, wd_ref, o_ref):
    x = x_ref[...]
    h = _rms(x, nw_ref[...]).astype(BF16)
    acc = x
    for c in range(0, D_FF, FF_CHUNK):
        u = jnp.maximum(_dot(h, wu_ref[:, c:c + FF_CHUNK]), 0.0)
        acc = acc + _dot((u * u).astype(BF16), wd_ref[c:c + FF_CHUNK, :])
    o_ref[...] = acc


def _mlp(x2, norm_w, w_up, w_down):
    n_tok = x2.shape[0]
    tok = pl.BlockSpec((TOK_TILE, D_MODEL), lambda i: (i, 0))
    fixed = lambda r, c: pl.BlockSpec((r, c), lambda i: (0, 0))
    return pl.pallas_call(
        _mlp_body,
        grid=(n_tok // TOK_TILE,),
        in_specs=[tok, fixed(1, D_MODEL), fixed(D_MODEL, D_FF), fixed(D_FF, D_MODEL)],
        out_specs=tok,
        out_shape=jax.ShapeDtypeStruct((n_tok, D_MODEL), F32),
        compiler_params=_params(("arbitrary",)),
        name="mlp",
    )(x2, norm_w.reshape(1, -1), w_up.astype(BF16), w_down.astype(BF16))


def kernel(x, mem, positions, norm_mix_w, w_in, b_forget, diff_q_norm_w, diff_k_norm_w,
           lambda_q1, lambda_k1, lambda_q2, lambda_k2, diff_subln_w, fox_q_norm_w,
           fox_k_norm_w, w_out, norm_mem_q_w, norm_mem_kv_w, w_mem_q, w_mem_kv,
           mem_q_norm_w, mem_k_norm_w, w_mem_o, norm_mlp_w, w_up, w_down):
    batch, seq, d = x.shape
    depth = w_in.shape[0]
    assert d == D_MODEL and seq % TOK_TILE == 0 and seq % ATT_TILE == 0
    assert (batch * mem.shape[1]) % TOK_TILE == 0

    cos_t, sin_t = _rope_tables(positions)
    xc = x.reshape(batch * seq, d)
    mem2d = mem.reshape(-1, d)
    for l in range(depth):
        lam_init = 0.8 - 0.6 * math.exp(-0.3 * l)
        dqt, dk, dvt, fqt, fk, fvt, qaugt, kaug = _in_proj(
            xc, batch, norm_mix_w[l], w_in[l], cos_t, sin_t, diff_q_norm_w[l],
            diff_k_norm_w[l], fox_q_norm_w[l], fox_k_norm_w[l], b_forget[l])
        lam_params = [p[l].reshape(1, -1) for p in
                      (lambda_q1, lambda_k1, lambda_q2, lambda_k2, diff_subln_w)]
        od = _attention(dqt, dk, dvt, batch, lam_params, fox=False, lam_init=lam_init)
        of = _attention(fqt, fk, fvt, batch, [qaugt, kaug], fox=True)
        x1, mq = _out_proj(xc, od, of, w_out[l], norm_mem_q_w[l], w_mem_q[l], mem_q_norm_w[l])
        mk, mv = _mem_kv(mem2d, norm_mem_kv_w[l], w_mem_kv[l], mem_k_norm_w[l])
        x2 = _mem_attn(x1, mq, mk, mv, w_mem_o[l], batch)
        xc = _mlp(x2, norm_mlp_w[l], w_up[l], w_down[l])
    return xc.reshape(batch, seq, d)
```

```python
import functools
import math

import jax
import jax.numpy as jnp
from jax import lax
from jax.experimental import pallas as pl
from jax.experimental.pallas import tpu as pltpu

F32 = jnp.float32
BF16 = jnp.bfloat16

D_MODEL = 1024
CHUNK = 64
HEAD_DIM = 64
N_DIFF_HEADS = 4
DIFF_V_DIM = 2 * HEAD_DIM
N_FOX_HEADS = 8
GROUP_W = 512
ROPE_DIM = HEAD_DIM // 4
ROPE_THETA = 500000.0
N_MEM_HEADS = 4
MEM_HEAD_DIM = D_MODEL // N_MEM_HEADS
D_FF = 4 * D_MODEL
EPS = 1e-6
NEG_INF = -1e30
LOG2E = math.log2(math.e)

LANES = 128
MXU_DIM = 256
VMEM_LIMIT = 56 * 1024 * 1024

TOK_TILE = 512
WIDE_TILE = 1024
ATT_TILE = 256


def _params(sem):
    return pltpu.CompilerParams(dimension_semantics=sem, vmem_limit_bytes=VMEM_LIMIT)


def _resident(shape):
    return pl.BlockSpec(shape, lambda *_: (0,) * len(shape), pipeline_mode=pl.Buffered(1))


def _rms(x, w):
    ms = jnp.mean(x * x, axis=-1, keepdims=True)
    return (x * lax.rsqrt(ms + EPS)) * w


def _dot(a, b):
    return jnp.dot(a, b, preferred_element_type=F32)


def _dot_nt(a, b):
    return lax.dot_general(a, b, (((1,), (1,)), ((), ())), preferred_element_type=F32)


def _rope_body(pos_ref, freq_ref, cos_ref, sin_ref):
    ang = pos_ref[...] * freq_ref[...]
    cos_ref[...] = jnp.cos(ang)
    sin_ref[...] = jnp.sin(ang)


def _rope_tables(positions):
    n_tok = positions.size
    half = ROPE_DIM // 2
    inv_freq = ROPE_THETA ** (-jnp.arange(0, ROPE_DIM, 2, dtype=F32) / ROPE_DIM)
    out = jax.ShapeDtypeStruct((half, n_tok), F32)
    return pl.pallas_call(_rope_body, out_shape=(out, out), name="rope_tables")(
        positions.reshape(1, n_tok).astype(F32), inv_freq.reshape(half, 1))


def _log_sigmoid(x):
    return jnp.minimum(x, 0.0) - jnp.log1p(jnp.exp(-jnp.abs(x)))


def _split3(x):
    hi = x.astype(BF16)
    rem = x - hi.astype(F32)
    mid = rem.astype(BF16)
    lo = (rem - mid.astype(F32)).astype(BF16)
    return hi, mid, lo


BIAS_SLOT = LANES // N_FOX_HEADS


def _head_norm_t(t, gain, rope):
    half = ROPE_DIM // 2
    outs = []
    for r0 in range(0, t.shape[0], HEAD_DIM):
        th = t[r0:r0 + HEAD_DIM]
        ms = jnp.sum(th * th, axis=0, keepdims=True) * (1.0 / HEAD_DIM)
        th = (th * lax.rsqrt(ms + EPS)) * gain[r0:r0 + HEAD_DIM]
        if rope is not None:
            cos, sin = rope
            t1, t2 = th[:half], th[half:ROPE_DIM]
            th = jnp.concatenate(
                [t1 * cos - t2 * sin, t2 * cos + t1 * sin, th[ROPE_DIM:]], axis=0)
        outs.append(th)
    return jnp.concatenate(outs, axis=0)


def _in_proj_body(x_ref, nw_ref, wt_ref, cos_ref, sin_ref, gain_ref, bf_ref, dqt_ref, dk_ref,
                  dvt_ref, fqt_ref, fk_ref, fvt_ref, qaugt_ref, kaug_ref, carry_ref):
    @pl.when(pl.program_id(1) == 0)
    def _():
        carry_ref[...] = jnp.zeros_like(carry_ref)

    h = _rms(x_ref[...], nw_ref[...]).astype(BF16)
    rope = (cos_ref[...], sin_ref[...])
    g = GROUP_W

    def proj_t(group, rows=GROUP_W):
        r0 = group * GROUP_W
        return _dot_nt(wt_ref[r0:r0 + rows, :], h)

    def store_keys(k_ref, kt):
        for c in range(0, g, LANES):
            k_ref[:, c:c + LANES] = kt[c:c + LANES].T.astype(BF16)

    def store_values(vt_ref, vt):
        for t in range(TOK_TILE // ATT_TILE):
            vt_ref[t] = vt[:, t * ATT_TILE:(t + 1) * ATT_TILE].astype(BF16)

    dqt_ref[...] = _head_norm_t(proj_t(0), gain_ref[0:g], rope).astype(BF16)
    store_keys(dk_ref, _head_norm_t(proj_t(1), gain_ref[g:2 * g], rope))
    store_values(dvt_ref, proj_t(2))
    fqt_ref[...] = _head_norm_t(proj_t(3), gain_ref[2 * g:3 * g], None).astype(BF16)
    store_keys(fk_ref, _head_norm_t(proj_t(4), gain_ref[3 * g:4 * g], None))
    store_values(fvt_ref, proj_t(5))

    log_f = _log_sigmoid(proj_t(6, LANES) + bf_ref[...])
    r = lax.broadcasted_iota(jnp.int32, (MXU_DIM, MXU_DIM), 0)
    c = lax.broadcasted_iota(jnp.int32, (MXU_DIM, MXU_DIM), 1)
    triu = jnp.where(r <= c, 1.0, 0.0).astype(BF16)
    part = lax.broadcasted_iota(jnp.int32, (LANES, MXU_DIM), 0) % BIAS_SLOT
    carry = carry_ref[:, 0:1]
    for c0 in range(0, TOK_TILE, MXU_DIM):
        hi, mid, lo = _split3(log_f[:, c0:c0 + MXU_DIM])
        cum = (_dot(hi, triu) + _dot(mid, triu)) + _dot(lo, triu) + carry
        carry = cum[:, MXU_DIM - 1:MXU_DIM]
        hi, mid, lo = (t.astype(F32) for t in _split3(cum * LOG2E))
        kaug_t = jnp.where(part == 0, -hi, jnp.where(part == 1, -mid, jnp.where(
            part == 2, -lo, jnp.where(part < 6, 1.0, 0.0))))
        qaug_t = jnp.where(part < 3, 1.0, jnp.where(part == 3, hi, jnp.where(
            part == 4, mid, jnp.where(part == 5, lo, 0.0))))
        kaug_ref[c0:c0 + MXU_DIM, :] = kaug_t.T.astype(BF16)
        qaugt_ref[:, c0:c0 + MXU_DIM] = qaug_t.astype(BF16)
    carry_ref[...] = jnp.broadcast_to(carry, carry_ref.shape)


def _in_proj(x2d, batch, norm_w, w_in, cos_t, sin_t, dqw, dkw, fqw, fkw, b_forget):
    n_tok = x2d.shape[0]
    tiles = n_tok // batch // TOK_TILE
    g = GROUP_W
    fl = jnp.repeat(w_in[:, 6 * g:6 * g + N_FOX_HEADS], BIAS_SLOT, axis=1)
    wt = jnp.concatenate([w_in[:, :6 * g], fl], axis=1).T.astype(BF16)
    bf = jnp.repeat(b_forget, BIAS_SLOT).reshape(LANES, 1)
    qscale = HEAD_DIM ** -0.5 * LOG2E
    gains = jnp.concatenate([jnp.tile(v, g // HEAD_DIM) for v in
                             (dqw * qscale, dkw, fqw * qscale, fkw)]).reshape(4 * g, 1)

    row = lambda b, i: (b * tiles + i, 0)
    col = lambda b, i: (0, b * tiles + i)
    fixed = lambda b, i: (0, 0)
    tok = lambda width: pl.BlockSpec((TOK_TILE, width), row)
    tok_t = lambda rows: pl.BlockSpec((rows, TOK_TILE), col)
    slabs = TOK_TILE // ATT_TILE
    vt_spec = pl.BlockSpec((slabs, g, ATT_TILE), lambda b, i: (b * tiles + i, 0, 0))
    k_shape = jax.ShapeDtypeStruct((n_tok, g), BF16)
    qt_shape = jax.ShapeDtypeStruct((g, n_tok), BF16)
    vt_shape = jax.ShapeDtypeStruct((n_tok // ATT_TILE, g, ATT_TILE), BF16)
    half = ROPE_DIM // 2
    return pl.pallas_call(
        _in_proj_body,
        grid=(batch, tiles),
        in_specs=[tok(D_MODEL), pl.BlockSpec((1, D_MODEL), fixed), pl.BlockSpec(wt.shape, fixed),
                  tok_t(half), tok_t(half), pl.BlockSpec(gains.shape, fixed),
                  pl.BlockSpec(bf.shape, fixed)],
        out_specs=[tok_t(g), tok(g), vt_spec, tok_t(g), tok(g), vt_spec, tok_t(LANES),
                   tok(LANES)],
        out_shape=[qt_shape, k_shape, vt_shape, qt_shape, k_shape, vt_shape,
                   jax.ShapeDtypeStruct((LANES, n_tok), BF16),
                   jax.ShapeDtypeStruct((n_tok, LANES), BF16)],
        scratch_shapes=[pltpu.VMEM((LANES, LANES), F32)],
        compiler_params=_params(("arbitrary", "arbitrary")),
        name="in_proj",
    )(x2d, norm_w.reshape(1, -1), wt, cos_t, sin_t, gains, bf)


N_CHAINS = 2 * (GROUP_W // LANES)
SUM_ROWS = 16


def _attn_body(qt_ref, k_ref, vt_ref, *rest, fox, lam_init):
    if fox:
        qaugt_ref, kaug_ref, o_ref, rhs_scr, s0_scr, s1_scr, m_scr, acc_scr = rest
    else:
        (lq1_ref, lk1_ref, lq2_ref, lk2_ref, subw_ref, o_ref, rhs_scr, s0_scr, s1_scr, m_scr,
         acc_scr) = rest
    s_scr = (s0_scr, s1_scr)
    i = pl.program_id(1)
    width = acc_scr.shape[1] - SUM_ROWS
    ones_rows = jnp.ones((SUM_ROWS, ATT_TILE), BF16)

    half_zero = jnp.zeros((HEAD_DIM, ATT_TILE), BF16)
    for g in range(GROUP_W // LANES):
        r0 = g * LANES
        rhs_scr[2 * g, :HEAD_DIM, :] = qt_ref[r0:r0 + HEAD_DIM, :]
        rhs_scr[2 * g, HEAD_DIM:LANES, :] = half_zero
        rhs_scr[2 * g + 1, :HEAD_DIM, :] = half_zero
        rhs_scr[2 * g + 1, HEAD_DIM:LANES, :] = qt_ref[r0 + HEAD_DIM:r0 + LANES, :]
    if fox:
        for c in range(N_CHAINS):
            b0 = c * BIAS_SLOT
            rhs_scr[c, LANES:, :] = jnp.zeros((LANES, ATT_TILE), BF16)
            rhs_scr[c, LANES + b0:LANES + b0 + BIAS_SLOT, :] = qaugt_ref[b0:b0 + BIAS_SLOT, :]
    m_scr[...] = jnp.full(m_scr.shape, NEG_INF, F32)
    acc_scr[...] = jnp.zeros(acc_scr.shape, F32)

    def scores(j, slot, g):
        start = pl.multiple_of(j * ATT_TILE, ATT_TILE)
        lhs = k_ref[pl.ds(start, ATT_TILE), g * LANES:(g + 1) * LANES]
        if fox:
            lhs = jnp.concatenate([lhs, kaug_ref[pl.ds(start, ATT_TILE), :]], axis=-1)
        for c in (2 * g, 2 * g + 1):
            s_scr[slot][c] = _dot(lhs, rhs_scr[c])

    def softmax_pv(j, slot, c, mask):
        s = s_scr[slot][c]
        if mask is not None:
            s = jnp.where(mask, s, NEG_INF)
        m_old = m_scr[c:c + 1, :]
        m_new = jnp.maximum(m_old, jnp.max(s, axis=0, keepdims=True))
        alpha = jnp.exp2(m_old - m_new)
        p = jnp.exp2(s - m_new)
        m_scr[c:c + 1, :] = m_new
        r0 = c * HEAD_DIM if fox else (c // 2) * LANES
        vt = jnp.concatenate([vt_ref[j, r0:r0 + width, :], ones_rows], axis=0)
        acc_scr[c] = alpha * acc_scr[c] + _dot(vt, p.astype(BF16))

    def step(score_args, soft_args):
        for g in range(GROUP_W // LANES):
            if score_args is not None:
                scores(*score_args, g)
            if soft_args is not None:
                j, slot, mask = soft_args
                softmax_pv(j, slot, 2 * g, mask)
                softmax_pv(j, slot, 2 * g + 1, mask)

    def pair(u, carry):
        t = 2 * u
        step((t + 1, 1), (t, 0, None))
        step((t + 2, 0), (t + 1, 1, None))
        return carry

    step((0, 0), None)
    lax.fori_loop(0, i // 2, pair, 0)
    div = 1 if fox else CHUNK
    k_id = lax.broadcasted_iota(jnp.int32, (ATT_TILE, ATT_TILE), 0) // div
    q_id = lax.broadcasted_iota(jnp.int32, (ATT_TILE, ATT_TILE), 1) // div
    mask = k_id <= q_id

    @pl.when(i % 2 == 0)
    def _():
        step(None, (i, 0, mask))

    @pl.when(i % 2 == 1)
    def _():
        step((i, 1), (i - 1, 0, None))
        step(None, (i, 1, mask))

    if not fox:
        lam = (jnp.exp(jnp.sum(lq1_ref[...] * lk1_ref[...], axis=-1, keepdims=True))
               - jnp.exp(jnp.sum(lq2_ref[...] * lk2_ref[...], axis=-1, keepdims=True))
               + lam_init)
    for g in range(GROUP_W // LANES):
        o1 = acc_scr[2 * g, :width, :] / acc_scr[2 * g, width:width + 1, :]
        o2 = acc_scr[2 * g + 1, :width, :] / acc_scr[2 * g + 1, width:width + 1, :]
        if fox:
            o = jnp.concatenate([o1, o2], axis=0).T
        else:
            o = _rms((o1 - lam * o2).T, subw_ref[...]) * (1.0 - lam_init)
        o_ref[:, g * LANES:(g + 1) * LANES] = o.astype(BF16)


def _attention(qt, k, vt, batch, extra, *, fox, lam_init=0.0):
    n_tok = k.shape[0]
    seq = n_tok // batch
    tiles = seq // ATT_TILE
    width = HEAD_DIM if fox else DIFF_V_DIM
    q_spec = lambda w: pl.BlockSpec((ATT_TILE, w), lambda b, i: (b * tiles + i, 0))
    qt_spec = lambda r: pl.BlockSpec((r, ATT_TILE), lambda b, i: (0, b * tiles + i))
    kv_spec = lambda w: pl.BlockSpec((seq, w), lambda b, i: (b, 0))
    vt_spec = pl.BlockSpec((tiles, GROUP_W, ATT_TILE), lambda b, i: (b, 0, 0))
    if fox:
        extra_specs = [qt_spec(LANES), kv_spec(LANES)]
    else:
        extra_specs = [pl.BlockSpec(e.shape, lambda b, i: (0, 0)) for e in extra]
    rhs_w = 2 * LANES if fox else LANES
    return pl.pallas_call(
        functools.partial(_attn_body, fox=fox, lam_init=lam_init),
        grid=(batch, tiles),
        in_specs=[qt_spec(GROUP_W), kv_spec(GROUP_W), vt_spec] + extra_specs,
        out_specs=q_spec(GROUP_W),
        out_shape=jax.ShapeDtypeStruct((n_tok, GROUP_W), BF16),
        scratch_shapes=[pltpu.VMEM((N_CHAINS, rhs_w, ATT_TILE), BF16),
                        pltpu.VMEM((N_CHAINS, ATT_TILE, ATT_TILE), F32),
                        pltpu.VMEM((N_CHAINS, ATT_TILE, ATT_TILE), F32),
                        pltpu.VMEM((N_CHAINS, ATT_TILE), F32),
                        pltpu.VMEM((N_CHAINS, width + SUM_ROWS, ATT_TILE), F32)],
        compiler_params=_params(("arbitrary", "arbitrary")),
        name="fox_attn" if fox else "diff_attn",
    )(qt, k, vt, *extra)


def _out_proj_body(x_ref, od_ref, of_ref, wo_ref, nw_ref, wq_ref, qw_ref, x1_ref, mq_ref):
    x1 = x_ref[...] + (_dot(od_ref[...], wo_ref[:GROUP_W, :])
                       + _dot(of_ref[...], wo_ref[GROUP_W:, :]))
    x1_ref[...] = x1
    hq = _rms(x1, nw_ref[...]).astype(BF16)
    mq = _dot(hq, wq_ref[...])
    qw = qw_ref[...]
    for c in range(0, D_MODEL, MEM_HEAD_DIM):
        mq_ref[:, c:c + MEM_HEAD_DIM] = _rms(mq[:, c:c + MEM_HEAD_DIM], qw).astype(BF16)


def _out_proj(x2d, od, of, w_out, norm_w, w_mem_q, mem_q_norm_w):
    n_tok = x2d.shape[0]
    tok = lambda width: pl.BlockSpec((WIDE_TILE, width), lambda i: (i, 0))
    scale = MEM_HEAD_DIM ** -0.5
    return pl.pallas_call(
        _out_proj_body,
        grid=(n_tok // WIDE_TILE,),
        in_specs=[tok(D_MODEL), tok(GROUP_W), tok(GROUP_W), _resident((D_MODEL, D_MODEL)),
                  _resident((1, D_MODEL)), _resident((D_MODEL, D_MODEL)),
                  _resident((1, MEM_HEAD_DIM))],
        out_specs=[tok(D_MODEL), tok(D_MODEL)],
        out_shape=[jax.ShapeDtypeStruct((n_tok, D_MODEL), F32),
                   jax.ShapeDtypeStruct((n_tok, D_MODEL), BF16)],
        compiler_params=_params(("arbitrary",)),
        name="out_proj",
    )(x2d, od, of, w_out.astype(BF16), norm_w.reshape(1, -1), w_mem_q.astype(BF16),
      (mem_q_norm_w * scale).reshape(1, -1))


def _mem_kv_body(m_ref, nw_ref, w_ref, kw_ref, mk_ref, mv_ref):
    hm = _rms(m_ref[...], nw_ref[...]).astype(BF16)
    kv = _dot(hm, w_ref[...])
    kw = kw_ref[...]
    for c in range(0, D_MODEL, MEM_HEAD_DIM):
        mk_ref[:, c:c + MEM_HEAD_DIM] = _rms(kv[:, c:c + MEM_HEAD_DIM], kw).astype(BF16)
    mv_ref[...] = kv[:, D_MODEL:].astype(BF16)


def _mem_kv(mem2d, norm_w, w_mem_kv, mem_k_norm_w):
    n_mem = mem2d.shape[0]
    tok = pl.BlockSpec((TOK_TILE, D_MODEL), lambda i: (i, 0))
    fixed = lambda r, c: pl.BlockSpec((r, c), lambda i: (0, 0))
    out = jax.ShapeDtypeStruct((n_mem, D_MODEL), BF16)
    return pl.pallas_call(
        _mem_kv_body,
        grid=(n_mem // TOK_TILE,),
        in_specs=[tok, fixed(1, D_MODEL), fixed(D_MODEL, 2 * D_MODEL), fixed(1, MEM_HEAD_DIM)],
        out_specs=[tok, tok],
        out_shape=[out, out],
        compiler_params=_params(("arbitrary",)),
        name="mem_kv",
    )(mem2d, norm_w.reshape(1, -1), w_mem_kv.astype(BF16), mem_k_norm_w.reshape(1, -1))


def _mem_attn_body(x1_ref, mq_ref, mk_ref, mv_ref, wo_ref, x2_ref):
    mq = mq_ref[...]
    heads = []
    for c in range(0, D_MODEL, MEM_HEAD_DIM):
        s = _dot_nt(mq[:, c:c + MEM_HEAD_DIM], mk_ref[:, c:c + MEM_HEAD_DIM])
        p = jnp.exp(s - jnp.max(s, axis=-1, keepdims=True))
        l = jnp.sum(p, axis=-1, keepdims=True)
        heads.append((_dot(p.astype(BF16), mv_ref[:, c:c + MEM_HEAD_DIM]) / l).astype(BF16))
    mo = jnp.concatenate(heads, axis=-1)
    x2_ref[...] = x1_ref[...] + _dot(mo, wo_ref[...])


def _mem_attn(x1, mq, mk, mv, w_mem_o, batch):
    n_tok = x1.shape[0]
    tiles = n_tok // batch // WIDE_TILE
    mem_len = mk.shape[0] // batch
    tok = pl.BlockSpec((WIDE_TILE, D_MODEL), lambda b, i: (b * tiles + i, 0))
    mem = pl.BlockSpec((mem_len, D_MODEL), lambda b, i: (b, 0))
    return pl.pallas_call(
        _mem_attn_body,
        grid=(batch, tiles),
        in_specs=[tok, tok, mem, mem, _resident((D_MODEL, D_MODEL))],
        out_specs=tok,
        out_shape=jax.ShapeDtypeStruct((n_tok, D_MODEL), F32),
        compiler_params=_params(("arbitrary", "arbitrary")),
        name="mem_attn",
    )(x1, mq, mk, mv, w_mem_o.astype(BF16))


FF_CHUNK = 1024


def _mlp_body(x_ref, nw_ref, wu_ref, wd_ref, o_ref):
    x = x_ref[...]
    h = _rms(x, nw_ref[...]).astype(BF16)
    acc = x
    for c in range(0, D_FF, FF_CHUNK):
        u = jnp.maximum(_dot(h, wu_ref[:, c:c + FF_CHUNK]), 0.0)
        acc = acc + _dot((u * u).astype(BF16), wd_ref[c:c + FF_CHUNK, :])
    o_ref[...] = acc


def _mlp(x2, norm_w, w_up, w_down):
    n_tok = x2.shape[0]
    tok = pl.BlockSpec((WIDE_TILE, D_MODEL), lambda i: (i, 0))
    return pl.pallas_call(
        _mlp_body,
        grid=(n_tok // WIDE_TILE,),
        in_specs=[tok, _resident((1, D_MODEL)), _resident((D_MODEL, D_FF)),
                  _resident((D_FF, D_MODEL))],
        out_specs=tok,
        out_shape=jax.ShapeDtypeStruct((n_tok, D_MODEL), F32),
        compiler_params=_params(("arbitrary",)),
        name="mlp",
    )(x2, norm_w.reshape(1, -1), w_up.astype(BF16), w_down.astype(BF16))


def kernel(x, mem, positions, norm_mix_w, w_in, b_forget, diff_q_norm_w, diff_k_norm_w,
           lambda_q1, lambda_k1, lambda_q2, lambda_k2, diff_subln_w, fox_q_norm_w,
           fox_k_norm_w, w_out, norm_mem_q_w, norm_mem_kv_w, w_mem_q, w_mem_kv,
           mem_q_norm_w, mem_k_norm_w, w_mem_o, norm_mlp_w, w_up, w_down):
    batch, seq, d = x.shape
    depth = w_in.shape[0]
    assert d == D_MODEL and seq % WIDE_TILE == 0 and WIDE_TILE % TOK_TILE == 0
    assert TOK_TILE % ATT_TILE == 0
    assert (batch * mem.shape[1]) % TOK_TILE == 0

    cos_t, sin_t = _rope_tables(positions)
    xc = x.reshape(batch * seq, d)
    mem2d = mem.reshape(-1, d)
    for l in range(depth):
        lam_init = 0.8 - 0.6 * math.exp(-0.3 * l)
        dqt, dk, dvt, fqt, fk, fvt, qaugt, kaug = _in_proj(
            xc, batch, norm_mix_w[l], w_in[l], cos_t, sin_t, diff_q_norm_w[l],
            diff_k_norm_w[l], fox_q_norm_w[l], fox_k_norm_w[l], b_forget[l])
        lam_params = [p[l].reshape(1, -1) for p in
                      (lambda_q1, lambda_k1, lambda_q2, lambda_k2, diff_subln_w)]
        od = _attention(dqt, dk, dvt, batch, lam_params, fox=False, lam_init=lam_init)
        of = _attention(fqt, fk, fvt, batch, [qaugt, kaug], fox=True)
        x1, mq = _out_proj(xc, od, of, w_out[l], norm_mem_q_w[l], w_mem_q[l], mem_q_norm_w[l])
        mk, mv = _mem_kv(mem2d, norm_mem_kv_w[l], w_mem_kv[l], mem_k_norm_w[l])
        x2 = _mem_attn(x1, mq, mk, mv, w_mem_o[l], batch)
        xc = _mlp(x2, norm_mlp_w[l], w_up[l], w_down[l])
    return xc.reshape(batch, seq, d)
```

```python
import functools
import math

import jax
import jax.numpy as jnp
from jax import lax
from jax.experimental import pallas as pl
from jax.experimental.pallas import tpu as pltpu

F32 = jnp.float32
BF16 = jnp.bfloat16

D_MODEL = 1024
CHUNK = 64
HEAD_DIM = 64
N_DIFF_HEADS = 4
DIFF_V_DIM = 2 * HEAD_DIM
N_FOX_HEADS = 8
GROUP_W = 512
ROPE_DIM = HEAD_DIM // 4
ROPE_THETA = 500000.0
N_MEM_HEADS = 4
MEM_HEAD_DIM = D_MODEL // N_MEM_HEADS
D_FF = 4 * D_MODEL
EPS = 1e-6
NEG_INF = -1e30
LOG2E = math.log2(math.e)

LANES = 128
MXU_DIM = 256
VMEM_LIMIT = 56 * 1024 * 1024

TOK_TILE = 1024
WIDE_TILE = 1024
ATT_TILE = 256


def _params(sem):
    return pltpu.CompilerParams(dimension_semantics=sem, vmem_limit_bytes=VMEM_LIMIT)


def _resident(shape):
    return pl.BlockSpec(shape, lambda *_: (0,) * len(shape), pipeline_mode=pl.Buffered(1))


def _rms(x, w):
    ms = jnp.mean(x * x, axis=-1, keepdims=True)
    return (x * lax.rsqrt(ms + EPS)) * w


def _dot(a, b):
    return jnp.dot(a, b, preferred_element_type=F32)


def _dot_nt(a, b):
    return lax.dot_general(a, b, (((1,), (1,)), ((), ())), preferred_element_type=F32)


def _rope_body(pos_ref, freq_ref, cos_ref, sin_ref):
    ang = pos_ref[...] * freq_ref[...]
    cos_ref[...] = jnp.cos(ang)
    sin_ref[...] = jnp.sin(ang)


def _rope_tables(positions):
    n_tok = positions.size
    half = ROPE_DIM // 2
    inv_freq = ROPE_THETA ** (-jnp.arange(0, ROPE_DIM, 2, dtype=F32) / ROPE_DIM)
    out = jax.ShapeDtypeStruct((half, n_tok), F32)
    return pl.pallas_call(_rope_body, out_shape=(out, out), name="rope_tables")(
        positions.reshape(1, n_tok).astype(F32), inv_freq.reshape(half, 1))


def _log_sigmoid(x):
    return jnp.minimum(x, 0.0) - jnp.log1p(jnp.exp(-jnp.abs(x)))


def _split3(x):
    hi = x.astype(BF16)
    rem = x - hi.astype(F32)
    mid = rem.astype(BF16)
    lo = (rem - mid.astype(F32)).astype(BF16)
    return hi, mid, lo


BIAS_SLOT = LANES // N_FOX_HEADS


def _head_norm_t(t, gain, rope):
    half = ROPE_DIM // 2
    outs = []
    for r0 in range(0, t.shape[0], HEAD_DIM):
        th = t[r0:r0 + HEAD_DIM]
        ms = jnp.sum(th * th, axis=0, keepdims=True) * (1.0 / HEAD_DIM)
        th = (th * lax.rsqrt(ms + EPS)) * gain[r0:r0 + HEAD_DIM]
        if rope is not None:
            cos, sin = rope
            t1, t2 = th[:half], th[half:ROPE_DIM]
            th = jnp.concatenate(
                [t1 * cos - t2 * sin, t2 * cos + t1 * sin, th[ROPE_DIM:]], axis=0)
        outs.append(th)
    return jnp.concatenate(outs, axis=0)


def _in_proj_body(x_ref, nw_ref, wt_ref, wft_ref, cos_ref, sin_ref, gain_ref, bf_ref, dqt_ref,
                  dk_ref, dvt_ref, fqt_ref, fk_ref, fvt_ref, qaugt_ref, kaug_ref, carry_ref):
    @pl.when(pl.program_id(1) == 0)
    def _():
        carry_ref[...] = jnp.zeros_like(carry_ref)

    h = _rms(x_ref[...], nw_ref[...]).astype(BF16)
    rope = (cos_ref[...], sin_ref[...])
    g = GROUP_W

    def proj_t(group):
        r0 = group * GROUP_W
        return _dot_nt(wt_ref[r0:r0 + GROUP_W, :], h)

    def store_keys(k_ref, kt):
        for c in range(0, g, LANES):
            k_ref[:, c:c + LANES] = kt[c:c + LANES].T.astype(BF16)

    def store_values(vt_ref, vt):
        for t in range(TOK_TILE // ATT_TILE):
            vt_ref[t] = vt[:, t * ATT_TILE:(t + 1) * ATT_TILE].astype(BF16)

    dqt_ref[...] = _head_norm_t(proj_t(0), gain_ref[0:g], rope).astype(BF16)
    store_keys(dk_ref, _head_norm_t(proj_t(1), gain_ref[g:2 * g], rope))
    store_values(dvt_ref, proj_t(2))
    fqt_ref[...] = _head_norm_t(proj_t(3), gain_ref[2 * g:3 * g], None).astype(BF16)
    store_keys(fk_ref, _head_norm_t(proj_t(4), gain_ref[3 * g:4 * g], None))
    store_values(fvt_ref, proj_t(5))

    log_f = _log_sigmoid(_dot_nt(wft_ref[...], h) + bf_ref[...])
    r = lax.broadcasted_iota(jnp.int32, (MXU_DIM, MXU_DIM), 0)
    c = lax.broadcasted_iota(jnp.int32, (MXU_DIM, MXU_DIM), 1)
    triu = jnp.where(r <= c, 1.0, 0.0).astype(BF16)
    part = lax.broadcasted_iota(jnp.int32, (LANES, MXU_DIM), 0) % BIAS_SLOT
    carry = carry_ref[:, 0:1]
    for c0 in range(0, TOK_TILE, MXU_DIM):
        hi, mid, lo = _split3(log_f[:, c0:c0 + MXU_DIM])
        cum = (_dot(hi, triu) + _dot(mid, triu)) + _dot(lo, triu) + carry
        carry = cum[:, MXU_DIM - 1:MXU_DIM]
        hi, mid, lo = (t.astype(F32) for t in _split3(cum * LOG2E))
        kaug_t = jnp.where(part == 0, -hi, jnp.where(part == 1, -mid, jnp.where(
            part == 2, -lo, jnp.where(part < 6, 1.0, 0.0))))
        qaug_t = jnp.where(part < 3, 1.0, jnp.where(part == 3, hi, jnp.where(
            part == 4, mid, jnp.where(part == 5, lo, 0.0))))
        kaug_ref[c0:c0 + MXU_DIM, :] = kaug_t.T.astype(BF16)
        qaugt_ref[:, c0:c0 + MXU_DIM] = qaug_t.astype(BF16)
    carry_ref[...] = jnp.broadcast_to(carry, carry_ref.shape)


def _in_proj(x2d, batch, norm_w, w_in, cos_t, sin_t, dqw, dkw, fqw, fkw, b_forget):
    n_tok = x2d.shape[0]
    tiles = n_tok // batch // TOK_TILE
    g = GROUP_W
    wt = w_in[:, :6 * g].T.astype(BF16)
    wft = jnp.repeat(w_in[:, 6 * g:6 * g + N_FOX_HEADS].T, BIAS_SLOT, axis=0).astype(BF16)
    bf = jnp.repeat(b_forget, BIAS_SLOT).reshape(LANES, 1)
    qscale = HEAD_DIM ** -0.5 * LOG2E
    gains = jnp.concatenate([jnp.tile(v, g // HEAD_DIM) for v in
                             (dqw * qscale, dkw, fqw * qscale, fkw)]).reshape(4 * g, 1)

    row = lambda b, i: (b * tiles + i, 0)
    col = lambda b, i: (0, b * tiles + i)
    tok = lambda width: pl.BlockSpec((TOK_TILE, width), row)
    tok_t = lambda rows: pl.BlockSpec((rows, TOK_TILE), col)
    slabs = TOK_TILE // ATT_TILE
    vt_spec = pl.BlockSpec((slabs, g, ATT_TILE), lambda b, i: (b * tiles + i, 0, 0))
    k_shape = jax.ShapeDtypeStruct((n_tok, g), BF16)
    qt_shape = jax.ShapeDtypeStruct((g, n_tok), BF16)
    vt_shape = jax.ShapeDtypeStruct((n_tok // ATT_TILE, g, ATT_TILE), BF16)
    half = ROPE_DIM // 2
    return pl.pallas_call(
        _in_proj_body,
        grid=(batch, tiles),
        in_specs=[tok(D_MODEL), _resident((1, D_MODEL)), _resident(wt.shape),
                  _resident(wft.shape), tok_t(half), tok_t(half), _resident(gains.shape),
                  _resident(bf.shape)],
        out_specs=[tok_t(g), tok(g), vt_spec, tok_t(g), tok(g), vt_spec, tok_t(LANES),
                   tok(LANES)],
        out_shape=[qt_shape, k_shape, vt_shape, qt_shape, k_shape, vt_shape,
                   jax.ShapeDtypeStruct((LANES, n_tok), BF16),
                   jax.ShapeDtypeStruct((n_tok, LANES), BF16)],
        scratch_shapes=[pltpu.VMEM((LANES, LANES), F32)],
        compiler_params=_params(("arbitrary", "arbitrary")),
        name="in_proj",
    )(x2d, norm_w.reshape(1, -1), wt, wft, cos_t, sin_t, gains, bf)


N_CHAINS = 2 * (GROUP_W // LANES)
SUM_ROWS = 16


def _attn_body(qt_ref, k_ref, vt_ref, *rest, fox, lam_init):
    if fox:
        qaugt_ref, kaug_ref, o_ref, rhs_scr, s0_scr, s1_scr, m_scr, acc_scr = rest
    else:
        (lq1_ref, lk1_ref, lq2_ref, lk2_ref, subw_ref, o_ref, rhs_scr, s0_scr, s1_scr, m_scr,
         acc_scr) = rest
    s_scr = (s0_scr, s1_scr)
    i = pl.program_id(1)
    width = acc_scr.shape[1] - SUM_ROWS
    ones_rows = jnp.ones((SUM_ROWS, ATT_TILE), BF16)

    half_zero = jnp.zeros((HEAD_DIM, ATT_TILE), BF16)
    for g in range(GROUP_W // LANES):
        r0 = g * LANES
        rhs_scr[2 * g, :HEAD_DIM, :] = qt_ref[r0:r0 + HEAD_DIM, :]
        rhs_scr[2 * g, HEAD_DIM:LANES, :] = half_zero
        rhs_scr[2 * g + 1, :HEAD_DIM, :] = half_zero
        rhs_scr[2 * g + 1, HEAD_DIM:LANES, :] = qt_ref[r0 + HEAD_DIM:r0 + LANES, :]
    if fox:
        for c in range(N_CHAINS):
            b0 = c * BIAS_SLOT
            rhs_scr[c, LANES:, :] = jnp.zeros((LANES, ATT_TILE), BF16)
            rhs_scr[c, LANES + b0:LANES + b0 + BIAS_SLOT, :] = qaugt_ref[b0:b0 + BIAS_SLOT, :]
    m_scr[...] = jnp.full(m_scr.shape, NEG_INF, F32)
    acc_scr[...] = jnp.zeros(acc_scr.shape, F32)

    def scores(j, slot, g):
        start = pl.multiple_of(j * ATT_TILE, ATT_TILE)
        lhs = k_ref[pl.ds(start, ATT_TILE), g * LANES:(g + 1) * LANES]
        if fox:
            lhs = jnp.concatenate([lhs, kaug_ref[pl.ds(start, ATT_TILE), :]], axis=-1)
        for c in (2 * g, 2 * g + 1):
            s_scr[slot][c] = _dot(lhs, rhs_scr[c])

    def softmax_pv(j, slot, c, mask):
        s = s_scr[slot][c]
        if mask is not None:
            s = jnp.where(mask, s, NEG_INF)
        m_old = m_scr[c:c + 1, :]
        m_new = jnp.maximum(m_old, jnp.max(s, axis=0, keepdims=True))
        alpha = jnp.exp2(m_old - m_new)
        p = jnp.exp2(s - m_new)
        m_scr[c:c + 1, :] = m_new
        r0 = c * HEAD_DIM if fox else (c // 2) * LANES
        vt = jnp.concatenate([vt_ref[j, r0:r0 + width, :], ones_rows], axis=0)
        acc_scr[c] = alpha * acc_scr[c] + _dot(vt, p.astype(BF16))

    def step(score_args, soft_args):
        for g in range(GROUP_W // LANES):
            if score_args is not None:
                scores(*score_args, g)
            if soft_args is not None:
                j, slot, mask = soft_args
                softmax_pv(j, slot, 2 * g, mask)
                softmax_pv(j, slot, 2 * g + 1, mask)

    def pair(u, carry):
        t = 2 * u
        step((t + 1, 1), (t, 0, None))
        step((t + 2, 0), (t + 1, 1, None))
        return carry

    step((0, 0), None)
    lax.fori_loop(0, i // 2, pair, 0)
    div = 1 if fox else CHUNK
    k_id = lax.broadcasted_iota(jnp.int32, (ATT_TILE, ATT_TILE), 0) // div
    q_id = lax.broadcasted_iota(jnp.int32, (ATT_TILE, ATT_TILE), 1) // div
    mask = k_id <= q_id

    @pl.when(i % 2 == 0)
    def _():
        step(None, (i, 0, mask))

    @pl.when(i % 2 == 1)
    def _():
        step((i, 1), (i - 1, 0, None))
        step(None, (i, 1, mask))

    if not fox:
        lam = (jnp.exp(jnp.sum(lq1_ref[...] * lk1_ref[...], axis=-1, keepdims=True))
               - jnp.exp(jnp.sum(lq2_ref[...] * lk2_ref[...], axis=-1, keepdims=True))
               + lam_init)
    for g in range(GROUP_W // LANES):
        o1 = acc_scr[2 * g, :width, :] / acc_scr[2 * g, width:width + 1, :]
        o2 = acc_scr[2 * g + 1, :width, :] / acc_scr[2 * g + 1, width:width + 1, :]
        if fox:
            o = jnp.concatenate([o1, o2], axis=0).T
        else:
            o = _rms((o1 - lam * o2).T, subw_ref[...]) * (1.0 - lam_init)
        o_ref[:, g * LANES:(g + 1) * LANES] = o.astype(BF16)


def _attention(qt, k, vt, batch, extra, *, fox, lam_init=0.0):
    n_tok = k.shape[0]
    seq = n_tok // batch
    tiles = seq // ATT_TILE
    width = HEAD_DIM if fox else DIFF_V_DIM
    q_spec = lambda w: pl.BlockSpec((ATT_TILE, w), lambda b, i: (b * tiles + i, 0))
    qt_spec = lambda r: pl.BlockSpec((r, ATT_TILE), lambda b, i: (0, b * tiles + i))
    kv_spec = lambda w: pl.BlockSpec((seq, w), lambda b, i: (b, 0))
    vt_spec = pl.BlockSpec((tiles, GROUP_W, ATT_TILE), lambda b, i: (b, 0, 0))
    if fox:
        extra_specs = [qt_spec(LANES), kv_spec(LANES)]
    else:
        extra_specs = [pl.BlockSpec(e.shape, lambda b, i: (0, 0)) for e in extra]
    rhs_w = 2 * LANES if fox else LANES
    return pl.pallas_call(
        functools.partial(_attn_body, fox=fox, lam_init=lam_init),
        grid=(batch, tiles),
        in_specs=[qt_spec(GROUP_W), kv_spec(GROUP_W), vt_spec] + extra_specs,
        out_specs=q_spec(GROUP_W),
        out_shape=jax.ShapeDtypeStruct((n_tok, GROUP_W), BF16),
        scratch_shapes=[pltpu.VMEM((N_CHAINS, rhs_w, ATT_TILE), BF16),
                        pltpu.VMEM((N_CHAINS, ATT_TILE, ATT_TILE), F32),
                        pltpu.VMEM((N_CHAINS, ATT_TILE, ATT_TILE), F32),
                        pltpu.VMEM((N_CHAINS, ATT_TILE), F32),
                        pltpu.VMEM((N_CHAINS, width + SUM_ROWS, ATT_TILE), F32)],
        compiler_params=_params(("arbitrary", "arbitrary")),
        name="fox_attn" if fox else "diff_attn",
    )(qt, k, vt, *extra)


def _out_proj_body(x_ref, od_ref, of_ref, wo_ref, nw_ref, wq_ref, qw_ref, x1_ref, mq_ref):
    x1 = x_ref[...] + (_dot(od_ref[...], wo_ref[:GROUP_W, :])
                       + _dot(of_ref[...], wo_ref[GROUP_W:, :]))
    x1_ref[...] = x1
    hq = _rms(x1, nw_ref[...]).astype(BF16)
    mq = _dot(hq, wq_ref[...])
    qw = qw_ref[...]
    for c in range(0, D_MODEL, MEM_HEAD_DIM):
        mq_ref[:, c:c + MEM_HEAD_DIM] = _rms(mq[:, c:c + MEM_HEAD_DIM], qw).astype(BF16)


def _out_proj(x2d, od, of, w_out, norm_w, w_mem_q, mem_q_norm_w):
    n_tok = x2d.shape[0]
    tok = lambda width: pl.BlockSpec((WIDE_TILE, width), lambda i: (i, 0))
    scale = MEM_HEAD_DIM ** -0.5
    return pl.pallas_call(
        _out_proj_body,
        grid=(n_tok // WIDE_TILE,),
        in_specs=[tok(D_MODEL), tok(GROUP_W), tok(GROUP_W), _resident((D_MODEL, D_MODEL)),
                  _resident((1, D_MODEL)), _resident((D_MODEL, D_MODEL)),
                  _resident((1, MEM_HEAD_DIM))],
        out_specs=[tok(D_MODEL), tok(D_MODEL)],
        out_shape=[jax.ShapeDtypeStruct((n_tok, D_MODEL), F32),
                   jax.ShapeDtypeStruct((n_tok, D_MODEL), BF16)],
        compiler_params=_params(("arbitrary",)),
        name="out_proj",
    )(x2d, od, of, w_out.astype(BF16), norm_w.reshape(1, -1), w_mem_q.astype(BF16),
      (mem_q_norm_w * scale).reshape(1, -1))


def _mem_kv_body(m_ref, nw_ref, w_ref, kw_ref, mk_ref, mv_ref):
    hm = _rms(m_ref[...], nw_ref[...]).astype(BF16)
    kv = _dot(hm, w_ref[...])
    kw = kw_ref[...]
    for c in range(0, D_MODEL, MEM_HEAD_DIM):
        mk_ref[:, c:c + MEM_HEAD_DIM] = _rms(kv[:, c:c + MEM_HEAD_DIM], kw).astype(BF16)
    mv_ref[...] = kv[:, D_MODEL:].astype(BF16)


def _mem_kv(mem2d, norm_w, w_mem_kv, mem_k_norm_w):
    n_mem = mem2d.shape[0]
    tok = pl.BlockSpec((TOK_TILE, D_MODEL), lambda i: (i, 0))
    out = jax.ShapeDtypeStruct((n_mem, D_MODEL), BF16)
    return pl.pallas_call(
        _mem_kv_body,
        grid=(n_mem // TOK_TILE,),
        in_specs=[tok, _resident((1, D_MODEL)), _resident((D_MODEL, 2 * D_MODEL)),
                  _resident((1, MEM_HEAD_DIM))],
        out_specs=[tok, tok],
        out_shape=[out, out],
        compiler_params=_params(("arbitrary",)),
        name="mem_kv",
    )(mem2d, norm_w.reshape(1, -1), w_mem_kv.astype(BF16), mem_k_norm_w.reshape(1, -1))


def _mem_attn_body(x1_ref, mq_ref, mk_ref, mv_ref, wo_ref, x2_ref):
    mq = mq_ref[...]
    heads = []
    for c in range(0, D_MODEL, MEM_HEAD_DIM):
        s = _dot_nt(mq[:, c:c + MEM_HEAD_DIM], mk_ref[:, c:c + MEM_HEAD_DIM])
        p = jnp.exp(s - jnp.max(s, axis=-1, keepdims=True))
        l = jnp.sum(p, axis=-1, keepdims=True)
        heads.append((_dot(p.astype(BF16), mv_ref[:, c:c + MEM_HEAD_DIM]) / l).astype(BF16))
    mo = jnp.concatenate(heads, axis=-1)
    x2_ref[...] = x1_ref[...] + _dot(mo, wo_ref[...])


def _mem_attn(x1, mq, mk, mv, w_mem_o, batch):
    n_tok = x1.shape[0]
    tiles = n_tok // batch // WIDE_TILE
    mem_len = mk.shape[0] // batch
    tok = pl.BlockSpec((WIDE_TILE, D_MODEL), lambda b, i: (b * tiles + i, 0))
    mem = pl.BlockSpec((mem_len, D_MODEL), lambda b, i: (b, 0))
    return pl.pallas_call(
        _mem_attn_body,
        grid=(batch, tiles),
        in_specs=[tok, tok, mem, mem, _resident((D_MODEL, D_MODEL))],
        out_specs=tok,
        out_shape=jax.ShapeDtypeStruct((n_tok, D_MODEL), F32),
        compiler_params=_params(("arbitrary", "arbitrary")),
        name="mem_attn",
    )(x1, mq, mk, mv, w_mem_o.astype(BF16))


FF_CHUNK = 1024


def _mlp_body(x_ref, nw_ref, wu_ref, wd_ref, o_ref):
    x = x_ref[...]
    h = _rms(x, nw_ref[...]).astype(BF16)
    acc = x
    for c in range(0, D_FF, FF_CHUNK):
        u = jnp.maximum(_dot(h, wu_ref[:, c:c + FF_CHUNK]), 0.0)
        acc = acc + _dot((u * u).astype(BF16), wd_ref[c:c + FF_CHUNK, :])
    o_ref[...] = acc


def _mlp(x2, norm_w, w_up, w_down):
    n_tok = x2.shape[0]
    tok = pl.BlockSpec((WIDE_TILE, D_MODEL), lambda i: (i, 0))
    return pl.pallas_call(
        _mlp_body,
        grid=(n_tok // WIDE_TILE,),
        in_specs=[tok, _resident((1, D_MODEL)), _resident((D_MODEL, D_FF)),
                  _resident((D_FF, D_MODEL))],
        out_specs=tok,
        out_shape=jax.ShapeDtypeStruct((n_tok, D_MODEL), F32),
        compiler_params=_params(("arbitrary",)),
        name="mlp",
    )(x2, norm_w.reshape(1, -1), w_up.astype(BF16), w_down.astype(BF16))


def kernel(x, mem, positions, norm_mix_w, w_in, b_forget, diff_q_norm_w, diff_k_norm_w,
           lambda_q1, lambda_k1, lambda_q2, lambda_k2, diff_subln_w, fox_q_norm_w,
           fox_k_norm_w, w_out, norm_mem_q_w, norm_mem_kv_w, w_mem_q, w_mem_kv,
           mem_q_norm_w, mem_k_norm_w, w_mem_o, norm_mlp_w, w_up, w_down):
    batch, seq, d = x.shape
    depth = w_in.shape[0]
    assert d == D_MODEL and seq % WIDE_TILE == 0 and WIDE_TILE % TOK_TILE == 0
    assert TOK_TILE % ATT_TILE == 0
    assert (batch * mem.shape[1]) % TOK_TILE == 0

    cos_t, sin_t = _rope_tables(positions)
    xc = x.reshape(batch * seq, d)
    mem2d = mem.reshape(-1, d)
    for l in range(depth):
        lam_init = 0.8 - 0.6 * math.exp(-0.3 * l)
        dqt, dk, dvt, fqt, fk, fvt, qaugt, kaug = _in_proj(
            xc, batch, norm_mix_w[l], w_in[l], cos_t, sin_t, diff_q_norm_w[l],
            diff_k_norm_w[l], fox_q_norm_w[l], fox_k_norm_w[l], b_forget[l])
        lam_params = [p[l].reshape(1, -1) for p in
                      (lambda_q1, lambda_k1, lambda_q2, lambda_k2, diff_subln_w)]
        od = _attention(dqt, dk, dvt, batch, lam_params, fox=False, lam_init=lam_init)
        of = _attention(fqt, fk, fvt, batch, [qaugt, kaug], fox=True)
        x1, mq = _out_proj(xc, od, of, w_out[l], norm_mem_q_w[l], w_mem_q[l], mem_q_norm_w[l])
        mk, mv = _mem_kv(mem2d, norm_mem_kv_w[l], w_mem_kv[l], mem_k_norm_w[l])
        x2 = _mem_attn(x1, mq, mk, mv, w_mem_o[l], batch)
        xc = _mlp(x2, norm_mlp_w[l], w_up[l], w_down[l])
    return xc.reshape(batch, seq, d)
```

```python
import functools
import math

import jax
import jax.numpy as jnp
from jax import lax
from jax.experimental import pallas as pl
from jax.experimental.pallas import tpu as pltpu

F32 = jnp.float32
BF16 = jnp.bfloat16

D_MODEL = 1024
CHUNK = 64
HEAD_DIM = 64
N_DIFF_HEADS = 4
DIFF_V_DIM = 2 * HEAD_DIM
N_FOX_HEADS = 8
GROUP_W = 512
ROPE_DIM = HEAD_DIM // 4
ROPE_THETA = 500000.0
N_MEM_HEADS = 4
MEM_HEAD_DIM = D_MODEL // N_MEM_HEADS
D_FF = 4 * D_MODEL
EPS = 1e-6
NEG_INF = -1e30
LOG2E = math.log2(math.e)

LANES = 128
MXU_DIM = 256
VMEM_LIMIT = 56 * 1024 * 1024

TOK_TILE = 1024
WIDE_TILE = 1024
ATT_TILE = 256


def _params(sem):
    return pltpu.CompilerParams(dimension_semantics=sem, vmem_limit_bytes=VMEM_LIMIT)


def _resident(shape):
    return pl.BlockSpec(shape, lambda *_: (0,) * len(shape), pipeline_mode=pl.Buffered(1))


def _rms(x, w):
    ms = jnp.mean(x * x, axis=-1, keepdims=True)
    return (x * lax.rsqrt(ms + EPS)) * w


def _dot(a, b):
    return jnp.dot(a, b, preferred_element_type=F32)


def _dot_nt(a, b):
    return lax.dot_general(a, b, (((1,), (1,)), ((), ())), preferred_element_type=F32)


def _rope_body(pos_ref, freq_ref, cos_ref, sin_ref):
    ang = pos_ref[...] * freq_ref[...]
    cos_ref[...] = jnp.cos(ang)
    sin_ref[...] = jnp.sin(ang)


def _rope_tables(positions):
    n_tok = positions.size
    half = ROPE_DIM // 2
    inv_freq = ROPE_THETA ** (-jnp.arange(0, ROPE_DIM, 2, dtype=F32) / ROPE_DIM)
    out = jax.ShapeDtypeStruct((half, n_tok), F32)
    return pl.pallas_call(_rope_body, out_shape=(out, out), name="rope_tables")(
        positions.reshape(1, n_tok).astype(F32), inv_freq.reshape(half, 1))


def _log_sigmoid(x):
    return jnp.minimum(x, 0.0) - jnp.log1p(jnp.exp(-jnp.abs(x)))


def _split3(x):
    hi = x.astype(BF16)
    rem = x - hi.astype(F32)
    mid = rem.astype(BF16)
    lo = (rem - mid.astype(F32)).astype(BF16)
    return hi, mid, lo


BIAS_SLOT = LANES // N_FOX_HEADS


def _head_norm_t(t, gain, rope):
    half = ROPE_DIM // 2
    outs = []
    for r0 in range(0, t.shape[0], HEAD_DIM):
        th = t[r0:r0 + HEAD_DIM]
        ms = jnp.sum(th * th, axis=0, keepdims=True) * (1.0 / HEAD_DIM)
        th = (th * lax.rsqrt(ms + EPS)) * gain[r0:r0 + HEAD_DIM]
        if rope is not None:
            cos, sin = rope
            t1, t2 = th[:half], th[half:ROPE_DIM]
            th = jnp.concatenate(
                [t1 * cos - t2 * sin, t2 * cos + t1 * sin, th[ROPE_DIM:]], axis=0)
        outs.append(th)
    return jnp.concatenate(outs, axis=0)


def _in_proj_body(x_ref, nw_ref, wt_ref, wft_ref, cos_ref, sin_ref, gain_ref, bf_ref, dqt_ref,
                  dk_ref, dvt_ref, fqt_ref, fk_ref, fvt_ref, qaugt_ref, kaug_ref, carry_ref):
    @pl.when(pl.program_id(1) == 0)
    def _():
        carry_ref[...] = jnp.zeros_like(carry_ref)

    h = _rms(x_ref[...], nw_ref[...]).astype(BF16)
    rope = (cos_ref[...], sin_ref[...])
    g = GROUP_W

    def proj_t(group):
        r0 = group * GROUP_W
        return _dot_nt(wt_ref[r0:r0 + GROUP_W, :], h)

    def store_keys(k_ref, kt):
        for c in range(0, g, LANES):
            k_ref[:, c:c + LANES] = kt[c:c + LANES].T.astype(BF16)

    def store_values(vt_ref, vt):
        for t in range(TOK_TILE // ATT_TILE):
            vt_ref[t] = vt[:, t * ATT_TILE:(t + 1) * ATT_TILE].astype(BF16)

    dqt_ref[...] = _head_norm_t(proj_t(0), gain_ref[0:g], rope).astype(BF16)
    store_keys(dk_ref, _head_norm_t(proj_t(1), gain_ref[g:2 * g], rope))
    store_values(dvt_ref, proj_t(2))
    fqt_ref[...] = _head_norm_t(proj_t(3), gain_ref[2 * g:3 * g], None).astype(BF16)
    store_keys(fk_ref, _head_norm_t(proj_t(4), gain_ref[3 * g:4 * g], None))
    store_values(fvt_ref, proj_t(5))

    log_f = _log_sigmoid(_dot_nt(wft_ref[...], h) + bf_ref[...])
    r = lax.broadcasted_iota(jnp.int32, (MXU_DIM, MXU_DIM), 0)
    c = lax.broadcasted_iota(jnp.int32, (MXU_DIM, MXU_DIM), 1)
    triu = jnp.where(r <= c, 1.0, 0.0).astype(BF16)
    part = lax.broadcasted_iota(jnp.int32, (LANES, MXU_DIM), 0) % BIAS_SLOT
    carry = carry_ref[:, 0:1]
    for c0 in range(0, TOK_TILE, MXU_DIM):
        hi, mid, lo = _split3(log_f[:, c0:c0 + MXU_DIM])
        cum = (_dot(hi, triu) + _dot(mid, triu)) + _dot(lo, triu) + carry
        carry = cum[:, MXU_DIM - 1:MXU_DIM]
        hi, mid, lo = (t.astype(F32) for t in _split3(cum * LOG2E))
        kaug_t = jnp.where(part == 0, -hi, jnp.where(part == 1, -mid, jnp.where(
            part == 2, -lo, jnp.where(part < 6, 1.0, 0.0))))
        qaug_t = jnp.where(part < 3, 1.0, jnp.where(part == 3, hi, jnp.where(
            part == 4, mid, jnp.where(part == 5, lo, 0.0))))
        kaug_ref[c0:c0 + MXU_DIM, :] = kaug_t.T.astype(BF16)
        qaugt_ref[:, c0:c0 + MXU_DIM] = qaug_t.astype(BF16)
    carry_ref[...] = jnp.broadcast_to(carry, carry_ref.shape)


def _in_proj(x2d, batch, norm_w, w_in, cos_t, sin_t, dqw, dkw, fqw, fkw, b_forget):
    n_tok = x2d.shape[0]
    tiles = n_tok // batch // TOK_TILE
    g = GROUP_W
    wt = w_in[:, :6 * g].T.astype(BF16)
    wft = jnp.repeat(w_in[:, 6 * g:6 * g + N_FOX_HEADS].T, BIAS_SLOT, axis=0).astype(BF16)
    bf = jnp.repeat(b_forget, BIAS_SLOT).reshape(LANES, 1)
    qscale = HEAD_DIM ** -0.5 * LOG2E
    gains = jnp.concatenate([jnp.tile(v, g // HEAD_DIM) for v in
                             (dqw * qscale, dkw, fqw * qscale, fkw)]).reshape(4 * g, 1)

    row = lambda b, i: (b * tiles + i, 0)
    col = lambda b, i: (0, b * tiles + i)
    tok = lambda width: pl.BlockSpec((TOK_TILE, width), row)
    tok_t = lambda rows: pl.BlockSpec((rows, TOK_TILE), col)
    slabs = TOK_TILE // ATT_TILE
    vt_spec = pl.BlockSpec((slabs, g, ATT_TILE), lambda b, i: (b * tiles + i, 0, 0))
    k_shape = jax.ShapeDtypeStruct((n_tok, g), BF16)
    qt_shape = jax.ShapeDtypeStruct((g, n_tok), BF16)
    vt_shape = jax.ShapeDtypeStruct((n_tok // ATT_TILE, g, ATT_TILE), BF16)
    half = ROPE_DIM // 2
    return pl.pallas_call(
        _in_proj_body,
        grid=(batch, tiles),
        in_specs=[tok(D_MODEL), _resident((1, D_MODEL)), _resident(wt.shape),
                  _resident(wft.shape), tok_t(half), tok_t(half), _resident(gains.shape),
                  _resident(bf.shape)],
        out_specs=[tok_t(g), tok(g), vt_spec, tok_t(g), tok(g), vt_spec, tok_t(LANES),
                   tok(LANES)],
        out_shape=[qt_shape, k_shape, vt_shape, qt_shape, k_shape, vt_shape,
                   jax.ShapeDtypeStruct((LANES, n_tok), BF16),
                   jax.ShapeDtypeStruct((n_tok, LANES), BF16)],
        scratch_shapes=[pltpu.VMEM((LANES, LANES), F32)],
        compiler_params=_params(("arbitrary", "arbitrary")),
        name="in_proj",
    )(x2d, norm_w.reshape(1, -1), wt, wft, cos_t, sin_t, gains, bf)


N_CHAINS = 2 * (GROUP_W // LANES)
SUM_ROWS = 16


def _attn_body(qta_ref, qtb_ref, k_ref, vt_ref, *rest, fox, lam_init, half):
    if fox:
        qauga_ref, qaugb_ref, kaug_ref, o_ref, rhs_scr, s0_scr, s1_scr, m_scr, acc_scr = rest
        qaug_refs = (qauga_ref, qaugb_ref)
    else:
        (lq1_ref, lk1_ref, lq2_ref, lk2_ref, subw_ref, o_ref, rhs_scr, s0_scr, s1_scr, m_scr,
         acc_scr) = rest
    s_scr = (s0_scr, s1_scr)
    s = pl.program_id(1)
    last = 2 * s + half + 1
    width = acc_scr.shape[2] - SUM_ROWS
    ones_rows = jnp.ones((SUM_ROWS, ATT_TILE), BF16)

    half_zero = jnp.zeros((HEAD_DIM, ATT_TILE), BF16)
    for tile, qt_ref in enumerate((qta_ref, qtb_ref)):
        for g in range(GROUP_W // LANES):
            r0 = g * LANES
            rhs_scr[tile, 2 * g, :HEAD_DIM, :] = qt_ref[r0:r0 + HEAD_DIM, :]
            rhs_scr[tile, 2 * g, HEAD_DIM:LANES, :] = half_zero
            rhs_scr[tile, 2 * g + 1, :HEAD_DIM, :] = half_zero
            rhs_scr[tile, 2 * g + 1, HEAD_DIM:LANES, :] = qt_ref[r0 + HEAD_DIM:r0 + LANES, :]
        if fox:
            for c in range(N_CHAINS):
                b0 = c * BIAS_SLOT
                rhs_scr[tile, c, LANES:, :] = jnp.zeros((LANES, ATT_TILE), BF16)
                rhs_scr[tile, c, LANES + b0:LANES + b0 + BIAS_SLOT, :] = (
                    qaug_refs[tile][b0:b0 + BIAS_SLOT, :])
    m_scr[...] = jnp.full(m_scr.shape, NEG_INF, F32)
    acc_scr[...] = jnp.zeros(acc_scr.shape, F32)

    def locate(p):
        in_b = p > s
        return in_b.astype(jnp.int32), jnp.where(in_b, p - s - 1, s - p)

    def scores(p, slot, g):
        tile, blk = locate(p)
        start = pl.multiple_of(blk * ATT_TILE, ATT_TILE)
        lhs = k_ref[pl.ds(start, ATT_TILE), g * LANES:(g + 1) * LANES]
        if fox:
            lhs = jnp.concatenate([lhs, kaug_ref[pl.ds(start, ATT_TILE), :]], axis=-1)
        for c in (2 * g, 2 * g + 1):
            s_scr[slot][c] = _dot(lhs, rhs_scr[tile, c])

    def softmax_pv(p, slot, c, mask):
        tile, blk = locate(p)
        sc = s_scr[slot][c]
        if mask is not None:
            sc = jnp.where(mask, sc, NEG_INF)
        m_old = m_scr[tile, c:c + 1, :]
        m_new = jnp.maximum(m_old, jnp.max(sc, axis=0, keepdims=True))
        alpha = jnp.exp2(m_old - m_new)
        prob = jnp.exp2(sc - m_new)
        m_scr[tile, c:c + 1, :] = m_new
        r0 = c * HEAD_DIM if fox else (c // 2) * LANES
        vt = jnp.concatenate([vt_ref[blk, r0:r0 + width, :], ones_rows], axis=0)
        acc_scr[tile, c] = alpha * acc_scr[tile, c] + _dot(vt, prob.astype(BF16))

    def step(score_args, soft_args):
        for g in range(GROUP_W // LANES):
            if score_args is not None:
                scores(*score_args, g)
            if soft_args is not None:
                p, slot, mask = soft_args
                softmax_pv(p, slot, 2 * g, mask)
                softmax_pv(p, slot, 2 * g + 1, mask)

    div = 1 if fox else CHUNK
    k_id = lax.broadcasted_iota(jnp.int32, (ATT_TILE, ATT_TILE), 0) // div
    q_id = lax.broadcasted_iota(jnp.int32, (ATT_TILE, ATT_TILE), 1) // div
    mask = k_id <= q_id

    def pair(u, carry):
        p = 2 * u + 1
        step((p + 1, 0), (p, 1, None))
        step((p + 2, 1), (p + 1, 0, None))
        return carry

    step((0, 0), None)
    step((1, 1), (0, 0, mask))
    lax.fori_loop(0, (last - 1) // 2, pair, 0)
    step(None, (last, 1, mask))

    if not fox:
        lam = (jnp.exp(jnp.sum(lq1_ref[...] * lk1_ref[...], axis=-1, keepdims=True))
               - jnp.exp(jnp.sum(lq2_ref[...] * lk2_ref[...], axis=-1, keepdims=True))
               + lam_init)
    for tile in range(2):
        for g in range(GROUP_W // LANES):
            a1, a2 = acc_scr[tile, 2 * g], acc_scr[tile, 2 * g + 1]
            o1 = a1[:width] / a1[width:width + 1]
            o2 = a2[:width] / a2[width:width + 1]
            if fox:
                o = jnp.concatenate([o1, o2], axis=0).T
            else:
                o = _rms((o1 - lam * o2).T, subw_ref[...]) * (1.0 - lam_init)
            o_ref[0, tile, 0, :, g * LANES:(g + 1) * LANES] = o.astype(BF16)


def _attention(qt, k, vt, batch, extra, *, fox, lam_init=0.0):
    n_tok = k.shape[0]
    seq = n_tok // batch
    tiles = seq // ATT_TILE
    assert tiles % 2 == 0
    half = tiles // 2
    width = HEAD_DIM if fox else DIFF_V_DIM
    qt_spec = lambda r, off: pl.BlockSpec((r, ATT_TILE), lambda b, s: (0, b * tiles + s + off))
    kv_spec = lambda w: pl.BlockSpec((seq, w), lambda b, s: (b, 0))
    vt_spec = pl.BlockSpec((tiles, GROUP_W, ATT_TILE), lambda b, s: (b, 0, 0))
    if fox:
        qaugt, kaug = extra
        extra = [qaugt, qaugt, kaug]
        extra_specs = [qt_spec(LANES, 0), qt_spec(LANES, half), kv_spec(LANES)]
    else:
        extra_specs = [_resident(e.shape) for e in extra]
    rhs_w = 2 * LANES if fox else LANES
    out = pl.pallas_call(
        functools.partial(_attn_body, fox=fox, lam_init=lam_init, half=half),
        grid=(batch, half),
        in_specs=[qt_spec(GROUP_W, 0), qt_spec(GROUP_W, half), kv_spec(GROUP_W), vt_spec]
        + extra_specs,
        out_specs=pl.BlockSpec((1, 2, 1, ATT_TILE, GROUP_W), lambda b, s: (b, 0, s, 0, 0)),
        out_shape=jax.ShapeDtypeStruct((batch, 2, half, ATT_TILE, GROUP_W), BF16),
        scratch_shapes=[pltpu.VMEM((2, N_CHAINS, rhs_w, ATT_TILE), BF16),
                        pltpu.VMEM((N_CHAINS, ATT_TILE, ATT_TILE), F32),
                        pltpu.VMEM((N_CHAINS, ATT_TILE, ATT_TILE), F32),
                        pltpu.VMEM((2, N_CHAINS, ATT_TILE), F32),
                        pltpu.VMEM((2, N_CHAINS, width + SUM_ROWS, ATT_TILE), F32)],
        compiler_params=_params(("arbitrary", "arbitrary")),
        name="fox_attn" if fox else "diff_attn",
    )(qt, qt, k, vt, *extra)
    return out.reshape(n_tok, GROUP_W)


def _out_proj_body(x_ref, od_ref, of_ref, wo_ref, nw_ref, wq_ref, qw_ref, x1_ref, mq_ref):
    x1 = x_ref[...] + (_dot(od_ref[...], wo_ref[:GROUP_W, :])
                       + _dot(of_ref[...], wo_ref[GROUP_W:, :]))
    x1_ref[...] = x1
    hq = _rms(x1, nw_ref[...]).astype(BF16)
    mq = _dot(hq, wq_ref[...])
    qw = qw_ref[...]
    for c in range(0, D_MODEL, MEM_HEAD_DIM):
        mq_ref[:, c:c + MEM_HEAD_DIM] = _rms(mq[:, c:c + MEM_HEAD_DIM], qw).astype(BF16)


def _out_proj(x2d, od, of, w_out, norm_w, w_mem_q, mem_q_norm_w):
    n_tok = x2d.shape[0]
    tok = lambda width: pl.BlockSpec((WIDE_TILE, width), lambda i: (i, 0))
    scale = MEM_HEAD_DIM ** -0.5
    return pl.pallas_call(
        _out_proj_body,
        grid=(n_tok // WIDE_TILE,),
        in_specs=[tok(D_MODEL), tok(GROUP_W), tok(GROUP_W), _resident((D_MODEL, D_MODEL)),
                  _resident((1, D_MODEL)), _resident((D_MODEL, D_MODEL)),
                  _resident((1, MEM_HEAD_DIM))],
        out_specs=[tok(D_MODEL), tok(D_MODEL)],
        out_shape=[jax.ShapeDtypeStruct((n_tok, D_MODEL), F32),
                   jax.ShapeDtypeStruct((n_tok, D_MODEL), BF16)],
        compiler_params=_params(("arbitrary",)),
        name="out_proj",
    )(x2d, od, of, w_out.astype(BF16), norm_w.reshape(1, -1), w_mem_q.astype(BF16),
      (mem_q_norm_w * scale).reshape(1, -1))


def _mem_kv_body(m_ref, nw_ref, w_ref, kw_ref, mk_ref, mv_ref):
    hm = _rms(m_ref[...], nw_ref[...]).astype(BF16)
    kv = _dot(hm, w_ref[...])
    kw = kw_ref[...]
    for c in range(0, D_MODEL, MEM_HEAD_DIM):
        mk_ref[:, c:c + MEM_HEAD_DIM] = _rms(kv[:, c:c + MEM_HEAD_DIM], kw).astype(BF16)
    mv_ref[...] = kv[:, D_MODEL:].astype(BF16)


def _mem_kv(mem2d, norm_w, w_mem_kv, mem_k_norm_w):
    n_mem = mem2d.shape[0]
    tok = pl.BlockSpec((TOK_TILE, D_MODEL), lambda i: (i, 0))
    out = jax.ShapeDtypeStruct((n_mem, D_MODEL), BF16)
    return pl.pallas_call(
        _mem_kv_body,
        grid=(n_mem // TOK_TILE,),
        in_specs=[tok, _resident((1, D_MODEL)), _resident((D_MODEL, 2 * D_MODEL)),
                  _resident((1, MEM_HEAD_DIM))],
        out_specs=[tok, tok],
        out_shape=[out, out],
        compiler_params=_params(("arbitrary",)),
        name="mem_kv",
    )(mem2d, norm_w.reshape(1, -1), w_mem_kv.astype(BF16), mem_k_norm_w.reshape(1, -1))


def _mem_attn_body(x1_ref, mq_ref, mk_ref, mv_ref, wo_ref, x2_ref):
    mq = mq_ref[...]
    heads = []
    for c in range(0, D_MODEL, MEM_HEAD_DIM):
        s = _dot_nt(mq[:, c:c + MEM_HEAD_DIM], mk_ref[:, c:c + MEM_HEAD_DIM])
        p = jnp.exp(s - jnp.max(s, axis=-1, keepdims=True))
        l = jnp.sum(p, axis=-1, keepdims=True)
        heads.append((_dot(p.astype(BF16), mv_ref[:, c:c + MEM_HEAD_DIM]) / l).astype(BF16))
    mo = jnp.concatenate(heads, axis=-1)
    x2_ref[...] = x1_ref[...] + _dot(mo, wo_ref[...])


def _mem_attn(x1, mq, mk, mv, w_mem_o, batch):
    n_tok = x1.shape[0]
    tiles = n_tok // batch // WIDE_TILE
    mem_len = mk.shape[0] // batch
    tok = pl.BlockSpec((WIDE_TILE, D_MODEL), lambda b, i: (b * tiles + i, 0))
    mem = pl.BlockSpec((mem_len, D_MODEL), lambda b, i: (b, 0))
    return pl.pallas_call(
        _mem_attn_body,
        grid=(batch, tiles),
        in_specs=[tok, tok, mem, mem, _resident((D_MODEL, D_MODEL))],
        out_specs=tok,
        out_shape=jax.ShapeDtypeStruct((n_tok, D_MODEL), F32),
        compiler_params=_params(("arbitrary", "arbitrary")),
        name="mem_attn",
    )(x1, mq, mk, mv, w_mem_o.astype(BF16))


FF_CHUNK = 1024


def _mlp_body(x_ref, nw_ref, wu_ref, wd_ref, o_ref):
    x = x_ref[...]
    h = _rms(x, nw_ref[...]).astype(BF16)
    acc = x
    for c in range(0, D_FF, FF_CHUNK):
        u = jnp.maximum(_dot(h, wu_ref[:, c:c + FF_CHUNK]), 0.0)
        acc = acc + _dot((u * u).astype(BF16), wd_ref[c:c + FF_CHUNK, :])
    o_ref[...] = acc


def _mlp(x2, norm_w, w_up, w_down):
    n_tok = x2.shape[0]
    tok = pl.BlockSpec((WIDE_TILE, D_MODEL), lambda i: (i, 0))
    return pl.pallas_call(
        _mlp_body,
        grid=(n_tok // WIDE_TILE,),
        in_specs=[tok, _resident((1, D_MODEL)), _resident((D_MODEL, D_FF)),
                  _resident((D_FF, D_MODEL))],
        out_specs=tok,
        out_shape=jax.ShapeDtypeStruct((n_tok, D_MODEL), F32),
        compiler_params=_params(("arbitrary",)),
        name="mlp",
    )(x2, norm_w.reshape(1, -1), w_up.astype(BF16), w_down.astype(BF16))


def kernel(x, mem, positions, norm_mix_w, w_in, b_forget, diff_q_norm_w, diff_k_norm_w,
           lambda_q1, lambda_k1, lambda_q2, lambda_k2, diff_subln_w, fox_q_norm_w,
           fox_k_norm_w, w_out, norm_mem_q_w, norm_mem_kv_w, w_mem_q, w_mem_kv,
           mem_q_norm_w, mem_k_norm_w, w_mem_o, norm_mlp_w, w_up, w_down):
    batch, seq, d = x.shape
    depth = w_in.shape[0]
    assert d == D_MODEL and seq % WIDE_TILE == 0 and WIDE_TILE % TOK_TILE == 0
    assert TOK_TILE % ATT_TILE == 0
    assert (batch * mem.shape[1]) % TOK_TILE == 0

    cos_t, sin_t = _rope_tables(positions)
    xc = x.reshape(batch * seq, d)
    mem2d = mem.reshape(-1, d)
    for l in range(depth):
        lam_init = 0.8 - 0.6 * math.exp(-0.3 * l)
        dqt, dk, dvt, fqt, fk, fvt, qaugt, kaug = _in_proj(
            xc, batch, norm_mix_w[l], w_in[l], cos_t, sin_t, diff_q_norm_w[l],
            diff_k_norm_w[l], fox_q_norm_w[l], fox_k_norm_w[l], b_forget[l])
        lam_params = [p[l].reshape(1, -1) for p in
                      (lambda_q1, lambda_k1, lambda_q2, lambda_k2, diff_subln_w)]
        od = _attention(dqt, dk, dvt, batch, lam_params, fox=False, lam_init=lam_init)
        of = _attention(fqt, fk, fvt, batch, [qaugt, kaug], fox=True)
        x1, mq = _out_proj(xc, od, of, w_out[l], norm_mem_q_w[l], w_mem_q[l], mem_q_norm_w[l])
        mk, mv = _mem_kv(mem2d, norm_mem_kv_w[l], w_mem_kv[l], mem_k_norm_w[l])
        x2 = _mem_attn(x1, mq, mk, mv, w_mem_o[l], batch)
        xc = _mlp(x2, norm_mlp_w[l], w_up[l], w_down[l])
    return xc.reshape(batch, seq, d)
```

```python
import functools
import math

import jax
import jax.numpy as jnp
from jax import lax
from jax.experimental import pallas as pl
from jax.experimental.pallas import tpu as pltpu

F32 = jnp.float32
BF16 = jnp.bfloat16

D_MODEL = 1024
CHUNK = 64
HEAD_DIM = 64
N_DIFF_HEADS = 4
DIFF_V_DIM = 2 * HEAD_DIM
N_FOX_HEADS = 8
GROUP_W = 512
ROPE_DIM = HEAD_DIM // 4
ROPE_THETA = 500000.0
N_MEM_HEADS = 4
MEM_HEAD_DIM = D_MODEL // N_MEM_HEADS
D_FF = 4 * D_MODEL
EPS = 1e-6
NEG_INF = -1e30
LOG2E = math.log2(math.e)

LANES = 128
MXU_DIM = 256
VMEM_LIMIT = 56 * 1024 * 1024

TOK_TILE = 1024
WIDE_TILE = 1024
ATT_TILE = 256


def _params(sem):
    return pltpu.CompilerParams(dimension_semantics=sem, vmem_limit_bytes=VMEM_LIMIT)


def _resident(shape):
    return pl.BlockSpec(shape, lambda *_: (0,) * len(shape), pipeline_mode=pl.Buffered(1))


def _rms(x, w):
    ms = jnp.mean(x * x, axis=-1, keepdims=True)
    return (x * lax.rsqrt(ms + EPS)) * w


def _dot(a, b):
    return jnp.dot(a, b, preferred_element_type=F32)


def _dot_nt(a, b):
    return lax.dot_general(a, b, (((1,), (1,)), ((), ())), preferred_element_type=F32)


def _rope_body(pos_ref, freq_ref, cos_ref, sin_ref):
    ang = pos_ref[...] * freq_ref[...]
    cos_ref[...] = jnp.cos(ang)
    sin_ref[...] = jnp.sin(ang)


def _rope_tables(positions):
    n_tok = positions.size
    half = ROPE_DIM // 2
    inv_freq = ROPE_THETA ** (-jnp.arange(0, ROPE_DIM, 2, dtype=F32) / ROPE_DIM)
    out = jax.ShapeDtypeStruct((half, n_tok), F32)
    return pl.pallas_call(_rope_body, out_shape=(out, out), name="rope_tables")(
        positions.reshape(1, n_tok).astype(F32), inv_freq.reshape(half, 1))


def _log_sigmoid(x):
    return jnp.minimum(x, 0.0) - jnp.log1p(jnp.exp(-jnp.abs(x)))


def _split3(x):
    hi = x.astype(BF16)
    rem = x - hi.astype(F32)
    mid = rem.astype(BF16)
    lo = (rem - mid.astype(F32)).astype(BF16)
    return hi, mid, lo


BIAS_SLOT = LANES // N_FOX_HEADS


def _head_norm_t(t, gain, rope):
    half = ROPE_DIM // 2
    outs = []
    for r0 in range(0, t.shape[0], HEAD_DIM):
        th = t[r0:r0 + HEAD_DIM]
        ms = jnp.sum(th * th, axis=0, keepdims=True) * (1.0 / HEAD_DIM)
        th = (th * lax.rsqrt(ms + EPS)) * gain[r0:r0 + HEAD_DIM]
        if rope is not None:
            cos, sin = rope
            t1, t2 = th[:half], th[half:ROPE_DIM]
            th = jnp.concatenate(
                [t1 * cos - t2 * sin, t2 * cos + t1 * sin, th[ROPE_DIM:]], axis=0)
        outs.append(th)
    return jnp.concatenate(outs, axis=0)


def _in_proj_body(x_ref, nw_ref, wt_ref, wft_ref, cos_ref, sin_ref, gain_ref, bf_ref, dqt_ref,
                  dk_ref, dvt_ref, fqt_ref, fk_ref, fvt_ref, qaugt_ref, kaug_ref, carry_ref):
    @pl.when(pl.program_id(1) == 0)
    def _():
        carry_ref[...] = jnp.zeros_like(carry_ref)

    h = _rms(x_ref[...], nw_ref[...]).astype(BF16)
    rope = (cos_ref[...], sin_ref[...])
    g = GROUP_W

    def proj_t(group):
        r0 = group * GROUP_W
        return _dot_nt(wt_ref[r0:r0 + GROUP_W, :], h)

    def store_keys(k_ref, kt):
        for c in range(0, g, LANES):
            k_ref[:, c:c + LANES] = kt[c:c + LANES].T.astype(BF16)

    def store_values(vt_ref, vt):
        for t in range(TOK_TILE // ATT_TILE):
            vt_ref[t] = vt[:, t * ATT_TILE:(t + 1) * ATT_TILE].astype(BF16)

    dqt_ref[...] = _head_norm_t(proj_t(0), gain_ref[0:g], rope).astype(BF16)
    store_keys(dk_ref, _head_norm_t(proj_t(1), gain_ref[g:2 * g], rope))
    store_values(dvt_ref, proj_t(2))
    fqt_ref[...] = _head_norm_t(proj_t(3), gain_ref[2 * g:3 * g], None).astype(BF16)
    store_keys(fk_ref, _head_norm_t(proj_t(4), gain_ref[3 * g:4 * g], None))
    store_values(fvt_ref, proj_t(5))

    log_f = _log_sigmoid(_dot_nt(wft_ref[...], h) + bf_ref[...])
    r = lax.broadcasted_iota(jnp.int32, (MXU_DIM, MXU_DIM), 0)
    c = lax.broadcasted_iota(jnp.int32, (MXU_DIM, MXU_DIM), 1)
    triu = jnp.where(r <= c, 1.0, 0.0).astype(BF16)
    part = lax.broadcasted_iota(jnp.int32, (LANES, MXU_DIM), 0) % BIAS_SLOT
    carry = carry_ref[:, 0:1]
    for c0 in range(0, TOK_TILE, MXU_DIM):
        hi, mid, lo = _split3(log_f[:, c0:c0 + MXU_DIM])
        cum = (_dot(hi, triu) + _dot(mid, triu)) + _dot(lo, triu) + carry
        carry = cum[:, MXU_DIM - 1:MXU_DIM]
        hi, mid, lo = (t.astype(F32) for t in _split3(cum * LOG2E))
        kaug_t = jnp.where(part == 0, -hi, jnp.where(part == 1, -mid, jnp.where(
            part == 2, -lo, jnp.where(part < 6, 1.0, 0.0))))
        qaug_t = jnp.where(part < 3, 1.0, jnp.where(part == 3, hi, jnp.where(
            part == 4, mid, jnp.where(part == 5, lo, 0.0))))
        kaug_ref[c0:c0 + MXU_DIM, :] = kaug_t.T.astype(BF16)
        qaugt_ref[:, c0:c0 + MXU_DIM] = qaug_t.astype(BF16)
    carry_ref[...] = jnp.broadcast_to(carry, carry_ref.shape)


def _in_proj(x2d, batch, norm_w, w_in, cos_t, sin_t, dqw, dkw, fqw, fkw, b_forget):
    n_tok = x2d.shape[0]
    tiles = n_tok // batch // TOK_TILE
    g = GROUP_W
    wt = w_in[:, :6 * g].T.astype(BF16)
    wft = jnp.repeat(w_in[:, 6 * g:6 * g + N_FOX_HEADS].T, BIAS_SLOT, axis=0).astype(BF16)
    bf = jnp.repeat(b_forget, BIAS_SLOT).reshape(LANES, 1)
    qscale = HEAD_DIM ** -0.5 * LOG2E
    gains = jnp.concatenate([jnp.tile(v, g // HEAD_DIM) for v in
                             (dqw * qscale, dkw, fqw * qscale, fkw)]).reshape(4 * g, 1)

    row = lambda b, i: (b * tiles + i, 0)
    col = lambda b, i: (0, b * tiles + i)
    tok = lambda width: pl.BlockSpec((TOK_TILE, width), row)
    tok_t = lambda rows: pl.BlockSpec((rows, TOK_TILE), col)
    slabs = TOK_TILE // ATT_TILE
    vt_spec = pl.BlockSpec((slabs, g, ATT_TILE), lambda b, i: (b * tiles + i, 0, 0))
    k_shape = jax.ShapeDtypeStruct((n_tok, g), BF16)
    qt_shape = jax.ShapeDtypeStruct((g, n_tok), BF16)
    vt_shape = jax.ShapeDtypeStruct((n_tok // ATT_TILE, g, ATT_TILE), BF16)
    half = ROPE_DIM // 2
    return pl.pallas_call(
        _in_proj_body,
        grid=(batch, tiles),
        in_specs=[tok(D_MODEL), _resident((1, D_MODEL)), _resident(wt.shape),
                  _resident(wft.shape), tok_t(half), tok_t(half), _resident(gains.shape),
                  _resident(bf.shape)],
        out_specs=[tok_t(g), tok(g), vt_spec, tok_t(g), tok(g), vt_spec, tok_t(LANES),
                   tok(LANES)],
        out_shape=[qt_shape, k_shape, vt_shape, qt_shape, k_shape, vt_shape,
                   jax.ShapeDtypeStruct((LANES, n_tok), BF16),
                   jax.ShapeDtypeStruct((n_tok, LANES), BF16)],
        scratch_shapes=[pltpu.VMEM((LANES, LANES), F32)],
        compiler_params=_params(("arbitrary", "arbitrary")),
        name="in_proj",
    )(x2d, norm_w.reshape(1, -1), wt, wft, cos_t, sin_t, gains, bf)


N_CHAINS = 2 * (GROUP_W // LANES)
SUM_ROWS = 16


def _attn_body(qta_ref, qtb_ref, k_ref, vt_ref, *rest, fox, lam_init, half):
    if fox:
        qauga_ref, qaugb_ref, kaug_ref, o_ref = rest[:4]
        qaug_refs = (qauga_ref, qaugb_ref)
    else:
        lq1_ref, lk1_ref, lq2_ref, lk2_ref, subw_ref, o_ref = rest[:6]
    rhs_scr, s0_scr, s1_scr, bmax_scr, m_scr, acc_scr = rest[-6:]
    s_scr = (s0_scr, s1_scr)
    s = pl.program_id(1)
    last = 2 * s + half + 1
    width = acc_scr.shape[2] - SUM_ROWS
    ones_rows = jnp.ones((SUM_ROWS, ATT_TILE), BF16)

    half_zero = jnp.zeros((HEAD_DIM, ATT_TILE), BF16)
    for tile, qt_ref in enumerate((qta_ref, qtb_ref)):
        for g in range(GROUP_W // LANES):
            r0 = g * LANES
            rhs_scr[tile, 2 * g, :HEAD_DIM, :] = qt_ref[r0:r0 + HEAD_DIM, :]
            rhs_scr[tile, 2 * g, HEAD_DIM:LANES, :] = half_zero
            rhs_scr[tile, 2 * g + 1, :HEAD_DIM, :] = half_zero
            rhs_scr[tile, 2 * g + 1, HEAD_DIM:LANES, :] = qt_ref[r0 + HEAD_DIM:r0 + LANES, :]
        if fox:
            for c in range(N_CHAINS):
                b0 = c * BIAS_SLOT
                rhs_scr[tile, c, LANES:, :] = jnp.zeros((LANES, ATT_TILE), BF16)
                rhs_scr[tile, c, LANES + b0:LANES + b0 + BIAS_SLOT, :] = (
                    qaug_refs[tile][b0:b0 + BIAS_SLOT, :])
    m_scr[...] = jnp.full(m_scr.shape, NEG_INF, F32)
    acc_scr[...] = jnp.zeros(acc_scr.shape, F32)

    def locate(p):
        in_b = p > s
        return in_b.astype(jnp.int32), jnp.where(in_b, p - s - 1, s - p)

    def scores(p, slot, g, mask):
        tile, blk = locate(p)
        start = pl.multiple_of(blk * ATT_TILE, ATT_TILE)
        lhs = k_ref[pl.ds(start, ATT_TILE), g * LANES:(g + 1) * LANES]
        if fox:
            lhs = jnp.concatenate([lhs, kaug_ref[pl.ds(start, ATT_TILE), :]], axis=-1)
        for c in (2 * g, 2 * g + 1):
            sc = _dot(lhs, rhs_scr[tile, c])
            if mask is not None:
                sc = jnp.where(mask, sc, NEG_INF)
            s_scr[slot][c] = sc
            bmax_scr[slot, c:c + 1, :] = jnp.max(sc, axis=0, keepdims=True)

    def softmax_pv(p, slot, c):
        tile, blk = locate(p)
        sc = s_scr[slot][c]
        m_old = m_scr[tile, c:c + 1, :]
        m_new = jnp.maximum(m_old, bmax_scr[slot, c:c + 1, :])
        alpha = jnp.exp2(m_old - m_new)
        prob = jnp.exp2(sc - m_new)
        m_scr[tile, c:c + 1, :] = m_new
        r0 = c * HEAD_DIM if fox else (c // 2) * LANES
        vt = jnp.concatenate([vt_ref[blk, r0:r0 + width, :], ones_rows], axis=0)
        acc_scr[tile, c] = alpha * acc_scr[tile, c] + _dot(vt, prob.astype(BF16))

    def step(score_args, soft_args, mask=None):
        for g in range(GROUP_W // LANES):
            if score_args is not None:
                scores(*score_args, g, mask)
            if soft_args is not None:
                softmax_pv(*soft_args, 2 * g)
                softmax_pv(*soft_args, 2 * g + 1)

    div = 1 if fox else CHUNK
    k_id = lax.broadcasted_iota(jnp.int32, (ATT_TILE, ATT_TILE), 0) // div
    q_id = lax.broadcasted_iota(jnp.int32, (ATT_TILE, ATT_TILE), 1) // div
    diag = k_id <= q_id

    def pair(u, carry):
        p = 2 * u + 1
        step((p + 1, 0), (p, 1))
        step((p + 2, 1), (p + 1, 0))
        return carry

    step((0, 0), None, diag)
    step((1, 1), (0, 0))
    lax.fori_loop(0, (last - 3) // 2, pair, 0)
    step((last - 1, 0), (last - 2, 1))
    step((last, 1), (last - 1, 0), diag)
    step(None, (last, 1))

    if not fox:
        lam = (jnp.exp(jnp.sum(lq1_ref[...] * lk1_ref[...], axis=-1, keepdims=True))
               - jnp.exp(jnp.sum(lq2_ref[...] * lk2_ref[...], axis=-1, keepdims=True))
               + lam_init)
    for tile in range(2):
        for g in range(GROUP_W // LANES):
            a1, a2 = acc_scr[tile, 2 * g], acc_scr[tile, 2 * g + 1]
            o1 = a1[:width] / a1[width:width + 1]
            o2 = a2[:width] / a2[width:width + 1]
            if fox:
                o = jnp.concatenate([o1, o2], axis=0).T
            else:
                o = _rms((o1 - lam * o2).T, subw_ref[...]) * (1.0 - lam_init)
            o_ref[0, tile, 0, :, g * LANES:(g + 1) * LANES] = o.astype(BF16)


def _attention(qt, k, vt, batch, extra, *, fox, lam_init=0.0):
    n_tok = k.shape[0]
    seq = n_tok // batch
    tiles = seq // ATT_TILE
    assert tiles % 2 == 0
    half = tiles // 2
    width = HEAD_DIM if fox else DIFF_V_DIM
    qt_spec = lambda r, off: pl.BlockSpec((r, ATT_TILE), lambda b, s: (0, b * tiles + s + off))
    kv_spec = lambda w: pl.BlockSpec((seq, w), lambda b, s: (b, 0))
    vt_spec = pl.BlockSpec((tiles, GROUP_W, ATT_TILE), lambda b, s: (b, 0, 0))
    if fox:
        qaugt, kaug = extra
        extra = [qaugt, qaugt, kaug]
        extra_specs = [qt_spec(LANES, 0), qt_spec(LANES, half), kv_spec(LANES)]
    else:
        extra_specs = [_resident(e.shape) for e in extra]
    rhs_w = 2 * LANES if fox else LANES
    out = pl.pallas_call(
        functools.partial(_attn_body, fox=fox, lam_init=lam_init, half=half),
        grid=(batch, half),
        in_specs=[qt_spec(GROUP_W, 0), qt_spec(GROUP_W, half), kv_spec(GROUP_W), vt_spec]
        + extra_specs,
        out_specs=pl.BlockSpec((1, 2, 1, ATT_TILE, GROUP_W), lambda b, s: (b, 0, s, 0, 0)),
        out_shape=jax.ShapeDtypeStruct((batch, 2, half, ATT_TILE, GROUP_W), BF16),
        scratch_shapes=[pltpu.VMEM((2, N_CHAINS, rhs_w, ATT_TILE), BF16),
                        pltpu.VMEM((N_CHAINS, ATT_TILE, ATT_TILE), F32),
                        pltpu.VMEM((N_CHAINS, ATT_TILE, ATT_TILE), F32),
                        pltpu.VMEM((2, N_CHAINS, ATT_TILE), F32),
                        pltpu.VMEM((2, N_CHAINS, ATT_TILE), F32),
                        pltpu.VMEM((2, N_CHAINS, width + SUM_ROWS, ATT_TILE), F32)],
        compiler_params=_params(("arbitrary", "arbitrary")),
        name="fox_attn" if fox else "diff_attn",
    )(qt, qt, k, vt, *extra)
    return out.reshape(n_tok, GROUP_W)


def _out_proj_body(x_ref, od_ref, of_ref, wo_ref, nw_ref, wq_ref, qw_ref, x1_ref, mq_ref):
    x1 = x_ref[...] + (_dot(od_ref[...], wo_ref[:GROUP_W, :])
                       + _dot(of_ref[...], wo_ref[GROUP_W:, :]))
    x1_ref[...] = x1
    hq = _rms(x1, nw_ref[...]).astype(BF16)
    mq = _dot(hq, wq_ref[...])
    qw = qw_ref[...]
    for c in range(0, D_MODEL, MEM_HEAD_DIM):
        mq_ref[:, c:c + MEM_HEAD_DIM] = _rms(mq[:, c:c + MEM_HEAD_DIM], qw).astype(BF16)


def _out_proj(x2d, od, of, w_out, norm_w, w_mem_q, mem_q_norm_w):
    n_tok = x2d.shape[0]
    tok = lambda width: pl.BlockSpec((WIDE_TILE, width), lambda i: (i, 0))
    scale = MEM_HEAD_DIM ** -0.5
    return pl.pallas_call(
        _out_proj_body,
        grid=(n_tok // WIDE_TILE,),
        in_specs=[tok(D_MODEL), tok(GROUP_W), tok(GROUP_W), _resident((D_MODEL, D_MODEL)),
                  _resident((1, D_MODEL)), _resident((D_MODEL, D_MODEL)),
                  _resident((1, MEM_HEAD_DIM))],
        out_specs=[tok(D_MODEL), tok(D_MODEL)],
        out_shape=[jax.ShapeDtypeStruct((n_tok, D_MODEL), F32),
                   jax.ShapeDtypeStruct((n_tok, D_MODEL), BF16)],
        compiler_params=_params(("arbitrary",)),
        name="out_proj",
    )(x2d, od, of, w_out.astype(BF16), norm_w.reshape(1, -1), w_mem_q.astype(BF16),
      (mem_q_norm_w * scale).reshape(1, -1))


def _mem_kv_body(m_ref, nw_ref, w_ref, kw_ref, mk_ref, mv_ref):
    hm = _rms(m_ref[...], nw_ref[...]).astype(BF16)
    kv = _dot(hm, w_ref[...])
    kw = kw_ref[...]
    for c in range(0, D_MODEL, MEM_HEAD_DIM):
        mk_ref[:, c:c + MEM_HEAD_DIM] = _rms(kv[:, c:c + MEM_HEAD_DIM], kw).astype(BF16)
    mv_ref[...] = kv[:, D_MODEL:].astype(BF16)


def _mem_kv(mem2d, norm_w, w_mem_kv, mem_k_norm_w):
    n_mem = mem2d.shape[0]
    tok = pl.BlockSpec((TOK_TILE, D_MODEL), lambda i: (i, 0))
    out = jax.ShapeDtypeStruct((n_mem, D_MODEL), BF16)
    return pl.pallas_call(
        _mem_kv_body,
        grid=(n_mem // TOK_TILE,),
        in_specs=[tok, _resident((1, D_MODEL)), _resident((D_MODEL, 2 * D_MODEL)),
                  _resident((1, MEM_HEAD_DIM))],
        out_specs=[tok, tok],
        out_shape=[out, out],
        compiler_params=_params(("arbitrary",)),
        name="mem_kv",
    )(mem2d, norm_w.reshape(1, -1), w_mem_kv.astype(BF16), mem_k_norm_w.reshape(1, -1))


def _mem_attn_body(x1_ref, mq_ref, mk_ref, mv_ref, wo_ref, x2_ref):
    mq = mq_ref[...]
    heads = []
    for c in range(0, D_MODEL, MEM_HEAD_DIM):
        s = _dot_nt(mq[:, c:c + MEM_HEAD_DIM], mk_ref[:, c:c + MEM_HEAD_DIM])
        p = jnp.exp(s - jnp.max(s, axis=-1, keepdims=True))
        l = jnp.sum(p, axis=-1, keepdims=True)
        heads.append((_dot(p.astype(BF16), mv_ref[:, c:c + MEM_HEAD_DIM]) / l).astype(BF16))
    mo = jnp.concatenate(heads, axis=-1)
    x2_ref[...] = x1_ref[...] + _dot(mo, wo_ref[...])


def _mem_attn(x1, mq, mk, mv, w_mem_o, batch):
    n_tok = x1.shape[0]
    tiles = n_tok // batch // WIDE_TILE
    mem_len = mk.shape[0] // batch
    tok = pl.BlockSpec((WIDE_TILE, D_MODEL), lambda b, i: (b * tiles + i, 0))
    mem = pl.BlockSpec((mem_len, D_MODEL), lambda b, i: (b, 0))
    return pl.pallas_call(
        _mem_attn_body,
        grid=(batch, tiles),
        in_specs=[tok, tok, mem, mem, _resident((D_MODEL, D_MODEL))],
        out_specs=tok,
        out_shape=jax.ShapeDtypeStruct((n_tok, D_MODEL), F32),
        compiler_params=_params(("arbitrary", "arbitrary")),
        name="mem_attn",
    )(x1, mq, mk, mv, w_mem_o.astype(BF16))


FF_CHUNK = 1024


def _mlp_body(x_ref, nw_ref, wu_ref, wd_ref, o_ref):
    x = x_ref[...]
    h = _rms(x, nw_ref[...]).astype(BF16)
    acc = x
    for c in range(0, D_FF, FF_CHUNK):
        u = jnp.maximum(_dot(h, wu_ref[:, c:c + FF_CHUNK]), 0.0)
        acc = acc + _dot((u * u).astype(BF16), wd_ref[c:c + FF_CHUNK, :])
    o_ref[...] = acc


def _mlp(x2, norm_w, w_up, w_down):
    n_tok = x2.shape[0]
    tok = pl.BlockSpec((WIDE_TILE, D_MODEL), lambda i: (i, 0))
    return pl.pallas_call(
        _mlp_body,
        grid=(n_tok // WIDE_TILE,),
        in_specs=[tok, _resident((1, D_MODEL)), _resident((D_MODEL, D_FF)),
                  _resident((D_FF, D_MODEL))],
        out_specs=tok,
        out_shape=jax.ShapeDtypeStruct((n_tok, D_MODEL), F32),
        compiler_params=_params(("arbitrary",)),
        name="mlp",
    )(x2, norm_w.reshape(1, -1), w_up.astype(BF16), w_down.astype(BF16))


def kernel(x, mem, positions, norm_mix_w, w_in, b_forget, diff_q_norm_w, diff_k_norm_w,
           lambda_q1, lambda_k1, lambda_q2, lambda_k2, diff_subln_w, fox_q_norm_w,
           fox_k_norm_w, w_out, norm_mem_q_w, norm_mem_kv_w, w_mem_q, w_mem_kv,
           mem_q_norm_w, mem_k_norm_w, w_mem_o, norm_mlp_w, w_up, w_down):
    batch, seq, d = x.shape
    depth = w_in.shape[0]
    assert d == D_MODEL and seq % WIDE_TILE == 0 and WIDE_TILE % TOK_TILE == 0
    assert TOK_TILE % ATT_TILE == 0
    assert (batch * mem.shape[1]) % TOK_TILE == 0

    cos_t, sin_t = _rope_tables(positions)
    xc = x.reshape(batch * seq, d)
    mem2d = mem.reshape(-1, d)
    for l in range(depth):
        lam_init = 0.8 - 0.6 * math.exp(-0.3 * l)
        dqt, dk, dvt, fqt, fk, fvt, qaugt, kaug = _in_proj(
            xc, batch, norm_mix_w[l], w_in[l], cos_t, sin_t, diff_q_norm_w[l],
            diff_k_norm_w[l], fox_q_norm_w[l], fox_k_norm_w[l], b_forget[l])
        lam_params = [p[l].reshape(1, -1) for p in
                      (lambda_q1, lambda_k1, lambda_q2, lambda_k2, diff_subln_w)]
        od = _attention(dqt, dk, dvt, batch, lam_params, fox=False, lam_init=lam_init)
        of = _attention(fqt, fk, fvt, batch, [qaugt, kaug], fox=True)
        x1, mq = _out_proj(xc, od, of, w_out[l], norm_mem_q_w[l], w_mem_q[l], mem_q_norm_w[l])
        mk, mv = _mem_kv(mem2d, norm_mem_kv_w[l], w_mem_kv[l], mem_k_norm_w[l])
        x2 = _mem_attn(x1, mq, mk, mv, w_mem_o[l], batch)
        xc = _mlp(x2, norm_mlp_w[l], w_up[l], w_down[l])
    return xc.reshape(batch, seq, d)
```

```python
import functools
import math

import jax
import jax.numpy as jnp
from jax import lax
from jax.experimental import pallas as pl
from jax.experimental.pallas import tpu as pltpu

F32 = jnp.float32
BF16 = jnp.bfloat16

D_MODEL = 1024
CHUNK = 64
HEAD_DIM = 64
N_DIFF_HEADS = 4
DIFF_V_DIM = 2 * HEAD_DIM
N_FOX_HEADS = 8
GROUP_W = 512
ROPE_DIM = HEAD_DIM // 4
ROPE_THETA = 500000.0
N_MEM_HEADS = 4
MEM_HEAD_DIM = D_MODEL // N_MEM_HEADS
D_FF = 4 * D_MODEL
EPS = 1e-6
NEG_INF = -1e30
LOG2E = math.log2(math.e)

LANES = 128
MXU_DIM = 256
VMEM_LIMIT = 56 * 1024 * 1024

TOK_TILE = 1024
WIDE_TILE = 1024
ATT_TILE = 256


def _params(sem):
    return pltpu.CompilerParams(dimension_semantics=sem, vmem_limit_bytes=VMEM_LIMIT)


def _resident(shape):
    return pl.BlockSpec(shape, lambda *_: (0,) * len(shape), pipeline_mode=pl.Buffered(1))


def _rms(x, w):
    ms = jnp.mean(x * x, axis=-1, keepdims=True)
    return (x * lax.rsqrt(ms + EPS)) * w


def _dot(a, b):
    return jnp.dot(a, b, preferred_element_type=F32)


def _dot_nt(a, b):
    return lax.dot_general(a, b, (((1,), (1,)), ((), ())), preferred_element_type=F32)


def _rope_body(pos_ref, freq_ref, cos_ref, sin_ref):
    ang = pos_ref[...] * freq_ref[...]
    cos_ref[...] = jnp.cos(ang)
    sin_ref[...] = jnp.sin(ang)


def _rope_tables(positions):
    n_tok = positions.size
    half = ROPE_DIM // 2
    inv_freq = ROPE_THETA ** (-jnp.arange(0, ROPE_DIM, 2, dtype=F32) / ROPE_DIM)
    out = jax.ShapeDtypeStruct((half, n_tok), F32)
    return pl.pallas_call(_rope_body, out_shape=(out, out), name="rope_tables")(
        positions.reshape(1, n_tok).astype(F32), inv_freq.reshape(half, 1))


def _log_sigmoid(x):
    return jnp.minimum(x, 0.0) - jnp.log1p(jnp.exp(-jnp.abs(x)))


def _split3(x):
    hi = x.astype(BF16)
    rem = x - hi.astype(F32)
    mid = rem.astype(BF16)
    lo = (rem - mid.astype(F32)).astype(BF16)
    return hi, mid, lo


BIAS_SLOT = LANES // N_FOX_HEADS


def _head_norm_t(t, gain, rope):
    half = ROPE_DIM // 2
    outs = []
    for r0 in range(0, t.shape[0], HEAD_DIM):
        th = t[r0:r0 + HEAD_DIM]
        ms = jnp.sum(th * th, axis=0, keepdims=True) * (1.0 / HEAD_DIM)
        th = (th * lax.rsqrt(ms + EPS)) * gain[r0:r0 + HEAD_DIM]
        if rope is not None:
            cos, sin = rope
            t1, t2 = th[:half], th[half:ROPE_DIM]
            th = jnp.concatenate(
                [t1 * cos - t2 * sin, t2 * cos + t1 * sin, th[ROPE_DIM:]], axis=0)
        outs.append(th)
    return jnp.concatenate(outs, axis=0)


def _in_proj_body(x_ref, nw_ref, wt_ref, wft_ref, cos_ref, sin_ref, gain_ref, bf_ref, dqt_ref,
                  dk_ref, dvt_ref, fqt_ref, fk_ref, fvt_ref, qaugt_ref, kaug_ref, carry_ref):
    @pl.when(pl.program_id(1) == 0)
    def _():
        carry_ref[...] = jnp.zeros_like(carry_ref)

    h = _rms(x_ref[...], nw_ref[...]).astype(BF16)
    rope = (cos_ref[...], sin_ref[...])
    g = GROUP_W

    def proj_t(group):
        r0 = group * GROUP_W
        return _dot_nt(wt_ref[r0:r0 + GROUP_W, :], h)

    def store_keys(k_ref, kt):
        for c in range(0, g, LANES):
            k_ref[:, c:c + LANES] = kt[c:c + LANES].T.astype(BF16)

    def store_values(vt_ref, vt):
        for t in range(TOK_TILE // ATT_TILE):
            vt_ref[t] = vt[:, t * ATT_TILE:(t + 1) * ATT_TILE].astype(BF16)

    dqt_ref[...] = _head_norm_t(proj_t(0), gain_ref[0:g], rope).astype(BF16)
    store_keys(dk_ref, _head_norm_t(proj_t(1), gain_ref[g:2 * g], rope))
    store_values(dvt_ref, proj_t(2))
    fqt_ref[...] = _head_norm_t(proj_t(3), gain_ref[2 * g:3 * g], None).astype(BF16)
    store_keys(fk_ref, _head_norm_t(proj_t(4), gain_ref[3 * g:4 * g], None))
    store_values(fvt_ref, proj_t(5))

    log_f = _log_sigmoid(_dot_nt(wft_ref[...], h) + bf_ref[...])
    r = lax.broadcasted_iota(jnp.int32, (MXU_DIM, MXU_DIM), 0)
    c = lax.broadcasted_iota(jnp.int32, (MXU_DIM, MXU_DIM), 1)
    triu = jnp.where(r <= c, 1.0, 0.0).astype(BF16)
    part = lax.broadcasted_iota(jnp.int32, (LANES, MXU_DIM), 0) % BIAS_SLOT
    carry = carry_ref[:, 0:1]
    for c0 in range(0, TOK_TILE, MXU_DIM):
        hi, mid, lo = _split3(log_f[:, c0:c0 + MXU_DIM])
        cum = (_dot(hi, triu) + _dot(mid, triu)) + _dot(lo, triu) + carry
        carry = cum[:, MXU_DIM - 1:MXU_DIM]
        hi, mid, lo = (t.astype(F32) for t in _split3(cum * LOG2E))
        kaug_t = jnp.where(part == 0, -hi, jnp.where(part == 1, -mid, jnp.where(
            part == 2, -lo, jnp.where(part < 6, 1.0, 0.0))))
        qaug_t = jnp.where(part < 3, 1.0, jnp.where(part == 3, hi, jnp.where(
            part == 4, mid, jnp.where(part == 5, lo, 0.0))))
        kaug_ref[c0:c0 + MXU_DIM, :] = kaug_t.T.astype(BF16)
        qaugt_ref[:, c0:c0 + MXU_DIM] = qaug_t.astype(BF16)
    carry_ref[...] = jnp.broadcast_to(carry, carry_ref.shape)


def _in_proj(x2d, batch, norm_w, w_in, cos_t, sin_t, dqw, dkw, fqw, fkw, b_forget):
    n_tok = x2d.shape[0]
    tiles = n_tok // batch // TOK_TILE
    g = GROUP_W
    wt = w_in[:, :6 * g].T.astype(BF16)
    wft = jnp.repeat(w_in[:, 6 * g:6 * g + N_FOX_HEADS].T, BIAS_SLOT, axis=0).astype(BF16)
    bf = jnp.repeat(b_forget, BIAS_SLOT).reshape(LANES, 1)
    qscale = HEAD_DIM ** -0.5 * LOG2E
    gains = jnp.concatenate([jnp.tile(v, g // HEAD_DIM) for v in
                             (dqw * qscale, dkw, fqw * qscale, fkw)]).reshape(4 * g, 1)

    row = lambda b, i: (b * tiles + i, 0)
    col = lambda b, i: (0, b * tiles + i)
    tok = lambda width: pl.BlockSpec((TOK_TILE, width), row)
    tok_t = lambda rows: pl.BlockSpec((rows, TOK_TILE), col)
    slabs = TOK_TILE // ATT_TILE
    vt_spec = pl.BlockSpec((slabs, g, ATT_TILE), lambda b, i: (b * tiles + i, 0, 0))
    k_shape = jax.ShapeDtypeStruct((n_tok, g), BF16)
    qt_shape = jax.ShapeDtypeStruct((g, n_tok), BF16)
    vt_shape = jax.ShapeDtypeStruct((n_tok // ATT_TILE, g, ATT_TILE), BF16)
    half = ROPE_DIM // 2
    return pl.pallas_call(
        _in_proj_body,
        grid=(batch, tiles),
        in_specs=[tok(D_MODEL), _resident((1, D_MODEL)), _resident(wt.shape),
                  _resident(wft.shape), tok_t(half), tok_t(half), _resident(gains.shape),
                  _resident(bf.shape)],
        out_specs=[tok_t(g), tok(g), vt_spec, tok_t(g), tok(g), vt_spec, tok_t(LANES),
                   tok(LANES)],
        out_shape=[qt_shape, k_shape, vt_shape, qt_shape, k_shape, vt_shape,
                   jax.ShapeDtypeStruct((LANES, n_tok), BF16),
                   jax.ShapeDtypeStruct((n_tok, LANES), BF16)],
        scratch_shapes=[pltpu.VMEM((LANES, LANES), F32)],
        compiler_params=_params(("arbitrary", "arbitrary")),
        name="in_proj",
    )(x2d, norm_w.reshape(1, -1), wt, wft, cos_t, sin_t, gains, bf)


N_CHAINS = 2 * (GROUP_W // LANES)
SUM_ROWS = 16


def _attn_body(qta_ref, qtb_ref, k_ref, vt_ref, *rest, fox, lam_init, half):
    if fox:
        qauga_ref, qaugb_ref, kaug_ref, o_ref = rest[:4]
        qaug_refs = (qauga_ref, qaugb_ref)
    else:
        lq1_ref, lk1_ref, lq2_ref, lk2_ref, subw_ref, o_ref = rest[:6]
    rhs_scr, s0_scr, s1_scr, bmax_scr, m_scr, acc_scr = rest[-6:]
    s_scr = (s0_scr, s1_scr)
    s = pl.program_id(1)
    last = 2 * s + half + 1
    width = acc_scr.shape[2] - SUM_ROWS
    ones_rows = jnp.ones((SUM_ROWS, ATT_TILE), BF16)

    half_zero = jnp.zeros((HEAD_DIM, ATT_TILE), BF16)
    for tile, qt_ref in enumerate((qta_ref, qtb_ref)):
        for g in range(GROUP_W // LANES):
            r0 = g * LANES
            rhs_scr[tile, 2 * g, :HEAD_DIM, :] = qt_ref[r0:r0 + HEAD_DIM, :]
            rhs_scr[tile, 2 * g, HEAD_DIM:LANES, :] = half_zero
            rhs_scr[tile, 2 * g + 1, :HEAD_DIM, :] = half_zero
            rhs_scr[tile, 2 * g + 1, HEAD_DIM:LANES, :] = qt_ref[r0 + HEAD_DIM:r0 + LANES, :]
        if fox:
            for c in range(N_CHAINS):
                b0 = c * BIAS_SLOT
                rhs_scr[tile, c, LANES:, :] = jnp.zeros((LANES, ATT_TILE), BF16)
                rhs_scr[tile, c, LANES + b0:LANES + b0 + BIAS_SLOT, :] = (
                    qaug_refs[tile][b0:b0 + BIAS_SLOT, :])
    m_scr[...] = jnp.full(m_scr.shape, NEG_INF, F32)
    acc_scr[...] = jnp.zeros(acc_scr.shape, F32)

    def locate(p):
        in_b = p > s
        return in_b.astype(jnp.int32), jnp.where(in_b, p - s - 1, s - p)

    def scores(p, slot, g, mask):
        tile, blk = locate(p)
        start = pl.multiple_of(blk * ATT_TILE, ATT_TILE)
        lhs = k_ref[pl.ds(start, ATT_TILE), g * LANES:(g + 1) * LANES]
        if fox:
            lhs = jnp.concatenate([lhs, kaug_ref[pl.ds(start, ATT_TILE), :]], axis=-1)
        for c in (2 * g, 2 * g + 1):
            sc = _dot(lhs, rhs_scr[tile, c])
            if mask is not None:
                sc = jnp.where(mask, sc, NEG_INF)
            s_scr[slot][c] = sc
            bmax_scr[slot, c:c + 1, :] = jnp.max(sc, axis=0, keepdims=True)

    def softmax_pv(p, slot, c):
        tile, blk = locate(p)
        sc = s_scr[slot][c]
        m_old = m_scr[tile, c:c + 1, :]
        m_new = jnp.maximum(m_old, bmax_scr[slot, c:c + 1, :])
        alpha = jnp.exp2(m_old - m_new)
        prob = jnp.exp2(sc - m_new)
        m_scr[tile, c:c + 1, :] = m_new
        r0 = c * HEAD_DIM if fox else (c // 2) * LANES
        vt = jnp.concatenate([vt_ref[blk, r0:r0 + width, :], ones_rows], axis=0)
        acc_scr[tile, c] = alpha * acc_scr[tile, c] + _dot(vt, prob.astype(BF16))

    def step(score_args, soft_args, mask=None):
        for g in range(GROUP_W // LANES):
            if score_args is not None:
                scores(*score_args, g, mask)
            if soft_args is not None:
                softmax_pv(*soft_args, 2 * g)
                softmax_pv(*soft_args, 2 * g + 1)

    div = 1 if fox else CHUNK
    k_id = lax.broadcasted_iota(jnp.int32, (ATT_TILE, ATT_TILE), 0) // div
    q_id = lax.broadcasted_iota(jnp.int32, (ATT_TILE, ATT_TILE), 1) // div
    diag = k_id <= q_id

    def pair(u, carry):
        p = 2 * u + 1
        step((p + 1, 0), (p, 1))
        step((p + 2, 1), (p + 1, 0))
        return carry

    step((0, 0), None, diag)
    step((1, 1), (0, 0))
    lax.fori_loop(0, (last - 3) // 2, pair, 0)
    step((last - 1, 0), (last - 2, 1))
    step((last, 1), (last - 1, 0), diag)
    step(None, (last, 1))

    if not fox:
        lam = (jnp.exp(jnp.sum(lq1_ref[...] * lk1_ref[...], axis=-1, keepdims=True))
               - jnp.exp(jnp.sum(lq2_ref[...] * lk2_ref[...], axis=-1, keepdims=True))
               + lam_init)
    for tile in range(2):
        for g in range(GROUP_W // LANES):
            a1, a2 = acc_scr[tile, 2 * g], acc_scr[tile, 2 * g + 1]
            o1 = a1[:width] / a1[width:width + 1]
            o2 = a2[:width] / a2[width:width + 1]
            if fox:
                o = jnp.concatenate([o1, o2], axis=0).T
            else:
                o = _rms((o1 - lam * o2).T, subw_ref[...]) * (1.0 - lam_init)
            o_ref[0, tile, 0, :, g * LANES:(g + 1) * LANES] = o.astype(BF16)


def _attention(qt, k, vt, batch, extra, *, fox, lam_init=0.0):
    n_tok = k.shape[0]
    seq = n_tok // batch
    tiles = seq // ATT_TILE
    assert tiles % 2 == 0
    half = tiles // 2
    width = HEAD_DIM if fox else DIFF_V_DIM
    qt_spec = lambda r, off: pl.BlockSpec((r, ATT_TILE), lambda b, s: (0, b * tiles + s + off))
    kv_spec = lambda w: pl.BlockSpec((seq, w), lambda b, s: (b, 0))
    vt_spec = pl.BlockSpec((tiles, GROUP_W, ATT_TILE), lambda b, s: (b, 0, 0))
    if fox:
        qaugt, kaug = extra
        extra = [qaugt, qaugt, kaug]
        extra_specs = [qt_spec(LANES, 0), qt_spec(LANES, half), kv_spec(LANES)]
    else:
        extra_specs = [_resident(e.shape) for e in extra]
    rhs_w = 2 * LANES if fox else LANES
    out = pl.pallas_call(
        functools.partial(_attn_body, fox=fox, lam_init=lam_init, half=half),
        grid=(batch, half),
        in_specs=[qt_spec(GROUP_W, 0), qt_spec(GROUP_W, half), kv_spec(GROUP_W), vt_spec]
        + extra_specs,
        out_specs=pl.BlockSpec((1, 2, 1, ATT_TILE, GROUP_W), lambda b, s: (b, 0, s, 0, 0)),
        out_shape=jax.ShapeDtypeStruct((batch, 2, half, ATT_TILE, GROUP_W), BF16),
        scratch_shapes=[pltpu.VMEM((2, N_CHAINS, rhs_w, ATT_TILE), BF16),
                        pltpu.VMEM((N_CHAINS, ATT_TILE, ATT_TILE), F32),
                        pltpu.VMEM((N_CHAINS, ATT_TILE, ATT_TILE), F32),
                        pltpu.VMEM((2, N_CHAINS, ATT_TILE), F32),
                        pltpu.VMEM((2, N_CHAINS, ATT_TILE), F32),
                        pltpu.VMEM((2, N_CHAINS, width + SUM_ROWS, ATT_TILE), F32)],
        compiler_params=_params(("arbitrary", "arbitrary")),
        name="fox_attn" if fox else "diff_attn",
    )(qt, qt, k, vt, *extra)
    return out.reshape(n_tok, GROUP_W)


def _out_proj_body(x_ref, od_ref, of_ref, wo_ref, nw_ref, wq_ref, qw_ref, x1_ref, mq_ref):
    x1 = x_ref[...] + (_dot(od_ref[...], wo_ref[:GROUP_W, :])
                       + _dot(of_ref[...], wo_ref[GROUP_W:, :]))
    x1_ref[...] = x1
    hq = _rms(x1, nw_ref[...]).astype(BF16)
    mq = _dot(hq, wq_ref[...])
    qw = qw_ref[...]
    for c in range(0, D_MODEL, MEM_HEAD_DIM):
        mq_ref[:, c:c + MEM_HEAD_DIM] = _rms(mq[:, c:c + MEM_HEAD_DIM], qw).astype(BF16)


def _out_proj(x2d, od, of, w_out, norm_w, w_mem_q, mem_q_norm_w):
    n_tok = x2d.shape[0]
    tok = lambda width: pl.BlockSpec((WIDE_TILE, width), lambda i: (i, 0))
    scale = MEM_HEAD_DIM ** -0.5
    return pl.pallas_call(
        _out_proj_body,
        grid=(n_tok // WIDE_TILE,),
        in_specs=[tok(D_MODEL), tok(GROUP_W), tok(GROUP_W), _resident((D_MODEL, D_MODEL)),
                  _resident((1, D_MODEL)), _resident((D_MODEL, D_MODEL)),
                  _resident((1, MEM_HEAD_DIM))],
        out_specs=[tok(D_MODEL), tok(D_MODEL)],
        out_shape=[jax.ShapeDtypeStruct((n_tok, D_MODEL), F32),
                   jax.ShapeDtypeStruct((n_tok, D_MODEL), BF16)],
        compiler_params=_params(("arbitrary",)),
        name="out_proj",
    )(x2d, od, of, w_out.astype(BF16), norm_w.reshape(1, -1), w_mem_q.astype(BF16),
      (mem_q_norm_w * scale).reshape(1, -1))


def _mem_kv_body(m_ref, nw_ref, w_ref, kw_ref, mk_ref, mv_ref):
    hm = _rms(m_ref[...], nw_ref[...]).astype(BF16)
    kv = _dot(hm, w_ref[...])
    kw = kw_ref[...]
    for c in range(0, D_MODEL, MEM_HEAD_DIM):
        mk_ref[:, c:c + MEM_HEAD_DIM] = _rms(kv[:, c:c + MEM_HEAD_DIM], kw).astype(BF16)
    mv_ref[...] = kv[:, D_MODEL:].astype(BF16)


def _mem_kv(mem2d, norm_w, w_mem_kv, mem_k_norm_w):
    n_mem = mem2d.shape[0]
    tok = pl.BlockSpec((TOK_TILE, D_MODEL), lambda i: (i, 0))
    out = jax.ShapeDtypeStruct((n_mem, D_MODEL), BF16)
    return pl.pallas_call(
        _mem_kv_body,
        grid=(n_mem // TOK_TILE,),
        in_specs=[tok, _resident((1, D_MODEL)), _resident((D_MODEL, 2 * D_MODEL)),
                  _resident((1, MEM_HEAD_DIM))],
        out_specs=[tok, tok],
        out_shape=[out, out],
        compiler_params=_params(("arbitrary",)),
        name="mem_kv",
    )(mem2d, norm_w.reshape(1, -1), w_mem_kv.astype(BF16), mem_k_norm_w.reshape(1, -1))


def _mem_attn_body(x1_ref, mq_ref, mk_ref, mv_ref, wo_ref, x2_ref):
    mq = mq_ref[...]
    heads = []
    for c in range(0, D_MODEL, MEM_HEAD_DIM):
        s = _dot_nt(mq[:, c:c + MEM_HEAD_DIM], mk_ref[:, c:c + MEM_HEAD_DIM])
        p = jnp.exp(s - jnp.max(s, axis=-1, keepdims=True))
        l = jnp.sum(p, axis=-1, keepdims=True)
        heads.append((_dot(p.astype(BF16), mv_ref[:, c:c + MEM_HEAD_DIM]) / l).astype(BF16))
    mo = jnp.concatenate(heads, axis=-1)
    x2_ref[...] = x1_ref[...] + _dot(mo, wo_ref[...])


def _mem_attn(x1, mq, mk, mv, w_mem_o, batch):
    n_tok = x1.shape[0]
    tiles = n_tok // batch // WIDE_TILE
    mem_len = mk.shape[0] // batch
    tok = pl.BlockSpec((WIDE_TILE, D_MODEL), lambda b, i: (b * tiles + i, 0))
    mem = pl.BlockSpec((mem_len, D_MODEL), lambda b, i: (b, 0))
    return pl.pallas_call(
        _mem_attn_body,
        grid=(batch, tiles),
        in_specs=[tok, tok, mem, mem, _resident((D_MODEL, D_MODEL))],
        out_specs=tok,
        out_shape=jax.ShapeDtypeStruct((n_tok, D_MODEL), F32),
        compiler_params=_params(("arbitrary", "arbitrary")),
        name="mem_attn",
    )(x1, mq, mk, mv, w_mem_o.astype(BF16))


FF_CHUNK = 1024


def _mlp_body(x_ref, nw_ref, wu_ref, wd_ref, o_ref):
    x = x_ref[...]
    h = _rms(x, nw_ref[...]).astype(BF16)
    acc = x
    for c in range(0, D_FF, FF_CHUNK):
        u = jnp.maximum(_dot(h, wu_ref[:, c:c + FF_CHUNK]), 0.0)
        acc = acc + _dot((u * u).astype(BF16), wd_ref[c:c + FF_CHUNK, :])
    o_ref[...] = acc


def _mlp(x2, norm_w, w_up, w_down):
    n_tok = x2.shape[0]
    tok = pl.BlockSpec((WIDE_TILE, D_MODEL), lambda i: (i, 0))
    return pl.pallas_call(
        _mlp_body,
        grid=(n_tok // WIDE_TILE,),
        in_specs=[tok, _resident((1, D_MODEL)), _resident((D_MODEL, D_FF)),
                  _resident((D_FF, D_MODEL))],
        out_specs=tok,
        out_shape=jax.ShapeDtypeStruct((n_tok, D_MODEL), F32),
        compiler_params=_params(("arbitrary",)),
        name="mlp",
    )(x2, norm_w.reshape(1, -1), w_up.astype(BF16), w_down.astype(BF16))


POST_TILE = 512


def _post_attn_body(x_ref, od_ref, of_ref, mk_ref, mv_ref, wo_ref, nq_ref, wq_ref, qw_ref,
                    wmo_ref, nm_ref, wu_ref, wd_ref, o_ref):
    x1 = x_ref[...] + (_dot(od_ref[...], wo_ref[:GROUP_W, :])
                       + _dot(of_ref[...], wo_ref[GROUP_W:, :]))

    mq = _dot(_rms(x1, nq_ref[...]).astype(BF16), wq_ref[...])
    qw = qw_ref[...]
    heads = []
    for c in range(0, D_MODEL, MEM_HEAD_DIM):
        q = _rms(mq[:, c:c + MEM_HEAD_DIM], qw).astype(BF16)
        s = _dot_nt(q, mk_ref[:, c:c + MEM_HEAD_DIM])
        p = jnp.exp(s - jnp.max(s, axis=-1, keepdims=True))
        l = jnp.sum(p, axis=-1, keepdims=True)
        heads.append((_dot(p.astype(BF16), mv_ref[:, c:c + MEM_HEAD_DIM]) / l).astype(BF16))
    x2 = x1 + _dot(jnp.concatenate(heads, axis=-1), wmo_ref[...])

    h = _rms(x2, nm_ref[...]).astype(BF16)
    acc = x2
    for c in range(0, D_FF, FF_CHUNK):
        u = jnp.maximum(_dot(h, wu_ref[:, c:c + FF_CHUNK]), 0.0)
        acc = acc + _dot((u * u).astype(BF16), wd_ref[c:c + FF_CHUNK, :])
    o_ref[...] = acc


def _post_attn(x2d, od, of, mk, mv, batch, w_out, norm_q, w_mem_q, mem_q_norm_w, w_mem_o,
               norm_mlp, w_up, w_down):
    n_tok = x2d.shape[0]
    tiles = n_tok // batch // POST_TILE
    mem_len = mk.shape[0] // batch
    tok = lambda width: pl.BlockSpec((POST_TILE, width), lambda b, i: (b * tiles + i, 0))
    mem = pl.BlockSpec((mem_len, D_MODEL), lambda b, i: (b, 0))
    scale = MEM_HEAD_DIM ** -0.5
    square = _resident((D_MODEL, D_MODEL))
    vec = _resident((1, D_MODEL))
    return pl.pallas_call(
        _post_attn_body,
        grid=(batch, tiles),
        in_specs=[tok(D_MODEL), tok(GROUP_W), tok(GROUP_W), mem, mem, square, vec, square,
                  _resident((1, MEM_HEAD_DIM)), square, vec, _resident((D_MODEL, D_FF)),
                  _resident((D_FF, D_MODEL))],
        out_specs=tok(D_MODEL),
        out_shape=jax.ShapeDtypeStruct((n_tok, D_MODEL), F32),
        compiler_params=_params(("arbitrary", "arbitrary")),
        name="post_attn",
    )(x2d, od, of, mk, mv, w_out.astype(BF16), norm_q.reshape(1, -1), w_mem_q.astype(BF16),
      (mem_q_norm_w * scale).reshape(1, -1), w_mem_o.astype(BF16), norm_mlp.reshape(1, -1),
      w_up.astype(BF16), w_down.astype(BF16))


def kernel(x, mem, positions, norm_mix_w, w_in, b_forget, diff_q_norm_w, diff_k_norm_w,
           lambda_q1, lambda_k1, lambda_q2, lambda_k2, diff_subln_w, fox_q_norm_w,
           fox_k_norm_w, w_out, norm_mem_q_w, norm_mem_kv_w, w_mem_q, w_mem_kv,
           mem_q_norm_w, mem_k_norm_w, w_mem_o, norm_mlp_w, w_up, w_down):
    batch, seq, d = x.shape
    depth = w_in.shape[0]
    assert d == D_MODEL and seq % WIDE_TILE == 0 and WIDE_TILE % TOK_TILE == 0
    assert TOK_TILE % ATT_TILE == 0
    assert (batch * mem.shape[1]) % TOK_TILE == 0

    cos_t, sin_t = _rope_tables(positions)
    xc = x.reshape(batch * seq, d)
    mem2d = mem.reshape(-1, d)
    for l in range(depth):
        lam_init = 0.8 - 0.6 * math.exp(-0.3 * l)
        dqt, dk, dvt, fqt, fk, fvt, qaugt, kaug = _in_proj(
            xc, batch, norm_mix_w[l], w_in[l], cos_t, sin_t, diff_q_norm_w[l],
            diff_k_norm_w[l], fox_q_norm_w[l], fox_k_norm_w[l], b_forget[l])
        lam_params = [p[l].reshape(1, -1) for p in
                      (lambda_q1, lambda_k1, lambda_q2, lambda_k2, diff_subln_w)]
        od = _attention(dqt, dk, dvt, batch, lam_params, fox=False, lam_init=lam_init)
        of = _attention(fqt, fk, fvt, batch, [qaugt, kaug], fox=True)
        mk, mv = _mem_kv(mem2d, norm_mem_kv_w[l], w_mem_kv[l], mem_k_norm_w[l])
        xc = _post_attn(xc, od, of, mk, mv, batch, w_out[l], norm_mem_q_w[l], w_mem_q[l],
                        mem_q_norm_w[l], w_mem_o[l], norm_mlp_w[l], w_up[l], w_down[l])
    return xc.reshape(batch, seq, d)
```

```python
import functools
import math

import jax
import jax.numpy as jnp
from jax import lax
from jax.experimental import pallas as pl
from jax.experimental.pallas import tpu as pltpu

F32 = jnp.float32
BF16 = jnp.bfloat16

D_MODEL = 1024
CHUNK = 64
HEAD_DIM = 64
N_DIFF_HEADS = 4
DIFF_V_DIM = 2 * HEAD_DIM
N_FOX_HEADS = 8
GROUP_W = 512
ROPE_DIM = HEAD_DIM // 4
ROPE_THETA = 500000.0
N_MEM_HEADS = 4
MEM_HEAD_DIM = D_MODEL // N_MEM_HEADS
D_FF = 4 * D_MODEL
EPS = 1e-6
NEG_INF = -1e30
LOG2E = math.log2(math.e)

LANES = 128
MXU_DIM = 256
VMEM_LIMIT = 56 * 1024 * 1024

TOK_TILE = 1024
ATT_TILE = 256


def _params(sem):
    return pltpu.CompilerParams(dimension_semantics=sem, vmem_limit_bytes=VMEM_LIMIT)


def _resident(shape):
    return pl.BlockSpec(shape, lambda *_: (0,) * len(shape), pipeline_mode=pl.Buffered(1))


def _rms(x, w):
    ms = jnp.mean(x * x, axis=-1, keepdims=True)
    return (x * lax.rsqrt(ms + EPS)) * w


def _dot(a, b):
    return jnp.dot(a, b, preferred_element_type=F32)


def _dot_nt(a, b):
    return lax.dot_general(a, b, (((1,), (1,)), ((), ())), preferred_element_type=F32)


def _rope_body(pos_ref, freq_ref, cos_ref, sin_ref):
    ang = pos_ref[...] * freq_ref[...]
    cos_ref[...] = jnp.cos(ang)
    sin_ref[...] = jnp.sin(ang)


def _rope_tables(positions):
    n_tok = positions.size
    half = ROPE_DIM // 2
    inv_freq = ROPE_THETA ** (-jnp.arange(0, ROPE_DIM, 2, dtype=F32) / ROPE_DIM)
    out = jax.ShapeDtypeStruct((half, n_tok), F32)
    return pl.pallas_call(_rope_body, out_shape=(out, out), name="rope_tables")(
        positions.reshape(1, n_tok).astype(F32), inv_freq.reshape(half, 1))


def _log_sigmoid(x):
    return jnp.minimum(x, 0.0) - jnp.log1p(jnp.exp(-jnp.abs(x)))


def _split3(x):
    hi = x.astype(BF16)
    rem = x - hi.astype(F32)
    mid = rem.astype(BF16)
    lo = (rem - mid.astype(F32)).astype(BF16)
    return hi, mid, lo


BIAS_SLOT = LANES // N_FOX_HEADS


def _head_norm_t(t, gain, rope):
    half = ROPE_DIM // 2
    outs = []
    for r0 in range(0, t.shape[0], HEAD_DIM):
        th = t[r0:r0 + HEAD_DIM]
        ms = jnp.sum(th * th, axis=0, keepdims=True) * (1.0 / HEAD_DIM)
        th = (th * lax.rsqrt(ms + EPS)) * gain[r0:r0 + HEAD_DIM]
        if rope is not None:
            cos, sin = rope
            t1, t2 = th[:half], th[half:ROPE_DIM]
            th = jnp.concatenate(
                [t1 * cos - t2 * sin, t2 * cos + t1 * sin, th[ROPE_DIM:]], axis=0)
        outs.append(th)
    return jnp.concatenate(outs, axis=0)


def _in_proj_body(x_ref, nw_ref, wt_ref, wft_ref, cos_ref, sin_ref, gain_ref, bf_ref, dqt_ref,
                  dk_ref, dvt_ref, fqt_ref, fk_ref, fvt_ref, qaugt_ref, kaug_ref, carry_ref):
    @pl.when(pl.program_id(1) == 0)
    def _():
        carry_ref[...] = jnp.zeros_like(carry_ref)

    h = _rms(x_ref[...], nw_ref[...]).astype(BF16)
    rope = (cos_ref[...], sin_ref[...])
    g = GROUP_W

    def proj_t(group):
        r0 = group * GROUP_W
        return _dot_nt(wt_ref[r0:r0 + GROUP_W, :], h)

    def store_keys(k_ref, kt):
        for c in range(0, g, LANES):
            k_ref[:, c:c + LANES] = kt[c:c + LANES].T.astype(BF16)

    def store_values(vt_ref, vt):
        for t in range(TOK_TILE // ATT_TILE):
            vt_ref[t] = vt[:, t * ATT_TILE:(t + 1) * ATT_TILE].astype(BF16)

    dqt_ref[...] = _head_norm_t(proj_t(0), gain_ref[0:g], rope).astype(BF16)
    store_keys(dk_ref, _head_norm_t(proj_t(1), gain_ref[g:2 * g], rope))
    store_values(dvt_ref, proj_t(2))
    fqt_ref[...] = _head_norm_t(proj_t(3), gain_ref[2 * g:3 * g], None).astype(BF16)
    store_keys(fk_ref, _head_norm_t(proj_t(4), gain_ref[3 * g:4 * g], None))
    store_values(fvt_ref, proj_t(5))

    log_f = _log_sigmoid(_dot_nt(wft_ref[...], h) + bf_ref[...])
    r = lax.broadcasted_iota(jnp.int32, (MXU_DIM, MXU_DIM), 0)
    c = lax.broadcasted_iota(jnp.int32, (MXU_DIM, MXU_DIM), 1)
    triu = jnp.where(r <= c, 1.0, 0.0).astype(BF16)
    part = lax.broadcasted_iota(jnp.int32, (LANES, MXU_DIM), 0) % BIAS_SLOT
    carry = carry_ref[:, 0:1]
    for c0 in range(0, TOK_TILE, MXU_DIM):
        hi, mid, lo = _split3(log_f[:, c0:c0 + MXU_DIM])
        cum = (_dot(hi, triu) + _dot(mid, triu)) + _dot(lo, triu) + carry
        carry = cum[:, MXU_DIM - 1:MXU_DIM]
        hi, mid, lo = (t.astype(F32) for t in _split3(cum * LOG2E))
        kaug_t = jnp.where(part == 0, -hi, jnp.where(part == 1, -mid, jnp.where(
            part == 2, -lo, jnp.where(part < 6, 1.0, 0.0))))
        qaug_t = jnp.where(part < 3, 1.0, jnp.where(part == 3, hi, jnp.where(
            part == 4, mid, jnp.where(part == 5, lo, 0.0))))
        kaug_ref[c0:c0 + MXU_DIM, :] = kaug_t.T.astype(BF16)
        qaugt_ref[:, c0:c0 + MXU_DIM] = qaug_t.astype(BF16)
    carry_ref[...] = jnp.broadcast_to(carry, carry_ref.shape)


def _in_proj(x2d, batch, norm_w, w_in, cos_t, sin_t, dqw, dkw, fqw, fkw, b_forget):
    n_tok = x2d.shape[0]
    tiles = n_tok // batch // TOK_TILE
    g = GROUP_W
    wt = w_in[:, :6 * g].T.astype(BF16)
    wft = jnp.repeat(w_in[:, 6 * g:6 * g + N_FOX_HEADS].T, BIAS_SLOT, axis=0).astype(BF16)
    bf = jnp.repeat(b_forget, BIAS_SLOT).reshape(LANES, 1)
    qscale = HEAD_DIM ** -0.5 * LOG2E
    gains = jnp.concatenate([jnp.tile(v, g // HEAD_DIM) for v in
                             (dqw * qscale, dkw, fqw * qscale, fkw)]).reshape(4 * g, 1)

    row = lambda b, i: (b * tiles + i, 0)
    col = lambda b, i: (0, b * tiles + i)
    tok = lambda width: pl.BlockSpec((TOK_TILE, width), row)
    tok_t = lambda rows: pl.BlockSpec((rows, TOK_TILE), col)
    slabs = TOK_TILE // ATT_TILE
    vt_spec = pl.BlockSpec((slabs, g, ATT_TILE), lambda b, i: (b * tiles + i, 0, 0))
    k_shape = jax.ShapeDtypeStruct((n_tok, g), BF16)
    qt_shape = jax.ShapeDtypeStruct((g, n_tok), BF16)
    vt_shape = jax.ShapeDtypeStruct((n_tok // ATT_TILE, g, ATT_TILE), BF16)
    half = ROPE_DIM // 2
    return pl.pallas_call(
        _in_proj_body,
        grid=(batch, tiles),
        in_specs=[tok(D_MODEL), _resident((1, D_MODEL)), _resident(wt.shape),
                  _resident(wft.shape), tok_t(half), tok_t(half), _resident(gains.shape),
                  _resident(bf.shape)],
        out_specs=[tok_t(g), tok(g), vt_spec, tok_t(g), tok(g), vt_spec, tok_t(LANES),
                   tok(LANES)],
        out_shape=[qt_shape, k_shape, vt_shape, qt_shape, k_shape, vt_shape,
                   jax.ShapeDtypeStruct((LANES, n_tok), BF16),
                   jax.ShapeDtypeStruct((n_tok, LANES), BF16)],
        scratch_shapes=[pltpu.VMEM((LANES, LANES), F32)],
        compiler_params=_params(("arbitrary", "arbitrary")),
        name="in_proj",
    )(x2d, norm_w.reshape(1, -1), wt, wft, cos_t, sin_t, gains, bf)


N_CHAINS = 2 * (GROUP_W // LANES)
SUM_ROWS = 16


def _attn_body(qta_ref, qtb_ref, k_ref, vt_ref, *rest, fox, lam_init, half):
    if fox:
        qauga_ref, qaugb_ref, kaug_ref, o_ref = rest[:4]
        qaug_refs = (qauga_ref, qaugb_ref)
    else:
        lq1_ref, lk1_ref, lq2_ref, lk2_ref, subw_ref, o_ref = rest[:6]
    rhs_scr, s0_scr, s1_scr, bmax_scr, m_scr, acc_scr = rest[-6:]
    s_scr = (s0_scr, s1_scr)
    s = pl.program_id(1)
    last = 2 * s + half + 1
    width = acc_scr.shape[2] - SUM_ROWS
    ones_rows = jnp.ones((SUM_ROWS, ATT_TILE), BF16)

    half_zero = jnp.zeros((HEAD_DIM, ATT_TILE), BF16)
    for tile, qt_ref in enumerate((qta_ref, qtb_ref)):
        for g in range(GROUP_W // LANES):
            r0 = g * LANES
            rhs_scr[tile, 2 * g, :HEAD_DIM, :] = qt_ref[r0:r0 + HEAD_DIM, :]
            rhs_scr[tile, 2 * g, HEAD_DIM:LANES, :] = half_zero
            rhs_scr[tile, 2 * g + 1, :HEAD_DIM, :] = half_zero
            rhs_scr[tile, 2 * g + 1, HEAD_DIM:LANES, :] = qt_ref[r0 + HEAD_DIM:r0 + LANES, :]
        if fox:
            for c in range(N_CHAINS):
                b0 = c * BIAS_SLOT
                rhs_scr[tile, c, LANES:, :] = jnp.zeros((LANES, ATT_TILE), BF16)
                rhs_scr[tile, c, LANES + b0:LANES + b0 + BIAS_SLOT, :] = (
                    qaug_refs[tile][b0:b0 + BIAS_SLOT, :])
    m_scr[...] = jnp.full(m_scr.shape, NEG_INF, F32)
    acc_scr[...] = jnp.zeros(acc_scr.shape, F32)

    def locate(p):
        in_b = p > s
        return in_b.astype(jnp.int32), jnp.where(in_b, p - s - 1, s - p)

    def scores(p, slot, g, mask):
        tile, blk = locate(p)
        start = pl.multiple_of(blk * ATT_TILE, ATT_TILE)
        lhs = k_ref[pl.ds(start, ATT_TILE), g * LANES:(g + 1) * LANES]
        if fox:
            lhs = jnp.concatenate([lhs, kaug_ref[pl.ds(start, ATT_TILE), :]], axis=-1)
        for c in (2 * g, 2 * g + 1):
            sc = _dot(lhs, rhs_scr[tile, c])
            if mask is not None:
                sc = jnp.where(mask, sc, NEG_INF)
            s_scr[slot][c] = sc
            bmax_scr[slot, c:c + 1, :] = jnp.max(sc, axis=0, keepdims=True)

    def softmax_pv(p, slot, c):
        tile, blk = locate(p)
        sc = s_scr[slot][c]
        m_old = m_scr[tile, c:c + 1, :]
        m_new = jnp.maximum(m_old, bmax_scr[slot, c:c + 1, :])
        alpha = jnp.exp2(m_old - m_new)
        prob = jnp.exp2(sc - m_new)
        m_scr[tile, c:c + 1, :] = m_new
        r0 = c * HEAD_DIM if fox else (c // 2) * LANES
        vt = jnp.concatenate([vt_ref[blk, r0:r0 + width, :], ones_rows], axis=0)
        acc_scr[tile, c] = alpha * acc_scr[tile, c] + _dot(vt, prob.astype(BF16))

    def step(score_args, soft_args, mask=None):
        for g in range(GROUP_W // LANES):
            if score_args is not None:
                scores(*score_args, g, mask)
            if soft_args is not None:
                softmax_pv(*soft_args, 2 * g)
                softmax_pv(*soft_args, 2 * g + 1)

    div = 1 if fox else CHUNK
    k_id = lax.broadcasted_iota(jnp.int32, (ATT_TILE, ATT_TILE), 0) // div
    q_id = lax.broadcasted_iota(jnp.int32, (ATT_TILE, ATT_TILE), 1) // div
    diag = k_id <= q_id

    def pair(u, carry):
        p = 2 * u + 1
        step((p + 1, 0), (p, 1))
        step((p + 2, 1), (p + 1, 0))
        return carry

    step((0, 0), None, diag)
    step((1, 1), (0, 0))
    lax.fori_loop(0, (last - 3) // 2, pair, 0)
    step((last - 1, 0), (last - 2, 1))
    step((last, 1), (last - 1, 0), diag)
    step(None, (last, 1))

    if not fox:
        lam = (jnp.exp(jnp.sum(lq1_ref[...] * lk1_ref[...], axis=-1, keepdims=True))
               - jnp.exp(jnp.sum(lq2_ref[...] * lk2_ref[...], axis=-1, keepdims=True))
               + lam_init)
    for tile in range(2):
        for g in range(GROUP_W // LANES):
            a1, a2 = acc_scr[tile, 2 * g], acc_scr[tile, 2 * g + 1]
            o1 = a1[:width] / a1[width:width + 1]
            o2 = a2[:width] / a2[width:width + 1]
            if fox:
                o = jnp.concatenate([o1, o2], axis=0).T
            else:
                o = _rms((o1 - lam * o2).T, subw_ref[...]) * (1.0 - lam_init)
            o_ref[0, tile, 0, :, g * LANES:(g + 1) * LANES] = o.astype(BF16)


def _attention(qt, k, vt, batch, extra, *, fox, lam_init=0.0):
    n_tok = k.shape[0]
    seq = n_tok // batch
    tiles = seq // ATT_TILE
    assert tiles % 2 == 0
    half = tiles // 2
    width = HEAD_DIM if fox else DIFF_V_DIM
    qt_spec = lambda r, off: pl.BlockSpec((r, ATT_TILE), lambda b, s: (0, b * tiles + s + off))
    kv_spec = lambda w: pl.BlockSpec((seq, w), lambda b, s: (b, 0))
    vt_spec = pl.BlockSpec((tiles, GROUP_W, ATT_TILE), lambda b, s: (b, 0, 0))
    if fox:
        qaugt, kaug = extra
        extra = [qaugt, qaugt, kaug]
        extra_specs = [qt_spec(LANES, 0), qt_spec(LANES, half), kv_spec(LANES)]
    else:
        extra_specs = [_resident(e.shape) for e in extra]
    rhs_w = 2 * LANES if fox else LANES
    out = pl.pallas_call(
        functools.partial(_attn_body, fox=fox, lam_init=lam_init, half=half),
        grid=(batch, half),
        in_specs=[qt_spec(GROUP_W, 0), qt_spec(GROUP_W, half), kv_spec(GROUP_W), vt_spec]
        + extra_specs,
        out_specs=pl.BlockSpec((1, 2, 1, ATT_TILE, GROUP_W), lambda b, s: (b, 0, s, 0, 0)),
        out_shape=jax.ShapeDtypeStruct((batch, 2, half, ATT_TILE, GROUP_W), BF16),
        scratch_shapes=[pltpu.VMEM((2, N_CHAINS, rhs_w, ATT_TILE), BF16),
                        pltpu.VMEM((N_CHAINS, ATT_TILE, ATT_TILE), F32),
                        pltpu.VMEM((N_CHAINS, ATT_TILE, ATT_TILE), F32),
                        pltpu.VMEM((2, N_CHAINS, ATT_TILE), F32),
                        pltpu.VMEM((2, N_CHAINS, ATT_TILE), F32),
                        pltpu.VMEM((2, N_CHAINS, width + SUM_ROWS, ATT_TILE), F32)],
        compiler_params=_params(("arbitrary", "arbitrary")),
        name="fox_attn" if fox else "diff_attn",
    )(qt, qt, k, vt, *extra)
    return out.reshape(n_tok, GROUP_W)


def _mem_kv_body(m_ref, nw_ref, w_ref, kw_ref, mk_ref, mv_ref):
    hm = _rms(m_ref[...], nw_ref[...]).astype(BF16)
    kv = _dot(hm, w_ref[...])
    kw = kw_ref[...]
    for c in range(0, D_MODEL, MEM_HEAD_DIM):
        mk_ref[:, c:c + MEM_HEAD_DIM] = _rms(kv[:, c:c + MEM_HEAD_DIM], kw).astype(BF16)
    mv_ref[...] = kv[:, D_MODEL:].astype(BF16)


def _mem_kv(mem2d, norm_w, w_mem_kv, mem_k_norm_w):
    n_mem = mem2d.shape[0]
    tok = pl.BlockSpec((TOK_TILE, D_MODEL), lambda i: (i, 0))
    out = jax.ShapeDtypeStruct((n_mem, D_MODEL), BF16)
    return pl.pallas_call(
        _mem_kv_body,
        grid=(n_mem // TOK_TILE,),
        in_specs=[tok, _resident((1, D_MODEL)), _resident((D_MODEL, 2 * D_MODEL)),
                  _resident((1, MEM_HEAD_DIM))],
        out_specs=[tok, tok],
        out_shape=[out, out],
        compiler_params=_params(("arbitrary",)),
        name="mem_kv",
    )(mem2d, norm_w.reshape(1, -1), w_mem_kv.astype(BF16), mem_k_norm_w.reshape(1, -1))


POST_TILE = 512
FF_CHUNK = 1024
STAGE_CHUNKS = 8


def _load_as_bf16(src_hbm, dst_ref, stage_ref, sem_ref):
    rows = stage_ref.shape[1]
    n_chunks = src_hbm.shape[0] // rows

    def chunk_copy(c):
        slot = c % 2
        return pltpu.make_async_copy(src_hbm.at[pl.ds(c * rows, rows), :],
                                     stage_ref.at[slot], sem_ref.at[slot])

    chunk_copy(0).start()
    for c in range(n_chunks):
        if c + 1 < n_chunks:
            chunk_copy(c + 1).start()
        chunk_copy(c).wait()
        dst_ref[c * rows:(c + 1) * rows, :] = stage_ref[c % 2].astype(BF16)


def _post_attn_body(x_ref, od_ref, of_ref, mk_ref, mv_ref, wo_ref, nq_ref, wq_ref, qw_ref,
                    wmo_ref, nm_ref, wu_hbm, wd_hbm, o_ref, wu_ref, wd_ref, stage_u, stage_d,
                    sem_u, sem_d):
    @pl.when((pl.program_id(0) == 0) & (pl.program_id(1) == 0))
    def _():
        _load_as_bf16(wu_hbm, wu_ref, stage_u, sem_u)
        _load_as_bf16(wd_hbm, wd_ref, stage_d, sem_d)

    x1 = x_ref[...] + (_dot(od_ref[...], wo_ref[:GROUP_W, :])
                       + _dot(of_ref[...], wo_ref[GROUP_W:, :]))

    mq = _dot(_rms(x1, nq_ref[...]).astype(BF16), wq_ref[...])
    qw = qw_ref[...]
    heads = []
    for c in range(0, D_MODEL, MEM_HEAD_DIM):
        q = _rms(mq[:, c:c + MEM_HEAD_DIM], qw).astype(BF16)
        s = _dot_nt(q, mk_ref[:, c:c + MEM_HEAD_DIM])
        p = jnp.exp(s - jnp.max(s, axis=-1, keepdims=True))
        l = jnp.sum(p, axis=-1, keepdims=True)
        heads.append((_dot(p.astype(BF16), mv_ref[:, c:c + MEM_HEAD_DIM]) / l).astype(BF16))
    x2 = x1 + _dot(jnp.concatenate(heads, axis=-1), wmo_ref[...])

    h = _rms(x2, nm_ref[...]).astype(BF16)
    acc = x2
    for c in range(0, D_FF, FF_CHUNK):
        u = jnp.maximum(_dot(h, wu_ref[:, c:c + FF_CHUNK]), 0.0)
        acc = acc + _dot((u * u).astype(BF16), wd_ref[c:c + FF_CHUNK, :])
    o_ref[...] = acc


def _post_attn(x2d, od, of, mk, mv, batch, w_out, norm_q, w_mem_q, mem_q_norm_w, w_mem_o,
               norm_mlp, w_up, w_down):
    n_tok = x2d.shape[0]
    tiles = n_tok // batch // POST_TILE
    mem_len = mk.shape[0] // batch
    tok = lambda width: pl.BlockSpec((POST_TILE, width), lambda b, i: (b * tiles + i, 0))
    mem = pl.BlockSpec((mem_len, D_MODEL), lambda b, i: (b, 0))
    scale = MEM_HEAD_DIM ** -0.5
    square = _resident((D_MODEL, D_MODEL))
    vec = _resident((1, D_MODEL))
    hbm = pl.BlockSpec(memory_space=pl.ANY)
    return pl.pallas_call(
        _post_attn_body,
        grid=(batch, tiles),
        in_specs=[tok(D_MODEL), tok(GROUP_W), tok(GROUP_W), mem, mem, square, vec, square,
                  _resident((1, MEM_HEAD_DIM)), square, vec, hbm, hbm],
        out_specs=tok(D_MODEL),
        out_shape=jax.ShapeDtypeStruct((n_tok, D_MODEL), F32),
        scratch_shapes=[pltpu.VMEM((D_MODEL, D_FF), BF16), pltpu.VMEM((D_FF, D_MODEL), BF16),
                        pltpu.VMEM((2, D_MODEL // STAGE_CHUNKS, D_FF), F32),
                        pltpu.VMEM((2, D_FF // STAGE_CHUNKS, D_MODEL), F32),
                        pltpu.SemaphoreType.DMA((2,)), pltpu.SemaphoreType.DMA((2,))],
        compiler_params=_params(("arbitrary", "arbitrary")),
        name="post_attn",
    )(x2d, od, of, mk, mv, w_out.astype(BF16), norm_q.reshape(1, -1), w_mem_q.astype(BF16),
      (mem_q_norm_w * scale).reshape(1, -1), w_mem_o.astype(BF16), norm_mlp.reshape(1, -1),
      w_up, w_down)


def kernel(x, mem, positions, norm_mix_w, w_in, b_forget, diff_q_norm_w, diff_k_norm_w,
           lambda_q1, lambda_k1, lambda_q2, lambda_k2, diff_subln_w, fox_q_norm_w,
           fox_k_norm_w, w_out, norm_mem_q_w, norm_mem_kv_w, w_mem_q, w_mem_kv,
           mem_q_norm_w, mem_k_norm_w, w_mem_o, norm_mlp_w, w_up, w_down):
    batch, seq, d = x.shape
    depth = w_in.shape[0]
    assert d == D_MODEL and seq % TOK_TILE == 0 and seq % POST_TILE == 0
    assert TOK_TILE % ATT_TILE == 0
    assert (batch * mem.shape[1]) % TOK_TILE == 0

    cos_t, sin_t = _rope_tables(positions)
    xc = x.reshape(batch * seq, d)
    mem2d = mem.reshape(-1, d)
    for l in range(depth):
        lam_init = 0.8 - 0.6 * math.exp(-0.3 * l)
        dqt, dk, dvt, fqt, fk, fvt, qaugt, kaug = _in_proj(
            xc, batch, norm_mix_w[l], w_in[l], cos_t, sin_t, diff_q_norm_w[l],
            diff_k_norm_w[l], fox_q_norm_w[l], fox_k_norm_w[l], b_forget[l])
        lam_params = [p[l].reshape(1, -1) for p in
                      (lambda_q1, lambda_k1, lambda_q2, lambda_k2, diff_subln_w)]
        od = _attention(dqt, dk, dvt, batch, lam_params, fox=False, lam_init=lam_init)
        of = _attention(fqt, fk, fvt, batch, [qaugt, kaug], fox=True)
        mk, mv = _mem_kv(mem2d, norm_mem_kv_w[l], w_mem_kv[l], mem_k_norm_w[l])
        xc = _post_attn(xc, od, of, mk, mv, batch, w_out[l], norm_mem_q_w[l], w_mem_q[l],
                        mem_q_norm_w[l], w_mem_o[l], norm_mlp_w[l], w_up[l], w_down[l])
    return xc.reshape(batch, seq, d)
```

```python
import functools
import math
from typing import Any, NamedTuple

import jax
import jax.numpy as jnp
from jax import lax
from jax.experimental import pallas as pl
from jax.experimental.pallas import tpu as pltpu

F32 = jnp.float32
BF16 = jnp.bfloat16

D_MODEL = 1024
CHUNK = 64
HEAD_DIM = 64
N_DIFF_HEADS = 4
DIFF_V_DIM = 2 * HEAD_DIM
N_FOX_HEADS = 8
GROUP_W = 512
ROPE_DIM = HEAD_DIM // 4
ROPE_THETA = 500000.0
N_MEM_HEADS = 4
MEM_HEAD_DIM = D_MODEL // N_MEM_HEADS
D_FF = 4 * D_MODEL
EPS = 1e-6
NEG_INF = -1e30
LOG2E = math.log2(math.e)

LANES = 128
MXU_DIM = 256
VMEM_LIMIT = 56 * 1024 * 1024

TOK_TILE = 1024
ATT_TILE = 256


def _params(sem):
    return pltpu.CompilerParams(dimension_semantics=sem, vmem_limit_bytes=VMEM_LIMIT)


def _resident(shape):
    return pl.BlockSpec(shape, lambda *_: (0,) * len(shape), pipeline_mode=pl.Buffered(1))


def _rms(x, w):
    ms = jnp.mean(x * x, axis=-1, keepdims=True)
    return (x * lax.rsqrt(ms + EPS)) * w


def _dot(a, b):
    return jnp.dot(a, b, preferred_element_type=F32)


def _dot_nt(a, b):
    return lax.dot_general(a, b, (((1,), (1,)), ((), ())), preferred_element_type=F32)


def _rope_body(pos_ref, freq_ref, cos_ref, sin_ref):
    ang = pos_ref[...] * freq_ref[...]
    cos_ref[...] = jnp.cos(ang)
    sin_ref[...] = jnp.sin(ang)


def _rope_tables(positions):
    n_tok = positions.size
    half = ROPE_DIM // 2
    inv_freq = ROPE_THETA ** (-jnp.arange(0, ROPE_DIM, 2, dtype=F32) / ROPE_DIM)
    out = jax.ShapeDtypeStruct((half, n_tok), F32)
    return pl.pallas_call(_rope_body, out_shape=(out, out), name="rope_tables")(
        positions.reshape(1, n_tok).astype(F32), inv_freq.reshape(half, 1))


def _log_sigmoid(x):
    return jnp.minimum(x, 0.0) - jnp.log1p(jnp.exp(-jnp.abs(x)))


def _split3(x):
    hi = x.astype(BF16)
    rem = x - hi.astype(F32)
    mid = rem.astype(BF16)
    lo = (rem - mid.astype(F32)).astype(BF16)
    return hi, mid, lo


BIAS_SLOT = LANES // N_FOX_HEADS


def _head_norm_t(t, gain, rope):
    half = ROPE_DIM // 2
    outs = []
    for r0 in range(0, t.shape[0], HEAD_DIM):
        th = t[r0:r0 + HEAD_DIM]
        ms = jnp.sum(th * th, axis=0, keepdims=True) * (1.0 / HEAD_DIM)
        th = (th * lax.rsqrt(ms + EPS)) * gain[r0:r0 + HEAD_DIM]
        if rope is not None:
            cos, sin = rope
            t1, t2 = th[:half], th[half:ROPE_DIM]
            th = jnp.concatenate(
                [t1 * cos - t2 * sin, t2 * cos + t1 * sin, th[ROPE_DIM:]], axis=0)
        outs.append(th)
    return jnp.concatenate(outs, axis=0)


def _in_proj_body(x_ref, nw_ref, wt_ref, wft_ref, cos_ref, sin_ref, gain_ref, bf_ref, dqt_ref,
                  dk_ref, dvt_ref, fqt_ref, fk_ref, fvt_ref, qaugt_ref, kaug_ref, carry_ref):
    @pl.when(pl.program_id(1) == 0)
    def _():
        carry_ref[...] = jnp.zeros_like(carry_ref)

    h = _rms(x_ref[...], nw_ref[...]).astype(BF16)
    rope = (cos_ref[...], sin_ref[...])
    g = GROUP_W

    def proj_t(group):
        r0 = group * GROUP_W
        return _dot_nt(wt_ref[r0:r0 + GROUP_W, :], h)

    def store_keys(k_ref, kt):
        for c in range(0, g, LANES):
            k_ref[:, c:c + LANES] = kt[c:c + LANES].T.astype(BF16)

    def store_values(vt_ref, vt):
        for t in range(TOK_TILE // ATT_TILE):
            vt_ref[t] = vt[:, t * ATT_TILE:(t + 1) * ATT_TILE].astype(BF16)

    dqt_ref[...] = _head_norm_t(proj_t(0), gain_ref[0:g], rope).astype(BF16)
    store_keys(dk_ref, _head_norm_t(proj_t(1), gain_ref[g:2 * g], rope))
    store_values(dvt_ref, proj_t(2))
    fqt_ref[...] = _head_norm_t(proj_t(3), gain_ref[2 * g:3 * g], None).astype(BF16)
    store_keys(fk_ref, _head_norm_t(proj_t(4), gain_ref[3 * g:4 * g], None))
    store_values(fvt_ref, proj_t(5))

    log_f = _log_sigmoid(_dot_nt(wft_ref[...], h) + bf_ref[...])
    r = lax.broadcasted_iota(jnp.int32, (MXU_DIM, MXU_DIM), 0)
    c = lax.broadcasted_iota(jnp.int32, (MXU_DIM, MXU_DIM), 1)
    triu = jnp.where(r <= c, 1.0, 0.0).astype(BF16)
    part = lax.broadcasted_iota(jnp.int32, (LANES, MXU_DIM), 0) % BIAS_SLOT
    carry = carry_ref[:, 0:1]
    for c0 in range(0, TOK_TILE, MXU_DIM):
        hi, mid, lo = _split3(log_f[:, c0:c0 + MXU_DIM])
        cum = (_dot(hi, triu) + _dot(mid, triu)) + _dot(lo, triu) + carry
        carry = cum[:, MXU_DIM - 1:MXU_DIM]
        hi, mid, lo = (t.astype(F32) for t in _split3(cum * LOG2E))
        kaug_t = jnp.where(part == 0, -hi, jnp.where(part == 1, -mid, jnp.where(
            part == 2, -lo, jnp.where(part < 6, 1.0, 0.0))))
        qaug_t = jnp.where(part < 3, 1.0, jnp.where(part == 3, hi, jnp.where(
            part == 4, mid, jnp.where(part == 5, lo, 0.0))))
        kaug_ref[c0:c0 + MXU_DIM, :] = kaug_t.T.astype(BF16)
        qaugt_ref[:, c0:c0 + MXU_DIM] = qaug_t.astype(BF16)
    carry_ref[...] = jnp.broadcast_to(carry, carry_ref.shape)


def _in_proj(x2d, batch, norm_w, w_in, cos_t, sin_t, dqw, dkw, fqw, fkw, b_forget):
    n_tok = x2d.shape[0]
    tiles = n_tok // batch // TOK_TILE
    g = GROUP_W
    wt = w_in[:, :6 * g].T.astype(BF16)
    wft = jnp.repeat(w_in[:, 6 * g:6 * g + N_FOX_HEADS].T, BIAS_SLOT, axis=0).astype(BF16)
    bf = jnp.repeat(b_forget, BIAS_SLOT).reshape(LANES, 1)
    qscale = HEAD_DIM ** -0.5 * LOG2E
    gains = jnp.concatenate([jnp.tile(v, g // HEAD_DIM) for v in
                             (dqw * qscale, dkw, fqw * qscale, fkw)]).reshape(4 * g, 1)

    row = lambda b, i: (b * tiles + i, 0)
    col = lambda b, i: (0, b * tiles + i)
    tok = lambda width: pl.BlockSpec((TOK_TILE, width), row)
    tok_t = lambda rows: pl.BlockSpec((rows, TOK_TILE), col)
    slabs = TOK_TILE // ATT_TILE
    vt_spec = pl.BlockSpec((slabs, g, ATT_TILE), lambda b, i: (b * tiles + i, 0, 0))
    k_shape = jax.ShapeDtypeStruct((n_tok, g), BF16)
    qt_shape = jax.ShapeDtypeStruct((g, n_tok), BF16)
    vt_shape = jax.ShapeDtypeStruct((n_tok // ATT_TILE, g, ATT_TILE), BF16)
    half = ROPE_DIM // 2
    return pl.pallas_call(
        _in_proj_body,
        grid=(batch, tiles),
        in_specs=[tok(D_MODEL), _resident((1, D_MODEL)), _resident(wt.shape),
                  _resident(wft.shape), tok_t(half), tok_t(half), _resident(gains.shape),
                  _resident(bf.shape)],
        out_specs=[tok_t(g), tok(g), vt_spec, tok_t(g), tok(g), vt_spec, tok_t(LANES),
                   tok(LANES)],
        out_shape=[qt_shape, k_shape, vt_shape, qt_shape, k_shape, vt_shape,
                   jax.ShapeDtypeStruct((LANES, n_tok), BF16),
                   jax.ShapeDtypeStruct((n_tok, LANES), BF16)],
        scratch_shapes=[pltpu.VMEM((LANES, LANES), F32)],
        compiler_params=_params(("arbitrary", "arbitrary")),
        name="in_proj",
    )(x2d, norm_w.reshape(1, -1), wt, wft, cos_t, sin_t, gains, bf)


N_CHAINS = 2 * (GROUP_W // LANES)
SUM_ROWS = 16


def _old_attn_body(qta_ref, qtb_ref, k_ref, vt_ref, *rest, fox, lam_init, half):
    if fox:
        qauga_ref, qaugb_ref, kaug_ref, o_ref = rest[:4]
        qaug_refs = (qauga_ref, qaugb_ref)
    else:
        lq1_ref, lk1_ref, lq2_ref, lk2_ref, subw_ref, o_ref = rest[:6]
    rhs_scr, s0_scr, s1_scr, bmax_scr, m_scr, acc_scr = rest[-6:]
    s_scr = (s0_scr, s1_scr)
    s = pl.program_id(1)
    last = 2 * s + half + 1
    width = acc_scr.shape[2] - SUM_ROWS
    ones_rows = jnp.ones((SUM_ROWS, ATT_TILE), BF16)

    half_zero = jnp.zeros((HEAD_DIM, ATT_TILE), BF16)
    for tile, qt_ref in enumerate((qta_ref, qtb_ref)):
        for g in range(GROUP_W // LANES):
            r0 = g * LANES
            rhs_scr[tile, 2 * g, :HEAD_DIM, :] = qt_ref[r0:r0 + HEAD_DIM, :]
            rhs_scr[tile, 2 * g, HEAD_DIM:LANES, :] = half_zero
            rhs_scr[tile, 2 * g + 1, :HEAD_DIM, :] = half_zero
            rhs_scr[tile, 2 * g + 1, HEAD_DIM:LANES, :] = qt_ref[r0 + HEAD_DIM:r0 + LANES, :]
        if fox:
            for c in range(N_CHAINS):
                b0 = c * BIAS_SLOT
                rhs_scr[tile, c, LANES:, :] = jnp.zeros((LANES, ATT_TILE), BF16)
                rhs_scr[tile, c, LANES + b0:LANES + b0 + BIAS_SLOT, :] = (
                    qaug_refs[tile][b0:b0 + BIAS_SLOT, :])
    m_scr[...] = jnp.full(m_scr.shape, NEG_INF, F32)
    acc_scr[...] = jnp.zeros(acc_scr.shape, F32)

    def locate(p):
        in_b = p > s
        return in_b.astype(jnp.int32), jnp.where(in_b, p - s - 1, s - p)

    def scores(p, slot, g, mask):
        tile, blk = locate(p)
        start = pl.multiple_of(blk * ATT_TILE, ATT_TILE)
        lhs = k_ref[pl.ds(start, ATT_TILE), g * LANES:(g + 1) * LANES]
        if fox:
            lhs = jnp.concatenate([lhs, kaug_ref[pl.ds(start, ATT_TILE), :]], axis=-1)
        for c in (2 * g, 2 * g + 1):
            sc = _dot(lhs, rhs_scr[tile, c])
            if mask is not None:
                sc = jnp.where(mask, sc, NEG_INF)
            s_scr[slot][c] = sc
            bmax_scr[slot, c:c + 1, :] = jnp.max(sc, axis=0, keepdims=True)

    def softmax_pv(p, slot, c):
        tile, blk = locate(p)
        sc = s_scr[slot][c]
        m_old = m_scr[tile, c:c + 1, :]
        m_new = jnp.maximum(m_old, bmax_scr[slot, c:c + 1, :])
        alpha = jnp.exp2(m_old - m_new)
        prob = jnp.exp2(sc - m_new)
        m_scr[tile, c:c + 1, :] = m_new
        r0 = c * HEAD_DIM if fox else (c // 2) * LANES
        vt = jnp.concatenate([vt_ref[blk, r0:r0 + width, :], ones_rows], axis=0)
        acc_scr[tile, c] = alpha * acc_scr[tile, c] + _dot(vt, prob.astype(BF16))

    def step(score_args, soft_args, mask=None):
        for g in range(GROUP_W // LANES):
            if score_args is not None:
                scores(*score_args, g, mask)
            if soft_args is not None:
                softmax_pv(*soft_args, 2 * g)
                softmax_pv(*soft_args, 2 * g + 1)

    div = 1 if fox else CHUNK
    k_id = lax.broadcasted_iota(jnp.int32, (ATT_TILE, ATT_TILE), 0) // div
    q_id = lax.broadcasted_iota(jnp.int32, (ATT_TILE, ATT_TILE), 1) // div
    diag = k_id <= q_id

    def pair(u, carry):
        p = 2 * u + 1
        step((p + 1, 0), (p, 1))
        step((p + 2, 1), (p + 1, 0))
        return carry

    step((0, 0), None, diag)
    step((1, 1), (0, 0))
    lax.fori_loop(0, (last - 3) // 2, pair, 0)
    step((last - 1, 0), (last - 2, 1))
    step((last, 1), (last - 1, 0), diag)
    step(None, (last, 1))

    if not fox:
        lam = (jnp.exp(jnp.sum(lq1_ref[...] * lk1_ref[...], axis=-1, keepdims=True))
               - jnp.exp(jnp.sum(lq2_ref[...] * lk2_ref[...], axis=-1, keepdims=True))
               + lam_init)
    for tile in range(2):
        for g in range(GROUP_W // LANES):
            a1, a2 = acc_scr[tile, 2 * g], acc_scr[tile, 2 * g + 1]
            o1 = a1[:width] / a1[width:width + 1]
            o2 = a2[:width] / a2[width:width + 1]
            if fox:
                o = jnp.concatenate([o1, o2], axis=0).T
            else:
                o = _rms((o1 - lam * o2).T, subw_ref[...]) * (1.0 - lam_init)
            o_ref[0, tile, 0, :, g * LANES:(g + 1) * LANES] = o.astype(BF16)


def _old_attention(qt, k, vt, batch, extra, *, fox, lam_init=0.0):
    n_tok = k.shape[0]
    seq = n_tok // batch
    tiles = seq // ATT_TILE
    assert tiles % 2 == 0
    half = tiles // 2
    width = HEAD_DIM if fox else DIFF_V_DIM
    qt_spec = lambda r, off: pl.BlockSpec((r, ATT_TILE), lambda b, s: (0, b * tiles + s + off))
    kv_spec = lambda w: pl.BlockSpec((seq, w), lambda b, s: (b, 0))
    vt_spec = pl.BlockSpec((tiles, GROUP_W, ATT_TILE), lambda b, s: (b, 0, 0))
    if fox:
        qaugt, kaug = extra
        extra = [qaugt, qaugt, kaug]
        extra_specs = [qt_spec(LANES, 0), qt_spec(LANES, half), kv_spec(LANES)]
    else:
        extra_specs = [_resident(e.shape) for e in extra]
    rhs_w = 2 * LANES if fox else LANES
    out = pl.pallas_call(
        functools.partial(_old_attn_body, fox=fox, lam_init=lam_init, half=half),
        grid=(batch, half),
        in_specs=[qt_spec(GROUP_W, 0), qt_spec(GROUP_W, half), kv_spec(GROUP_W), vt_spec]
        + extra_specs,
        out_specs=pl.BlockSpec((1, 2, 1, ATT_TILE, GROUP_W), lambda b, s: (b, 0, s, 0, 0)),
        out_shape=jax.ShapeDtypeStruct((batch, 2, half, ATT_TILE, GROUP_W), BF16),
        scratch_shapes=[pltpu.VMEM((2, N_CHAINS, rhs_w, ATT_TILE), BF16),
                        pltpu.VMEM((N_CHAINS, ATT_TILE, ATT_TILE), F32),
                        pltpu.VMEM((N_CHAINS, ATT_TILE, ATT_TILE), F32),
                        pltpu.VMEM((2, N_CHAINS, ATT_TILE), F32),
                        pltpu.VMEM((2, N_CHAINS, ATT_TILE), F32),
                        pltpu.VMEM((2, N_CHAINS, width + SUM_ROWS, ATT_TILE), F32)],
        compiler_params=_params(("arbitrary", "arbitrary")),
        name="fox_attn" if fox else "diff_attn",
    )(qt, qt, k, vt, *extra)
    return out.reshape(n_tok, GROUP_W)


class _Group(NamedTuple):
    fox: bool
    qt: tuple
    k: Any
    vt: Any
    qaug: Any
    kaug: Any
    out: Any
    rhs: Any
    s: tuple
    bmax: Any
    m: Any
    acc: Any

    @property
    def width(self):
        return HEAD_DIM if self.fox else DIFF_V_DIM


def _attn_scratch(fox):
    width = HEAD_DIM if fox else DIFF_V_DIM
    rhs_w = 2 * LANES if fox else LANES
    return [pltpu.VMEM((2, N_CHAINS, rhs_w, ATT_TILE), BF16),
            pltpu.VMEM((N_CHAINS, ATT_TILE, ATT_TILE), F32),
            pltpu.VMEM((N_CHAINS, ATT_TILE, ATT_TILE), F32),
            pltpu.VMEM((2, N_CHAINS, ATT_TILE), F32),
            pltpu.VMEM((2, N_CHAINS, ATT_TILE), F32),
            pltpu.VMEM((2, N_CHAINS, width + SUM_ROWS, ATT_TILE), F32)]


def _attn_body(dqa, dqb, dk, dvt, fqa, fqb, fk, fvt, qauga, qaugb, kaug, lq1_ref, lk1_ref,
               lq2_ref, lk2_ref, subw_ref, od_ref, of_ref, *scratch, lam_init, half):
    def group(fox, qt, k, vt, qaug, kaug_ref, out, scr):
        rhs, s0, s1, bmax, m, acc = scr
        return _Group(fox, qt, k, vt, qaug, kaug_ref, out, rhs, (s0, s1), bmax, m, acc)

    groups = (group(False, (dqa, dqb), dk, dvt, None, None, od_ref, scratch[:6]),
              group(True, (fqa, fqb), fk, fvt, (qauga, qaugb), kaug, of_ref, scratch[6:]))
    s = pl.program_id(1)
    last = 2 * s + half + 1
    ones_rows = jnp.ones((SUM_ROWS, ATT_TILE), BF16)

    half_zero = jnp.zeros((HEAD_DIM, ATT_TILE), BF16)
    for gr in groups:
        for tile, qt_ref in enumerate(gr.qt):
            for g in range(GROUP_W // LANES):
                r0 = g * LANES
                gr.rhs[tile, 2 * g, :HEAD_DIM, :] = qt_ref[r0:r0 + HEAD_DIM, :]
                gr.rhs[tile, 2 * g, HEAD_DIM:LANES, :] = half_zero
                gr.rhs[tile, 2 * g + 1, :HEAD_DIM, :] = half_zero
                gr.rhs[tile, 2 * g + 1, HEAD_DIM:LANES, :] = qt_ref[r0 + HEAD_DIM:r0 + LANES, :]
            if gr.fox:
                for c in range(N_CHAINS):
                    b0 = c * BIAS_SLOT
                    gr.rhs[tile, c, LANES:, :] = jnp.zeros((LANES, ATT_TILE), BF16)
                    gr.rhs[tile, c, LANES + b0:LANES + b0 + BIAS_SLOT, :] = (
                        gr.qaug[tile][b0:b0 + BIAS_SLOT, :])
        gr.m[...] = jnp.full(gr.m.shape, NEG_INF, F32)
        gr.acc[...] = jnp.zeros(gr.acc.shape, F32)

    def locate(p):
        in_b = p > s
        return in_b.astype(jnp.int32), jnp.where(in_b, p - s - 1, s - p)

    def diagonal_mask(gr):
        div = 1 if gr.fox else CHUNK
        k_id = lax.broadcasted_iota(jnp.int32, (ATT_TILE, ATT_TILE), 0) // div
        q_id = lax.broadcasted_iota(jnp.int32, (ATT_TILE, ATT_TILE), 1) // div
        return k_id <= q_id

    def scores(gr, p, slot, g, masked):
        tile, blk = locate(p)
        start = pl.multiple_of(blk * ATT_TILE, ATT_TILE)
        lhs = gr.k[pl.ds(start, ATT_TILE), g * LANES:(g + 1) * LANES]
        if gr.fox:
            lhs = jnp.concatenate([lhs, gr.kaug[pl.ds(start, ATT_TILE), :]], axis=-1)
        for c in (2 * g, 2 * g + 1):
            sc = _dot(lhs, gr.rhs[tile, c])
            if masked:
                sc = jnp.where(diagonal_mask(gr), sc, NEG_INF)
            gr.s[slot][c] = sc
            gr.bmax[slot, c:c + 1, :] = jnp.max(sc, axis=0, keepdims=True)

    def softmax_pv(gr, p, slot, c):
        tile, blk = locate(p)
        sc = gr.s[slot][c]
        m_old = gr.m[tile, c:c + 1, :]
        m_new = jnp.maximum(m_old, gr.bmax[slot, c:c + 1, :])
        alpha = jnp.exp2(m_old - m_new)
        prob = jnp.exp2(sc - m_new)
        gr.m[tile, c:c + 1, :] = m_new
        r0 = c * HEAD_DIM if gr.fox else (c // 2) * LANES
        vt = jnp.concatenate([gr.vt[blk, r0:r0 + gr.width, :], ones_rows], axis=0)
        gr.acc[tile, c] = alpha * gr.acc[tile, c] + _dot(vt, prob.astype(BF16))

    def step(score_args, soft_args, masked=False):
        for g in range(GROUP_W // LANES):
            for gr in groups:
                if score_args is not None:
                    scores(gr, *score_args, g, masked)
                if soft_args is not None:
                    softmax_pv(gr, *soft_args, 2 * g)
                    softmax_pv(gr, *soft_args, 2 * g + 1)

    def pair(u, carry):
        p = 2 * u + 1
        step((p + 1, 0), (p, 1))
        step((p + 2, 1), (p + 1, 0))
        return carry

    step((0, 0), None, masked=True)
    step((1, 1), (0, 0))
    lax.fori_loop(0, (last - 3) // 2, pair, 0)
    step((last - 1, 0), (last - 2, 1))
    step((last, 1), (last - 1, 0), masked=True)
    step(None, (last, 1))

    lam = (jnp.exp(jnp.sum(lq1_ref[...] * lk1_ref[...], axis=-1, keepdims=True))
           - jnp.exp(jnp.sum(lq2_ref[...] * lk2_ref[...], axis=-1, keepdims=True))
           + lam_init)
    for gr in groups:
        width = gr.width
        for tile in range(2):
            for g in range(GROUP_W // LANES):
                a1, a2 = gr.acc[tile, 2 * g], gr.acc[tile, 2 * g + 1]
                o1 = a1[:width] / a1[width:width + 1]
                o2 = a2[:width] / a2[width:width + 1]
                if gr.fox:
                    o = jnp.concatenate([o1, o2], axis=0).T
                else:
                    o = _rms((o1 - lam * o2).T, subw_ref[...]) * (1.0 - lam_init)
                gr.out[0, tile, 0, :, g * LANES:(g + 1) * LANES] = o.astype(BF16)


def _attention(dqt, dk, dvt, fqt, fk, fvt, qaugt, kaug, lam_params, batch, lam_init):
    n_tok = dk.shape[0]
    seq = n_tok // batch
    tiles = seq // ATT_TILE
    assert tiles % 2 == 0
    half = tiles // 2
    qt_spec = lambda r, off: pl.BlockSpec((r, ATT_TILE), lambda b, s: (0, b * tiles + s + off))
    kv_spec = lambda w: pl.BlockSpec((seq, w), lambda b, s: (b, 0))
    vt_spec = pl.BlockSpec((tiles, GROUP_W, ATT_TILE), lambda b, s: (b, 0, 0))
    qkv_specs = [qt_spec(GROUP_W, 0), qt_spec(GROUP_W, half), kv_spec(GROUP_W), vt_spec]
    out_spec = pl.BlockSpec((1, 2, 1, ATT_TILE, GROUP_W), lambda b, s: (b, 0, s, 0, 0))
    out_shape = jax.ShapeDtypeStruct((batch, 2, half, ATT_TILE, GROUP_W), BF16)
    od, of = pl.pallas_call(
        functools.partial(_attn_body, lam_init=lam_init, half=half),
        grid=(batch, half),
        in_specs=qkv_specs + qkv_specs
        + [qt_spec(LANES, 0), qt_spec(LANES, half), kv_spec(LANES)]
        + [_resident(p.shape) for p in lam_params],
        out_specs=[out_spec, out_spec],
        out_shape=[out_shape, out_shape],
        scratch_shapes=_attn_scratch(False) + _attn_scratch(True),
        compiler_params=_params(("arbitrary", "arbitrary")),
        name="attention",
    )(dqt, dqt, dk, dvt, fqt, fqt, fk, fvt, qaugt, qaugt, kaug, *lam_params)
    return od.reshape(n_tok, GROUP_W), of.reshape(n_tok, GROUP_W)


def _mem_kv_body(m_ref, nw_ref, w_ref, kw_ref, mk_ref, mv_ref):
    hm = _rms(m_ref[...], nw_ref[...]).astype(BF16)
    kv = _dot(hm, w_ref[...])
    kw = kw_ref[...]
    for c in range(0, D_MODEL, MEM_HEAD_DIM):
        mk_ref[:, c:c + MEM_HEAD_DIM] = _rms(kv[:, c:c + MEM_HEAD_DIM], kw).astype(BF16)
    mv_ref[...] = kv[:, D_MODEL:].astype(BF16)


def _mem_kv(mem2d, norm_w, w_mem_kv, mem_k_norm_w):
    n_mem = mem2d.shape[0]
    tok = pl.BlockSpec((TOK_TILE, D_MODEL), lambda i: (i, 0))
    out = jax.ShapeDtypeStruct((n_mem, D_MODEL), BF16)
    return pl.pallas_call(
        _mem_kv_body,
        grid=(n_mem // TOK_TILE,),
        in_specs=[tok, _resident((1, D_MODEL)), _resident((D_MODEL, 2 * D_MODEL)),
                  _resident((1, MEM_HEAD_DIM))],
        out_specs=[tok, tok],
        out_shape=[out, out],
        compiler_params=_params(("arbitrary",)),
        name="mem_kv",
    )(mem2d, norm_w.reshape(1, -1), w_mem_kv.astype(BF16), mem_k_norm_w.reshape(1, -1))


POST_TILE = 512
FF_CHUNK = 1024


def _post_attn_body(x_ref, od_ref, of_ref, mk_ref, mv_ref, wo_ref, nq_ref, wq_ref, qw_ref,
                    wmo_ref, nm_ref, wu_ref, wd_ref, o_ref):
    x1 = x_ref[...] + (_dot(od_ref[...], wo_ref[:GROUP_W, :])
                       + _dot(of_ref[...], wo_ref[GROUP_W:, :]))

    mq = _dot(_rms(x1, nq_ref[...]).astype(BF16), wq_ref[...])
    qw = qw_ref[...]
    heads = []
    for c in range(0, D_MODEL, MEM_HEAD_DIM):
        q = _rms(mq[:, c:c + MEM_HEAD_DIM], qw).astype(BF16)
        s = _dot_nt(q, mk_ref[:, c:c + MEM_HEAD_DIM])
        p = jnp.exp(s - jnp.max(s, axis=-1, keepdims=True))
        l = jnp.sum(p, axis=-1, keepdims=True)
        heads.append((_dot(p.astype(BF16), mv_ref[:, c:c + MEM_HEAD_DIM]) / l).astype(BF16))
    x2 = x1 + _dot(jnp.concatenate(heads, axis=-1), wmo_ref[...])

    h = _rms(x2, nm_ref[...]).astype(BF16)
    acc = x2
    for c in range(0, D_FF, FF_CHUNK):
        u = jnp.maximum(_dot(h, wu_ref[:, c:c + FF_CHUNK]), 0.0)
        acc = acc + _dot((u * u).astype(BF16), wd_ref[c:c + FF_CHUNK, :])
    o_ref[...] = acc


def _post_attn(x2d, od, of, mk, mv, batch, w_out, norm_q, w_mem_q, mem_q_norm_w, w_mem_o,
               norm_mlp, w_up, w_down):
    n_tok = x2d.shape[0]
    tiles = n_tok // batch // POST_TILE
    mem_len = mk.shape[0] // batch
    tok = lambda width: pl.BlockSpec((POST_TILE, width), lambda b, i: (b * tiles + i, 0))
    mem = pl.BlockSpec((mem_len, D_MODEL), lambda b, i: (b, 0))
    scale = MEM_HEAD_DIM ** -0.5
    square = _resident((D_MODEL, D_MODEL))
    vec = _resident((1, D_MODEL))
    return pl.pallas_call(
        _post_attn_body,
        grid=(batch, tiles),
        in_specs=[tok(D_MODEL), tok(GROUP_W), tok(GROUP_W), mem, mem, square, vec, square,
                  _resident((1, MEM_HEAD_DIM)), square, vec, _resident((D_MODEL, D_FF)),
                  _resident((D_FF, D_MODEL))],
        out_specs=tok(D_MODEL),
        out_shape=jax.ShapeDtypeStruct((n_tok, D_MODEL), F32),
        compiler_params=_params(("arbitrary", "arbitrary")),
        name="post_attn",
    )(x2d, od, of, mk, mv, w_out.astype(BF16), norm_q.reshape(1, -1), w_mem_q.astype(BF16),
      (mem_q_norm_w * scale).reshape(1, -1), w_mem_o.astype(BF16), norm_mlp.reshape(1, -1),
      w_up.astype(BF16), w_down.astype(BF16))


def kernel(x, mem, positions, norm_mix_w, w_in, b_forget, diff_q_norm_w, diff_k_norm_w,
           lambda_q1, lambda_k1, lambda_q2, lambda_k2, diff_subln_w, fox_q_norm_w,
           fox_k_norm_w, w_out, norm_mem_q_w, norm_mem_kv_w, w_mem_q, w_mem_kv,
           mem_q_norm_w, mem_k_norm_w, w_mem_o, norm_mlp_w, w_up, w_down):
    batch, seq, d = x.shape
    depth = w_in.shape[0]
    assert d == D_MODEL and seq % TOK_TILE == 0 and seq % POST_TILE == 0
    assert TOK_TILE % ATT_TILE == 0
    assert (batch * mem.shape[1]) % TOK_TILE == 0

    cos_t, sin_t = _rope_tables(positions)
    xc = x.reshape(batch * seq, d)
    mem2d = mem.reshape(-1, d)
    for l in range(depth):
        lam_init = 0.8 - 0.6 * math.exp(-0.3 * l)
        dqt, dk, dvt, fqt, fk, fvt, qaugt, kaug = _in_proj(
            xc, batch, norm_mix_w[l], w_in[l], cos_t, sin_t, diff_q_norm_w[l],
            diff_k_norm_w[l], fox_q_norm_w[l], fox_k_norm_w[l], b_forget[l])
        lam_params = [p[l].reshape(1, -1) for p in
                      (lambda_q1, lambda_k1, lambda_q2, lambda_k2, diff_subln_w)]
        od, of = _attention(dqt, dk, dvt, fqt, fk, fvt, qaugt, kaug, lam_params, batch, lam_init)
        mk, mv = _mem_kv(mem2d, norm_mem_kv_w[l], w_mem_kv[l], mem_k_norm_w[l])
        xc = _post_attn(xc, od, of, mk, mv, batch, w_out[l], norm_mem_q_w[l], w_mem_q[l],
                        mem_q_norm_w[l], w_mem_o[l], norm_mlp_w[l], w_up[l], w_down[l])
    return xc.reshape(batch, seq, d)
```

```python
import functools
import math
from typing import Any, NamedTuple

import jax
import jax.numpy as jnp
from jax import lax
from jax.experimental import pallas as pl
from jax.experimental.pallas import tpu as pltpu

F32 = jnp.float32
BF16 = jnp.bfloat16

D_MODEL = 1024
CHUNK = 64
HEAD_DIM = 64
N_DIFF_HEADS = 4
DIFF_V_DIM = 2 * HEAD_DIM
N_FOX_HEADS = 8
GROUP_W = 512
ROPE_DIM = HEAD_DIM // 4
ROPE_THETA = 500000.0
N_MEM_HEADS = 4
MEM_HEAD_DIM = D_MODEL // N_MEM_HEADS
D_FF = 4 * D_MODEL
EPS = 1e-6
NEG_INF = -1e30
LOG2E = math.log2(math.e)

LANES = 128
MXU_DIM = 256
VMEM_LIMIT = 56 * 1024 * 1024

TOK_TILE = 1024
ATT_TILE = 256


def _params(sem):
    return pltpu.CompilerParams(dimension_semantics=sem, vmem_limit_bytes=VMEM_LIMIT)


def _resident(shape):
    return pl.BlockSpec(shape, lambda *_: (0,) * len(shape), pipeline_mode=pl.Buffered(1))


def _rms(x, w):
    ms = jnp.mean(x * x, axis=-1, keepdims=True)
    return (x * lax.rsqrt(ms + EPS)) * w


def _dot(a, b):
    return jnp.dot(a, b, preferred_element_type=F32)


def _dot_nt(a, b):
    return lax.dot_general(a, b, (((1,), (1,)), ((), ())), preferred_element_type=F32)


def _rope_body(pos_ref, freq_ref, cos_ref, sin_ref):
    ang = pos_ref[...] * freq_ref[...]
    cos_ref[...] = jnp.cos(ang)
    sin_ref[...] = jnp.sin(ang)


def _rope_tables(positions):
    n_tok = positions.size
    half = ROPE_DIM // 2
    inv_freq = ROPE_THETA ** (-jnp.arange(0, ROPE_DIM, 2, dtype=F32) / ROPE_DIM)
    out = jax.ShapeDtypeStruct((half, n_tok), F32)
    return pl.pallas_call(_rope_body, out_shape=(out, out), name="rope_tables")(
        positions.reshape(1, n_tok).astype(F32), inv_freq.reshape(half, 1))


def _log_sigmoid(x):
    return jnp.minimum(x, 0.0) - jnp.log1p(jnp.exp(-jnp.abs(x)))


def _split3(x):
    hi = x.astype(BF16)
    rem = x - hi.astype(F32)
    mid = rem.astype(BF16)
    lo = (rem - mid.astype(F32)).astype(BF16)
    return hi, mid, lo


BIAS_SLOT = LANES // N_FOX_HEADS


def _head_norm_t(t, gain, rope):
    half = ROPE_DIM // 2
    outs = []
    for r0 in range(0, t.shape[0], HEAD_DIM):
        th = t[r0:r0 + HEAD_DIM]
        ms = jnp.sum(th * th, axis=0, keepdims=True) * (1.0 / HEAD_DIM)
        th = (th * lax.rsqrt(ms + EPS)) * gain[r0:r0 + HEAD_DIM]
        if rope is not None:
            cos, sin = rope
            t1, t2 = th[:half], th[half:ROPE_DIM]
            th = jnp.concatenate(
                [t1 * cos - t2 * sin, t2 * cos + t1 * sin, th[ROPE_DIM:]], axis=0)
        outs.append(th)
    return jnp.concatenate(outs, axis=0)


def _in_proj_body(x_ref, nw_ref, wt_ref, wft_ref, cos_ref, sin_ref, gain_ref, bf_ref, dqt_ref,
                  dk_ref, dvt_ref, fqt_ref, fk_ref, fvt_ref, qaugt_ref, kaug_ref, carry_ref):
    @pl.when(pl.program_id(1) == 0)
    def _():
        carry_ref[...] = jnp.zeros_like(carry_ref)

    h = _rms(x_ref[...], nw_ref[...]).astype(BF16)
    rope = (cos_ref[...], sin_ref[...])
    g = GROUP_W

    def proj_t(group):
        r0 = group * GROUP_W
        return _dot_nt(wt_ref[r0:r0 + GROUP_W, :], h)

    def store_keys(k_ref, kt):
        for c in range(0, g, LANES):
            k_ref[:, c:c + LANES] = kt[c:c + LANES].T.astype(BF16)

    def store_values(vt_ref, vt):
        for t in range(TOK_TILE // ATT_TILE):
            vt_ref[t] = vt[:, t * ATT_TILE:(t + 1) * ATT_TILE].astype(BF16)

    dqt_ref[...] = _head_norm_t(proj_t(0), gain_ref[0:g], rope).astype(BF16)
    store_keys(dk_ref, _head_norm_t(proj_t(1), gain_ref[g:2 * g], rope))
    store_values(dvt_ref, proj_t(2))
    fqt_ref[...] = _head_norm_t(proj_t(3), gain_ref[2 * g:3 * g], None).astype(BF16)
    store_keys(fk_ref, _head_norm_t(proj_t(4), gain_ref[3 * g:4 * g], None))
    store_values(fvt_ref, proj_t(5))

    log_f = _log_sigmoid(_dot_nt(wft_ref[...], h) + bf_ref[...])
    r = lax.broadcasted_iota(jnp.int32, (MXU_DIM, MXU_DIM), 0)
    c = lax.broadcasted_iota(jnp.int32, (MXU_DIM, MXU_DIM), 1)
    triu = jnp.where(r <= c, 1.0, 0.0).astype(BF16)
    part = lax.broadcasted_iota(jnp.int32, (LANES, MXU_DIM), 0) % BIAS_SLOT
    carry = carry_ref[:, 0:1]
    for c0 in range(0, TOK_TILE, MXU_DIM):
        hi, mid, lo = _split3(log_f[:, c0:c0 + MXU_DIM])
        cum = (_dot(hi, triu) + _dot(mid, triu)) + _dot(lo, triu) + carry
        carry = cum[:, MXU_DIM - 1:MXU_DIM]
        hi, mid, lo = (t.astype(F32) for t in _split3(cum * LOG2E))
        kaug_t = jnp.where(part == 0, -hi, jnp.where(part == 1, -mid, jnp.where(
            part == 2, -lo, jnp.where(part < 6, 1.0, 0.0))))
        qaug_t = jnp.where(part < 3, 1.0, jnp.where(part == 3, hi, jnp.where(
            part == 4, mid, jnp.where(part == 5, lo, 0.0))))
        kaug_ref[c0:c0 + MXU_DIM, :] = kaug_t.T.astype(BF16)
        qaugt_ref[:, c0:c0 + MXU_DIM] = qaug_t.astype(BF16)
    carry_ref[...] = jnp.broadcast_to(carry, carry_ref.shape)


def _in_proj(x2d, batch, norm_w, w_in, cos_t, sin_t, dqw, dkw, fqw, fkw, b_forget):
    n_tok = x2d.shape[0]
    tiles = n_tok // batch // TOK_TILE
    g = GROUP_W
    wt = w_in[:, :6 * g].T.astype(BF16)
    wft = jnp.repeat(w_in[:, 6 * g:6 * g + N_FOX_HEADS].T, BIAS_SLOT, axis=0).astype(BF16)
    bf = jnp.repeat(b_forget, BIAS_SLOT).reshape(LANES, 1)
    qscale = HEAD_DIM ** -0.5 * LOG2E
    gains = jnp.concatenate([jnp.tile(v, g // HEAD_DIM) for v in
                             (dqw * qscale, dkw, fqw * qscale, fkw)]).reshape(4 * g, 1)

    row = lambda b, i: (b * tiles + i, 0)
    col = lambda b, i: (0, b * tiles + i)
    tok = lambda width: pl.BlockSpec((TOK_TILE, width), row)
    tok_t = lambda rows: pl.BlockSpec((rows, TOK_TILE), col)
    slabs = TOK_TILE // ATT_TILE
    vt_spec = pl.BlockSpec((slabs, g, ATT_TILE), lambda b, i: (b * tiles + i, 0, 0))
    k_shape = jax.ShapeDtypeStruct((n_tok, g), BF16)
    qt_shape = jax.ShapeDtypeStruct((g, n_tok), BF16)
    vt_shape = jax.ShapeDtypeStruct((n_tok // ATT_TILE, g, ATT_TILE), BF16)
    half = ROPE_DIM // 2
    return pl.pallas_call(
        _in_proj_body,
        grid=(batch, tiles),
        in_specs=[tok(D_MODEL), _resident((1, D_MODEL)), _resident(wt.shape),
                  _resident(wft.shape), tok_t(half), tok_t(half), _resident(gains.shape),
                  _resident(bf.shape)],
        out_specs=[tok_t(g), tok(g), vt_spec, tok_t(g), tok(g), vt_spec, tok_t(LANES),
                   tok(LANES)],
        out_shape=[qt_shape, k_shape, vt_shape, qt_shape, k_shape, vt_shape,
                   jax.ShapeDtypeStruct((LANES, n_tok), BF16),
                   jax.ShapeDtypeStruct((n_tok, LANES), BF16)],
        scratch_shapes=[pltpu.VMEM((LANES, LANES), F32)],
        compiler_params=_params(("arbitrary", "arbitrary")),
        name="in_proj",
    )(x2d, norm_w.reshape(1, -1), wt, wft, cos_t, sin_t, gains, bf)


N_CHAINS = 2 * (GROUP_W // LANES)
SUM_ROWS = 16


class _Group(NamedTuple):
    fox: bool
    qt: tuple
    k: Any
    vt: Any
    qaug: Any
    kaug: Any
    out: Any
    rhs: Any
    s: tuple
    bmax: Any
    m: Any
    acc: Any

    @property
    def width(self):
        return HEAD_DIM if self.fox else DIFF_V_DIM


def _attn_scratch(fox):
    width = HEAD_DIM if fox else DIFF_V_DIM
    rhs_w = 2 * LANES if fox else LANES
    return [pltpu.VMEM((2, N_CHAINS, rhs_w, ATT_TILE), BF16),
            pltpu.VMEM((N_CHAINS, ATT_TILE, ATT_TILE), F32),
            pltpu.VMEM((N_CHAINS, ATT_TILE, ATT_TILE), F32),
            pltpu.VMEM((2, N_CHAINS, ATT_TILE), F32),
            pltpu.VMEM((2, N_CHAINS, ATT_TILE), F32),
            pltpu.VMEM((2, N_CHAINS, width + SUM_ROWS, ATT_TILE), F32)]


def _attn_body(dqa, dqb, dk, dvt, fqa, fqb, fk, fvt, qauga, qaugb, kaug, lq1_ref, lk1_ref,
               lq2_ref, lk2_ref, subw_ref, od_ref, of_ref, *scratch, lam_init, half):
    def group(fox, qt, k, vt, qaug, kaug_ref, out, scr):
        rhs, s0, s1, bmax, m, acc = scr
        return _Group(fox, qt, k, vt, qaug, kaug_ref, out, rhs, (s0, s1), bmax, m, acc)

    groups = (group(False, (dqa, dqb), dk, dvt, None, None, od_ref, scratch[:6]),
              group(True, (fqa, fqb), fk, fvt, (qauga, qaugb), kaug, of_ref, scratch[6:]))
    s = pl.program_id(1)
    last = 2 * s + half + 1
    ones_rows = jnp.ones((SUM_ROWS, ATT_TILE), BF16)

    half_zero = jnp.zeros((HEAD_DIM, ATT_TILE), BF16)
    for gr in groups:
        for tile, qt_ref in enumerate(gr.qt):
            for g in range(GROUP_W // LANES):
                r0 = g * LANES
                gr.rhs[tile, 2 * g, :HEAD_DIM, :] = qt_ref[r0:r0 + HEAD_DIM, :]
                gr.rhs[tile, 2 * g, HEAD_DIM:LANES, :] = half_zero
                gr.rhs[tile, 2 * g + 1, :HEAD_DIM, :] = half_zero
                gr.rhs[tile, 2 * g + 1, HEAD_DIM:LANES, :] = qt_ref[r0 + HEAD_DIM:r0 + LANES, :]
            if gr.fox:
                for c in range(N_CHAINS):
                    b0 = c * BIAS_SLOT
                    gr.rhs[tile, c, LANES:, :] = jnp.zeros((LANES, ATT_TILE), BF16)
                    gr.rhs[tile, c, LANES + b0:LANES + b0 + BIAS_SLOT, :] = (
                        gr.qaug[tile][b0:b0 + BIAS_SLOT, :])
        gr.m[...] = jnp.full(gr.m.shape, NEG_INF, F32)
        gr.acc[...] = jnp.zeros(gr.acc.shape, F32)

    def locate(p):
        in_b = p > s
        return in_b.astype(jnp.int32), jnp.where(in_b, p - s - 1, s - p)

    def diagonal_mask(gr):
        div = 1 if gr.fox else CHUNK
        k_id = lax.broadcasted_iota(jnp.int32, (ATT_TILE, ATT_TILE), 0) // div
        q_id = lax.broadcasted_iota(jnp.int32, (ATT_TILE, ATT_TILE), 1) // div
        return k_id <= q_id

    def scores(gr, p, slot, g, masked):
        tile, blk = locate(p)
        start = pl.multiple_of(blk * ATT_TILE, ATT_TILE)
        lhs = gr.k[pl.ds(start, ATT_TILE), g * LANES:(g + 1) * LANES]
        if gr.fox:
            lhs = jnp.concatenate([lhs, gr.kaug[pl.ds(start, ATT_TILE), :]], axis=-1)
        for c in (2 * g, 2 * g + 1):
            sc = _dot(lhs, gr.rhs[tile, c])
            if masked:
                sc = jnp.where(diagonal_mask(gr), sc, NEG_INF)
            gr.s[slot][c] = sc
            gr.bmax[slot, c:c + 1, :] = jnp.max(sc, axis=0, keepdims=True)

    def softmax_pv(gr, p, slot, c):
        tile, blk = locate(p)
        sc = gr.s[slot][c]
        m_old = gr.m[tile, c:c + 1, :]
        m_new = jnp.maximum(m_old, gr.bmax[slot, c:c + 1, :])
        alpha = jnp.exp2(m_old - m_new)
        prob = jnp.exp2(sc - m_new)
        gr.m[tile, c:c + 1, :] = m_new
        r0 = c * HEAD_DIM if gr.fox else (c // 2) * LANES
        vt = jnp.concatenate([gr.vt[blk, r0:r0 + gr.width, :], ones_rows], axis=0)
        gr.acc[tile, c] = alpha * gr.acc[tile, c] + _dot(vt, prob.astype(BF16))

    def step(score_args, soft_args, masked=False):
        for g in range(GROUP_W // LANES):
            for gr in groups:
                if score_args is not None:
                    scores(gr, *score_args, g, masked)
                if soft_args is not None:
                    softmax_pv(gr, *soft_args, 2 * g)
                    softmax_pv(gr, *soft_args, 2 * g + 1)

    def pair(u, carry):
        p = 2 * u + 1
        step((p + 1, 0), (p, 1))
        step((p + 2, 1), (p + 1, 0))
        return carry

    step((0, 0), None, masked=True)
    step((1, 1), (0, 0))
    lax.fori_loop(0, (last - 3) // 2, pair, 0)
    step((last - 1, 0), (last - 2, 1))
    step((last, 1), (last - 1, 0), masked=True)
    step(None, (last, 1))

    lam = (jnp.exp(jnp.sum(lq1_ref[...] * lk1_ref[...], axis=-1, keepdims=True))
           - jnp.exp(jnp.sum(lq2_ref[...] * lk2_ref[...], axis=-1, keepdims=True))
           + lam_init)
    for gr in groups:
        width = gr.width
        for tile in range(2):
            for g in range(GROUP_W // LANES):
                a1, a2 = gr.acc[tile, 2 * g], gr.acc[tile, 2 * g + 1]
                o1 = a1[:width] * (1.0 / a1[width:width + 1])
                o2 = a2[:width] * (1.0 / a2[width:width + 1])
                if gr.fox:
                    o_t = jnp.concatenate([o1, o2], axis=0)
                else:
                    o_t = o1 - lam * o2
                    ms = jnp.mean(o_t * o_t, axis=0, keepdims=True)
                    o_t = ((o_t * lax.rsqrt(ms + EPS)) * subw_ref[...]) * (1.0 - lam_init)
                gr.out[0, tile, 0, :, g * LANES:(g + 1) * LANES] = o_t.T.astype(BF16)


def _attention(dqt, dk, dvt, fqt, fk, fvt, qaugt, kaug, lam_params, batch, lam_init):
    n_tok = dk.shape[0]
    seq = n_tok // batch
    tiles = seq // ATT_TILE
    assert tiles % 2 == 0
    half = tiles // 2
    qt_spec = lambda r, off: pl.BlockSpec((r, ATT_TILE), lambda b, s: (0, b * tiles + s + off))
    kv_spec = lambda w: pl.BlockSpec((seq, w), lambda b, s: (b, 0))
    vt_spec = pl.BlockSpec((tiles, GROUP_W, ATT_TILE), lambda b, s: (b, 0, 0))
    qkv_specs = [qt_spec(GROUP_W, 0), qt_spec(GROUP_W, half), kv_spec(GROUP_W), vt_spec]
    out_spec = pl.BlockSpec((1, 2, 1, ATT_TILE, GROUP_W), lambda b, s: (b, 0, s, 0, 0))
    out_shape = jax.ShapeDtypeStruct((batch, 2, half, ATT_TILE, GROUP_W), BF16)
    od, of = pl.pallas_call(
        functools.partial(_attn_body, lam_init=lam_init, half=half),
        grid=(batch, half),
        in_specs=qkv_specs + qkv_specs
        + [qt_spec(LANES, 0), qt_spec(LANES, half), kv_spec(LANES)]
        + [_resident(p.shape) for p in lam_params],
        out_specs=[out_spec, out_spec],
        out_shape=[out_shape, out_shape],
        scratch_shapes=_attn_scratch(False) + _attn_scratch(True),
        compiler_params=_params(("arbitrary", "arbitrary")),
        name="attention",
    )(dqt, dqt, dk, dvt, fqt, fqt, fk, fvt, qaugt, qaugt, kaug, *lam_params)
    return od.reshape(n_tok, GROUP_W), of.reshape(n_tok, GROUP_W)


def _mem_kv_body(m_ref, nw_ref, w_ref, kw_ref, mk_ref, mv_ref):
    hm = _rms(m_ref[...], nw_ref[...]).astype(BF16)
    kv = _dot(hm, w_ref[...])
    kw = kw_ref[...]
    for c in range(0, D_MODEL, MEM_HEAD_DIM):
        mk_ref[:, c:c + MEM_HEAD_DIM] = _rms(kv[:, c:c + MEM_HEAD_DIM], kw).astype(BF16)
    mv_ref[...] = kv[:, D_MODEL:].astype(BF16)


def _mem_kv(mem2d, norm_w, w_mem_kv, mem_k_norm_w):
    n_mem = mem2d.shape[0]
    tok = pl.BlockSpec((TOK_TILE, D_MODEL), lambda i: (i, 0))
    out = jax.ShapeDtypeStruct((n_mem, D_MODEL), BF16)
    return pl.pallas_call(
        _mem_kv_body,
        grid=(n_mem // TOK_TILE,),
        in_specs=[tok, _resident((1, D_MODEL)), _resident((D_MODEL, 2 * D_MODEL)),
                  _resident((1, MEM_HEAD_DIM))],
        out_specs=[tok, tok],
        out_shape=[out, out],
        compiler_params=_params(("arbitrary",)),
        name="mem_kv",
    )(mem2d, norm_w.reshape(1, -1), w_mem_kv.astype(BF16), mem_k_norm_w.reshape(1, -1))


POST_TILE = 512
FF_CHUNK = 1024


def _post_attn_body(x_ref, od_ref, of_ref, mk_ref, mv_ref, wo_ref, nq_ref, wq_ref, qw_ref,
                    wmo_ref, nm_ref, wu_ref, wd_ref, o_ref):
    x1 = x_ref[...] + (_dot(od_ref[...], wo_ref[:GROUP_W, :])
                       + _dot(of_ref[...], wo_ref[GROUP_W:, :]))

    mq = _dot(_rms(x1, nq_ref[...]).astype(BF16), wq_ref[...])
    qw = qw_ref[...]
    heads = []
    for c in range(0, D_MODEL, MEM_HEAD_DIM):
        q = _rms(mq[:, c:c + MEM_HEAD_DIM], qw).astype(BF16)
        s = _dot_nt(q, mk_ref[:, c:c + MEM_HEAD_DIM])
        p = jnp.exp(s - jnp.max(s, axis=-1, keepdims=True))
        l = jnp.sum(p, axis=-1, keepdims=True)
        heads.append((_dot(p.astype(BF16), mv_ref[:, c:c + MEM_HEAD_DIM]) / l).astype(BF16))
    x2 = x1 + _dot(jnp.concatenate(heads, axis=-1), wmo_ref[...])

    h = _rms(x2, nm_ref[...]).astype(BF16)
    acc = x2
    for c in range(0, D_FF, FF_CHUNK):
        u = jnp.maximum(_dot(h, wu_ref[:, c:c + FF_CHUNK]), 0.0)
        acc = acc + _dot((u * u).astype(BF16), wd_ref[c:c + FF_CHUNK, :])
    o_ref[...] = acc


def _post_attn(x2d, od, of, mk, mv, batch, w_out, norm_q, w_mem_q, mem_q_norm_w, w_mem_o,
               norm_mlp, w_up, w_down):
    n_tok = x2d.shape[0]
    tiles = n_tok // batch // POST_TILE
    mem_len = mk.shape[0] // batch
    tok = lambda width: pl.BlockSpec((POST_TILE, width), lambda b, i: (b * tiles + i, 0))
    mem = pl.BlockSpec((mem_len, D_MODEL), lambda b, i: (b, 0))
    scale = MEM_HEAD_DIM ** -0.5
    square = _resident((D_MODEL, D_MODEL))
    vec = _resident((1, D_MODEL))
    return pl.pallas_call(
        _post_attn_body,
        grid=(batch, tiles),
        in_specs=[tok(D_MODEL), tok(GROUP_W), tok(GROUP_W), mem, mem, square, vec, square,
                  _resident((1, MEM_HEAD_DIM)), square, vec, _resident((D_MODEL, D_FF)),
                  _resident((D_FF, D_MODEL))],
        out_specs=tok(D_MODEL),
        out_shape=jax.ShapeDtypeStruct((n_tok, D_MODEL), F32),
        compiler_params=_params(("arbitrary", "arbitrary")),
        name="post_attn",
    )(x2d, od, of, mk, mv, w_out.astype(BF16), norm_q.reshape(1, -1), w_mem_q.astype(BF16),
      (mem_q_norm_w * scale).reshape(1, -1), w_mem_o.astype(BF16), norm_mlp.reshape(1, -1),
      w_up.astype(BF16), w_down.astype(BF16))


def kernel(x, mem, positions, norm_mix_w, w_in, b_forget, diff_q_norm_w, diff_k_norm_w,
           lambda_q1, lambda_k1, lambda_q2, lambda_k2, diff_subln_w, fox_q_norm_w,
           fox_k_norm_w, w_out, norm_mem_q_w, norm_mem_kv_w, w_mem_q, w_mem_kv,
           mem_q_norm_w, mem_k_norm_w, w_mem_o, norm_mlp_w, w_up, w_down):
    batch, seq, d = x.shape
    depth = w_in.shape[0]
    assert d == D_MODEL and seq % TOK_TILE == 0 and seq % POST_TILE == 0
    assert TOK_TILE % ATT_TILE == 0
    assert (batch * mem.shape[1]) % TOK_TILE == 0

    cos_t, sin_t = _rope_tables(positions)
    xc = x.reshape(batch * seq, d)
    mem2d = mem.reshape(-1, d)
    for l in range(depth):
        lam_init = 0.8 - 0.6 * math.exp(-0.3 * l)
        dqt, dk, dvt, fqt, fk, fvt, qaugt, kaug = _in_proj(
            xc, batch, norm_mix_w[l], w_in[l], cos_t, sin_t, diff_q_norm_w[l],
            diff_k_norm_w[l], fox_q_norm_w[l], fox_k_norm_w[l], b_forget[l])
        lam_params = [p[l].reshape(1, -1) for p in (lambda_q1, lambda_k1, lambda_q2, lambda_k2)]
        lam_params.append(diff_subln_w[l].reshape(-1, 1))
        od, of = _attention(dqt, dk, dvt, fqt, fk, fvt, qaugt, kaug, lam_params, batch, lam_init)
        mk, mv = _mem_kv(mem2d, norm_mem_kv_w[l], w_mem_kv[l], mem_k_norm_w[l])
        xc = _post_attn(xc, od, of, mk, mv, batch, w_out[l], norm_mem_q_w[l], w_mem_q[l],
                        mem_q_norm_w[l], w_mem_o[l], norm_mlp_w[l], w_up[l], w_down[l])
    return xc.reshape(batch, seq, d)
```

```python
import functools
import math
from typing import Any, NamedTuple

import jax
import jax.numpy as jnp
from jax import lax
from jax.experimental import pallas as pl
from jax.experimental.pallas import tpu as pltpu

F32 = jnp.float32
BF16 = jnp.bfloat16

D_MODEL = 1024
CHUNK = 64
HEAD_DIM = 64
N_DIFF_HEADS = 4
DIFF_V_DIM = 2 * HEAD_DIM
N_FOX_HEADS = 8
GROUP_W = 512
ROPE_DIM = HEAD_DIM // 4
ROPE_THETA = 500000.0
N_MEM_HEADS = 4
MEM_HEAD_DIM = D_MODEL // N_MEM_HEADS
D_FF = 4 * D_MODEL
EPS = 1e-6
NEG_INF = -1e30
LOG2E = math.log2(math.e)

LANES = 128
MXU_DIM = 256
VMEM_LIMIT = 58 * 1024 * 1024

TOK_TILE = 1024
ATT_TILE = 256


def _params(sem):
    return pltpu.CompilerParams(dimension_semantics=sem, vmem_limit_bytes=VMEM_LIMIT)


def _resident(shape):
    return pl.BlockSpec(shape, lambda *_: (0,) * len(shape), pipeline_mode=pl.Buffered(1))


def _rms(x, w):
    ms = jnp.mean(x * x, axis=-1, keepdims=True)
    return (x * lax.rsqrt(ms + EPS)) * w


def _dot(a, b):
    return jnp.dot(a, b, preferred_element_type=F32)


def _dot_nt(a, b):
    return lax.dot_general(a, b, (((1,), (1,)), ((), ())), preferred_element_type=F32)


def _rope_body(pos_ref, freq_ref, cos_ref, sin_ref):
    ang = pos_ref[...] * freq_ref[...]
    cos_ref[...] = jnp.cos(ang)
    sin_ref[...] = jnp.sin(ang)


def _rope_tables(positions):
    n_tok = positions.size
    half = ROPE_DIM // 2
    inv_freq = ROPE_THETA ** (-jnp.arange(0, ROPE_DIM, 2, dtype=F32) / ROPE_DIM)
    out = jax.ShapeDtypeStruct((half, n_tok), F32)
    return pl.pallas_call(_rope_body, out_shape=(out, out), name="rope_tables")(
        positions.reshape(1, n_tok).astype(F32), inv_freq.reshape(half, 1))


def _log_sigmoid(x):
    return jnp.minimum(x, 0.0) - jnp.log1p(jnp.exp(-jnp.abs(x)))


def _split3(x):
    hi = x.astype(BF16)
    rem = x - hi.astype(F32)
    mid = rem.astype(BF16)
    lo = (rem - mid.astype(F32)).astype(BF16)
    return hi, mid, lo


BIAS_SLOT = LANES // N_FOX_HEADS


def _head_norm_t(t, gain, rope):
    half = ROPE_DIM // 2
    outs = []
    for r0 in range(0, t.shape[0], HEAD_DIM):
        th = t[r0:r0 + HEAD_DIM]
        ms = jnp.sum(th * th, axis=0, keepdims=True) * (1.0 / HEAD_DIM)
        th = (th * lax.rsqrt(ms + EPS)) * gain[r0:r0 + HEAD_DIM]
        if rope is not None:
            cos, sin = rope
            t1, t2 = th[:half], th[half:ROPE_DIM]
            th = jnp.concatenate(
                [t1 * cos - t2 * sin, t2 * cos + t1 * sin, th[ROPE_DIM:]], axis=0)
        outs.append(th)
    return jnp.concatenate(outs, axis=0)


def _in_proj_body(x_ref, nw_ref, wt_ref, wft_ref, cos_ref, sin_ref, gain_ref, bf_ref, dqt_ref,
                  dk_ref, dvt_ref, fqt_ref, fk_ref, fvt_ref, qaugt_ref, kaug_ref, carry_ref):
    @pl.when(pl.program_id(1) == 0)
    def _():
        carry_ref[...] = jnp.zeros_like(carry_ref)

    h = _rms(x_ref[...], nw_ref[...]).astype(BF16)
    rope = (cos_ref[...], sin_ref[...])
    g = GROUP_W

    def proj_t(group):
        r0 = group * GROUP_W
        return _dot_nt(wt_ref[r0:r0 + GROUP_W, :], h)

    def store_keys(k_ref, kt):
        for c in range(0, g, LANES):
            k_ref[:, c:c + LANES] = kt[c:c + LANES].T.astype(BF16)

    def store_values(vt_ref, vt):
        for t in range(TOK_TILE // ATT_TILE):
            vt_ref[t] = vt[:, t * ATT_TILE:(t + 1) * ATT_TILE].astype(BF16)

    dqt_ref[...] = _head_norm_t(proj_t(0), gain_ref[0:g], rope).astype(BF16)
    store_keys(dk_ref, _head_norm_t(proj_t(1), gain_ref[g:2 * g], rope))
    store_values(dvt_ref, proj_t(2))
    fqt_ref[...] = _head_norm_t(proj_t(3), gain_ref[2 * g:3 * g], None).astype(BF16)
    store_keys(fk_ref, _head_norm_t(proj_t(4), gain_ref[3 * g:4 * g], None))
    store_values(fvt_ref, proj_t(5))

    log_f = _log_sigmoid(_dot_nt(wft_ref[...], h) + bf_ref[...])
    r = lax.broadcasted_iota(jnp.int32, (MXU_DIM, MXU_DIM), 0)
    c = lax.broadcasted_iota(jnp.int32, (MXU_DIM, MXU_DIM), 1)
    triu = jnp.where(r <= c, 1.0, 0.0).astype(BF16)
    part = lax.broadcasted_iota(jnp.int32, (LANES, MXU_DIM), 0) % BIAS_SLOT
    carry = carry_ref[:, 0:1]
    for c0 in range(0, TOK_TILE, MXU_DIM):
        hi, mid, lo = _split3(log_f[:, c0:c0 + MXU_DIM])
        cum = (_dot(hi, triu) + _dot(mid, triu)) + _dot(lo, triu) + carry
        carry = cum[:, MXU_DIM - 1:MXU_DIM]
        hi, mid, lo = (t.astype(F32) for t in _split3(cum * LOG2E))
        kaug_t = jnp.where(part == 0, -hi, jnp.where(part == 1, -mid, jnp.where(
            part == 2, -lo, jnp.where(part < 6, 1.0, 0.0))))
        qaug_t = jnp.where(part < 3, 1.0, jnp.where(part == 3, hi, jnp.where(
            part == 4, mid, jnp.where(part == 5, lo, 0.0))))
        kaug_ref[c0:c0 + MXU_DIM, :] = kaug_t.T.astype(BF16)
        qaugt_ref[:, c0:c0 + MXU_DIM] = qaug_t.astype(BF16)
    carry_ref[...] = jnp.broadcast_to(carry, carry_ref.shape)


def _in_proj(x2d, batch, norm_w, w_in, cos_t, sin_t, dqw, dkw, fqw, fkw, b_forget):
    n_tok = x2d.shape[0]
    tiles = n_tok // batch // TOK_TILE
    g = GROUP_W
    wt = w_in[:, :6 * g].T.astype(BF16)
    wft = jnp.repeat(w_in[:, 6 * g:6 * g + N_FOX_HEADS].T, BIAS_SLOT, axis=0).astype(BF16)
    bf = jnp.repeat(b_forget, BIAS_SLOT).reshape(LANES, 1)
    qscale = HEAD_DIM ** -0.5 * LOG2E
    gains = jnp.concatenate([jnp.tile(v, g // HEAD_DIM) for v in
                             (dqw * qscale, dkw, fqw * qscale, fkw)]).reshape(4 * g, 1)

    row = lambda b, i: (b * tiles + i, 0)
    col = lambda b, i: (0, b * tiles + i)
    tok = lambda width: pl.BlockSpec((TOK_TILE, width), row)
    tok_t = lambda rows: pl.BlockSpec((rows, TOK_TILE), col)
    slabs = TOK_TILE // ATT_TILE
    vt_spec = pl.BlockSpec((slabs, g, ATT_TILE), lambda b, i: (b * tiles + i, 0, 0))
    k_shape = jax.ShapeDtypeStruct((n_tok, g), BF16)
    qt_shape = jax.ShapeDtypeStruct((g, n_tok), BF16)
    vt_shape = jax.ShapeDtypeStruct((n_tok // ATT_TILE, g, ATT_TILE), BF16)
    half = ROPE_DIM // 2
    return pl.pallas_call(
        _in_proj_body,
        grid=(batch, tiles),
        in_specs=[tok(D_MODEL), _resident((1, D_MODEL)), _resident(wt.shape),
                  _resident(wft.shape), tok_t(half), tok_t(half), _resident(gains.shape),
                  _resident(bf.shape)],
        out_specs=[tok_t(g), tok(g), vt_spec, tok_t(g), tok(g), vt_spec, tok_t(LANES),
                   tok(LANES)],
        out_shape=[qt_shape, k_shape, vt_shape, qt_shape, k_shape, vt_shape,
                   jax.ShapeDtypeStruct((LANES, n_tok), BF16),
                   jax.ShapeDtypeStruct((n_tok, LANES), BF16)],
        scratch_shapes=[pltpu.VMEM((LANES, LANES), F32)],
        compiler_params=_params(("arbitrary", "arbitrary")),
        name="in_proj",
    )(x2d, norm_w.reshape(1, -1), wt, wft, cos_t, sin_t, gains, bf)


N_CHAINS = 2 * (GROUP_W // LANES)
SUM_ROWS = 16


class _Group(NamedTuple):
    fox: bool
    qt: tuple
    k: Any
    vt: Any
    qaug: Any
    kaug: Any
    out: Any
    rhs: Any
    s: tuple
    bmax: Any
    m: Any
    acc: Any

    @property
    def width(self):
        return HEAD_DIM if self.fox else DIFF_V_DIM


def _attn_scratch(fox):
    width = HEAD_DIM if fox else DIFF_V_DIM
    rhs_w = 2 * LANES if fox else LANES
    return [pltpu.VMEM((2, N_CHAINS, rhs_w, ATT_TILE), BF16),
            pltpu.VMEM((N_CHAINS, ATT_TILE, ATT_TILE), F32),
            pltpu.VMEM((N_CHAINS, ATT_TILE, ATT_TILE), F32),
            pltpu.VMEM((2, N_CHAINS, ATT_TILE), F32),
            pltpu.VMEM((2, N_CHAINS, ATT_TILE), F32),
            pltpu.VMEM((2, N_CHAINS, width + SUM_ROWS, ATT_TILE), F32)]


def _attn_body(dqa, dqb, dk, dvt, fqa, fqb, fk, fvt, qauga, qaugb, kaug, lq1_ref, lk1_ref,
               lq2_ref, lk2_ref, subw_ref, od_ref, of_ref, *scratch, lam_init, half):
    def group(fox, qt, k, vt, qaug, kaug_ref, out, scr):
        rhs, s0, s1, bmax, m, acc = scr
        return _Group(fox, qt, k, vt, qaug, kaug_ref, out, rhs, (s0, s1), bmax, m, acc)

    groups = (group(False, (dqa, dqb), dk, dvt, None, None, od_ref, scratch[:6]),
              group(True, (fqa, fqb), fk, fvt, (qauga, qaugb), kaug, of_ref, scratch[6:]))
    s = pl.program_id(1)
    last = 2 * s + half + 1
    ones_rows = jnp.ones((SUM_ROWS, ATT_TILE), BF16)

    half_zero = jnp.zeros((HEAD_DIM, ATT_TILE), BF16)
    for gr in groups:
        for tile, qt_ref in enumerate(gr.qt):
            for g in range(GROUP_W // LANES):
                r0 = g * LANES
                gr.rhs[tile, 2 * g, :HEAD_DIM, :] = qt_ref[r0:r0 + HEAD_DIM, :]
                gr.rhs[tile, 2 * g, HEAD_DIM:LANES, :] = half_zero
                gr.rhs[tile, 2 * g + 1, :HEAD_DIM, :] = half_zero
                gr.rhs[tile, 2 * g + 1, HEAD_DIM:LANES, :] = qt_ref[r0 + HEAD_DIM:r0 + LANES, :]
            if gr.fox:
                for c in range(N_CHAINS):
                    b0 = c * BIAS_SLOT
                    gr.rhs[tile, c, LANES:, :] = jnp.zeros((LANES, ATT_TILE), BF16)
                    gr.rhs[tile, c, LANES + b0:LANES + b0 + BIAS_SLOT, :] = (
                        gr.qaug[tile][b0:b0 + BIAS_SLOT, :])
        gr.m[...] = jnp.full(gr.m.shape, NEG_INF, F32)
        gr.acc[...] = jnp.zeros(gr.acc.shape, F32)

    def locate(p):
        in_b = p > s
        return in_b.astype(jnp.int32), jnp.where(in_b, p - s - 1, s - p)

    def diagonal_mask(gr):
        div = 1 if gr.fox else CHUNK
        k_id = lax.broadcasted_iota(jnp.int32, (ATT_TILE, ATT_TILE), 0) // div
        q_id = lax.broadcasted_iota(jnp.int32, (ATT_TILE, ATT_TILE), 1) // div
        return k_id <= q_id

    def scores(gr, p, slot, g, masked):
        tile, blk = locate(p)
        start = pl.multiple_of(blk * ATT_TILE, ATT_TILE)
        lhs = gr.k[pl.ds(start, ATT_TILE), g * LANES:(g + 1) * LANES]
        if gr.fox:
            lhs = jnp.concatenate([lhs, gr.kaug[pl.ds(start, ATT_TILE), :]], axis=-1)
        for c in (2 * g, 2 * g + 1):
            sc = _dot(lhs, gr.rhs[tile, c])
            if masked:
                sc = jnp.where(diagonal_mask(gr), sc, NEG_INF)
            gr.s[slot][c] = sc
            gr.bmax[slot, c:c + 1, :] = jnp.max(sc, axis=0, keepdims=True)

    def softmax_pv(gr, p, slot, c):
        tile, blk = locate(p)
        sc = gr.s[slot][c]
        m_old = gr.m[tile, c:c + 1, :]
        m_new = jnp.maximum(m_old, gr.bmax[slot, c:c + 1, :])
        alpha = jnp.exp2(m_old - m_new)
        prob = jnp.exp2(sc - m_new)
        gr.m[tile, c:c + 1, :] = m_new
        r0 = c * HEAD_DIM if gr.fox else (c // 2) * LANES
        vt = jnp.concatenate([gr.vt[blk, r0:r0 + gr.width, :], ones_rows], axis=0)
        gr.acc[tile, c] = alpha * gr.acc[tile, c] + _dot(vt, prob.astype(BF16))

    def step(score_args, soft_args, masked=False):
        for g in range(GROUP_W // LANES):
            for gr in groups:
                if score_args is not None:
                    scores(gr, *score_args, g, masked)
                if soft_args is not None:
                    softmax_pv(gr, *soft_args, 2 * g)
                    softmax_pv(gr, *soft_args, 2 * g + 1)

    def pair(u, carry):
        p = 2 * u + 1
        step((p + 1, 0), (p, 1))
        step((p + 2, 1), (p + 1, 0))
        return carry

    step((0, 0), None, masked=True)
    step((1, 1), (0, 0))
    lax.fori_loop(0, (last - 3) // 2, pair, 0)
    step((last - 1, 0), (last - 2, 1))
    step((last, 1), (last - 1, 0), masked=True)
    step(None, (last, 1))

    lam = (jnp.exp(jnp.sum(lq1_ref[...] * lk1_ref[...], axis=-1, keepdims=True))
           - jnp.exp(jnp.sum(lq2_ref[...] * lk2_ref[...], axis=-1, keepdims=True))
           + lam_init)
    for gr in groups:
        width = gr.width
        for tile in range(2):
            for g in range(GROUP_W // LANES):
                a1, a2 = gr.acc[tile, 2 * g], gr.acc[tile, 2 * g + 1]
                o1 = a1[:width] * (1.0 / a1[width:width + 1])
                o2 = a2[:width] * (1.0 / a2[width:width + 1])
                if gr.fox:
                    o_t = jnp.concatenate([o1, o2], axis=0)
                else:
                    o_t = o1 - lam * o2
                    ms = jnp.mean(o_t * o_t, axis=0, keepdims=True)
                    o_t = ((o_t * lax.rsqrt(ms + EPS)) * subw_ref[...]) * (1.0 - lam_init)
                gr.out[0, tile, 0, :, g * LANES:(g + 1) * LANES] = o_t.T.astype(BF16)


def _attention(dqt, dk, dvt, fqt, fk, fvt, qaugt, kaug, lam_params, batch, lam_init):
    n_tok = dk.shape[0]
    seq = n_tok // batch
    tiles = seq // ATT_TILE
    assert tiles % 2 == 0
    half = tiles // 2
    qt_spec = lambda r, off: pl.BlockSpec((r, ATT_TILE), lambda b, s: (0, b * tiles + s + off))
    kv_spec = lambda w: pl.BlockSpec((seq, w), lambda b, s: (b, 0))
    vt_spec = pl.BlockSpec((tiles, GROUP_W, ATT_TILE), lambda b, s: (b, 0, 0))
    qkv_specs = [qt_spec(GROUP_W, 0), qt_spec(GROUP_W, half), kv_spec(GROUP_W), vt_spec]
    out_spec = pl.BlockSpec((1, 2, 1, ATT_TILE, GROUP_W), lambda b, s: (b, 0, s, 0, 0))
    out_shape = jax.ShapeDtypeStruct((batch, 2, half, ATT_TILE, GROUP_W), BF16)
    od, of = pl.pallas_call(
        functools.partial(_attn_body, lam_init=lam_init, half=half),
        grid=(batch, half),
        in_specs=qkv_specs + qkv_specs
        + [qt_spec(LANES, 0), qt_spec(LANES, half), kv_spec(LANES)]
        + [_resident(p.shape) for p in lam_params],
        out_specs=[out_spec, out_spec],
        out_shape=[out_shape, out_shape],
        scratch_shapes=_attn_scratch(False) + _attn_scratch(True),
        compiler_params=_params(("arbitrary", "arbitrary")),
        name="attention",
    )(dqt, dqt, dk, dvt, fqt, fqt, fk, fvt, qaugt, qaugt, kaug, *lam_params)
    return od.reshape(n_tok, GROUP_W), of.reshape(n_tok, GROUP_W)


def _mem_kv_body(m_ref, nw_ref, w_ref, kw_ref, mk_ref, mv_ref):
    hm = _rms(m_ref[...], nw_ref[...]).astype(BF16)
    kv = _dot(hm, w_ref[...])
    kw = kw_ref[...]
    for c in range(0, D_MODEL, MEM_HEAD_DIM):
        mk_ref[:, c:c + MEM_HEAD_DIM] = _rms(kv[:, c:c + MEM_HEAD_DIM], kw).astype(BF16)
    mv_ref[...] = kv[:, D_MODEL:].astype(BF16)


def _mem_kv(mem2d, norm_w, w_mem_kv, mem_k_norm_w):
    n_mem = mem2d.shape[0]
    tok = pl.BlockSpec((TOK_TILE, D_MODEL), lambda i: (i, 0))
    out = jax.ShapeDtypeStruct((n_mem, D_MODEL), BF16)
    return pl.pallas_call(
        _mem_kv_body,
        grid=(n_mem // TOK_TILE,),
        in_specs=[tok, _resident((1, D_MODEL)), _resident((D_MODEL, 2 * D_MODEL)),
                  _resident((1, MEM_HEAD_DIM))],
        out_specs=[tok, tok],
        out_shape=[out, out],
        compiler_params=_params(("arbitrary",)),
        name="mem_kv",
    )(mem2d, norm_w.reshape(1, -1), w_mem_kv.astype(BF16), mem_k_norm_w.reshape(1, -1))


POST_TILE = 1024
FF_CHUNK = 1024


def _post_attn_body(x_ref, od_ref, of_ref, mk_ref, mv_ref, wo_ref, nq_ref, wq_ref, qw_ref,
                    wmo_ref, nm_ref, wu_ref, wd_ref, o_ref):
    x1 = x_ref[...] + (_dot(od_ref[...], wo_ref[:GROUP_W, :])
                       + _dot(of_ref[...], wo_ref[GROUP_W:, :]))

    mq = _dot(_rms(x1, nq_ref[...]).astype(BF16), wq_ref[...])
    qw = qw_ref[...]
    heads = []
    for c in range(0, D_MODEL, MEM_HEAD_DIM):
        q = _rms(mq[:, c:c + MEM_HEAD_DIM], qw).astype(BF16)
        s = _dot_nt(q, mk_ref[:, c:c + MEM_HEAD_DIM])
        p = jnp.exp(s - jnp.max(s, axis=-1, keepdims=True))
        l = jnp.sum(p, axis=-1, keepdims=True)
        heads.append((_dot(p.astype(BF16), mv_ref[:, c:c + MEM_HEAD_DIM]) / l).astype(BF16))
    x2 = x1 + _dot(jnp.concatenate(heads, axis=-1), wmo_ref[...])

    h = _rms(x2, nm_ref[...]).astype(BF16)
    acc = x2
    for c in range(0, D_FF, FF_CHUNK):
        u = jnp.maximum(_dot(h, wu_ref[:, c:c + FF_CHUNK]), 0.0)
        acc = acc + _dot((u * u).astype(BF16), wd_ref[c:c + FF_CHUNK, :])
    o_ref[...] = acc


def _post_attn(x2d, od, of, mk, mv, batch, w_out, norm_q, w_mem_q, mem_q_norm_w, w_mem_o,
               norm_mlp, w_up, w_down):
    n_tok = x2d.shape[0]
    tiles = n_tok // batch // POST_TILE
    mem_len = mk.shape[0] // batch
    tok = lambda width: pl.BlockSpec((POST_TILE, width), lambda b, i: (b * tiles + i, 0))
    mem = pl.BlockSpec((mem_len, D_MODEL), lambda b, i: (b, 0))
    scale = MEM_HEAD_DIM ** -0.5
    square = _resident((D_MODEL, D_MODEL))
    vec = _resident((1, D_MODEL))
    return pl.pallas_call(
        _post_attn_body,
        grid=(batch, tiles),
        in_specs=[tok(D_MODEL), tok(GROUP_W), tok(GROUP_W), mem, mem, square, vec, square,
                  _resident((1, MEM_HEAD_DIM)), square, vec, _resident((D_MODEL, D_FF)),
                  _resident((D_FF, D_MODEL))],
        out_specs=tok(D_MODEL),
        out_shape=jax.ShapeDtypeStruct((n_tok, D_MODEL), F32),
        compiler_params=_params(("arbitrary", "arbitrary")),
        name="post_attn",
    )(x2d, od, of, mk, mv, w_out.astype(BF16), norm_q.reshape(1, -1), w_mem_q.astype(BF16),
      (mem_q_norm_w * scale).reshape(1, -1), w_mem_o.astype(BF16), norm_mlp.reshape(1, -1),
      w_up.astype(BF16), w_down.astype(BF16))


def kernel(x, mem, positions, norm_mix_w, w_in, b_forget, diff_q_norm_w, diff_k_norm_w,
           lambda_q1, lambda_k1, lambda_q2, lambda_k2, diff_subln_w, fox_q_norm_w,
           fox_k_norm_w, w_out, norm_mem_q_w, norm_mem_kv_w, w_mem_q, w_mem_kv,
           mem_q_norm_w, mem_k_norm_w, w_mem_o, norm_mlp_w, w_up, w_down):
    batch, seq, d = x.shape
    depth = w_in.shape[0]
    assert d == D_MODEL and seq % TOK_TILE == 0 and seq % POST_TILE == 0
    assert TOK_TILE % ATT_TILE == 0
    assert (batch * mem.shape[1]) % TOK_TILE == 0

    cos_t, sin_t = _rope_tables(positions)
    xc = x.reshape(batch * seq, d)
    mem2d = mem.reshape(-1, d)
    for l in range(depth):
        lam_init = 0.8 - 0.6 * math.exp(-0.3 * l)
        dqt, dk, dvt, fqt, fk, fvt, qaugt, kaug = _in_proj(
            xc, batch, norm_mix_w[l], w_in[l], cos_t, sin_t, diff_q_norm_w[l],
            diff_k_norm_w[l], fox_q_norm_w[l], fox_k_norm_w[l], b_forget[l])
        lam_params = [p[l].reshape(1, -1) for p in (lambda_q1, lambda_k1, lambda_q2, lambda_k2)]
        lam_params.append(diff_subln_w[l].reshape(-1, 1))
        od, of = _attention(dqt, dk, dvt, fqt, fk, fvt, qaugt, kaug, lam_params, batch, lam_init)
        mk, mv = _mem_kv(mem2d, norm_mem_kv_w[l], w_mem_kv[l], mem_k_norm_w[l])
        xc = _post_attn(xc, od, of, mk, mv, batch, w_out[l], norm_mem_q_w[l], w_mem_q[l],
                        mem_q_norm_w[l], w_mem_o[l], norm_mlp_w[l], w_up[l], w_down[l])
    return xc.reshape(batch, seq, d)
```

```python
import functools
import math
from typing import Any, NamedTuple

import jax
import jax.numpy as jnp
from jax import lax
from jax.experimental import pallas as pl
from jax.experimental.pallas import tpu as pltpu

F32 = jnp.float32
BF16 = jnp.bfloat16

D_MODEL = 1024
CHUNK = 64
HEAD_DIM = 64
N_DIFF_HEADS = 4
DIFF_V_DIM = 2 * HEAD_DIM
N_FOX_HEADS = 8
GROUP_W = 512
ROPE_DIM = HEAD_DIM // 4
ROPE_THETA = 500000.0
N_MEM_HEADS = 4
MEM_HEAD_DIM = D_MODEL // N_MEM_HEADS
D_FF = 4 * D_MODEL
EPS = 1e-6
NEG_INF = -1e30
LOG2E = math.log2(math.e)

LANES = 128
MXU_DIM = 256
VMEM_LIMIT = 58 * 1024 * 1024

TOK_TILE = 1024
ATT_TILE = 256


def _params(sem):
    return pltpu.CompilerParams(dimension_semantics=sem, vmem_limit_bytes=VMEM_LIMIT)


def _resident(shape):
    return pl.BlockSpec(shape, lambda *_: (0,) * len(shape), pipeline_mode=pl.Buffered(1))


def _rms(x, w):
    ms = jnp.mean(x * x, axis=-1, keepdims=True)
    return (x * lax.rsqrt(ms + EPS)) * w


def _dot(a, b):
    return jnp.dot(a, b, preferred_element_type=F32)


def _dot_nt(a, b):
    return lax.dot_general(a, b, (((1,), (1,)), ((), ())), preferred_element_type=F32)


def _rope_body(pos_ref, freq_ref, cos_ref, sin_ref):
    ang = pos_ref[...] * freq_ref[...]
    cos_ref[...] = jnp.cos(ang)
    sin_ref[...] = jnp.sin(ang)


def _rope_tables(positions):
    n_tok = positions.size
    half = ROPE_DIM // 2
    inv_freq = ROPE_THETA ** (-jnp.arange(0, ROPE_DIM, 2, dtype=F32) / ROPE_DIM)
    out = jax.ShapeDtypeStruct((half, n_tok), F32)
    return pl.pallas_call(_rope_body, out_shape=(out, out), name="rope_tables")(
        positions.reshape(1, n_tok).astype(F32), inv_freq.reshape(half, 1))


def _log_sigmoid(x):
    return jnp.minimum(x, 0.0) - jnp.log1p(jnp.exp(-jnp.abs(x)))


def _split3(x):
    hi = x.astype(BF16)
    rem = x - hi.astype(F32)
    mid = rem.astype(BF16)
    lo = (rem - mid.astype(F32)).astype(BF16)
    return hi, mid, lo


BIAS_SLOT = LANES // N_FOX_HEADS


def _head_norm_t(t, gain, rope):
    half = ROPE_DIM // 2
    outs = []
    for r0 in range(0, t.shape[0], HEAD_DIM):
        th = t[r0:r0 + HEAD_DIM]
        ms = jnp.sum(th * th, axis=0, keepdims=True) * (1.0 / HEAD_DIM)
        th = (th * lax.rsqrt(ms + EPS)) * gain[r0:r0 + HEAD_DIM]
        if rope is not None:
            cos, sin = rope
            t1, t2 = th[:half], th[half:ROPE_DIM]
            th = jnp.concatenate(
                [t1 * cos - t2 * sin, t2 * cos + t1 * sin, th[ROPE_DIM:]], axis=0)
        outs.append(th)
    return jnp.concatenate(outs, axis=0)


def _in_proj_body(x_ref, nw_ref, wt_ref, wft_ref, cos_ref, sin_ref, gain_ref, bf_ref, dqt_ref,
                  dk_ref, dvt_ref, fqt_ref, fk_ref, fvt_ref, qaugt_ref, kaug_ref, carry_ref):
    @pl.when(pl.program_id(1) == 0)
    def _():
        carry_ref[...] = jnp.zeros_like(carry_ref)

    h = _rms(x_ref[...], nw_ref[...]).astype(BF16)
    rope = (cos_ref[...], sin_ref[...])
    g = GROUP_W

    def proj_t(group):
        r0 = group * GROUP_W
        return _dot_nt(wt_ref[r0:r0 + GROUP_W, :], h)

    def store_keys(k_ref, kt):
        for c in range(0, g, LANES):
            k_ref[:, c:c + LANES] = kt[c:c + LANES].T.astype(BF16)

    def store_values(vt_ref, vt):
        for t in range(TOK_TILE // ATT_TILE):
            vt_ref[t] = vt[:, t * ATT_TILE:(t + 1) * ATT_TILE].astype(BF16)

    dqt_ref[...] = _head_norm_t(proj_t(0), gain_ref[0:g], rope).astype(BF16)
    store_keys(dk_ref, _head_norm_t(proj_t(1), gain_ref[g:2 * g], rope))
    store_values(dvt_ref, proj_t(2))
    fqt_ref[...] = _head_norm_t(proj_t(3), gain_ref[2 * g:3 * g], None).astype(BF16)
    store_keys(fk_ref, _head_norm_t(proj_t(4), gain_ref[3 * g:4 * g], None))
    store_values(fvt_ref, proj_t(5))

    log_f = _log_sigmoid(_dot_nt(wft_ref[...], h) + bf_ref[...])
    r = lax.broadcasted_iota(jnp.int32, (MXU_DIM, MXU_DIM), 0)
    c = lax.broadcasted_iota(jnp.int32, (MXU_DIM, MXU_DIM), 1)
    triu = jnp.where(r <= c, 1.0, 0.0).astype(BF16)
    part = lax.broadcasted_iota(jnp.int32, (LANES, MXU_DIM), 0) % BIAS_SLOT
    carry = carry_ref[:, 0:1]
    for c0 in range(0, TOK_TILE, MXU_DIM):
        hi, mid, lo = _split3(log_f[:, c0:c0 + MXU_DIM])
        cum = (_dot(hi, triu) + _dot(mid, triu)) + _dot(lo, triu) + carry
        carry = cum[:, MXU_DIM - 1:MXU_DIM]
        hi, mid, lo = (t.astype(F32) for t in _split3(cum * LOG2E))
        kaug_t = jnp.where(part == 0, -hi, jnp.where(part == 1, -mid, jnp.where(
            part == 2, -lo, jnp.where(part < 6, 1.0, 0.0))))
        qaug_t = jnp.where(part < 3, 1.0, jnp.where(part == 3, hi, jnp.where(
            part == 4, mid, jnp.where(part == 5, lo, 0.0))))
        kaug_ref[c0:c0 + MXU_DIM, :] = kaug_t.T.astype(BF16)
        qaugt_ref[:, c0:c0 + MXU_DIM] = qaug_t.astype(BF16)
    carry_ref[...] = jnp.broadcast_to(carry, carry_ref.shape)


def _in_proj(x2d, batch, norm_w, w_in, cos_t, sin_t, dqw, dkw, fqw, fkw, b_forget):
    n_tok = x2d.shape[0]
    tiles = n_tok // batch // TOK_TILE
    g = GROUP_W
    wt = w_in[:, :6 * g].T.astype(BF16)
    wft = jnp.repeat(w_in[:, 6 * g:6 * g + N_FOX_HEADS].T, BIAS_SLOT, axis=0).astype(BF16)
    bf = jnp.repeat(b_forget, BIAS_SLOT).reshape(LANES, 1)
    qscale = HEAD_DIM ** -0.5 * LOG2E
    gains = jnp.concatenate([jnp.tile(v, g // HEAD_DIM) for v in
                             (dqw * qscale, dkw, fqw * qscale, fkw)]).reshape(4 * g, 1)

    row = lambda b, i: (b * tiles + i, 0)
    col = lambda b, i: (0, b * tiles + i)
    tok = lambda width: pl.BlockSpec((TOK_TILE, width), row)
    tok_t = lambda rows: pl.BlockSpec((rows, TOK_TILE), col)
    slabs = TOK_TILE // ATT_TILE
    vt_spec = pl.BlockSpec((slabs, g, ATT_TILE), lambda b, i: (b * tiles + i, 0, 0))
    k_shape = jax.ShapeDtypeStruct((n_tok, g), BF16)
    qt_shape = jax.ShapeDtypeStruct((g, n_tok), BF16)
    vt_shape = jax.ShapeDtypeStruct((n_tok // ATT_TILE, g, ATT_TILE), BF16)
    half = ROPE_DIM // 2
    return pl.pallas_call(
        _in_proj_body,
        grid=(batch, tiles),
        in_specs=[tok(D_MODEL), _resident((1, D_MODEL)), _resident(wt.shape),
                  _resident(wft.shape), tok_t(half), tok_t(half), _resident(gains.shape),
                  _resident(bf.shape)],
        out_specs=[tok_t(g), tok(g), vt_spec, tok_t(g), tok(g), vt_spec, tok_t(LANES),
                   tok(LANES)],
        out_shape=[qt_shape, k_shape, vt_shape, qt_shape, k_shape, vt_shape,
                   jax.ShapeDtypeStruct((LANES, n_tok), BF16),
                   jax.ShapeDtypeStruct((n_tok, LANES), BF16)],
        scratch_shapes=[pltpu.VMEM((LANES, LANES), F32)],
        compiler_params=_params(("arbitrary", "arbitrary")),
        name="in_proj",
    )(x2d, norm_w.reshape(1, -1), wt, wft, cos_t, sin_t, gains, bf)


N_CHAINS = 2 * (GROUP_W // LANES)
SUM_ROWS = 16


class _Group(NamedTuple):
    fox: bool
    qt: tuple
    k: Any
    vt: Any
    qaug: Any
    kaug: Any
    out: Any
    rhs: Any
    s: tuple
    bmax: Any
    m: Any
    acc: Any

    @property
    def width(self):
        return HEAD_DIM if self.fox else DIFF_V_DIM


def _attn_scratch(fox):
    width = HEAD_DIM if fox else DIFF_V_DIM
    rhs_w = 2 * LANES if fox else LANES
    return [pltpu.VMEM((2, N_CHAINS, rhs_w, ATT_TILE), BF16),
            pltpu.VMEM((N_CHAINS, ATT_TILE, ATT_TILE), F32),
            pltpu.VMEM((N_CHAINS, ATT_TILE, ATT_TILE), F32),
            pltpu.VMEM((2, N_CHAINS, ATT_TILE), F32),
            pltpu.VMEM((2, N_CHAINS, ATT_TILE), F32),
            pltpu.VMEM((2, N_CHAINS, width + SUM_ROWS, ATT_TILE), F32)]


def _attn_body(*refs, lam_init, half):
    for s in range(half):
        @pl.when(pl.program_id(1) == s)
        def _(s=s):
            _attn_step(s, *refs, lam_init=lam_init, half=half)


def _attn_step(s, dqa, dqb, dk, dvt, fqa, fqb, fk, fvt, qauga, qaugb, kaug, lq1_ref, lk1_ref,
               lq2_ref, lk2_ref, subw_ref, od_ref, of_ref, *scratch, lam_init, half):
    def group(fox, qt, k, vt, qaug, kaug_ref, out, scr):
        rhs, s0, s1, bmax, m, acc = scr
        return _Group(fox, qt, k, vt, qaug, kaug_ref, out, rhs, (s0, s1), bmax, m, acc)

    groups = (group(False, (dqa, dqb), dk, dvt, None, None, od_ref, scratch[:6]),
              group(True, (fqa, fqb), fk, fvt, (qauga, qaugb), kaug, of_ref, scratch[6:]))
    last = 2 * s + half + 1
    ones_rows = jnp.ones((SUM_ROWS, ATT_TILE), BF16)

    half_zero = jnp.zeros((HEAD_DIM, ATT_TILE), BF16)
    for gr in groups:
        for tile, qt_ref in enumerate(gr.qt):
            for g in range(GROUP_W // LANES):
                r0 = g * LANES
                gr.rhs[tile, 2 * g, :HEAD_DIM, :] = qt_ref[r0:r0 + HEAD_DIM, :]
                gr.rhs[tile, 2 * g, HEAD_DIM:LANES, :] = half_zero
                gr.rhs[tile, 2 * g + 1, :HEAD_DIM, :] = half_zero
                gr.rhs[tile, 2 * g + 1, HEAD_DIM:LANES, :] = qt_ref[r0 + HEAD_DIM:r0 + LANES, :]
            if gr.fox:
                for c in range(N_CHAINS):
                    b0 = c * BIAS_SLOT
                    gr.rhs[tile, c, LANES:, :] = jnp.zeros((LANES, ATT_TILE), BF16)
                    gr.rhs[tile, c, LANES + b0:LANES + b0 + BIAS_SLOT, :] = (
                        gr.qaug[tile][b0:b0 + BIAS_SLOT, :])
        gr.m[...] = jnp.full(gr.m.shape, NEG_INF, F32)
        gr.acc[...] = jnp.zeros(gr.acc.shape, F32)

    def locate(p):
        return (1, p - s - 1) if p > s else (0, s - p)

    def diagonal_mask(gr):
        div = 1 if gr.fox else CHUNK
        k_id = lax.broadcasted_iota(jnp.int32, (ATT_TILE, ATT_TILE), 0) // div
        q_id = lax.broadcasted_iota(jnp.int32, (ATT_TILE, ATT_TILE), 1) // div
        return k_id <= q_id

    def scores(gr, p, slot, g, masked):
        tile, blk = locate(p)
        rows = slice(blk * ATT_TILE, (blk + 1) * ATT_TILE)
        lhs = gr.k[rows, g * LANES:(g + 1) * LANES]
        if gr.fox:
            lhs = jnp.concatenate([lhs, gr.kaug[rows, :]], axis=-1)
        for c in (2 * g, 2 * g + 1):
            sc = _dot(lhs, gr.rhs[tile, c])
            if masked:
                sc = jnp.where(diagonal_mask(gr), sc, NEG_INF)
            gr.s[slot][c] = sc
            gr.bmax[slot, c:c + 1, :] = jnp.max(sc, axis=0, keepdims=True)

    def softmax_pv(gr, p, slot, c):
        tile, blk = locate(p)
        sc = gr.s[slot][c]
        m_old = gr.m[tile, c:c + 1, :]
        m_new = jnp.maximum(m_old, gr.bmax[slot, c:c + 1, :])
        alpha = jnp.exp2(m_old - m_new)
        prob = jnp.exp2(sc - m_new)
        gr.m[tile, c:c + 1, :] = m_new
        r0 = c * HEAD_DIM if gr.fox else (c // 2) * LANES
        vt = jnp.concatenate([gr.vt[blk, r0:r0 + gr.width, :], ones_rows], axis=0)
        gr.acc[tile, c] = alpha * gr.acc[tile, c] + _dot(vt, prob.astype(BF16))

    def step(score_args, soft_args, masked=False):
        for g in range(GROUP_W // LANES):
            for gr in groups:
                if score_args is not None:
                    scores(gr, *score_args, g, masked)
                if soft_args is not None:
                    softmax_pv(gr, *soft_args, 2 * g)
                    softmax_pv(gr, *soft_args, 2 * g + 1)

    step((0, 0), None, masked=True)
    step((1, 1), (0, 0))
    for p in range(1, last - 2, 2):
        step((p + 1, 0), (p, 1))
        step((p + 2, 1), (p + 1, 0))
    step((last - 1, 0), (last - 2, 1))
    step((last, 1), (last - 1, 0), masked=True)
    step(None, (last, 1))

    lam = (jnp.exp(jnp.sum(lq1_ref[...] * lk1_ref[...], axis=-1, keepdims=True))
           - jnp.exp(jnp.sum(lq2_ref[...] * lk2_ref[...], axis=-1, keepdims=True))
           + lam_init)
    for gr in groups:
        width = gr.width
        for tile in range(2):
            for g in range(GROUP_W // LANES):
                a1, a2 = gr.acc[tile, 2 * g], gr.acc[tile, 2 * g + 1]
                o1 = a1[:width] * (1.0 / a1[width:width + 1])
                o2 = a2[:width] * (1.0 / a2[width:width + 1])
                if gr.fox:
                    o_t = jnp.concatenate([o1, o2], axis=0)
                else:
                    o_t = o1 - lam * o2
                    ms = jnp.mean(o_t * o_t, axis=0, keepdims=True)
                    o_t = ((o_t * lax.rsqrt(ms + EPS)) * subw_ref[...]) * (1.0 - lam_init)
                gr.out[0, tile, 0, :, g * LANES:(g + 1) * LANES] = o_t.T.astype(BF16)


def _attention(dqt, dk, dvt, fqt, fk, fvt, qaugt, kaug, lam_params, batch, lam_init):
    n_tok = dk.shape[0]
    seq = n_tok // batch
    tiles = seq // ATT_TILE
    assert tiles % 2 == 0
    half = tiles // 2
    qt_spec = lambda r, off: pl.BlockSpec((r, ATT_TILE), lambda b, s: (0, b * tiles + s + off))
    kv_spec = lambda w: pl.BlockSpec((seq, w), lambda b, s: (b, 0))
    vt_spec = pl.BlockSpec((tiles, GROUP_W, ATT_TILE), lambda b, s: (b, 0, 0))
    qkv_specs = [qt_spec(GROUP_W, 0), qt_spec(GROUP_W, half), kv_spec(GROUP_W), vt_spec]
    out_spec = pl.BlockSpec((1, 2, 1, ATT_TILE, GROUP_W), lambda b, s: (b, 0, s, 0, 0))
    out_shape = jax.ShapeDtypeStruct((batch, 2, half, ATT_TILE, GROUP_W), BF16)
    od, of = pl.pallas_call(
        functools.partial(_attn_body, lam_init=lam_init, half=half),
        grid=(batch, half),
        in_specs=qkv_specs + qkv_specs
        + [qt_spec(LANES, 0), qt_spec(LANES, half), kv_spec(LANES)]
        + [_resident(p.shape) for p in lam_params],
        out_specs=[out_spec, out_spec],
        out_shape=[out_shape, out_shape],
        scratch_shapes=_attn_scratch(False) + _attn_scratch(True),
        compiler_params=_params(("arbitrary", "arbitrary")),
        name="attention",
    )(dqt, dqt, dk, dvt, fqt, fqt, fk, fvt, qaugt, qaugt, kaug, *lam_params)
    return od.reshape(n_tok, GROUP_W), of.reshape(n_tok, GROUP_W)


def _mem_kv_body(m_ref, nw_ref, w_ref, kw_ref, mk_ref, mv_ref):
    hm = _rms(m_ref[...], nw_ref[...]).astype(BF16)
    kv = _dot(hm, w_ref[...])
    kw = kw_ref[...]
    for c in range(0, D_MODEL, MEM_HEAD_DIM):
        mk_ref[:, c:c + MEM_HEAD_DIM] = _rms(kv[:, c:c + MEM_HEAD_DIM], kw).astype(BF16)
    mv_ref[...] = kv[:, D_MODEL:].astype(BF16)


def _mem_kv(mem2d, norm_w, w_mem_kv, mem_k_norm_w):
    n_mem = mem2d.shape[0]
    tok = pl.BlockSpec((TOK_TILE, D_MODEL), lambda i: (i, 0))
    out = jax.ShapeDtypeStruct((n_mem, D_MODEL), BF16)
    return pl.pallas_call(
        _mem_kv_body,
        grid=(n_mem // TOK_TILE,),
        in_specs=[tok, _resident((1, D_MODEL)), _resident((D_MODEL, 2 * D_MODEL)),
                  _resident((1, MEM_HEAD_DIM))],
        out_specs=[tok, tok],
        out_shape=[out, out],
        compiler_params=_params(("arbitrary",)),
        name="mem_kv",
    )(mem2d, norm_w.reshape(1, -1), w_mem_kv.astype(BF16), mem_k_norm_w.reshape(1, -1))


POST_TILE = 1024
FF_CHUNK = 1024


def _post_attn_body(x_ref, od_ref, of_ref, mk_ref, mv_ref, wo_ref, nq_ref, wq_ref, qw_ref,
                    wmo_ref, nm_ref, wu_ref, wd_ref, o_ref):
    x1 = x_ref[...] + (_dot(od_ref[...], wo_ref[:GROUP_W, :])
                       + _dot(of_ref[...], wo_ref[GROUP_W:, :]))

    mq = _dot(_rms(x1, nq_ref[...]).astype(BF16), wq_ref[...])
    qw = qw_ref[...]
    heads = []
    for c in range(0, D_MODEL, MEM_HEAD_DIM):
        q = _rms(mq[:, c:c + MEM_HEAD_DIM], qw).astype(BF16)
        s = _dot_nt(q, mk_ref[:, c:c + MEM_HEAD_DIM])
        p = jnp.exp(s - jnp.max(s, axis=-1, keepdims=True))
        l = jnp.sum(p, axis=-1, keepdims=True)
        heads.append((_dot(p.astype(BF16), mv_ref[:, c:c + MEM_HEAD_DIM]) / l).astype(BF16))
    x2 = x1 + _dot(jnp.concatenate(heads, axis=-1), wmo_ref[...])

    h = _rms(x2, nm_ref[...]).astype(BF16)
    acc = x2
    for c in range(0, D_FF, FF_CHUNK):
        u = jnp.maximum(_dot(h, wu_ref[:, c:c + FF_CHUNK]), 0.0)
        acc = acc + _dot((u * u).astype(BF16), wd_ref[c:c + FF_CHUNK, :])
    o_ref[...] = acc


def _post_attn(x2d, od, of, mk, mv, batch, w_out, norm_q, w_mem_q, mem_q_norm_w, w_mem_o,
               norm_mlp, w_up, w_down):
    n_tok = x2d.shape[0]
    tiles = n_tok // batch // POST_TILE
    mem_len = mk.shape[0] // batch
    tok = lambda width: pl.BlockSpec((POST_TILE, width), lambda b, i: (b * tiles + i, 0))
    mem = pl.BlockSpec((mem_len, D_MODEL), lambda b, i: (b, 0))
    scale = MEM_HEAD_DIM ** -0.5
    square = _resident((D_MODEL, D_MODEL))
    vec = _resident((1, D_MODEL))
    return pl.pallas_call(
        _post_attn_body,
        grid=(batch, tiles),
        in_specs=[tok(D_MODEL), tok(GROUP_W), tok(GROUP_W), mem, mem, square, vec, square,
                  _resident((1, MEM_HEAD_DIM)), square, vec, _resident((D_MODEL, D_FF)),
                  _resident((D_FF, D_MODEL))],
        out_specs=tok(D_MODEL),
        out_shape=jax.ShapeDtypeStruct((n_tok, D_MODEL), F32),
        compiler_params=_params(("arbitrary", "arbitrary")),
        name="post_attn",
    )(x2d, od, of, mk, mv, w_out.astype(BF16), norm_q.reshape(1, -1), w_mem_q.astype(BF16),
      (mem_q_norm_w * scale).reshape(1, -1), w_mem_o.astype(BF16), norm_mlp.reshape(1, -1),
      w_up.astype(BF16), w_down.astype(BF16))


def kernel(x, mem, positions, norm_mix_w, w_in, b_forget, diff_q_norm_w, diff_k_norm_w,
           lambda_q1, lambda_k1, lambda_q2, lambda_k2, diff_subln_w, fox_q_norm_w,
           fox_k_norm_w, w_out, norm_mem_q_w, norm_mem_kv_w, w_mem_q, w_mem_kv,
           mem_q_norm_w, mem_k_norm_w, w_mem_o, norm_mlp_w, w_up, w_down):
    batch, seq, d = x.shape
    depth = w_in.shape[0]
    assert d == D_MODEL and seq % TOK_TILE == 0 and seq % POST_TILE == 0
    assert TOK_TILE % ATT_TILE == 0
    assert (batch * mem.shape[1]) % TOK_TILE == 0

    cos_t, sin_t = _rope_tables(positions)
    xc = x.reshape(batch * seq, d)
    mem2d = mem.reshape(-1, d)
    for l in range(depth):
        lam_init = 0.8 - 0.6 * math.exp(-0.3 * l)
        dqt, dk, dvt, fqt, fk, fvt, qaugt, kaug = _in_proj(
            xc, batch, norm_mix_w[l], w_in[l], cos_t, sin_t, diff_q_norm_w[l],
            diff_k_norm_w[l], fox_q_norm_w[l], fox_k_norm_w[l], b_forget[l])
        lam_params = [p[l].reshape(1, -1) for p in (lambda_q1, lambda_k1, lambda_q2, lambda_k2)]
        lam_params.append(diff_subln_w[l].reshape(-1, 1))
        od, of = _attention(dqt, dk, dvt, fqt, fk, fvt, qaugt, kaug, lam_params, batch, lam_init)
        mk, mv = _mem_kv(mem2d, norm_mem_kv_w[l], w_mem_kv[l], mem_k_norm_w[l])
        xc = _post_attn(xc, od, of, mk, mv, batch, w_out[l], norm_mem_q_w[l], w_mem_q[l],
                        mem_q_norm_w[l], w_mem_o[l], norm_mlp_w[l], w_up[l], w_down[l])
    return xc.reshape(batch, seq, d)
```

```python
import functools
import math
from typing import Any, NamedTuple

import jax
import jax.numpy as jnp
from jax import lax
from jax.experimental import pallas as pl
from jax.experimental.pallas import tpu as pltpu

F32 = jnp.float32
BF16 = jnp.bfloat16

D_MODEL = 1024
CHUNK = 64
HEAD_DIM = 64
N_DIFF_HEADS = 4
DIFF_V_DIM = 2 * HEAD_DIM
N_FOX_HEADS = 8
GROUP_W = 512
ROPE_DIM = HEAD_DIM // 4
ROPE_THETA = 500000.0
N_MEM_HEADS = 4
MEM_HEAD_DIM = D_MODEL // N_MEM_HEADS
D_FF = 4 * D_MODEL
EPS = 1e-6
NEG_INF = -1e30
LOG2E = math.log2(math.e)

LANES = 128
MXU_DIM = 256
VMEM_LIMIT = 58 * 1024 * 1024

TOK_TILE = 1024
ATT_TILE = 256


def _params(sem):
    return pltpu.CompilerParams(dimension_semantics=sem, vmem_limit_bytes=VMEM_LIMIT)


def _resident(shape):
    return pl.BlockSpec(shape, lambda *_: (0,) * len(shape), pipeline_mode=pl.Buffered(1))


def _rms(x, w):
    ms = jnp.mean(x * x, axis=-1, keepdims=True)
    return (x * lax.rsqrt(ms + EPS)) * w


def _dot(a, b):
    return jnp.dot(a, b, preferred_element_type=F32)


def _dot_nt(a, b):
    return lax.dot_general(a, b, (((1,), (1,)), ((), ())), preferred_element_type=F32)


def _rope_body(pos_ref, freq_ref, cos_ref, sin_ref):
    ang = pos_ref[...] * freq_ref[...]
    cos_ref[...] = jnp.cos(ang)
    sin_ref[...] = jnp.sin(ang)


def _rope_tables(positions):
    n_tok = positions.size
    half = ROPE_DIM // 2
    inv_freq = ROPE_THETA ** (-jnp.arange(0, ROPE_DIM, 2, dtype=F32) / ROPE_DIM)
    out = jax.ShapeDtypeStruct((half, n_tok), F32)
    return pl.pallas_call(_rope_body, out_shape=(out, out), name="rope_tables")(
        positions.reshape(1, n_tok).astype(F32), inv_freq.reshape(half, 1))


def _log_sigmoid(x):
    return jnp.minimum(x, 0.0) - jnp.log1p(jnp.exp(-jnp.abs(x)))


def _split3(x):
    hi = x.astype(BF16)
    rem = x - hi.astype(F32)
    mid = rem.astype(BF16)
    lo = (rem - mid.astype(F32)).astype(BF16)
    return hi, mid, lo


BIAS_SLOT = LANES // N_FOX_HEADS


def _head_norm_t(t, gain, rope):
    half = ROPE_DIM // 2
    outs = []
    for r0 in range(0, t.shape[0], HEAD_DIM):
        th = t[r0:r0 + HEAD_DIM]
        ms = jnp.sum(th * th, axis=0, keepdims=True) * (1.0 / HEAD_DIM)
        th = (th * lax.rsqrt(ms + EPS)) * gain[r0:r0 + HEAD_DIM]
        if rope is not None:
            cos, sin = rope
            t1, t2 = th[:half], th[half:ROPE_DIM]
            th = jnp.concatenate(
                [t1 * cos - t2 * sin, t2 * cos + t1 * sin, th[ROPE_DIM:]], axis=0)
        outs.append(th)
    return jnp.concatenate(outs, axis=0)


def _in_proj_body(x_ref, nw_ref, wt_ref, wft_ref, cos_ref, sin_ref, gain_ref, bf_ref, dqt_ref,
                  dk_ref, dvt_ref, fqt_ref, fk_ref, fvt_ref, qaugt_ref, kaug_ref, carry_ref):
    @pl.when(pl.program_id(1) == 0)
    def _():
        carry_ref[...] = jnp.zeros_like(carry_ref)

    h = _rms(x_ref[...], nw_ref[...]).astype(BF16)
    rope = (cos_ref[...], sin_ref[...])
    g = GROUP_W

    def proj_t(group):
        r0 = group * GROUP_W
        return _dot_nt(wt_ref[r0:r0 + GROUP_W, :], h)

    def store_keys(k_ref, kt):
        for c in range(0, g, LANES):
            k_ref[:, c:c + LANES] = kt[c:c + LANES].T.astype(BF16)

    def store_values(vt_ref, vt):
        for t in range(TOK_TILE // ATT_TILE):
            vt_ref[t] = vt[:, t * ATT_TILE:(t + 1) * ATT_TILE].astype(BF16)

    dqt_ref[...] = _head_norm_t(proj_t(0), gain_ref[0:g], rope).astype(BF16)
    store_keys(dk_ref, _head_norm_t(proj_t(1), gain_ref[g:2 * g], rope))
    store_values(dvt_ref, proj_t(2))
    fqt_ref[...] = _head_norm_t(proj_t(3), gain_ref[2 * g:3 * g], None).astype(BF16)
    store_keys(fk_ref, _head_norm_t(proj_t(4), gain_ref[3 * g:4 * g], None))
    store_values(fvt_ref, proj_t(5))

    log_f = _log_sigmoid(_dot_nt(wft_ref[...], h) + bf_ref[...])
    r = lax.broadcasted_iota(jnp.int32, (MXU_DIM, MXU_DIM), 0)
    c = lax.broadcasted_iota(jnp.int32, (MXU_DIM, MXU_DIM), 1)
    triu = jnp.where(r <= c, 1.0, 0.0).astype(BF16)
    part = lax.broadcasted_iota(jnp.int32, (LANES, MXU_DIM), 0) % BIAS_SLOT
    carry = carry_ref[:, 0:1]
    for c0 in range(0, TOK_TILE, MXU_DIM):
        hi, mid, lo = _split3(log_f[:, c0:c0 + MXU_DIM])
        cum = (_dot(hi, triu) + _dot(mid, triu)) + _dot(lo, triu) + carry
        carry = cum[:, MXU_DIM - 1:MXU_DIM]
        hi, mid, lo = (t.astype(F32) for t in _split3(cum * LOG2E))
        kaug_t = jnp.where(part == 0, -hi, jnp.where(part == 1, -mid, jnp.where(
            part == 2, -lo, jnp.where(part < 6, 1.0, 0.0))))
        qaug_t = jnp.where(part < 3, 1.0, jnp.where(part == 3, hi, jnp.where(
            part == 4, mid, jnp.where(part == 5, lo, 0.0))))
        kaug_ref[c0:c0 + MXU_DIM, :] = kaug_t.T.astype(BF16)
        qaugt_ref[:, c0:c0 + MXU_DIM] = qaug_t.astype(BF16)
    carry_ref[...] = jnp.broadcast_to(carry, carry_ref.shape)


def _in_proj(x2d, batch, norm_w, w_in, cos_t, sin_t, dqw, dkw, fqw, fkw, b_forget):
    n_tok = x2d.shape[0]
    tiles = n_tok // batch // TOK_TILE
    g = GROUP_W
    wt = w_in[:, :6 * g].T.astype(BF16)
    wft = jnp.repeat(w_in[:, 6 * g:6 * g + N_FOX_HEADS].T, BIAS_SLOT, axis=0).astype(BF16)
    bf = jnp.repeat(b_forget, BIAS_SLOT).reshape(LANES, 1)
    qscale = HEAD_DIM ** -0.5 * LOG2E
    gains = jnp.concatenate([jnp.tile(v, g // HEAD_DIM) for v in
                             (dqw * qscale, dkw, fqw * qscale, fkw)]).reshape(4 * g, 1)

    row = lambda b, i: (b * tiles + i, 0)
    col = lambda b, i: (0, b * tiles + i)
    tok = lambda width: pl.BlockSpec((TOK_TILE, width), row)
    tok_t = lambda rows: pl.BlockSpec((rows, TOK_TILE), col)
    slabs = TOK_TILE // ATT_TILE
    vt_spec = pl.BlockSpec((slabs, g, ATT_TILE), lambda b, i: (b * tiles + i, 0, 0))
    k_shape = jax.ShapeDtypeStruct((n_tok, g), BF16)
    qt_shape = jax.ShapeDtypeStruct((g, n_tok), BF16)
    vt_shape = jax.ShapeDtypeStruct((n_tok // ATT_TILE, g, ATT_TILE), BF16)
    half = ROPE_DIM // 2
    return pl.pallas_call(
        _in_proj_body,
        grid=(batch, tiles),
        in_specs=[tok(D_MODEL), _resident((1, D_MODEL)), _resident(wt.shape),
                  _resident(wft.shape), tok_t(half), tok_t(half), _resident(gains.shape),
                  _resident(bf.shape)],
        out_specs=[tok_t(g), tok(g), vt_spec, tok_t(g), tok(g), vt_spec, tok_t(LANES),
                   tok(LANES)],
        out_shape=[qt_shape, k_shape, vt_shape, qt_shape, k_shape, vt_shape,
                   jax.ShapeDtypeStruct((LANES, n_tok), BF16),
                   jax.ShapeDtypeStruct((n_tok, LANES), BF16)],
        scratch_shapes=[pltpu.VMEM((LANES, LANES), F32)],
        compiler_params=_params(("arbitrary", "arbitrary")),
        name="in_proj",
    )(x2d, norm_w.reshape(1, -1), wt, wft, cos_t, sin_t, gains, bf)


N_CHAINS = 2 * (GROUP_W // LANES)
SUM_ROWS = 16


class _Group(NamedTuple):
    fox: bool
    qt: tuple
    k: Any
    vt: Any
    qaug: Any
    kaug: Any
    out: Any
    rhs: Any
    s: tuple
    bmax: Any
    m: Any
    acc: Any

    @property
    def width(self):
        return HEAD_DIM if self.fox else DIFF_V_DIM


def _attn_scratch(fox):
    width = HEAD_DIM if fox else DIFF_V_DIM
    rhs_w = 2 * LANES if fox else LANES
    return [pltpu.VMEM((2, N_CHAINS, rhs_w, ATT_TILE), BF16),
            pltpu.VMEM((N_CHAINS, ATT_TILE, ATT_TILE), F32),
            pltpu.VMEM((N_CHAINS, ATT_TILE, ATT_TILE), F32),
            pltpu.VMEM((2, N_CHAINS, ATT_TILE), F32),
            pltpu.VMEM((2, N_CHAINS, ATT_TILE), F32),
            pltpu.VMEM((2, N_CHAINS, width + SUM_ROWS, ATT_TILE), F32)]


def _attn_body(*refs, lam_init, half, static_steps):
    s = pl.program_id(1)
    for k in static_steps:
        @pl.when(s == k)
        def _(k=k):
            _attn_step(k, *refs, lam_init=lam_init, half=half)

    generic = [k for k in range(half) if k not in static_steps]
    if generic:
        @pl.when(s <= max(generic))
        def _():
            _attn_step(s, *refs, lam_init=lam_init, half=half)


def _attn_step(s, dqa, dqb, dk, dvt, fqa, fqb, fk, fvt, qauga, qaugb, kaug, lq1_ref, lk1_ref,
               lq2_ref, lk2_ref, subw_ref, od_ref, of_ref, *scratch, lam_init, half):
    def group(fox, qt, k, vt, qaug, kaug_ref, out, scr):
        rhs, s0, s1, bmax, m, acc = scr
        return _Group(fox, qt, k, vt, qaug, kaug_ref, out, rhs, (s0, s1), bmax, m, acc)

    groups = (group(False, (dqa, dqb), dk, dvt, None, None, od_ref, scratch[:6]),
              group(True, (fqa, fqb), fk, fvt, (qauga, qaugb), kaug, of_ref, scratch[6:]))
    static = isinstance(s, int)
    last = 2 * s + half + 1
    ones_rows = jnp.ones((SUM_ROWS, ATT_TILE), BF16)

    half_zero = jnp.zeros((HEAD_DIM, ATT_TILE), BF16)
    for gr in groups:
        for tile, qt_ref in enumerate(gr.qt):
            for g in range(GROUP_W // LANES):
                r0 = g * LANES
                gr.rhs[tile, 2 * g, :HEAD_DIM, :] = qt_ref[r0:r0 + HEAD_DIM, :]
                gr.rhs[tile, 2 * g, HEAD_DIM:LANES, :] = half_zero
                gr.rhs[tile, 2 * g + 1, :HEAD_DIM, :] = half_zero
                gr.rhs[tile, 2 * g + 1, HEAD_DIM:LANES, :] = qt_ref[r0 + HEAD_DIM:r0 + LANES, :]
            if gr.fox:
                for c in range(N_CHAINS):
                    b0 = c * BIAS_SLOT
                    gr.rhs[tile, c, LANES:, :] = jnp.zeros((LANES, ATT_TILE), BF16)
                    gr.rhs[tile, c, LANES + b0:LANES + b0 + BIAS_SLOT, :] = (
                        gr.qaug[tile][b0:b0 + BIAS_SLOT, :])
        gr.m[...] = jnp.full(gr.m.shape, NEG_INF, F32)
        gr.acc[...] = jnp.zeros(gr.acc.shape, F32)

    def locate(p):
        if static:
            return (1, p - s - 1) if p > s else (0, s - p)
        in_b = p > s
        return in_b.astype(jnp.int32), jnp.where(in_b, p - s - 1, s - p)

    def diagonal_mask(gr):
        div = 1 if gr.fox else CHUNK
        k_id = lax.broadcasted_iota(jnp.int32, (ATT_TILE, ATT_TILE), 0) // div
        q_id = lax.broadcasted_iota(jnp.int32, (ATT_TILE, ATT_TILE), 1) // div
        return k_id <= q_id

    def scores(gr, p, slot, g, masked):
        tile, blk = locate(p)
        start = blk * ATT_TILE
        rows = pl.ds(start if static else pl.multiple_of(start, ATT_TILE), ATT_TILE)
        lhs = gr.k[rows, g * LANES:(g + 1) * LANES]
        if gr.fox:
            lhs = jnp.concatenate([lhs, gr.kaug[rows, :]], axis=-1)
        for c in (2 * g, 2 * g + 1):
            sc = _dot(lhs, gr.rhs[tile, c])
            if masked:
                sc = jnp.where(diagonal_mask(gr), sc, NEG_INF)
            gr.s[slot][c] = sc
            gr.bmax[slot, c:c + 1, :] = jnp.max(sc, axis=0, keepdims=True)

    def softmax_pv(gr, p, slot, c):
        tile, blk = locate(p)
        sc = gr.s[slot][c]
        m_old = gr.m[tile, c:c + 1, :]
        m_new = jnp.maximum(m_old, gr.bmax[slot, c:c + 1, :])
        alpha = jnp.exp2(m_old - m_new)
        prob = jnp.exp2(sc - m_new)
        gr.m[tile, c:c + 1, :] = m_new
        r0 = c * HEAD_DIM if gr.fox else (c // 2) * LANES
        vt = jnp.concatenate([gr.vt[blk, r0:r0 + gr.width, :], ones_rows], axis=0)
        gr.acc[tile, c] = alpha * gr.acc[tile, c] + _dot(vt, prob.astype(BF16))

    def step(score_args, soft_args, masked=False):
        for g in range(GROUP_W // LANES):
            for gr in groups:
                if score_args is not None:
                    scores(gr, *score_args, g, masked)
                if soft_args is not None:
                    softmax_pv(gr, *soft_args, 2 * g)
                    softmax_pv(gr, *soft_args, 2 * g + 1)

    def pair(u, carry):
        p = 2 * u + 1
        step((p + 1, 0), (p, 1))
        step((p + 2, 1), (p + 1, 0))
        return carry

    step((0, 0), None, masked=True)
    step((1, 1), (0, 0))
    if static:
        for u in range((last - 3) // 2):
            pair(u, 0)
    else:
        lax.fori_loop(0, (last - 3) // 2, pair, 0)
    step((last - 1, 0), (last - 2, 1))
    step((last, 1), (last - 1, 0), masked=True)
    step(None, (last, 1))

    lam = (jnp.exp(jnp.sum(lq1_ref[...] * lk1_ref[...], axis=-1, keepdims=True))
           - jnp.exp(jnp.sum(lq2_ref[...] * lk2_ref[...], axis=-1, keepdims=True))
           + lam_init)
    for gr in groups:
        width = gr.width
        for tile in range(2):
            for g in range(GROUP_W // LANES):
                a1, a2 = gr.acc[tile, 2 * g], gr.acc[tile, 2 * g + 1]
                o1 = a1[:width] * (1.0 / a1[width:width + 1])
                o2 = a2[:width] * (1.0 / a2[width:width + 1])
                if gr.fox:
                    o_t = jnp.concatenate([o1, o2], axis=0)
                else:
                    o_t = o1 - lam * o2
                    ms = jnp.mean(o_t * o_t, axis=0, keepdims=True)
                    o_t = ((o_t * lax.rsqrt(ms + EPS)) * subw_ref[...]) * (1.0 - lam_init)
                gr.out[0, tile, 0, :, g * LANES:(g + 1) * LANES] = o_t.T.astype(BF16)


def _attention(dqt, dk, dvt, fqt, fk, fvt, qaugt, kaug, lam_params, batch, lam_init):
    n_tok = dk.shape[0]
    seq = n_tok // batch
    tiles = seq // ATT_TILE
    assert tiles % 2 == 0
    half = tiles // 2
    qt_spec = lambda r, off: pl.BlockSpec((r, ATT_TILE), lambda b, s: (0, b * tiles + s + off))
    kv_spec = lambda w: pl.BlockSpec((seq, w), lambda b, s: (b, 0))
    vt_spec = pl.BlockSpec((tiles, GROUP_W, ATT_TILE), lambda b, s: (b, 0, 0))
    qkv_specs = [qt_spec(GROUP_W, 0), qt_spec(GROUP_W, half), kv_spec(GROUP_W), vt_spec]
    out_spec = pl.BlockSpec((1, 2, 1, ATT_TILE, GROUP_W), lambda b, s: (b, 0, s, 0, 0))
    out_shape = jax.ShapeDtypeStruct((batch, 2, half, ATT_TILE, GROUP_W), BF16)
    od, of = pl.pallas_call(
        functools.partial(_attn_body, lam_init=lam_init, half=half, static_steps=(half - 1,)),
        grid=(batch, half),
        in_specs=qkv_specs + qkv_specs
        + [qt_spec(LANES, 0), qt_spec(LANES, half), kv_spec(LANES)]
        + [_resident(p.shape) for p in lam_params],
        out_specs=[out_spec, out_spec],
        out_shape=[out_shape, out_shape],
        scratch_shapes=_attn_scratch(False) + _attn_scratch(True),
        compiler_params=_params(("arbitrary", "arbitrary")),
        name="attention",
    )(dqt, dqt, dk, dvt, fqt, fqt, fk, fvt, qaugt, qaugt, kaug, *lam_params)
    return od.reshape(n_tok, GROUP_W), of.reshape(n_tok, GROUP_W)


def _mem_kv_body(m_ref, nw_ref, w_ref, kw_ref, mk_ref, mv_ref):
    hm = _rms(m_ref[...], nw_ref[...]).astype(BF16)
    kv = _dot(hm, w_ref[...])
    kw = kw_ref[...]
    for c in range(0, D_MODEL, MEM_HEAD_DIM):
        mk_ref[:, c:c + MEM_HEAD_DIM] = _rms(kv[:, c:c + MEM_HEAD_DIM], kw).astype(BF16)
    mv_ref[...] = kv[:, D_MODEL:].astype(BF16)


def _mem_kv(mem2d, norm_w, w_mem_kv, mem_k_norm_w):
    n_mem = mem2d.shape[0]
    tok = pl.BlockSpec((TOK_TILE, D_MODEL), lambda i: (i, 0))
    out = jax.ShapeDtypeStruct((n_mem, D_MODEL), BF16)
    return pl.pallas_call(
        _mem_kv_body,
        grid=(n_mem // TOK_TILE,),
        in_specs=[tok, _resident((1, D_MODEL)), _resident((D_MODEL, 2 * D_MODEL)),
                  _resident((1, MEM_HEAD_DIM))],
        out_specs=[tok, tok],
        out_shape=[out, out],
        compiler_params=_params(("arbitrary",)),
        name="mem_kv",
    )(mem2d, norm_w.reshape(1, -1), w_mem_kv.astype(BF16), mem_k_norm_w.reshape(1, -1))


POST_TILE = 1024
FF_CHUNK = 1024


def _post_attn_body(x_ref, od_ref, of_ref, mk_ref, mv_ref, wo_ref, nq_ref, wq_ref, qw_ref,
                    wmo_ref, nm_ref, wu_ref, wd_ref, o_ref):
    x1 = x_ref[...] + (_dot(od_ref[...], wo_ref[:GROUP_W, :])
                       + _dot(of_ref[...], wo_ref[GROUP_W:, :]))

    mq = _dot(_rms(x1, nq_ref[...]).astype(BF16), wq_ref[...])
    qw = qw_ref[...]
    heads = []
    for c in range(0, D_MODEL, MEM_HEAD_DIM):
        q = _rms(mq[:, c:c + MEM_HEAD_DIM], qw).astype(BF16)
        s = _dot_nt(q, mk_ref[:, c:c + MEM_HEAD_DIM])
        p = jnp.exp(s - jnp.max(s, axis=-1, keepdims=True))
        l = jnp.sum(p, axis=-1, keepdims=True)
        heads.append((_dot(p.astype(BF16), mv_ref[:, c:c + MEM_HEAD_DIM]) / l).astype(BF16))
    x2 = x1 + _dot(jnp.concatenate(heads, axis=-1), wmo_ref[...])

    h = _rms(x2, nm_ref[...]).astype(BF16)
    acc = x2
    for c in range(0, D_FF, FF_CHUNK):
        u = jnp.maximum(_dot(h, wu_ref[:, c:c + FF_CHUNK]), 0.0)
        acc = acc + _dot((u * u).astype(BF16), wd_ref[c:c + FF_CHUNK, :])
    o_ref[...] = acc


def _post_attn(x2d, od, of, mk, mv, batch, w_out, norm_q, w_mem_q, mem_q_norm_w, w_mem_o,
               norm_mlp, w_up, w_down):
    n_tok = x2d.shape[0]
    tiles = n_tok // batch // POST_TILE
    mem_len = mk.shape[0] // batch
    tok = lambda width: pl.BlockSpec((POST_TILE, width), lambda b, i: (b * tiles + i, 0))
    mem = pl.BlockSpec((mem_len, D_MODEL), lambda b, i: (b, 0))
    scale = MEM_HEAD_DIM ** -0.5
    square = _resident((D_MODEL, D_MODEL))
    vec = _resident((1, D_MODEL))
    return pl.pallas_call(
        _post_attn_body,
        grid=(batch, tiles),
        in_specs=[tok(D_MODEL), tok(GROUP_W), tok(GROUP_W), mem, mem, square, vec, square,
                  _resident((1, MEM_HEAD_DIM)), square, vec, _resident((D_MODEL, D_FF)),
                  _resident((D_FF, D_MODEL))],
        out_specs=tok(D_MODEL),
        out_shape=jax.ShapeDtypeStruct((n_tok, D_MODEL), F32),
        compiler_params=_params(("arbitrary", "arbitrary")),
        name="post_attn",
    )(x2d, od, of, mk, mv, w_out.astype(BF16), norm_q.reshape(1, -1), w_mem_q.astype(BF16),
      (mem_q_norm_w * scale).reshape(1, -1), w_mem_o.astype(BF16), norm_mlp.reshape(1, -1),
      w_up.astype(BF16), w_down.astype(BF16))


def kernel(x, mem, positions, norm_mix_w, w_in, b_forget, diff_q_norm_w, diff_k_norm_w,
           lambda_q1, lambda_k1, lambda_q2, lambda_k2, diff_subln_w, fox_q_norm_w,
           fox_k_norm_w, w_out, norm_mem_q_w, norm_mem_kv_w, w_mem_q, w_mem_kv,
           mem_q_norm_w, mem_k_norm_w, w_mem_o, norm_mlp_w, w_up, w_down):
    batch, seq, d = x.shape
    depth = w_in.shape[0]
    assert d == D_MODEL and seq % TOK_TILE == 0 and seq % POST_TILE == 0
    assert TOK_TILE % ATT_TILE == 0
    assert (batch * mem.shape[1]) % TOK_TILE == 0

    cos_t, sin_t = _rope_tables(positions)
    xc = x.reshape(batch * seq, d)
    mem2d = mem.reshape(-1, d)
    for l in range(depth):
        lam_init = 0.8 - 0.6 * math.exp(-0.3 * l)
        dqt, dk, dvt, fqt, fk, fvt, qaugt, kaug = _in_proj(
            xc, batch, norm_mix_w[l], w_in[l], cos_t, sin_t, diff_q_norm_w[l],
            diff_k_norm_w[l], fox_q_norm_w[l], fox_k_norm_w[l], b_forget[l])
        lam_params = [p[l].reshape(1, -1) for p in (lambda_q1, lambda_k1, lambda_q2, lambda_k2)]
        lam_params.append(diff_subln_w[l].reshape(-1, 1))
        od, of = _attention(dqt, dk, dvt, fqt, fk, fvt, qaugt, kaug, lam_params, batch, lam_init)
        mk, mv = _mem_kv(mem2d, norm_mem_kv_w[l], w_mem_kv[l], mem_k_norm_w[l])
        xc = _post_attn(xc, od, of, mk, mv, batch, w_out[l], norm_mem_q_w[l], w_mem_q[l],
                        mem_q_norm_w[l], w_mem_o[l], norm_mlp_w[l], w_up[l], w_down[l])
    return xc.reshape(batch, seq, d)
```

```python
import functools
import math
from typing import Any, NamedTuple

import jax
import jax.numpy as jnp
from jax import lax
from jax.experimental import pallas as pl
from jax.experimental.pallas import tpu as pltpu

F32 = jnp.float32
BF16 = jnp.bfloat16

D_MODEL = 1024
CHUNK = 64
HEAD_DIM = 64
N_DIFF_HEADS = 4
DIFF_V_DIM = 2 * HEAD_DIM
N_FOX_HEADS = 8
GROUP_W = 512
ROPE_DIM = HEAD_DIM // 4
ROPE_THETA = 500000.0
N_MEM_HEADS = 4
MEM_HEAD_DIM = D_MODEL // N_MEM_HEADS
D_FF = 4 * D_MODEL
EPS = 1e-6
NEG_INF = -1e30
LOG2E = math.log2(math.e)

LANES = 128
MXU_DIM = 256
VMEM_LIMIT = 58 * 1024 * 1024

TOK_TILE = 1024
ATT_TILE = 256


def _params(sem):
    return pltpu.CompilerParams(dimension_semantics=sem, vmem_limit_bytes=VMEM_LIMIT)


def _resident(shape):
    return pl.BlockSpec(shape, lambda *_: (0,) * len(shape), pipeline_mode=pl.Buffered(1))


def _rms(x, w):
    ms = jnp.mean(x * x, axis=-1, keepdims=True)
    return (x * lax.rsqrt(ms + EPS)) * w


def _dot(a, b):
    return jnp.dot(a, b, preferred_element_type=F32)


def _dot_nt(a, b):
    return lax.dot_general(a, b, (((1,), (1,)), ((), ())), preferred_element_type=F32)


def _rope_body(pos_ref, freq_ref, cos_ref, sin_ref):
    ang = pos_ref[...] * freq_ref[...]
    cos_ref[...] = jnp.cos(ang)
    sin_ref[...] = jnp.sin(ang)


def _rope_tables(positions):
    n_tok = positions.size
    half = ROPE_DIM // 2
    inv_freq = ROPE_THETA ** (-jnp.arange(0, ROPE_DIM, 2, dtype=F32) / ROPE_DIM)
    out = jax.ShapeDtypeStruct((half, n_tok), F32)
    return pl.pallas_call(_rope_body, out_shape=(out, out), name="rope_tables")(
        positions.reshape(1, n_tok).astype(F32), inv_freq.reshape(half, 1))


def _log_sigmoid(x):
    return jnp.minimum(x, 0.0) - jnp.log1p(jnp.exp(-jnp.abs(x)))


def _split3(x):
    hi = x.astype(BF16)
    rem = x - hi.astype(F32)
    mid = rem.astype(BF16)
    lo = (rem - mid.astype(F32)).astype(BF16)
    return hi, mid, lo


BIAS_SLOT = LANES // N_FOX_HEADS


def _head_norm_t(t, gain, rope):
    half = ROPE_DIM // 2
    outs = []
    for r0 in range(0, t.shape[0], HEAD_DIM):
        th = t[r0:r0 + HEAD_DIM]
        ms = jnp.sum(th * th, axis=0, keepdims=True) * (1.0 / HEAD_DIM)
        th = (th * lax.rsqrt(ms + EPS)) * gain[r0:r0 + HEAD_DIM]
        if rope is not None:
            cos, sin = rope
            t1, t2 = th[:half], th[half:ROPE_DIM]
            th = jnp.concatenate(
                [t1 * cos - t2 * sin, t2 * cos + t1 * sin, th[ROPE_DIM:]], axis=0)
        outs.append(th)
    return jnp.concatenate(outs, axis=0)


def _in_proj_body(x_ref, nw_ref, wt_ref, wft_ref, cos_ref, sin_ref, gain_ref, bf_ref, dqt_ref,
                  dk_ref, dvt_ref, fqt_ref, fk_ref, fvt_ref, qaugt_ref, kaug_ref, carry_ref):
    @pl.when(pl.program_id(1) == 0)
    def _():
        carry_ref[...] = jnp.zeros_like(carry_ref)

    h = _rms(x_ref[...], nw_ref[...]).astype(BF16)
    rope = (cos_ref[...], sin_ref[...])
    g = GROUP_W

    def proj_t(group):
        r0 = group * GROUP_W
        return _dot_nt(wt_ref[r0:r0 + GROUP_W, :], h)

    def store_keys(k_ref, kt):
        for c in range(0, g, LANES):
            k_ref[:, c:c + LANES] = kt[c:c + LANES].T.astype(BF16)

    def store_values(vt_ref, vt):
        for t in range(TOK_TILE // ATT_TILE):
            vt_ref[t] = vt[:, t * ATT_TILE:(t + 1) * ATT_TILE].astype(BF16)

    dqt_ref[...] = _head_norm_t(proj_t(0), gain_ref[0:g], rope).astype(BF16)
    store_keys(dk_ref, _head_norm_t(proj_t(1), gain_ref[g:2 * g], rope))
    store_values(dvt_ref, proj_t(2))
    fqt_ref[...] = _head_norm_t(proj_t(3), gain_ref[2 * g:3 * g], None).astype(BF16)
    store_keys(fk_ref, _head_norm_t(proj_t(4), gain_ref[3 * g:4 * g], None))
    store_values(fvt_ref, proj_t(5))

    log_f = _log_sigmoid(_dot_nt(wft_ref[...], h) + bf_ref[...])
    r = lax.broadcasted_iota(jnp.int32, (MXU_DIM, MXU_DIM), 0)
    c = lax.broadcasted_iota(jnp.int32, (MXU_DIM, MXU_DIM), 1)
    triu = jnp.where(r <= c, 1.0, 0.0).astype(BF16)
    part = lax.broadcasted_iota(jnp.int32, (LANES, MXU_DIM), 0) % BIAS_SLOT
    carry = carry_ref[:, 0:1]
    for c0 in range(0, TOK_TILE, MXU_DIM):
        hi, mid, lo = _split3(log_f[:, c0:c0 + MXU_DIM])
        cum = (_dot(hi, triu) + _dot(mid, triu)) + _dot(lo, triu) + carry
        carry = cum[:, MXU_DIM - 1:MXU_DIM]
        hi, mid, lo = (t.astype(F32) for t in _split3(cum * LOG2E))
        kaug_t = jnp.where(part == 0, -hi, jnp.where(part == 1, -mid, jnp.where(
            part == 2, -lo, jnp.where(part < 6, 1.0, 0.0))))
        qaug_t = jnp.where(part < 3, 1.0, jnp.where(part == 3, hi, jnp.where(
            part == 4, mid, jnp.where(part == 5, lo, 0.0))))
        kaug_ref[c0:c0 + MXU_DIM, :] = kaug_t.T.astype(BF16)
        qaugt_ref[:, c0:c0 + MXU_DIM] = qaug_t.astype(BF16)
    carry_ref[...] = jnp.broadcast_to(carry, carry_ref.shape)


def _in_proj(x2d, batch, norm_w, w_in, cos_t, sin_t, dqw, dkw, fqw, fkw, b_forget):
    n_tok = x2d.shape[0]
    tiles = n_tok // batch // TOK_TILE
    g = GROUP_W
    wt = w_in[:, :6 * g].T.astype(BF16)
    wft = jnp.repeat(w_in[:, 6 * g:6 * g + N_FOX_HEADS].T, BIAS_SLOT, axis=0).astype(BF16)
    bf = jnp.repeat(b_forget, BIAS_SLOT).reshape(LANES, 1)
    qscale = HEAD_DIM ** -0.5 * LOG2E
    gains = jnp.concatenate([jnp.tile(v, g // HEAD_DIM) for v in
                             (dqw * qscale, dkw, fqw * qscale, fkw)]).reshape(4 * g, 1)

    row = lambda b, i: (b * tiles + i, 0)
    col = lambda b, i: (0, b * tiles + i)
    tok = lambda width: pl.BlockSpec((TOK_TILE, width), row)
    tok_t = lambda rows: pl.BlockSpec((rows, TOK_TILE), col)
    slabs = TOK_TILE // ATT_TILE
    vt_spec = pl.BlockSpec((slabs, g, ATT_TILE), lambda b, i: (b * tiles + i, 0, 0))
    k_shape = jax.ShapeDtypeStruct((n_tok, g), BF16)
    qt_shape = jax.ShapeDtypeStruct((g, n_tok), BF16)
    vt_shape = jax.ShapeDtypeStruct((n_tok // ATT_TILE, g, ATT_TILE), BF16)
    half = ROPE_DIM // 2
    return pl.pallas_call(
        _in_proj_body,
        grid=(batch, tiles),
        in_specs=[tok(D_MODEL), _resident((1, D_MODEL)), _resident(wt.shape),
                  _resident(wft.shape), tok_t(half), tok_t(half), _resident(gains.shape),
                  _resident(bf.shape)],
        out_specs=[tok_t(g), tok(g), vt_spec, tok_t(g), tok(g), vt_spec, tok_t(LANES),
                   tok(LANES)],
        out_shape=[qt_shape, k_shape, vt_shape, qt_shape, k_shape, vt_shape,
                   jax.ShapeDtypeStruct((LANES, n_tok), BF16),
                   jax.ShapeDtypeStruct((n_tok, LANES), BF16)],
        scratch_shapes=[pltpu.VMEM((LANES, LANES), F32)],
        compiler_params=_params(("arbitrary", "arbitrary")),
        name="in_proj",
    )(x2d, norm_w.reshape(1, -1), wt, wft, cos_t, sin_t, gains, bf)


N_CHAINS = 2 * (GROUP_W // LANES)
SUM_ROWS = 16


class _Group(NamedTuple):
    fox: bool
    qt: tuple
    k: Any
    vt: Any
    qaug: Any
    kaug: Any
    out: Any
    rhs: Any
    s: tuple
    bmax: Any
    m: Any
    acc: Any

    @property
    def width(self):
        return HEAD_DIM if self.fox else DIFF_V_DIM


def _attn_scratch(fox):
    width = HEAD_DIM if fox else DIFF_V_DIM
    rhs_w = 2 * LANES if fox else LANES
    return [pltpu.VMEM((2, N_CHAINS, rhs_w, ATT_TILE), BF16),
            pltpu.VMEM((N_CHAINS, ATT_TILE, ATT_TILE), F32),
            pltpu.VMEM((N_CHAINS, ATT_TILE, ATT_TILE), F32),
            pltpu.VMEM((2, N_CHAINS, ATT_TILE), F32),
            pltpu.VMEM((2, N_CHAINS, ATT_TILE), F32),
            pltpu.VMEM((2, N_CHAINS, width + SUM_ROWS, ATT_TILE), F32)]


def _attn_body(*refs, lam_init, half, static_steps):
    s = pl.program_id(1)
    for k in static_steps:
        @pl.when(s == k)
        def _(k=k):
            _attn_step(k, *refs, lam_init=lam_init, half=half)

    generic = [k for k in range(half) if k not in static_steps]
    if generic:
        @pl.when(s <= max(generic))
        def _():
            _attn_step(s, *refs, lam_init=lam_init, half=half)


def _attn_step(s, dqa, dqb, dk, dvt, fqa, fqb, fk, fvt, qauga, qaugb, kaug, lq1_ref, lk1_ref,
               lq2_ref, lk2_ref, subw_ref, od_ref, of_ref, *scratch, lam_init, half):
    def group(fox, qt, k, vt, qaug, kaug_ref, out, scr):
        rhs, s0, s1, bmax, m, acc = scr
        return _Group(fox, qt, k, vt, qaug, kaug_ref, out, rhs, (s0, s1), bmax, m, acc)

    groups = (group(False, (dqa, dqb), dk, dvt, None, None, od_ref, scratch[:6]),
              group(True, (fqa, fqb), fk, fvt, (qauga, qaugb), kaug, of_ref, scratch[6:]))
    static = isinstance(s, int)
    last = 2 * s + half + 1
    ones_rows = jnp.ones((SUM_ROWS, ATT_TILE), BF16)

    half_zero = jnp.zeros((HEAD_DIM, ATT_TILE), BF16)
    for gr in groups:
        for tile, qt_ref in enumerate(gr.qt):
            for g in range(GROUP_W // LANES):
                r0 = g * LANES
                gr.rhs[tile, 2 * g, :HEAD_DIM, :] = qt_ref[r0:r0 + HEAD_DIM, :]
                gr.rhs[tile, 2 * g, HEAD_DIM:LANES, :] = half_zero
                gr.rhs[tile, 2 * g + 1, :HEAD_DIM, :] = half_zero
                gr.rhs[tile, 2 * g + 1, HEAD_DIM:LANES, :] = qt_ref[r0 + HEAD_DIM:r0 + LANES, :]
            if gr.fox:
                for c in range(N_CHAINS):
                    b0 = c * BIAS_SLOT
                    gr.rhs[tile, c, LANES:, :] = jnp.zeros((LANES, ATT_TILE), BF16)
                    gr.rhs[tile, c, LANES + b0:LANES + b0 + BIAS_SLOT, :] = (
                        gr.qaug[tile][b0:b0 + BIAS_SLOT, :])
        gr.m[...] = jnp.full(gr.m.shape, NEG_INF, F32)
        gr.acc[...] = jnp.zeros(gr.acc.shape, F32)

    def locate(p):
        if static:
            return (1, p - s - 1) if p > s else (0, s - p)
        in_b = p > s
        return in_b.astype(jnp.int32), jnp.where(in_b, p - s - 1, s - p)

    def diagonal_mask(gr):
        div = 1 if gr.fox else CHUNK
        k_id = lax.broadcasted_iota(jnp.int32, (ATT_TILE, ATT_TILE), 0) // div
        q_id = lax.broadcasted_iota(jnp.int32, (ATT_TILE, ATT_TILE), 1) // div
        return k_id <= q_id

    def scores(gr, p, slot, g, masked):
        tile, blk = locate(p)
        start = blk * ATT_TILE
        rows = pl.ds(start if static else pl.multiple_of(start, ATT_TILE), ATT_TILE)
        lhs = gr.k[rows, g * LANES:(g + 1) * LANES]
        if gr.fox:
            lhs = jnp.concatenate([lhs, gr.kaug[rows, :]], axis=-1)
        for c in (2 * g, 2 * g + 1):
            sc = _dot(lhs, gr.rhs[tile, c])
            if masked:
                sc = jnp.where(diagonal_mask(gr), sc, NEG_INF)
            gr.s[slot][c] = sc
            gr.bmax[slot, c:c + 1, :] = jnp.max(sc, axis=0, keepdims=True)

    def softmax_pv(gr, p, slot, c):
        tile, blk = locate(p)
        sc = gr.s[slot][c]
        m_old = gr.m[tile, c:c + 1, :]
        m_new = jnp.maximum(m_old, gr.bmax[slot, c:c + 1, :])
        alpha = jnp.exp2(m_old - m_new)
        prob = jnp.exp2(sc - m_new)
        gr.m[tile, c:c + 1, :] = m_new
        r0 = c * HEAD_DIM if gr.fox else (c // 2) * LANES
        vt = jnp.concatenate([gr.vt[blk, r0:r0 + gr.width, :], ones_rows], axis=0)
        gr.acc[tile, c] = alpha * gr.acc[tile, c] + _dot(vt, prob.astype(BF16))

    def step(score_args, soft_args, masked=False):
        for g in range(GROUP_W // LANES):
            for gr in groups:
                if score_args is not None:
                    scores(gr, *score_args, g, masked)
                if soft_args is not None:
                    softmax_pv(gr, *soft_args, 2 * g)
                    softmax_pv(gr, *soft_args, 2 * g + 1)

    def pair(u, carry):
        p = 2 * u + 1
        step((p + 1, 0), (p, 1))
        step((p + 2, 1), (p + 1, 0))
        return carry

    step((0, 0), None, masked=True)
    step((1, 1), (0, 0))
    if static:
        for u in range((last - 3) // 2):
            pair(u, 0)
    else:
        lax.fori_loop(0, (last - 3) // 2, pair, 0)
    step((last - 1, 0), (last - 2, 1))
    step((last, 1), (last - 1, 0), masked=True)
    step(None, (last, 1))

    lam = (jnp.exp(jnp.sum(lq1_ref[...] * lk1_ref[...], axis=-1, keepdims=True))
           - jnp.exp(jnp.sum(lq2_ref[...] * lk2_ref[...], axis=-1, keepdims=True))
           + lam_init)
    for gr in groups:
        width = gr.width
        for tile in range(2):
            for g in range(GROUP_W // LANES):
                a1, a2 = gr.acc[tile, 2 * g], gr.acc[tile, 2 * g + 1]
                o1 = a1[:width] * (1.0 / a1[width:width + 1])
                o2 = a2[:width] * (1.0 / a2[width:width + 1])
                if gr.fox:
                    o_t = jnp.concatenate([o1, o2], axis=0)
                else:
                    o_t = o1 - lam * o2
                    ms = jnp.mean(o_t * o_t, axis=0, keepdims=True)
                    o_t = ((o_t * lax.rsqrt(ms + EPS)) * subw_ref[...]) * (1.0 - lam_init)
                gr.out[0, tile, 0, :, g * LANES:(g + 1) * LANES] = o_t.T.astype(BF16)


def _attention(dqt, dk, dvt, fqt, fk, fvt, qaugt, kaug, lam_params, batch, lam_init):
    n_tok = dk.shape[0]
    seq = n_tok // batch
    tiles = seq // ATT_TILE
    assert tiles % 2 == 0
    half = tiles // 2
    qt_spec = lambda r, off: pl.BlockSpec((r, ATT_TILE), lambda b, s: (0, b * tiles + s + off))
    kv_spec = lambda w: pl.BlockSpec((seq, w), lambda b, s: (b, 0))
    vt_spec = pl.BlockSpec((tiles, GROUP_W, ATT_TILE), lambda b, s: (b, 0, 0))
    qkv_specs = [qt_spec(GROUP_W, 0), qt_spec(GROUP_W, half), kv_spec(GROUP_W), vt_spec]
    out_spec = pl.BlockSpec((1, 2, 1, ATT_TILE, GROUP_W), lambda b, s: (b, 0, s, 0, 0))
    out_shape = jax.ShapeDtypeStruct((batch, 2, half, ATT_TILE, GROUP_W), BF16)
    od, of = pl.pallas_call(
        functools.partial(_attn_body, lam_init=lam_init, half=half,
                          static_steps=(half - 2, half - 1)),
        grid=(batch, half),
        in_specs=qkv_specs + qkv_specs
        + [qt_spec(LANES, 0), qt_spec(LANES, half), kv_spec(LANES)]
        + [_resident(p.shape) for p in lam_params],
        out_specs=[out_spec, out_spec],
        out_shape=[out_shape, out_shape],
        scratch_shapes=_attn_scratch(False) + _attn_scratch(True),
        compiler_params=_params(("arbitrary", "arbitrary")),
        name="attention",
    )(dqt, dqt, dk, dvt, fqt, fqt, fk, fvt, qaugt, qaugt, kaug, *lam_params)
    return od.reshape(n_tok, GROUP_W), of.reshape(n_tok, GROUP_W)


def _mem_kv_body(m_ref, nw_ref, w_ref, kw_ref, mk_ref, mv_ref):
    hm = _rms(m_ref[...], nw_ref[...]).astype(BF16)
    kv = _dot(hm, w_ref[...])
    kw = kw_ref[...]
    for c in range(0, D_MODEL, MEM_HEAD_DIM):
        mk_ref[:, c:c + MEM_HEAD_DIM] = _rms(kv[:, c:c + MEM_HEAD_DIM], kw).astype(BF16)
    mv_ref[...] = kv[:, D_MODEL:].astype(BF16)


def _mem_kv(mem2d, norm_w, w_mem_kv, mem_k_norm_w):
    n_mem = mem2d.shape[0]
    tok = pl.BlockSpec((TOK_TILE, D_MODEL), lambda i: (i, 0))
    out = jax.ShapeDtypeStruct((n_mem, D_MODEL), BF16)
    return pl.pallas_call(
        _mem_kv_body,
        grid=(n_mem // TOK_TILE,),
        in_specs=[tok, _resident((1, D_MODEL)), _resident((D_MODEL, 2 * D_MODEL)),
                  _resident((1, MEM_HEAD_DIM))],
        out_specs=[tok, tok],
        out_shape=[out, out],
        compiler_params=_params(("arbitrary",)),
        name="mem_kv",
    )(mem2d, norm_w.reshape(1, -1), w_mem_kv.astype(BF16), mem_k_norm_w.reshape(1, -1))


POST_TILE = 1024
FF_CHUNK = 1024


def _post_attn_body(x_ref, od_ref, of_ref, mk_ref, mv_ref, wo_ref, nq_ref, wq_ref, qw_ref,
                    wmo_ref, nm_ref, wu_ref, wd_ref, o_ref):
    x1 = x_ref[...] + (_dot(od_ref[...], wo_ref[:GROUP_W, :])
                       + _dot(of_ref[...], wo_ref[GROUP_W:, :]))

    mq = _dot(_rms(x1, nq_ref[...]).astype(BF16), wq_ref[...])
    qw = qw_ref[...]
    heads = []
    for c in range(0, D_MODEL, MEM_HEAD_DIM):
        q = _rms(mq[:, c:c + MEM_HEAD_DIM], qw).astype(BF16)
        s = _dot_nt(q, mk_ref[:, c:c + MEM_HEAD_DIM])
        p = jnp.exp(s - jnp.max(s, axis=-1, keepdims=True))
        l = jnp.sum(p, axis=-1, keepdims=True)
        heads.append((_dot(p.astype(BF16), mv_ref[:, c:c + MEM_HEAD_DIM]) / l).astype(BF16))
    x2 = x1 + _dot(jnp.concatenate(heads, axis=-1), wmo_ref[...])

    h = _rms(x2, nm_ref[...]).astype(BF16)
    acc = x2
    for c in range(0, D_FF, FF_CHUNK):
        u = jnp.maximum(_dot(h, wu_ref[:, c:c + FF_CHUNK]), 0.0)
        acc = acc + _dot((u * u).astype(BF16), wd_ref[c:c + FF_CHUNK, :])
    o_ref[...] = acc


def _post_attn(x2d, od, of, mk, mv, batch, w_out, norm_q, w_mem_q, mem_q_norm_w, w_mem_o,
               norm_mlp, w_up, w_down):
    n_tok = x2d.shape[0]
    tiles = n_tok // batch // POST_TILE
    mem_len = mk.shape[0] // batch
    tok = lambda width: pl.BlockSpec((POST_TILE, width), lambda b, i: (b * tiles + i, 0))
    mem = pl.BlockSpec((mem_len, D_MODEL), lambda b, i: (b, 0))
    scale = MEM_HEAD_DIM ** -0.5
    square = _resident((D_MODEL, D_MODEL))
    vec = _resident((1, D_MODEL))
    return pl.pallas_call(
        _post_attn_body,
        grid=(batch, tiles),
        in_specs=[tok(D_MODEL), tok(GROUP_W), tok(GROUP_W), mem, mem, square, vec, square,
                  _resident((1, MEM_HEAD_DIM)), square, vec, _resident((D_MODEL, D_FF)),
                  _resident((D_FF, D_MODEL))],
        out_specs=tok(D_MODEL),
        out_shape=jax.ShapeDtypeStruct((n_tok, D_MODEL), F32),
        compiler_params=_params(("arbitrary", "arbitrary")),
        name="post_attn",
    )(x2d, od, of, mk, mv, w_out.astype(BF16), norm_q.reshape(1, -1), w_mem_q.astype(BF16),
      (mem_q_norm_w * scale).reshape(1, -1), w_mem_o.astype(BF16), norm_mlp.reshape(1, -1),
      w_up.astype(BF16), w_down.astype(BF16))


def kernel(x, mem, positions, norm_mix_w, w_in, b_forget, diff_q_norm_w, diff_k_norm_w,
           lambda_q1, lambda_k1, lambda_q2, lambda_k2, diff_subln_w, fox_q_norm_w,
           fox_k_norm_w, w_out, norm_mem_q_w, norm_mem_kv_w, w_mem_q, w_mem_kv,
           mem_q_norm_w, mem_k_norm_w, w_mem_o, norm_mlp_w, w_up, w_down):
    batch, seq, d = x.shape
    depth = w_in.shape[0]
    assert d == D_MODEL and seq % TOK_TILE == 0 and seq % POST_TILE == 0
    assert TOK_TILE % ATT_TILE == 0
    assert (batch * mem.shape[1]) % TOK_TILE == 0

    cos_t, sin_t = _rope_tables(positions)
    xc = x.reshape(batch * seq, d)
    mem2d = mem.reshape(-1, d)
    for l in range(depth):
        lam_init = 0.8 - 0.6 * math.exp(-0.3 * l)
        dqt, dk, dvt, fqt, fk, fvt, qaugt, kaug = _in_proj(
            xc, batch, norm_mix_w[l], w_in[l], cos_t, sin_t, diff_q_norm_w[l],
            diff_k_norm_w[l], fox_q_norm_w[l], fox_k_norm_w[l], b_forget[l])
        lam_params = [p[l].reshape(1, -1) for p in (lambda_q1, lambda_k1, lambda_q2, lambda_k2)]
        lam_params.append(diff_subln_w[l].reshape(-1, 1))
        od, of = _attention(dqt, dk, dvt, fqt, fk, fvt, qaugt, kaug, lam_params, batch, lam_init)
        mk, mv = _mem_kv(mem2d, norm_mem_kv_w[l], w_mem_kv[l], mem_k_norm_w[l])
        xc = _post_attn(xc, od, of, mk, mv, batch, w_out[l], norm_mem_q_w[l], w_mem_q[l],
                        mem_q_norm_w[l], w_mem_o[l], norm_mlp_w[l], w_up[l], w_down[l])
    return xc.reshape(batch, seq, d)
```

```python
import functools
import math
from typing import Any, NamedTuple

import jax
import jax.numpy as jnp
from jax import lax
from jax.experimental import pallas as pl
from jax.experimental.pallas import tpu as pltpu

F32 = jnp.float32
BF16 = jnp.bfloat16

D_MODEL = 1024
CHUNK = 64
HEAD_DIM = 64
DIFF_V_DIM = 2 * HEAD_DIM
N_FOX_HEADS = 8
GROUP_W = 512
ROPE_DIM = HEAD_DIM // 4
ROPE_THETA = 500000.0
N_MEM_HEADS = 4
MEM_HEAD_DIM = D_MODEL // N_MEM_HEADS
D_FF = 4 * D_MODEL
EPS = 1e-6
NEG_INF = -1e30
LOG2E = math.log2(math.e)

LANES = 128
MXU_DIM = 256
VMEM_LIMIT = 58 * 1024 * 1024

TOK_TILE = 1024
ATT_TILE = 256


def _params(sem):
    return pltpu.CompilerParams(dimension_semantics=sem, vmem_limit_bytes=VMEM_LIMIT)


def _resident(shape):
    return pl.BlockSpec(shape, lambda *_: (0,) * len(shape), pipeline_mode=pl.Buffered(1))


def _rms(x, w):
    ms = jnp.mean(x * x, axis=-1, keepdims=True)
    return (x * lax.rsqrt(ms + EPS)) * w


def _dot(a, b):
    return jnp.dot(a, b, preferred_element_type=F32)


def _dot_nt(a, b):
    return lax.dot_general(a, b, (((1,), (1,)), ((), ())), preferred_element_type=F32)


def _rope_body(pos_ref, freq_ref, cos_ref, sin_ref):
    ang = pos_ref[...] * freq_ref[...]
    cos_ref[...] = jnp.cos(ang)
    sin_ref[...] = jnp.sin(ang)


def _rope_tables(positions):
    n_tok = positions.size
    half = ROPE_DIM // 2
    inv_freq = ROPE_THETA ** (-jnp.arange(0, ROPE_DIM, 2, dtype=F32) / ROPE_DIM)
    out = jax.ShapeDtypeStruct((half, n_tok), F32)
    return pl.pallas_call(_rope_body, out_shape=(out, out), name="rope_tables")(
        positions.reshape(1, n_tok).astype(F32), inv_freq.reshape(half, 1))


def _log_sigmoid(x):
    return jnp.minimum(x, 0.0) - jnp.log1p(jnp.exp(-jnp.abs(x)))


def _split3(x):
    hi = x.astype(BF16)
    rem = x - hi.astype(F32)
    mid = rem.astype(BF16)
    lo = (rem - mid.astype(F32)).astype(BF16)
    return hi, mid, lo


BIAS_SLOT = LANES // N_FOX_HEADS


def _head_norm_t(t, gain, rope):
    half = ROPE_DIM // 2
    outs = []
    for r0 in range(0, t.shape[0], HEAD_DIM):
        th = t[r0:r0 + HEAD_DIM]
        ms = jnp.sum(th * th, axis=0, keepdims=True) * (1.0 / HEAD_DIM)
        th = (th * lax.rsqrt(ms + EPS)) * gain[r0:r0 + HEAD_DIM]
        if rope is not None:
            cos, sin = rope
            t1, t2 = th[:half], th[half:ROPE_DIM]
            th = jnp.concatenate(
                [t1 * cos - t2 * sin, t2 * cos + t1 * sin, th[ROPE_DIM:]], axis=0)
        outs.append(th)
    return jnp.concatenate(outs, axis=0)


def _in_proj_body(x_ref, nw_ref, wt_ref, wft_ref, cos_ref, sin_ref, gain_ref, bf_ref, dqt_ref,
                  dk_ref, dvt_ref, fqt_ref, fk_ref, fvt_ref, qaugt_ref, kaug_ref, carry_ref):
    @pl.when(pl.program_id(1) == 0)
    def _():
        carry_ref[...] = jnp.zeros_like(carry_ref)

    h = _rms(x_ref[...], nw_ref[...]).astype(BF16)
    rope = (cos_ref[...], sin_ref[...])
    g = GROUP_W

    def proj_t(group):
        r0 = group * GROUP_W
        return _dot_nt(wt_ref[r0:r0 + GROUP_W, :], h)

    def store_keys(k_ref, kt):
        for c in range(0, g, LANES):
            k_ref[:, c:c + LANES] = kt[c:c + LANES].T.astype(BF16)

    def store_values(vt_ref, vt):
        for t in range(TOK_TILE // ATT_TILE):
            vt_ref[t] = vt[:, t * ATT_TILE:(t + 1) * ATT_TILE].astype(BF16)

    dqt_ref[...] = _head_norm_t(proj_t(0), gain_ref[0:g], rope).astype(BF16)
    store_keys(dk_ref, _head_norm_t(proj_t(1), gain_ref[g:2 * g], rope))
    store_values(dvt_ref, proj_t(2))
    fqt_ref[...] = _head_norm_t(proj_t(3), gain_ref[2 * g:3 * g], None).astype(BF16)
    store_keys(fk_ref, _head_norm_t(proj_t(4), gain_ref[3 * g:4 * g], None))
    store_values(fvt_ref, proj_t(5))

    log_f = _log_sigmoid(_dot_nt(wft_ref[...], h) + bf_ref[...])
    r = lax.broadcasted_iota(jnp.int32, (MXU_DIM, MXU_DIM), 0)
    c = lax.broadcasted_iota(jnp.int32, (MXU_DIM, MXU_DIM), 1)
    triu = jnp.where(r <= c, 1.0, 0.0).astype(BF16)
    part = lax.broadcasted_iota(jnp.int32, (LANES, MXU_DIM), 0) % BIAS_SLOT
    carry = carry_ref[:, 0:1]
    for c0 in range(0, TOK_TILE, MXU_DIM):
        hi, mid, lo = _split3(log_f[:, c0:c0 + MXU_DIM])
        cum = (_dot(hi, triu) + _dot(mid, triu)) + _dot(lo, triu) + carry
        carry = cum[:, MXU_DIM - 1:MXU_DIM]
        hi, mid, lo = (t.astype(F32) for t in _split3(cum * LOG2E))
        kaug_t = jnp.where(part == 0, -hi, jnp.where(part == 1, -mid, jnp.where(
            part == 2, -lo, jnp.where(part < 6, 1.0, 0.0))))
        qaug_t = jnp.where(part < 3, 1.0, jnp.where(part == 3, hi, jnp.where(
            part == 4, mid, jnp.where(part == 5, lo, 0.0))))
        kaug_ref[c0:c0 + MXU_DIM, :] = kaug_t.T.astype(BF16)
        qaugt_ref[:, c0:c0 + MXU_DIM] = qaug_t.astype(BF16)
    carry_ref[...] = jnp.broadcast_to(carry, carry_ref.shape)


def _in_proj(x2d, batch, norm_w, w_in, cos_t, sin_t, dqw, dkw, fqw, fkw, b_forget):
    n_tok = x2d.shape[0]
    tiles = n_tok // batch // TOK_TILE
    g = GROUP_W
    wt = w_in[:, :6 * g].T.astype(BF16)
    wft = jnp.repeat(w_in[:, 6 * g:6 * g + N_FOX_HEADS].T, BIAS_SLOT, axis=0).astype(BF16)
    bf = jnp.repeat(b_forget, BIAS_SLOT).reshape(LANES, 1)
    qscale = HEAD_DIM ** -0.5 * LOG2E
    gains = jnp.concatenate([jnp.tile(v, g // HEAD_DIM) for v in
                             (dqw * qscale, dkw, fqw * qscale, fkw)]).reshape(4 * g, 1)

    row = lambda b, i: (b * tiles + i, 0)
    col = lambda b, i: (0, b * tiles + i)
    tok = lambda width: pl.BlockSpec((TOK_TILE, width), row)
    tok_t = lambda rows: pl.BlockSpec((rows, TOK_TILE), col)
    slabs = TOK_TILE // ATT_TILE
    vt_spec = pl.BlockSpec((slabs, g, ATT_TILE), lambda b, i: (b * tiles + i, 0, 0))
    k_shape = jax.ShapeDtypeStruct((n_tok, g), BF16)
    qt_shape = jax.ShapeDtypeStruct((g, n_tok), BF16)
    vt_shape = jax.ShapeDtypeStruct((n_tok // ATT_TILE, g, ATT_TILE), BF16)
    half = ROPE_DIM // 2
    return pl.pallas_call(
        _in_proj_body,
        grid=(batch, tiles),
        in_specs=[tok(D_MODEL), _resident((1, D_MODEL)), _resident(wt.shape),
                  _resident(wft.shape), tok_t(half), tok_t(half), _resident(gains.shape),
                  _resident(bf.shape)],
        out_specs=[tok_t(g), tok(g), vt_spec, tok_t(g), tok(g), vt_spec, tok_t(LANES),
                   tok(LANES)],
        out_shape=[qt_shape, k_shape, vt_shape, qt_shape, k_shape, vt_shape,
                   jax.ShapeDtypeStruct((LANES, n_tok), BF16),
                   jax.ShapeDtypeStruct((n_tok, LANES), BF16)],
        scratch_shapes=[pltpu.VMEM((LANES, LANES), F32)],
        compiler_params=_params(("arbitrary", "arbitrary")),
        name="in_proj",
    )(x2d, norm_w.reshape(1, -1), wt, wft, cos_t, sin_t, gains, bf)


N_CHAINS = 2 * (GROUP_W // LANES)
SUM_ROWS = 16


class _Group(NamedTuple):
    fox: bool
    qt: tuple
    k: Any
    vt: Any
    qaug: Any
    kaug: Any
    out: Any
    rhs: Any
    s: tuple
    bmax: Any
    m: Any
    acc: Any

    @property
    def width(self):
        return HEAD_DIM if self.fox else DIFF_V_DIM


def _attn_scratch(fox):
    width = HEAD_DIM if fox else DIFF_V_DIM
    rhs_w = 2 * LANES if fox else LANES
    return [pltpu.VMEM((2, N_CHAINS, rhs_w, ATT_TILE), BF16),
            pltpu.VMEM((N_CHAINS, ATT_TILE, ATT_TILE), F32),
            pltpu.VMEM((N_CHAINS, ATT_TILE, ATT_TILE), F32),
            pltpu.VMEM((2, N_CHAINS, ATT_TILE), F32),
            pltpu.VMEM((2, N_CHAINS, ATT_TILE), F32),
            pltpu.VMEM((2, N_CHAINS, width + SUM_ROWS, ATT_TILE), F32)]


def _attn_body(*refs, lam_init, half, static_steps):
    s = pl.program_id(1)
    for k in static_steps:
        @pl.when(s == k)
        def _(k=k):
            _attn_step(k, *refs, lam_init=lam_init, half=half)

    generic = [k for k in range(half) if k not in static_steps]
    if generic:
        @pl.when(s <= max(generic))
        def _():
            _attn_step(s, *refs, lam_init=lam_init, half=half)


def _attn_step(s, dqa, dqb, dk, dvt, fqa, fqb, fk, fvt, qauga, qaugb, kaug, lq1_ref, lk1_ref,
               lq2_ref, lk2_ref, subw_ref, od_ref, of_ref, *scratch, lam_init, half):
    def group(fox, qt, k, vt, qaug, kaug_ref, out, scr):
        rhs, s0, s1, bmax, m, acc = scr
        return _Group(fox, qt, k, vt, qaug, kaug_ref, out, rhs, (s0, s1), bmax, m, acc)

    groups = (group(False, (dqa, dqb), dk, dvt, None, None, od_ref, scratch[:6]),
              group(True, (fqa, fqb), fk, fvt, (qauga, qaugb), kaug, of_ref, scratch[6:]))
    static = isinstance(s, int)
    last = 2 * s + half + 1
    ones_rows = jnp.ones((SUM_ROWS, ATT_TILE), BF16)

    half_zero = jnp.zeros((HEAD_DIM, ATT_TILE), BF16)
    for gr in groups:
        for tile, qt_ref in enumerate(gr.qt):
            for g in range(GROUP_W // LANES):
                r0 = g * LANES
                gr.rhs[tile, 2 * g, :HEAD_DIM, :] = qt_ref[r0:r0 + HEAD_DIM, :]
                gr.rhs[tile, 2 * g, HEAD_DIM:LANES, :] = half_zero
                gr.rhs[tile, 2 * g + 1, :HEAD_DIM, :] = half_zero
                gr.rhs[tile, 2 * g + 1, HEAD_DIM:LANES, :] = qt_ref[r0 + HEAD_DIM:r0 + LANES, :]
            if gr.fox:
                for c in range(N_CHAINS):
                    b0 = c * BIAS_SLOT
                    gr.rhs[tile, c, LANES:, :] = jnp.zeros((LANES, ATT_TILE), BF16)
                    gr.rhs[tile, c, LANES + b0:LANES + b0 + BIAS_SLOT, :] = (
                        gr.qaug[tile][b0:b0 + BIAS_SLOT, :])
        gr.m[...] = jnp.full(gr.m.shape, NEG_INF, F32)
        gr.acc[...] = jnp.zeros(gr.acc.shape, F32)

    def locate(p):
        if static:
            return (1, p - s - 1) if p > s else (0, s - p)
        in_b = p > s
        return in_b.astype(jnp.int32), jnp.where(in_b, p - s - 1, s - p)

    def diagonal_mask(gr):
        div = 1 if gr.fox else CHUNK
        k_id = lax.broadcasted_iota(jnp.int32, (ATT_TILE, ATT_TILE), 0) // div
        q_id = lax.broadcasted_iota(jnp.int32, (ATT_TILE, ATT_TILE), 1) // div
        return k_id <= q_id

    def scores(gr, p, slot, g, masked):
        tile, blk = locate(p)
        start = blk * ATT_TILE
        rows = pl.ds(start if static else pl.multiple_of(start, ATT_TILE), ATT_TILE)
        lhs = gr.k[rows, g * LANES:(g + 1) * LANES]
        if gr.fox:
            lhs = jnp.concatenate([lhs, gr.kaug[rows, :]], axis=-1)
        for c in (2 * g, 2 * g + 1):
            sc = _dot(lhs, gr.rhs[tile, c])
            if masked:
                sc = jnp.where(diagonal_mask(gr), sc, NEG_INF)
            gr.s[slot][c] = sc
            gr.bmax[slot, c:c + 1, :] = jnp.max(sc, axis=0, keepdims=True)

    def softmax_pv(gr, p, slot, c):
        tile, blk = locate(p)
        sc = gr.s[slot][c]
        m_old = gr.m[tile, c:c + 1, :]
        m_new = jnp.maximum(m_old, gr.bmax[slot, c:c + 1, :])
        alpha = jnp.exp2(m_old - m_new)
        prob = jnp.exp2(sc - m_new)
        gr.m[tile, c:c + 1, :] = m_new
        r0 = c * HEAD_DIM if gr.fox else (c // 2) * LANES
        vt = jnp.concatenate([gr.vt[blk, r0:r0 + gr.width, :], ones_rows], axis=0)
        gr.acc[tile, c] = alpha * gr.acc[tile, c] + _dot(vt, prob.astype(BF16))

    def step(score_args, soft_args, masked=False):
        for g in range(GROUP_W // LANES):
            for gr in groups:
                if soft_args is not None:
                    softmax_pv(gr, *soft_args, 2 * g)
                if score_args is not None:
                    scores(gr, *score_args, g, masked)
                if soft_args is not None:
                    softmax_pv(gr, *soft_args, 2 * g + 1)

    def pair(u, carry):
        p = 2 * u + 1
        step((p + 1, 0), (p, 1))
        step((p + 2, 1), (p + 1, 0))
        return carry

    step((0, 0), None, masked=True)
    step((1, 1), (0, 0))
    if static:
        for u in range((last - 3) // 2):
            pair(u, 0)
    else:
        lax.fori_loop(0, (last - 3) // 2, pair, 0)
    step((last - 1, 0), (last - 2, 1))
    step((last, 1), (last - 1, 0), masked=True)
    step(None, (last, 1))

    lam = (jnp.exp(jnp.sum(lq1_ref[...] * lk1_ref[...], axis=-1, keepdims=True))
           - jnp.exp(jnp.sum(lq2_ref[...] * lk2_ref[...], axis=-1, keepdims=True))
           + lam_init)
    for gr in groups:
        width = gr.width
        for tile in range(2):
            for g in range(GROUP_W // LANES):
                a1, a2 = gr.acc[tile, 2 * g], gr.acc[tile, 2 * g + 1]
                o1 = a1[:width] * (1.0 / a1[width:width + 1])
                o2 = a2[:width] * (1.0 / a2[width:width + 1])
                if gr.fox:
                    o_t = jnp.concatenate([o1, o2], axis=0)
                else:
                    o_t = o1 - lam * o2
                    ms = jnp.mean(o_t * o_t, axis=0, keepdims=True)
                    o_t = ((o_t * lax.rsqrt(ms + EPS)) * subw_ref[...]) * (1.0 - lam_init)
                gr.out[0, tile, 0, :, g * LANES:(g + 1) * LANES] = o_t.T.astype(BF16)


def _attention(dqt, dk, dvt, fqt, fk, fvt, qaugt, kaug, lam_params, batch, lam_init):
    n_tok = dk.shape[0]
    seq = n_tok // batch
    tiles = seq // ATT_TILE
    assert tiles % 2 == 0
    half = tiles // 2
    qt_spec = lambda r, off: pl.BlockSpec((r, ATT_TILE), lambda b, s: (0, b * tiles + s + off))
    kv_spec = lambda w: pl.BlockSpec((seq, w), lambda b, s: (b, 0))
    vt_spec = pl.BlockSpec((tiles, GROUP_W, ATT_TILE), lambda b, s: (b, 0, 0))
    qkv_specs = [qt_spec(GROUP_W, 0), qt_spec(GROUP_W, half), kv_spec(GROUP_W), vt_spec]
    out_spec = pl.BlockSpec((1, 2, 1, ATT_TILE, GROUP_W), lambda b, s: (b, 0, s, 0, 0))
    out_shape = jax.ShapeDtypeStruct((batch, 2, half, ATT_TILE, GROUP_W), BF16)
    od, of = pl.pallas_call(
        functools.partial(_attn_body, lam_init=lam_init, half=half, static_steps=(half - 1,)),
        grid=(batch, half),
        in_specs=qkv_specs + qkv_specs
        + [qt_spec(LANES, 0), qt_spec(LANES, half), kv_spec(LANES)]
        + [_resident(p.shape) for p in lam_params],
        out_specs=[out_spec, out_spec],
        out_shape=[out_shape, out_shape],
        scratch_shapes=_attn_scratch(False) + _attn_scratch(True),
        compiler_params=_params(("arbitrary", "arbitrary")),
        name="attention",
    )(dqt, dqt, dk, dvt, fqt, fqt, fk, fvt, qaugt, qaugt, kaug, *lam_params)
    return od.reshape(n_tok, GROUP_W), of.reshape(n_tok, GROUP_W)


def _mem_kv_body(m_ref, nw_ref, w_ref, kw_ref, mk_ref, mv_ref):
    hm = _rms(m_ref[...], nw_ref[...]).astype(BF16)
    kv = _dot(hm, w_ref[...])
    kw = kw_ref[...]
    for c in range(0, D_MODEL, MEM_HEAD_DIM):
        mk_ref[:, c:c + MEM_HEAD_DIM] = _rms(kv[:, c:c + MEM_HEAD_DIM], kw).astype(BF16)
    mv_ref[...] = kv[:, D_MODEL:].astype(BF16)


def _mem_kv(mem2d, norm_w, w_mem_kv, mem_k_norm_w):
    n_mem = mem2d.shape[0]
    tok = pl.BlockSpec((TOK_TILE, D_MODEL), lambda i: (i, 0))
    out = jax.ShapeDtypeStruct((n_mem, D_MODEL), BF16)
    return pl.pallas_call(
        _mem_kv_body,
        grid=(n_mem // TOK_TILE,),
        in_specs=[tok, _resident((1, D_MODEL)), _resident((D_MODEL, 2 * D_MODEL)),
                  _resident((1, MEM_HEAD_DIM))],
        out_specs=[tok, tok],
        out_shape=[out, out],
        compiler_params=_params(("arbitrary",)),
        name="mem_kv",
    )(mem2d, norm_w.reshape(1, -1), w_mem_kv.astype(BF16), mem_k_norm_w.reshape(1, -1))


POST_TILE = 1024
FF_CHUNK = 1024


def _post_attn_body(x_ref, od_ref, of_ref, mk_ref, mv_ref, wo_ref, nq_ref, wq_ref, qw_ref,
                    wmo_ref, nm_ref, wu_ref, wd_ref, o_ref):
    x1 = x_ref[...] + (_dot(od_ref[...], wo_ref[:GROUP_W, :])
                       + _dot(of_ref[...], wo_ref[GROUP_W:, :]))

    mq = _dot(_rms(x1, nq_ref[...]).astype(BF16), wq_ref[...])
    qw = qw_ref[...]
    heads = []
    for c in range(0, D_MODEL, MEM_HEAD_DIM):
        q = _rms(mq[:, c:c + MEM_HEAD_DIM], qw).astype(BF16)
        s = _dot_nt(q, mk_ref[:, c:c + MEM_HEAD_DIM])
        p = jnp.exp(s - jnp.max(s, axis=-1, keepdims=True))
        l = jnp.sum(p, axis=-1, keepdims=True)
        heads.append((_dot(p.astype(BF16), mv_ref[:, c:c + MEM_HEAD_DIM]) / l).astype(BF16))
    x2 = x1 + _dot(jnp.concatenate(heads, axis=-1), wmo_ref[...])

    h = _rms(x2, nm_ref[...]).astype(BF16)
    acc = x2
    for c in range(0, D_FF, FF_CHUNK):
        u = jnp.maximum(_dot(h, wu_ref[:, c:c + FF_CHUNK]), 0.0)
        acc = acc + _dot((u * u).astype(BF16), wd_ref[c:c + FF_CHUNK, :])
    o_ref[...] = acc


def _post_attn(x2d, od, of, mk, mv, batch, w_out, norm_q, w_mem_q, mem_q_norm_w, w_mem_o,
               norm_mlp, w_up, w_down):
    n_tok = x2d.shape[0]
    tiles = n_tok // batch // POST_TILE
    mem_len = mk.shape[0] // batch
    tok = lambda width: pl.BlockSpec((POST_TILE, width), lambda b, i: (b * tiles + i, 0))
    mem = pl.BlockSpec((mem_len, D_MODEL), lambda b, i: (b, 0))
    scale = MEM_HEAD_DIM ** -0.5
    square = _resident((D_MODEL, D_MODEL))
    vec = _resident((1, D_MODEL))
    return pl.pallas_call(
        _post_attn_body,
        grid=(batch, tiles),
        in_specs=[tok(D_MODEL), tok(GROUP_W), tok(GROUP_W), mem, mem, square, vec, square,
                  _resident((1, MEM_HEAD_DIM)), square, vec, _resident((D_MODEL, D_FF)),
                  _resident((D_FF, D_MODEL))],
        out_specs=tok(D_MODEL),
        out_shape=jax.ShapeDtypeStruct((n_tok, D_MODEL), F32),
        compiler_params=_params(("arbitrary", "arbitrary")),
        name="post_attn",
    )(x2d, od, of, mk, mv, w_out.astype(BF16), norm_q.reshape(1, -1), w_mem_q.astype(BF16),
      (mem_q_norm_w * scale).reshape(1, -1), w_mem_o.astype(BF16), norm_mlp.reshape(1, -1),
      w_up.astype(BF16), w_down.astype(BF16))


def kernel(x, mem, positions, norm_mix_w, w_in, b_forget, diff_q_norm_w, diff_k_norm_w,
           lambda_q1, lambda_k1, lambda_q2, lambda_k2, diff_subln_w, fox_q_norm_w,
           fox_k_norm_w, w_out, norm_mem_q_w, norm_mem_kv_w, w_mem_q, w_mem_kv,
           mem_q_norm_w, mem_k_norm_w, w_mem_o, norm_mlp_w, w_up, w_down):
    batch, seq, d = x.shape
    depth = w_in.shape[0]
    assert d == D_MODEL and seq % TOK_TILE == 0 and seq % POST_TILE == 0
    assert TOK_TILE % ATT_TILE == 0
    assert (batch * mem.shape[1]) % TOK_TILE == 0

    cos_t, sin_t = _rope_tables(positions)
    xc = x.reshape(batch * seq, d)
    mem2d = mem.reshape(-1, d)
    for l in range(depth):
        lam_init = 0.8 - 0.6 * math.exp(-0.3 * l)
        dqt, dk, dvt, fqt, fk, fvt, qaugt, kaug = _in_proj(
            xc, batch, norm_mix_w[l], w_in[l], cos_t, sin_t, diff_q_norm_w[l],
            diff_k_norm_w[l], fox_q_norm_w[l], fox_k_norm_w[l], b_forget[l])
        lam_params = [p[l].reshape(1, -1) for p in (lambda_q1, lambda_k1, lambda_q2, lambda_k2)]
        lam_params.append(diff_subln_w[l].reshape(-1, 1))
        od, of = _attention(dqt, dk, dvt, fqt, fk, fvt, qaugt, kaug, lam_params, batch, lam_init)
        mk, mv = _mem_kv(mem2d, norm_mem_kv_w[l], w_mem_kv[l], mem_k_norm_w[l])
        xc = _post_attn(xc, od, of, mk, mv, batch, w_out[l], norm_mem_q_w[l], w_mem_q[l],
                        mem_q_norm_w[l], w_mem_o[l], norm_mlp_w[l], w_up[l], w_down[l])
    return xc.reshape(batch, seq, d)
```

```python
import functools
import math
from typing import Any, NamedTuple

import jax
import jax.numpy as jnp
from jax import lax
from jax.experimental import pallas as pl
from jax.experimental.pallas import tpu as pltpu

F32 = jnp.float32
BF16 = jnp.bfloat16

D_MODEL = 1024
CHUNK = 64
HEAD_DIM = 64
DIFF_V_DIM = 2 * HEAD_DIM
N_FOX_HEADS = 8
GROUP_W = 512
ROPE_DIM = HEAD_DIM // 4
ROPE_THETA = 500000.0
N_MEM_HEADS = 4
MEM_HEAD_DIM = D_MODEL // N_MEM_HEADS
D_FF = 4 * D_MODEL
EPS = 1e-6
NEG_INF = -1e30
LOG2E = math.log2(math.e)

LANES = 128
MXU_DIM = 256
VMEM_LIMIT = 58 * 1024 * 1024

TOK_TILE = 1024
ATT_TILE = 256


def _params(sem):
    return pltpu.CompilerParams(dimension_semantics=sem, vmem_limit_bytes=VMEM_LIMIT)


def _resident(shape):
    return pl.BlockSpec(shape, lambda *_: (0,) * len(shape), pipeline_mode=pl.Buffered(1))


def _rms(x, w):
    ms = jnp.mean(x * x, axis=-1, keepdims=True)
    return (x * lax.rsqrt(ms + EPS)) * w


def _dot(a, b):
    return jnp.dot(a, b, preferred_element_type=F32)


def _dot_nt(a, b):
    return lax.dot_general(a, b, (((1,), (1,)), ((), ())), preferred_element_type=F32)


def _rope_body(pos_ref, freq_ref, cos_ref, sin_ref):
    ang = pos_ref[...] * freq_ref[...]
    cos_ref[...] = jnp.cos(ang)
    sin_ref[...] = jnp.sin(ang)


def _rope_tables(positions):
    n_tok = positions.size
    half = ROPE_DIM // 2
    inv_freq = ROPE_THETA ** (-jnp.arange(0, ROPE_DIM, 2, dtype=F32) / ROPE_DIM)
    out = jax.ShapeDtypeStruct((half, n_tok), F32)
    return pl.pallas_call(_rope_body, out_shape=(out, out), name="rope_tables")(
        positions.reshape(1, n_tok).astype(F32), inv_freq.reshape(half, 1))


def _log_sigmoid(x):
    return jnp.minimum(x, 0.0) - jnp.log1p(jnp.exp(-jnp.abs(x)))


def _split3(x):
    hi = x.astype(BF16)
    rem = x - hi.astype(F32)
    mid = rem.astype(BF16)
    lo = (rem - mid.astype(F32)).astype(BF16)
    return hi, mid, lo


BIAS_SLOT = LANES // N_FOX_HEADS


def _head_norm_t(t, gain, rope):
    half = ROPE_DIM // 2
    outs = []
    for r0 in range(0, t.shape[0], HEAD_DIM):
        th = t[r0:r0 + HEAD_DIM]
        ms = jnp.sum(th * th, axis=0, keepdims=True) * (1.0 / HEAD_DIM)
        th = (th * lax.rsqrt(ms + EPS)) * gain[r0:r0 + HEAD_DIM]
        if rope is not None:
            cos, sin = rope
            t1, t2 = th[:half], th[half:ROPE_DIM]
            th = jnp.concatenate(
                [t1 * cos - t2 * sin, t2 * cos + t1 * sin, th[ROPE_DIM:]], axis=0)
        outs.append(th)
    return jnp.concatenate(outs, axis=0)


def _in_proj_body(x_ref, nw_ref, wt_ref, wft_ref, cos_ref, sin_ref, gain_ref, bf_ref, dqt_ref,
                  dk_ref, dvt_ref, fqt_ref, fk_ref, fvt_ref, qaugt_ref, kaug_ref, carry_ref):
    @pl.when(pl.program_id(1) == 0)
    def _():
        carry_ref[...] = jnp.zeros_like(carry_ref)

    h = _rms(x_ref[...], nw_ref[...]).astype(BF16)
    rope = (cos_ref[...], sin_ref[...])
    g = GROUP_W

    def proj_t(group):
        r0 = group * GROUP_W
        return _dot_nt(wt_ref[r0:r0 + GROUP_W, :], h)

    def store_keys(k_ref, kt):
        for c in range(0, g, LANES):
            k_ref[:, c:c + LANES] = kt[c:c + LANES].T.astype(BF16)

    def store_values(vt_ref, vt):
        for t in range(TOK_TILE // ATT_TILE):
            vt_ref[t] = vt[:, t * ATT_TILE:(t + 1) * ATT_TILE].astype(BF16)

    dqt_ref[...] = _head_norm_t(proj_t(0), gain_ref[0:g], rope).astype(BF16)
    store_keys(dk_ref, _head_norm_t(proj_t(1), gain_ref[g:2 * g], rope))
    store_values(dvt_ref, proj_t(2))
    fqt_ref[...] = _head_norm_t(proj_t(3), gain_ref[2 * g:3 * g], None).astype(BF16)
    store_keys(fk_ref, _head_norm_t(proj_t(4), gain_ref[3 * g:4 * g], None))
    store_values(fvt_ref, proj_t(5))

    log_f = _log_sigmoid(_dot_nt(wft_ref[...], h) + bf_ref[...])
    r = lax.broadcasted_iota(jnp.int32, (MXU_DIM, MXU_DIM), 0)
    c = lax.broadcasted_iota(jnp.int32, (MXU_DIM, MXU_DIM), 1)
    triu = jnp.where(r <= c, 1.0, 0.0).astype(BF16)
    part = lax.broadcasted_iota(jnp.int32, (LANES, MXU_DIM), 0) % BIAS_SLOT
    carry = carry_ref[:, 0:1]
    for c0 in range(0, TOK_TILE, MXU_DIM):
        hi, mid, lo = _split3(log_f[:, c0:c0 + MXU_DIM])
        cum = (_dot(hi, triu) + _dot(mid, triu)) + _dot(lo, triu) + carry
        carry = cum[:, MXU_DIM - 1:MXU_DIM]
        hi, mid, lo = (t.astype(F32) for t in _split3(cum * LOG2E))
        kaug_t = jnp.where(part == 0, -hi, jnp.where(part == 1, -mid, jnp.where(
            part == 2, -lo, jnp.where(part < 6, 1.0, 0.0))))
        qaug_t = jnp.where(part < 3, 1.0, jnp.where(part == 3, hi, jnp.where(
            part == 4, mid, jnp.where(part == 5, lo, 0.0))))
        kaug_ref[c0:c0 + MXU_DIM, :] = kaug_t.T.astype(BF16)
        qaugt_ref[:, c0:c0 + MXU_DIM] = qaug_t.astype(BF16)
    carry_ref[...] = jnp.broadcast_to(carry, carry_ref.shape)


def _in_proj(x2d, batch, norm_w, w_in, cos_t, sin_t, dqw, dkw, fqw, fkw, b_forget):
    n_tok = x2d.shape[0]
    tiles = n_tok // batch // TOK_TILE
    g = GROUP_W
    wt = w_in[:, :6 * g].T.astype(BF16)
    wft = jnp.repeat(w_in[:, 6 * g:6 * g + N_FOX_HEADS].T, BIAS_SLOT, axis=0).astype(BF16)
    bf = jnp.repeat(b_forget, BIAS_SLOT).reshape(LANES, 1)
    qscale = HEAD_DIM ** -0.5 * LOG2E
    gains = jnp.concatenate([jnp.tile(v, g // HEAD_DIM) for v in
                             (dqw * qscale, dkw, fqw * qscale, fkw)]).reshape(4 * g, 1)

    row = lambda b, i: (b * tiles + i, 0)
    col = lambda b, i: (0, b * tiles + i)
    tok = lambda width: pl.BlockSpec((TOK_TILE, width), row)
    tok_t = lambda rows: pl.BlockSpec((rows, TOK_TILE), col)
    slabs = TOK_TILE // ATT_TILE
    vt_spec = pl.BlockSpec((slabs, g, ATT_TILE), lambda b, i: (b * tiles + i, 0, 0))
    k_shape = jax.ShapeDtypeStruct((n_tok, g), BF16)
    qt_shape = jax.ShapeDtypeStruct((g, n_tok), BF16)
    vt_shape = jax.ShapeDtypeStruct((n_tok // ATT_TILE, g, ATT_TILE), BF16)
    half = ROPE_DIM // 2
    return pl.pallas_call(
        _in_proj_body,
        grid=(batch, tiles),
        in_specs=[tok(D_MODEL), _resident((1, D_MODEL)), _resident(wt.shape),
                  _resident(wft.shape), tok_t(half), tok_t(half), _resident(gains.shape),
                  _resident(bf.shape)],
        out_specs=[tok_t(g), tok(g), vt_spec, tok_t(g), tok(g), vt_spec, tok_t(LANES),
                   tok(LANES)],
        out_shape=[qt_shape, k_shape, vt_shape, qt_shape, k_shape, vt_shape,
                   jax.ShapeDtypeStruct((LANES, n_tok), BF16),
                   jax.ShapeDtypeStruct((n_tok, LANES), BF16)],
        scratch_shapes=[pltpu.VMEM((LANES, LANES), F32)],
        compiler_params=_params(("arbitrary", "arbitrary")),
        name="in_proj",
    )(x2d, norm_w.reshape(1, -1), wt, wft, cos_t, sin_t, gains, bf)


N_CHAINS = 2 * (GROUP_W // LANES)
SUM_ROWS = 16


class _Group(NamedTuple):
    fox: bool
    qt: tuple
    k: Any
    vt: Any
    qaug: Any
    kaug: Any
    out: Any
    rhs: Any
    s: tuple
    bmax: Any
    m: Any
    acc: Any

    @property
    def width(self):
        return HEAD_DIM if self.fox else DIFF_V_DIM


def _attn_scratch(fox):
    width = HEAD_DIM if fox else DIFF_V_DIM
    rhs_w = 2 * LANES if fox else LANES
    return [pltpu.VMEM((2, N_CHAINS, rhs_w, ATT_TILE), BF16),
            pltpu.VMEM((N_CHAINS, ATT_TILE, ATT_TILE), F32),
            pltpu.VMEM((N_CHAINS, ATT_TILE, ATT_TILE), F32),
            pltpu.VMEM((2, N_CHAINS, ATT_TILE), F32),
            pltpu.VMEM((2, N_CHAINS, ATT_TILE), F32),
            pltpu.VMEM((2, N_CHAINS, width + SUM_ROWS, ATT_TILE), F32)]


def _attn_body(*refs, lam_init, half, static_steps):
    s = pl.program_id(1)
    for k in static_steps:
        @pl.when(s == k)
        def _(k=k):
            _attn_step(k, *refs, lam_init=lam_init, half=half)

    generic = [k for k in range(half) if k not in static_steps]
    if generic:
        @pl.when(s <= max(generic))
        def _():
            _attn_step(s, *refs, lam_init=lam_init, half=half)


def _attn_step(s, dqa, dqb, dk, dvt, fqa, fqb, fk, fvt, qauga, qaugb, kaug, lq1_ref, lk1_ref,
               lq2_ref, lk2_ref, subw_ref, od_ref, of_ref, *scratch, lam_init, half):
    def group(fox, qt, k, vt, qaug, kaug_ref, out, scr):
        rhs, s0, s1, bmax, m, acc = scr
        return _Group(fox, qt, k, vt, qaug, kaug_ref, out, rhs, (s0, s1), bmax, m, acc)

    groups = (group(False, (dqa, dqb), dk, dvt, None, None, od_ref, scratch[:6]),
              group(True, (fqa, fqb), fk, fvt, (qauga, qaugb), kaug, of_ref, scratch[6:]))
    static = isinstance(s, int)
    last = 2 * s + half + 1
    ones_rows = jnp.ones((SUM_ROWS, ATT_TILE), BF16)

    half_zero = jnp.zeros((HEAD_DIM, ATT_TILE), BF16)
    for gr in groups:
        for tile, qt_ref in enumerate(gr.qt):
            for g in range(GROUP_W // LANES):
                r0 = g * LANES
                gr.rhs[tile, 2 * g, :HEAD_DIM, :] = qt_ref[r0:r0 + HEAD_DIM, :]
                gr.rhs[tile, 2 * g, HEAD_DIM:LANES, :] = half_zero
                gr.rhs[tile, 2 * g + 1, :HEAD_DIM, :] = half_zero
                gr.rhs[tile, 2 * g + 1, HEAD_DIM:LANES, :] = qt_ref[r0 + HEAD_DIM:r0 + LANES, :]
            if gr.fox:
                for c in range(N_CHAINS):
                    b0 = c * BIAS_SLOT
                    gr.rhs[tile, c, LANES:, :] = jnp.zeros((LANES, ATT_TILE), BF16)
                    gr.rhs[tile, c, LANES + b0:LANES + b0 + BIAS_SLOT, :] = (
                        gr.qaug[tile][b0:b0 + BIAS_SLOT, :])
        gr.m[...] = jnp.full(gr.m.shape, NEG_INF, F32)
        gr.acc[...] = jnp.zeros(gr.acc.shape, F32)

    def locate(p):
        if static:
            return (1, p - s - 1) if p > s else (0, s - p)
        in_b = p > s
        return in_b.astype(jnp.int32), jnp.where(in_b, p - s - 1, s - p)

    def diagonal_mask(gr):
        div = 1 if gr.fox else CHUNK
        k_id = lax.broadcasted_iota(jnp.int32, (ATT_TILE, ATT_TILE), 0) // div
        q_id = lax.broadcasted_iota(jnp.int32, (ATT_TILE, ATT_TILE), 1) // div
        return k_id <= q_id

    def scores(gr, p, slot, g, masked):
        tile, blk = locate(p)
        start = blk * ATT_TILE
        rows = pl.ds(start if static else pl.multiple_of(start, ATT_TILE), ATT_TILE)
        lhs = gr.k[rows, g * LANES:(g + 1) * LANES]
        if gr.fox:
            lhs = jnp.concatenate([lhs, gr.kaug[rows, :]], axis=-1)
        for c in (2 * g, 2 * g + 1):
            sc = _dot(lhs, gr.rhs[tile, c])
            if masked:
                sc = jnp.where(diagonal_mask(gr), sc, NEG_INF)
            gr.s[slot][c] = sc
            gr.bmax[slot, c:c + 1, :] = jnp.max(sc, axis=0, keepdims=True)

    def softmax_pv(gr, p, slot, c):
        tile, blk = locate(p)
        sc = gr.s[slot][c]
        m_old = gr.m[tile, c:c + 1, :]
        m_new = jnp.maximum(m_old, gr.bmax[slot, c:c + 1, :])
        alpha = jnp.exp2(m_old - m_new)
        prob = jnp.exp2(sc - m_new)
        gr.m[tile, c:c + 1, :] = m_new
        r0 = c * HEAD_DIM if gr.fox else (c // 2) * LANES
        vt = jnp.concatenate([gr.vt[blk, r0:r0 + gr.width, :], ones_rows], axis=0)
        gr.acc[tile, c] = alpha * gr.acc[tile, c] + _dot(vt, prob.astype(BF16))

    def step(score_args, soft_args, masked=False):
        for g in range(GROUP_W // LANES):
            if soft_args is not None:
                for gr in groups:
                    softmax_pv(gr, *soft_args, 2 * g)
            if score_args is not None:
                for gr in groups:
                    scores(gr, *score_args, g, masked)
            if soft_args is not None:
                for gr in groups:
                    softmax_pv(gr, *soft_args, 2 * g + 1)

    def pair(u, carry):
        p = 2 * u + 1
        step((p + 1, 0), (p, 1))
        step((p + 2, 1), (p + 1, 0))
        return carry

    step((0, 0), None, masked=True)
    step((1, 1), (0, 0))
    if static:
        for u in range((last - 3) // 2):
            pair(u, 0)
    else:
        lax.fori_loop(0, (last - 3) // 2, pair, 0)
    step((last - 1, 0), (last - 2, 1))
    step((last, 1), (last - 1, 0), masked=True)
    step(None, (last, 1))

    lam = (jnp.exp(jnp.sum(lq1_ref[...] * lk1_ref[...], axis=-1, keepdims=True))
           - jnp.exp(jnp.sum(lq2_ref[...] * lk2_ref[...], axis=-1, keepdims=True))
           + lam_init)
    for gr in groups:
        width = gr.width
        for tile in range(2):
            for g in range(GROUP_W // LANES):
                a1, a2 = gr.acc[tile, 2 * g], gr.acc[tile, 2 * g + 1]
                o1 = a1[:width] * (1.0 / a1[width:width + 1])
                o2 = a2[:width] * (1.0 / a2[width:width + 1])
                if gr.fox:
                    o_t = jnp.concatenate([o1, o2], axis=0)
                else:
                    o_t = o1 - lam * o2
                    ms = jnp.mean(o_t * o_t, axis=0, keepdims=True)
                    o_t = ((o_t * lax.rsqrt(ms + EPS)) * subw_ref[...]) * (1.0 - lam_init)
                gr.out[0, tile, 0, :, g * LANES:(g + 1) * LANES] = o_t.T.astype(BF16)


def _attention(dqt, dk, dvt, fqt, fk, fvt, qaugt, kaug, lam_params, batch, lam_init):
    n_tok = dk.shape[0]
    seq = n_tok // batch
    tiles = seq // ATT_TILE
    assert tiles % 2 == 0
    half = tiles // 2
    qt_spec = lambda r, off: pl.BlockSpec((r, ATT_TILE), lambda b, s: (0, b * tiles + s + off))
    kv_spec = lambda w: pl.BlockSpec((seq, w), lambda b, s: (b, 0))
    vt_spec = pl.BlockSpec((tiles, GROUP_W, ATT_TILE), lambda b, s: (b, 0, 0))
    qkv_specs = [qt_spec(GROUP_W, 0), qt_spec(GROUP_W, half), kv_spec(GROUP_W), vt_spec]
    out_spec = pl.BlockSpec((1, 2, 1, ATT_TILE, GROUP_W), lambda b, s: (b, 0, s, 0, 0))
    out_shape = jax.ShapeDtypeStruct((batch, 2, half, ATT_TILE, GROUP_W), BF16)
    od, of = pl.pallas_call(
        functools.partial(_attn_body, lam_init=lam_init, half=half, static_steps=(half - 1,)),
        grid=(batch, half),
        in_specs=qkv_specs + qkv_specs
        + [qt_spec(LANES, 0), qt_spec(LANES, half), kv_spec(LANES)]
        + [_resident(p.shape) for p in lam_params],
        out_specs=[out_spec, out_spec],
        out_shape=[out_shape, out_shape],
        scratch_shapes=_attn_scratch(False) + _attn_scratch(True),
        compiler_params=_params(("arbitrary", "arbitrary")),
        name="attention",
    )(dqt, dqt, dk, dvt, fqt, fqt, fk, fvt, qaugt, qaugt, kaug, *lam_params)
    return od.reshape(n_tok, GROUP_W), of.reshape(n_tok, GROUP_W)


def _mem_kv_body(m_ref, nw_ref, w_ref, kw_ref, mk_ref, mv_ref):
    hm = _rms(m_ref[...], nw_ref[...]).astype(BF16)
    kv = _dot(hm, w_ref[...])
    kw = kw_ref[...]
    for c in range(0, D_MODEL, MEM_HEAD_DIM):
        mk_ref[:, c:c + MEM_HEAD_DIM] = _rms(kv[:, c:c + MEM_HEAD_DIM], kw).astype(BF16)
    mv_ref[...] = kv[:, D_MODEL:].astype(BF16)


def _mem_kv(mem2d, norm_w, w_mem_kv, mem_k_norm_w):
    n_mem = mem2d.shape[0]
    tok = pl.BlockSpec((TOK_TILE, D_MODEL), lambda i: (i, 0))
    out = jax.ShapeDtypeStruct((n_mem, D_MODEL), BF16)
    return pl.pallas_call(
        _mem_kv_body,
        grid=(n_mem // TOK_TILE,),
        in_specs=[tok, _resident((1, D_MODEL)), _resident((D_MODEL, 2 * D_MODEL)),
                  _resident((1, MEM_HEAD_DIM))],
        out_specs=[tok, tok],
        out_shape=[out, out],
        compiler_params=_params(("arbitrary",)),
        name="mem_kv",
    )(mem2d, norm_w.reshape(1, -1), w_mem_kv.astype(BF16), mem_k_norm_w.reshape(1, -1))


POST_TILE = 1024
FF_CHUNK = 1024


def _post_attn_body(x_ref, od_ref, of_ref, mk_ref, mv_ref, wo_ref, nq_ref, wq_ref, qw_ref,
                    wmo_ref, nm_ref, wu_ref, wd_ref, o_ref):
    x1 = x_ref[...] + (_dot(od_ref[...], wo_ref[:GROUP_W, :])
                       + _dot(of_ref[...], wo_ref[GROUP_W:, :]))

    mq = _dot(_rms(x1, nq_ref[...]).astype(BF16), wq_ref[...])
    qw = qw_ref[...]
    heads = []
    for c in range(0, D_MODEL, MEM_HEAD_DIM):
        q = _rms(mq[:, c:c + MEM_HEAD_DIM], qw).astype(BF16)
        s = _dot_nt(q, mk_ref[:, c:c + MEM_HEAD_DIM])
        p = jnp.exp(s - jnp.max(s, axis=-1, keepdims=True))
        l = jnp.sum(p, axis=-1, keepdims=True)
        heads.append((_dot(p.astype(BF16), mv_ref[:, c:c + MEM_HEAD_DIM]) / l).astype(BF16))
    x2 = x1 + _dot(jnp.concatenate(heads, axis=-1), wmo_ref[...])

    h = _rms(x2, nm_ref[...]).astype(BF16)
    acc = x2
    for c in range(0, D_FF, FF_CHUNK):
        u = jnp.maximum(_dot(h, wu_ref[:, c:c + FF_CHUNK]), 0.0)
        acc = acc + _dot((u * u).astype(BF16), wd_ref[c:c + FF_CHUNK, :])
    o_ref[...] = acc


def _post_attn(x2d, od, of, mk, mv, batch, w_out, norm_q, w_mem_q, mem_q_norm_w, w_mem_o,
               norm_mlp, w_up, w_down):
    n_tok = x2d.shape[0]
    tiles = n_tok // batch // POST_TILE
    mem_len = mk.shape[0] // batch
    tok = lambda width: pl.BlockSpec((POST_TILE, width), lambda b, i: (b * tiles + i, 0))
    mem = pl.BlockSpec((mem_len, D_MODEL), lambda b, i: (b, 0))
    scale = MEM_HEAD_DIM ** -0.5
    square = _resident((D_MODEL, D_MODEL))
    vec = _resident((1, D_MODEL))
    return pl.pallas_call(
        _post_attn_body,
        grid=(batch, tiles),
        in_specs=[tok(D_MODEL), tok(GROUP_W), tok(GROUP_W), mem, mem, square, vec, square,
                  _resident((1, MEM_HEAD_DIM)), square, vec, _resident((D_MODEL, D_FF)),
                  _resident((D_FF, D_MODEL))],
        out_specs=tok(D_MODEL),
        out_shape=jax.ShapeDtypeStruct((n_tok, D_MODEL), F32),
        compiler_params=_params(("arbitrary", "arbitrary")),
        name="post_attn",
    )(x2d, od, of, mk, mv, w_out.astype(BF16), norm_q.reshape(1, -1), w_mem_q.astype(BF16),
      (mem_q_norm_w * scale).reshape(1, -1), w_mem_o.astype(BF16), norm_mlp.reshape(1, -1),
      w_up.astype(BF16), w_down.astype(BF16))


def kernel(x, mem, positions, norm_mix_w, w_in, b_forget, diff_q_norm_w, diff_k_norm_w,
           lambda_q1, lambda_k1, lambda_q2, lambda_k2, diff_subln_w, fox_q_norm_w,
           fox_k_norm_w, w_out, norm_mem_q_w, norm_mem_kv_w, w_mem_q, w_mem_kv,
           mem_q_norm_w, mem_k_norm_w, w_mem_o, norm_mlp_w, w_up, w_down):
    batch, seq, d = x.shape
    depth = w_in.shape[0]
    assert d == D_MODEL and seq % TOK_TILE == 0 and seq % POST_TILE == 0
    assert TOK_TILE % ATT_TILE == 0
    assert (batch * mem.shape[1]) % TOK_TILE == 0

    cos_t, sin_t = _rope_tables(positions)
    xc = x.reshape(batch * seq, d)
    mem2d = mem.reshape(-1, d)
    for l in range(depth):
        lam_init = 0.8 - 0.6 * math.exp(-0.3 * l)
        dqt, dk, dvt, fqt, fk, fvt, qaugt, kaug = _in_proj(
            xc, batch, norm_mix_w[l], w_in[l], cos_t, sin_t, diff_q_norm_w[l],
            diff_k_norm_w[l], fox_q_norm_w[l], fox_k_norm_w[l], b_forget[l])
        lam_params = [p[l].reshape(1, -1) for p in (lambda_q1, lambda_k1, lambda_q2, lambda_k2)]
        lam_params.append(diff_subln_w[l].reshape(-1, 1))
        od, of = _attention(dqt, dk, dvt, fqt, fk, fvt, qaugt, kaug, lam_params, batch, lam_init)
        mk, mv = _mem_kv(mem2d, norm_mem_kv_w[l], w_mem_kv[l], mem_k_norm_w[l])
        xc = _post_attn(xc, od, of, mk, mv, batch, w_out[l], norm_mem_q_w[l], w_mem_q[l],
                        mem_q_norm_w[l], w_mem_o[l], norm_mlp_w[l], w_up[l], w_down[l])
    return xc.reshape(batch, seq, d)
```

```python
import functools
import math
from typing import Any, NamedTuple

import jax
import jax.numpy as jnp
from jax import lax
from jax.experimental import pallas as pl
from jax.experimental.pallas import tpu as pltpu

F32 = jnp.float32
BF16 = jnp.bfloat16

D_MODEL = 1024
CHUNK = 64
HEAD_DIM = 64
DIFF_V_DIM = 2 * HEAD_DIM
N_FOX_HEADS = 8
GROUP_W = 512
ROPE_DIM = HEAD_DIM // 4
ROPE_THETA = 500000.0
N_MEM_HEADS = 4
MEM_HEAD_DIM = D_MODEL // N_MEM_HEADS
D_FF = 4 * D_MODEL
EPS = 1e-6
NEG_INF = -1e30
LOG2E = math.log2(math.e)

LANES = 128
MXU_DIM = 256
VMEM_LIMIT = 58 * 1024 * 1024

TOK_TILE = 1024
ATT_TILE = 256


def _params(sem):
    return pltpu.CompilerParams(dimension_semantics=sem, vmem_limit_bytes=VMEM_LIMIT)


def _resident(shape):
    return pl.BlockSpec(shape, lambda *_: (0,) * len(shape), pipeline_mode=pl.Buffered(1))


def _rms(x, w):
    ms = jnp.mean(x * x, axis=-1, keepdims=True)
    return (x * lax.rsqrt(ms + EPS)) * w


def _dot(a, b):
    return jnp.dot(a, b, preferred_element_type=F32)


def _dot_nt(a, b):
    return lax.dot_general(a, b, (((1,), (1,)), ((), ())), preferred_element_type=F32)


def _rope_body(pos_ref, freq_ref, cos_ref, sin_ref):
    ang = pos_ref[...] * freq_ref[...]
    cos_ref[...] = jnp.cos(ang)
    sin_ref[...] = jnp.sin(ang)


def _rope_tables(positions):
    n_tok = positions.size
    half = ROPE_DIM // 2
    inv_freq = ROPE_THETA ** (-jnp.arange(0, ROPE_DIM, 2, dtype=F32) / ROPE_DIM)
    out = jax.ShapeDtypeStruct((half, n_tok), F32)
    return pl.pallas_call(_rope_body, out_shape=(out, out), name="rope_tables")(
        positions.reshape(1, n_tok).astype(F32), inv_freq.reshape(half, 1))


def _log_sigmoid(x):
    return jnp.minimum(x, 0.0) - jnp.log1p(jnp.exp(-jnp.abs(x)))


def _split3(x):
    hi = x.astype(BF16)
    rem = x - hi.astype(F32)
    mid = rem.astype(BF16)
    lo = (rem - mid.astype(F32)).astype(BF16)
    return hi, mid, lo


BIAS_SLOT = LANES // N_FOX_HEADS


def _head_norm_t(t, gain, rope):
    half = ROPE_DIM // 2
    outs = []
    for r0 in range(0, t.shape[0], HEAD_DIM):
        th = t[r0:r0 + HEAD_DIM]
        ms = jnp.sum(th * th, axis=0, keepdims=True) * (1.0 / HEAD_DIM)
        th = (th * lax.rsqrt(ms + EPS)) * gain[r0:r0 + HEAD_DIM]
        if rope is not None:
            cos, sin = rope
            t1, t2 = th[:half], th[half:ROPE_DIM]
            th = jnp.concatenate(
                [t1 * cos - t2 * sin, t2 * cos + t1 * sin, th[ROPE_DIM:]], axis=0)
        outs.append(th)
    return jnp.concatenate(outs, axis=0)


def _in_proj_body(x_ref, nw_ref, wt_ref, wft_ref, cos_ref, sin_ref, gain_ref, bf_ref, dqt_ref,
                  dk_ref, dvt_ref, fqt_ref, fk_ref, fvt_ref, qaugt_ref, kaug_ref, carry_ref):
    @pl.when(pl.program_id(1) == 0)
    def _():
        carry_ref[...] = jnp.zeros_like(carry_ref)

    h = _rms(x_ref[...], nw_ref[...]).astype(BF16)
    rope = (cos_ref[...], sin_ref[...])
    g = GROUP_W

    def proj_t(group):
        r0 = group * GROUP_W
        return _dot_nt(wt_ref[r0:r0 + GROUP_W, :], h)

    def store_keys(k_ref, kt):
        for c in range(0, g, LANES):
            k_ref[:, c:c + LANES] = kt[c:c + LANES].T.astype(BF16)

    def store_values(vt_ref, vt):
        for t in range(TOK_TILE // ATT_TILE):
            vt_ref[t] = vt[:, t * ATT_TILE:(t + 1) * ATT_TILE].astype(BF16)

    dqt_ref[...] = _head_norm_t(proj_t(0), gain_ref[0:g], rope).astype(BF16)
    store_keys(dk_ref, _head_norm_t(proj_t(1), gain_ref[g:2 * g], rope))
    store_values(dvt_ref, proj_t(2))
    fqt_ref[...] = _head_norm_t(proj_t(3), gain_ref[2 * g:3 * g], None).astype(BF16)
    store_keys(fk_ref, _head_norm_t(proj_t(4), gain_ref[3 * g:4 * g], None))
    store_values(fvt_ref, proj_t(5))

    log_f = _log_sigmoid(_dot_nt(wft_ref[...], h) + bf_ref[...])
    r = lax.broadcasted_iota(jnp.int32, (MXU_DIM, MXU_DIM), 0)
    c = lax.broadcasted_iota(jnp.int32, (MXU_DIM, MXU_DIM), 1)
    triu = jnp.where(r <= c, 1.0, 0.0).astype(BF16)
    part = lax.broadcasted_iota(jnp.int32, (LANES, MXU_DIM), 0) % BIAS_SLOT
    carry = carry_ref[:, 0:1]
    for c0 in range(0, TOK_TILE, MXU_DIM):
        hi, mid, lo = _split3(log_f[:, c0:c0 + MXU_DIM])
        cum = (_dot(hi, triu) + _dot(mid, triu)) + _dot(lo, triu) + carry
        carry = cum[:, MXU_DIM - 1:MXU_DIM]
        hi, mid, lo = (t.astype(F32) for t in _split3(cum * LOG2E))
        kaug_t = jnp.where(part == 0, -hi, jnp.where(part == 1, -mid, jnp.where(
            part == 2, -lo, jnp.where(part < 6, 1.0, 0.0))))
        qaug_t = jnp.where(part < 3, 1.0, jnp.where(part == 3, hi, jnp.where(
            part == 4, mid, jnp.where(part == 5, lo, 0.0))))
        kaug_ref[c0:c0 + MXU_DIM, :] = kaug_t.T.astype(BF16)
        qaugt_ref[:, c0:c0 + MXU_DIM] = qaug_t.astype(BF16)
    carry_ref[...] = jnp.broadcast_to(carry, carry_ref.shape)


def _in_proj(x2d, batch, norm_w, w_in, cos_t, sin_t, dqw, dkw, fqw, fkw, b_forget):
    n_tok = x2d.shape[0]
    tiles = n_tok // batch // TOK_TILE
    g = GROUP_W
    wt = w_in[:, :6 * g].T.astype(BF16)
    wft = jnp.repeat(w_in[:, 6 * g:6 * g + N_FOX_HEADS].T, BIAS_SLOT, axis=0).astype(BF16)
    bf = jnp.repeat(b_forget, BIAS_SLOT).reshape(LANES, 1)
    qscale = HEAD_DIM ** -0.5 * LOG2E
    gains = jnp.concatenate([jnp.tile(v, g // HEAD_DIM) for v in
                             (dqw * qscale, dkw, fqw * qscale, fkw)]).reshape(4 * g, 1)

    row = lambda b, i: (b * tiles + i, 0)
    col = lambda b, i: (0, b * tiles + i)
    tok = lambda width: pl.BlockSpec((TOK_TILE, width), row)
    tok_t = lambda rows: pl.BlockSpec((rows, TOK_TILE), col)
    slabs = TOK_TILE // ATT_TILE
    vt_spec = pl.BlockSpec((slabs, g, ATT_TILE), lambda b, i: (b * tiles + i, 0, 0))
    k_shape = jax.ShapeDtypeStruct((n_tok, g), BF16)
    qt_shape = jax.ShapeDtypeStruct((g, n_tok), BF16)
    vt_shape = jax.ShapeDtypeStruct((n_tok // ATT_TILE, g, ATT_TILE), BF16)
    half = ROPE_DIM // 2
    return pl.pallas_call(
        _in_proj_body,
        grid=(batch, tiles),
        in_specs=[tok(D_MODEL), _resident((1, D_MODEL)), _resident(wt.shape),
                  _resident(wft.shape), tok_t(half), tok_t(half), _resident(gains.shape),
                  _resident(bf.shape)],
        out_specs=[tok_t(g), tok(g), vt_spec, tok_t(g), tok(g), vt_spec, tok_t(LANES),
                   tok(LANES)],
        out_shape=[qt_shape, k_shape, vt_shape, qt_shape, k_shape, vt_shape,
                   jax.ShapeDtypeStruct((LANES, n_tok), BF16),
                   jax.ShapeDtypeStruct((n_tok, LANES), BF16)],
        scratch_shapes=[pltpu.VMEM((LANES, LANES), F32)],
        compiler_params=_params(("arbitrary", "arbitrary")),
        name="in_proj",
    )(x2d, norm_w.reshape(1, -1), wt, wft, cos_t, sin_t, gains, bf)


N_CHAINS = 2 * (GROUP_W // LANES)
SUM_ROWS = 16


class _Group(NamedTuple):
    fox: bool
    qt: tuple
    k: Any
    vt: Any
    qaug: Any
    kaug: Any
    out: Any
    rhs: Any
    s: tuple
    bmax: Any
    m: Any
    acc: Any

    @property
    def width(self):
        return HEAD_DIM if self.fox else DIFF_V_DIM


def _attn_scratch(fox):
    width = HEAD_DIM if fox else DIFF_V_DIM
    rhs_w = 2 * LANES if fox else LANES
    return [pltpu.VMEM((2, N_CHAINS, rhs_w, ATT_TILE), BF16),
            pltpu.VMEM((N_CHAINS, ATT_TILE, ATT_TILE), F32),
            pltpu.VMEM((N_CHAINS, ATT_TILE, ATT_TILE), F32),
            pltpu.VMEM((2, N_CHAINS, ATT_TILE), F32),
            pltpu.VMEM((2, N_CHAINS, ATT_TILE), F32),
            pltpu.VMEM((2, N_CHAINS, width + SUM_ROWS, ATT_TILE), F32)]


def _attn_body(*refs, lam_init, half, static_steps):
    s = pl.program_id(1)
    for k in static_steps:
        @pl.when(s == k)
        def _(k=k):
            _attn_step(k, *refs, lam_init=lam_init, half=half)

    generic = [k for k in range(half) if k not in static_steps]
    if generic:
        @pl.when(s <= max(generic))
        def _():
            _attn_step(s, *refs, lam_init=lam_init, half=half)


def _attn_step(s, dqa, dqb, dk, dvt, fqa, fqb, fk, fvt, qauga, qaugb, kaug, lq1_ref, lk1_ref,
               lq2_ref, lk2_ref, subw_ref, od_ref, of_ref, *scratch, lam_init, half):
    def group(fox, qt, k, vt, qaug, kaug_ref, out, scr):
        rhs, s0, s1, bmax, m, acc = scr
        return _Group(fox, qt, k, vt, qaug, kaug_ref, out, rhs, (s0, s1), bmax, m, acc)

    groups = (group(True, (fqa, fqb), fk, fvt, (qauga, qaugb), kaug, of_ref, scratch[6:]),
              group(False, (dqa, dqb), dk, dvt, None, None, od_ref, scratch[:6]))
    static = isinstance(s, int)
    last = 2 * s + half + 1
    ones_rows = jnp.ones((SUM_ROWS, ATT_TILE), BF16)

    half_zero = jnp.zeros((HEAD_DIM, ATT_TILE), BF16)
    for gr in groups:
        for tile, qt_ref in enumerate(gr.qt):
            for g in range(GROUP_W // LANES):
                r0 = g * LANES
                gr.rhs[tile, 2 * g, :HEAD_DIM, :] = qt_ref[r0:r0 + HEAD_DIM, :]
                gr.rhs[tile, 2 * g, HEAD_DIM:LANES, :] = half_zero
                gr.rhs[tile, 2 * g + 1, :HEAD_DIM, :] = half_zero
                gr.rhs[tile, 2 * g + 1, HEAD_DIM:LANES, :] = qt_ref[r0 + HEAD_DIM:r0 + LANES, :]
            if gr.fox:
                for c in range(N_CHAINS):
                    b0 = c * BIAS_SLOT
                    gr.rhs[tile, c, LANES:, :] = jnp.zeros((LANES, ATT_TILE), BF16)
                    gr.rhs[tile, c, LANES + b0:LANES + b0 + BIAS_SLOT, :] = (
                        gr.qaug[tile][b0:b0 + BIAS_SLOT, :])
        gr.m[...] = jnp.full(gr.m.shape, NEG_INF, F32)
        gr.acc[...] = jnp.zeros(gr.acc.shape, F32)

    def locate(p):
        if static:
            return (1, p - s - 1) if p > s else (0, s - p)
        in_b = p > s
        return in_b.astype(jnp.int32), jnp.where(in_b, p - s - 1, s - p)

    def diagonal_mask(gr):
        div = 1 if gr.fox else CHUNK
        k_id = lax.broadcasted_iota(jnp.int32, (ATT_TILE, ATT_TILE), 0) // div
        q_id = lax.broadcasted_iota(jnp.int32, (ATT_TILE, ATT_TILE), 1) // div
        return k_id <= q_id

    def scores(gr, p, slot, g, masked):
        tile, blk = locate(p)
        start = blk * ATT_TILE
        rows = pl.ds(start if static else pl.multiple_of(start, ATT_TILE), ATT_TILE)
        lhs = gr.k[rows, g * LANES:(g + 1) * LANES]
        if gr.fox:
            lhs = jnp.concatenate([lhs, gr.kaug[rows, :]], axis=-1)
        for c in (2 * g, 2 * g + 1):
            sc = _dot(lhs, gr.rhs[tile, c])
            if masked:
                sc = jnp.where(diagonal_mask(gr), sc, NEG_INF)
            gr.s[slot][c] = sc
            gr.bmax[slot, c:c + 1, :] = jnp.max(sc, axis=0, keepdims=True)

    def softmax_pv(gr, p, slot, c):
        tile, blk = locate(p)
        sc = gr.s[slot][c]
        m_old = gr.m[tile, c:c + 1, :]
        m_new = jnp.maximum(m_old, gr.bmax[slot, c:c + 1, :])
        alpha = jnp.exp2(m_old - m_new)
        prob = jnp.exp2(sc - m_new)
        gr.m[tile, c:c + 1, :] = m_new
        r0 = c * HEAD_DIM if gr.fox else (c // 2) * LANES
        vt = jnp.concatenate([gr.vt[blk, r0:r0 + gr.width, :], ones_rows], axis=0)
        gr.acc[tile, c] = alpha * gr.acc[tile, c] + _dot(vt, prob.astype(BF16))

    def step(score_args, soft_args, masked=False):
        for g in range(GROUP_W // LANES):
            for gr in groups:
                if soft_args is not None:
                    softmax_pv(gr, *soft_args, 2 * g)
                if score_args is not None:
                    scores(gr, *score_args, g, masked)
                if soft_args is not None:
                    softmax_pv(gr, *soft_args, 2 * g + 1)

    def pair(u, carry):
        p = 2 * u + 1
        step((p + 1, 0), (p, 1))
        step((p + 2, 1), (p + 1, 0))
        return carry

    step((0, 0), None, masked=True)
    step((1, 1), (0, 0))
    if static:
        for u in range((last - 3) // 2):
            pair(u, 0)
    else:
        lax.fori_loop(0, (last - 3) // 2, pair, 0)
    step((last - 1, 0), (last - 2, 1))
    step((last, 1), (last - 1, 0), masked=True)
    step(None, (last, 1))

    lam = (jnp.exp(jnp.sum(lq1_ref[...] * lk1_ref[...], axis=-1, keepdims=True))
           - jnp.exp(jnp.sum(lq2_ref[...] * lk2_ref[...], axis=-1, keepdims=True))
           + lam_init)
    for gr in groups:
        width = gr.width
        for tile in range(2):
            for g in range(GROUP_W // LANES):
                a1, a2 = gr.acc[tile, 2 * g], gr.acc[tile, 2 * g + 1]
                o1 = a1[:width] * (1.0 / a1[width:width + 1])
                o2 = a2[:width] * (1.0 / a2[width:width + 1])
                if gr.fox:
                    o_t = jnp.concatenate([o1, o2], axis=0)
                else:
                    o_t = o1 - lam * o2
                    ms = jnp.mean(o_t * o_t, axis=0, keepdims=True)
                    o_t = ((o_t * lax.rsqrt(ms + EPS)) * subw_ref[...]) * (1.0 - lam_init)
                gr.out[0, tile, 0, :, g * LANES:(g + 1) * LANES] = o_t.T.astype(BF16)


def _attention(dqt, dk, dvt, fqt, fk, fvt, qaugt, kaug, lam_params, batch, lam_init):
    n_tok = dk.shape[0]
    seq = n_tok // batch
    tiles = seq // ATT_TILE
    assert tiles % 2 == 0
    half = tiles // 2
    qt_spec = lambda r, off: pl.BlockSpec((r, ATT_TILE), lambda b, s: (0, b * tiles + s + off))
    kv_spec = lambda w: pl.BlockSpec((seq, w), lambda b, s: (b, 0))
    vt_spec = pl.BlockSpec((tiles, GROUP_W, ATT_TILE), lambda b, s: (b, 0, 0))
    qkv_specs = [qt_spec(GROUP_W, 0), qt_spec(GROUP_W, half), kv_spec(GROUP_W), vt_spec]
    out_spec = pl.BlockSpec((1, 2, 1, ATT_TILE, GROUP_W), lambda b, s: (b, 0, s, 0, 0))
    out_shape = jax.ShapeDtypeStruct((batch, 2, half, ATT_TILE, GROUP_W), BF16)
    od, of = pl.pallas_call(
        functools.partial(_attn_body, lam_init=lam_init, half=half, static_steps=(half - 1,)),
        grid=(batch, half),
        in_specs=qkv_specs + qkv_specs
        + [qt_spec(LANES, 0), qt_spec(LANES, half), kv_spec(LANES)]
        + [_resident(p.shape) for p in lam_params],
        out_specs=[out_spec, out_spec],
        out_shape=[out_shape, out_shape],
        scratch_shapes=_attn_scratch(False) + _attn_scratch(True),
        compiler_params=_params(("arbitrary", "arbitrary")),
        name="attention",
    )(dqt, dqt, dk, dvt, fqt, fqt, fk, fvt, qaugt, qaugt, kaug, *lam_params)
    return od.reshape(n_tok, GROUP_W), of.reshape(n_tok, GROUP_W)


def _mem_kv_body(m_ref, nw_ref, w_ref, kw_ref, mk_ref, mv_ref):
    hm = _rms(m_ref[...], nw_ref[...]).astype(BF16)
    kv = _dot(hm, w_ref[...])
    kw = kw_ref[...]
    for c in range(0, D_MODEL, MEM_HEAD_DIM):
        mk_ref[:, c:c + MEM_HEAD_DIM] = _rms(kv[:, c:c + MEM_HEAD_DIM], kw).astype(BF16)
    mv_ref[...] = kv[:, D_MODEL:].astype(BF16)


def _mem_kv(mem2d, norm_w, w_mem_kv, mem_k_norm_w):
    n_mem = mem2d.shape[0]
    tok = pl.BlockSpec((TOK_TILE, D_MODEL), lambda i: (i, 0))
    out = jax.ShapeDtypeStruct((n_mem, D_MODEL), BF16)
    return pl.pallas_call(
        _mem_kv_body,
        grid=(n_mem // TOK_TILE,),
        in_specs=[tok, _resident((1, D_MODEL)), _resident((D_MODEL, 2 * D_MODEL)),
                  _resident((1, MEM_HEAD_DIM))],
        out_specs=[tok, tok],
        out_shape=[out, out],
        compiler_params=_params(("arbitrary",)),
        name="mem_kv",
    )(mem2d, norm_w.reshape(1, -1), w_mem_kv.astype(BF16), mem_k_norm_w.reshape(1, -1))


POST_TILE = 1024
FF_CHUNK = 1024


def _post_attn_body(x_ref, od_ref, of_ref, mk_ref, mv_ref, wo_ref, nq_ref, wq_ref, qw_ref,
                    wmo_ref, nm_ref, wu_ref, wd_ref, o_ref):
    x1 = x_ref[...] + (_dot(od_ref[...], wo_ref[:GROUP_W, :])
                       + _dot(of_ref[...], wo_ref[GROUP_W:, :]))

    mq = _dot(_rms(x1, nq_ref[...]).astype(BF16), wq_ref[...])
    qw = qw_ref[...]
    heads = []
    for c in range(0, D_MODEL, MEM_HEAD_DIM):
        q = _rms(mq[:, c:c + MEM_HEAD_DIM], qw).astype(BF16)
        s = _dot_nt(q, mk_ref[:, c:c + MEM_HEAD_DIM])
        p = jnp.exp(s - jnp.max(s, axis=-1, keepdims=True))
        l = jnp.sum(p, axis=-1, keepdims=True)
        heads.append((_dot(p.astype(BF16), mv_ref[:, c:c + MEM_HEAD_DIM]) / l).astype(BF16))
    x2 = x1 + _dot(jnp.concatenate(heads, axis=-1), wmo_ref[...])

    h = _rms(x2, nm_ref[...]).astype(BF16)
    acc = x2
    for c in range(0, D_FF, FF_CHUNK):
        u = jnp.maximum(_dot(h, wu_ref[:, c:c + FF_CHUNK]), 0.0)
        acc = acc + _dot((u * u).astype(BF16), wd_ref[c:c + FF_CHUNK, :])
    o_ref[...] = acc


def _post_attn(x2d, od, of, mk, mv, batch, w_out, norm_q, w_mem_q, mem_q_norm_w, w_mem_o,
               norm_mlp, w_up, w_down):
    n_tok = x2d.shape[0]
    tiles = n_tok // batch // POST_TILE
    mem_len = mk.shape[0] // batch
    tok = lambda width: pl.BlockSpec((POST_TILE, width), lambda b, i: (b * tiles + i, 0))
    mem = pl.BlockSpec((mem_len, D_MODEL), lambda b, i: (b, 0))
    scale = MEM_HEAD_DIM ** -0.5
    square = _resident((D_MODEL, D_MODEL))
    vec = _resident((1, D_MODEL))
    return pl.pallas_call(
        _post_attn_body,
        grid=(batch, tiles),
        in_specs=[tok(D_MODEL), tok(GROUP_W), tok(GROUP_W), mem, mem, square, vec, square,
                  _resident((1, MEM_HEAD_DIM)), square, vec, _resident((D_MODEL, D_FF)),
                  _resident((D_FF, D_MODEL))],
        out_specs=tok(D_MODEL),
        out_shape=jax.ShapeDtypeStruct((n_tok, D_MODEL), F32),
        compiler_params=_params(("arbitrary", "arbitrary")),
        name="post_attn",
    )(x2d, od, of, mk, mv, w_out.astype(BF16), norm_q.reshape(1, -1), w_mem_q.astype(BF16),
      (mem_q_norm_w * scale).reshape(1, -1), w_mem_o.astype(BF16), norm_mlp.reshape(1, -1),
      w_up.astype(BF16), w_down.astype(BF16))


def kernel(x, mem, positions, norm_mix_w, w_in, b_forget, diff_q_norm_w, diff_k_norm_w,
           lambda_q1, lambda_k1, lambda_q2, lambda_k2, diff_subln_w, fox_q_norm_w,
           fox_k_norm_w, w_out, norm_mem_q_w, norm_mem_kv_w, w_mem_q, w_mem_kv,
           mem_q_norm_w, mem_k_norm_w, w_mem_o, norm_mlp_w, w_up, w_down):
    batch, seq, d = x.shape
    depth = w_in.shape[0]
    assert d == D_MODEL and seq % TOK_TILE == 0 and seq % POST_TILE == 0
    assert TOK_TILE % ATT_TILE == 0
    assert (batch * mem.shape[1]) % TOK_TILE == 0

    cos_t, sin_t = _rope_tables(positions)
    xc = x.reshape(batch * seq, d)
    mem2d = mem.reshape(-1, d)
    for l in range(depth):
        lam_init = 0.8 - 0.6 * math.exp(-0.3 * l)
        dqt, dk, dvt, fqt, fk, fvt, qaugt, kaug = _in_proj(
            xc, batch, norm_mix_w[l], w_in[l], cos_t, sin_t, diff_q_norm_w[l],
            diff_k_norm_w[l], fox_q_norm_w[l], fox_k_norm_w[l], b_forget[l])
        lam_params = [p[l].reshape(1, -1) for p in (lambda_q1, lambda_k1, lambda_q2, lambda_k2)]
        lam_params.append(diff_subln_w[l].reshape(-1, 1))
        od, of = _attention(dqt, dk, dvt, fqt, fk, fvt, qaugt, kaug, lam_params, batch, lam_init)
        mk, mv = _mem_kv(mem2d, norm_mem_kv_w[l], w_mem_kv[l], mem_k_norm_w[l])
        xc = _post_attn(xc, od, of, mk, mv, batch, w_out[l], norm_mem_q_w[l], w_mem_q[l],
                        mem_q_norm_w[l], w_mem_o[l], norm_mlp_w[l], w_up[l], w_down[l])
    return xc.reshape(batch, seq, d)
```

```python
import functools
import math
from typing import Any, NamedTuple

import jax
import jax.numpy as jnp
from jax import lax
from jax.experimental import pallas as pl
from jax.experimental.pallas import tpu as pltpu

F32 = jnp.float32
BF16 = jnp.bfloat16

D_MODEL = 1024
CHUNK = 64
HEAD_DIM = 64
DIFF_V_DIM = 2 * HEAD_DIM
N_FOX_HEADS = 8
GROUP_W = 512
ROPE_DIM = HEAD_DIM // 4
ROPE_THETA = 500000.0
N_MEM_HEADS = 4
MEM_HEAD_DIM = D_MODEL // N_MEM_HEADS
D_FF = 4 * D_MODEL
EPS = 1e-6
NEG_INF = -1e30
LOG2E = math.log2(math.e)

LANES = 128
MXU_DIM = 256
VMEM_LIMIT = 58 * 1024 * 1024

TOK_TILE = 1024
ATT_TILE = 256


def _params(sem):
    return pltpu.CompilerParams(dimension_semantics=sem, vmem_limit_bytes=VMEM_LIMIT)


def _resident(shape):
    return pl.BlockSpec(shape, lambda *_: (0,) * len(shape), pipeline_mode=pl.Buffered(1))


def _rms(x, w):
    ms = jnp.mean(x * x, axis=-1, keepdims=True)
    return (x * lax.rsqrt(ms + EPS)) * w


def _dot(a, b):
    return jnp.dot(a, b, preferred_element_type=F32)


def _dot_nt(a, b):
    return lax.dot_general(a, b, (((1,), (1,)), ((), ())), preferred_element_type=F32)


def _rope_body(pos_ref, freq_ref, cos_ref, sin_ref):
    ang = pos_ref[...] * freq_ref[...]
    cos_ref[...] = jnp.cos(ang)
    sin_ref[...] = jnp.sin(ang)


def _rope_tables(positions):
    n_tok = positions.size
    half = ROPE_DIM // 2
    inv_freq = ROPE_THETA ** (-jnp.arange(0, ROPE_DIM, 2, dtype=F32) / ROPE_DIM)
    out = jax.ShapeDtypeStruct((half, n_tok), F32)
    return pl.pallas_call(_rope_body, out_shape=(out, out), name="rope_tables")(
        positions.reshape(1, n_tok).astype(F32), inv_freq.reshape(half, 1))


def _log_sigmoid(x):
    return jnp.minimum(x, 0.0) - jnp.log1p(jnp.exp(-jnp.abs(x)))


def _split3(x):
    hi = x.astype(BF16)
    rem = x - hi.astype(F32)
    mid = rem.astype(BF16)
    lo = (rem - mid.astype(F32)).astype(BF16)
    return hi, mid, lo


BIAS_SLOT = LANES // N_FOX_HEADS


def _head_norm_t(t, gain, rope):
    half = ROPE_DIM // 2
    outs = []
    for r0 in range(0, t.shape[0], HEAD_DIM):
        th = t[r0:r0 + HEAD_DIM]
        ms = jnp.sum(th * th, axis=0, keepdims=True) * (1.0 / HEAD_DIM)
        th = (th * lax.rsqrt(ms + EPS)) * gain[r0:r0 + HEAD_DIM]
        if rope is not None:
            cos, sin = rope
            t1, t2 = th[:half], th[half:ROPE_DIM]
            th = jnp.concatenate(
                [t1 * cos - t2 * sin, t2 * cos + t1 * sin, th[ROPE_DIM:]], axis=0)
        outs.append(th)
    return jnp.concatenate(outs, axis=0)


def _in_proj_body(x_ref, nw_ref, wt_ref, wft_ref, cos_ref, sin_ref, gain_ref, bf_ref, dqt_ref,
                  dk_ref, dvt_ref, fqt_ref, fk_ref, fvt_ref, qaugt_ref, kaug_ref, carry_ref):
    @pl.when(pl.program_id(1) == 0)
    def _():
        carry_ref[...] = jnp.zeros_like(carry_ref)

    h = _rms(x_ref[...], nw_ref[...]).astype(BF16)
    rope = (cos_ref[...], sin_ref[...])
    g = GROUP_W

    def proj_t(group):
        r0 = group * GROUP_W
        return _dot_nt(wt_ref[r0:r0 + GROUP_W, :], h)

    def store_keys(k_ref, kt):
        for c in range(0, g, LANES):
            k_ref[:, c:c + LANES] = kt[c:c + LANES].T.astype(BF16)

    def store_values(vt_ref, vt):
        for t in range(TOK_TILE // ATT_TILE):
            vt_ref[t] = vt[:, t * ATT_TILE:(t + 1) * ATT_TILE].astype(BF16)

    dqt_ref[...] = _head_norm_t(proj_t(0), gain_ref[0:g], rope).astype(BF16)
    store_keys(dk_ref, _head_norm_t(proj_t(1), gain_ref[g:2 * g], rope))
    store_values(dvt_ref, proj_t(2))
    fqt_ref[...] = _head_norm_t(proj_t(3), gain_ref[2 * g:3 * g], None).astype(BF16)
    store_keys(fk_ref, _head_norm_t(proj_t(4), gain_ref[3 * g:4 * g], None))
    store_values(fvt_ref, proj_t(5))

    log_f = _log_sigmoid(_dot_nt(wft_ref[...], h) + bf_ref[...])
    r = lax.broadcasted_iota(jnp.int32, (MXU_DIM, MXU_DIM), 0)
    c = lax.broadcasted_iota(jnp.int32, (MXU_DIM, MXU_DIM), 1)
    triu = jnp.where(r <= c, 1.0, 0.0).astype(BF16)
    part = lax.broadcasted_iota(jnp.int32, (LANES, MXU_DIM), 0) % BIAS_SLOT
    carry = carry_ref[:, 0:1]
    for c0 in range(0, TOK_TILE, MXU_DIM):
        hi, mid, lo = _split3(log_f[:, c0:c0 + MXU_DIM])
        cum = (_dot(hi, triu) + _dot(mid, triu)) + _dot(lo, triu) + carry
        carry = cum[:, MXU_DIM - 1:MXU_DIM]
        hi, mid, lo = (t.astype(F32) for t in _split3(cum * LOG2E))
        kaug_t = jnp.where(part == 0, -hi, jnp.where(part == 1, -mid, jnp.where(
            part == 2, -lo, jnp.where(part < 6, 1.0, 0.0))))
        qaug_t = jnp.where(part < 3, 1.0, jnp.where(part == 3, hi, jnp.where(
            part == 4, mid, jnp.where(part == 5, lo, 0.0))))
        kaug_ref[c0:c0 + MXU_DIM, :] = kaug_t.T.astype(BF16)
        qaugt_ref[:, c0:c0 + MXU_DIM] = qaug_t.astype(BF16)
    carry_ref[...] = jnp.broadcast_to(carry, carry_ref.shape)


def _in_proj(x2d, batch, norm_w, w_in, cos_t, sin_t, dqw, dkw, fqw, fkw, b_forget):
    n_tok = x2d.shape[0]
    tiles = n_tok // batch // TOK_TILE
    g = GROUP_W
    wt = w_in[:, :6 * g].T.astype(BF16)
    wft = jnp.repeat(w_in[:, 6 * g:6 * g + N_FOX_HEADS].T, BIAS_SLOT, axis=0).astype(BF16)
    bf = jnp.repeat(b_forget, BIAS_SLOT).reshape(LANES, 1)
    qscale = HEAD_DIM ** -0.5 * LOG2E
    gains = jnp.concatenate([jnp.tile(v, g // HEAD_DIM) for v in
                             (dqw * qscale, dkw, fqw * qscale, fkw)]).reshape(4 * g, 1)

    row = lambda b, i: (b * tiles + i, 0)
    col = lambda b, i: (0, b * tiles + i)
    tok = lambda width: pl.BlockSpec((TOK_TILE, width), row)
    tok_t = lambda rows: pl.BlockSpec((rows, TOK_TILE), col)
    slabs = TOK_TILE // ATT_TILE
    vt_spec = pl.BlockSpec((slabs, g, ATT_TILE), lambda b, i: (b * tiles + i, 0, 0))
    k_shape = jax.ShapeDtypeStruct((n_tok, g), BF16)
    qt_shape = jax.ShapeDtypeStruct((g, n_tok), BF16)
    vt_shape = jax.ShapeDtypeStruct((n_tok // ATT_TILE, g, ATT_TILE), BF16)
    half = ROPE_DIM // 2
    return pl.pallas_call(
        _in_proj_body,
        grid=(batch, tiles),
        in_specs=[tok(D_MODEL), _resident((1, D_MODEL)), _resident(wt.shape),
                  _resident(wft.shape), tok_t(half), tok_t(half), _resident(gains.shape),
                  _resident(bf.shape)],
        out_specs=[tok_t(g), tok(g), vt_spec, tok_t(g), tok(g), vt_spec, tok_t(LANES),
                   tok(LANES)],
        out_shape=[qt_shape, k_shape, vt_shape, qt_shape, k_shape, vt_shape,
                   jax.ShapeDtypeStruct((LANES, n_tok), BF16),
                   jax.ShapeDtypeStruct((n_tok, LANES), BF16)],
        scratch_shapes=[pltpu.VMEM((LANES, LANES), F32)],
        compiler_params=_params(("arbitrary", "arbitrary")),
        name="in_proj",
    )(x2d, norm_w.reshape(1, -1), wt, wft, cos_t, sin_t, gains, bf)


N_CHAINS = 2 * (GROUP_W // LANES)
SUM_ROWS = 16


class _Group(NamedTuple):
    fox: bool
    qt: tuple
    k: Any
    vt: Any
    qaug: Any
    kaug: Any
    out: Any
    rhs: Any
    s: tuple
    bmax: Any
    m: Any
    acc: Any

    @property
    def width(self):
        return HEAD_DIM if self.fox else DIFF_V_DIM


def _attn_scratch(fox):
    width = HEAD_DIM if fox else DIFF_V_DIM
    rhs_w = 2 * LANES if fox else LANES
    return [pltpu.VMEM((2, N_CHAINS, rhs_w, ATT_TILE), BF16),
            pltpu.VMEM((N_CHAINS, ATT_TILE, ATT_TILE), F32),
            pltpu.VMEM((N_CHAINS, ATT_TILE, ATT_TILE), F32),
            pltpu.VMEM((2, N_CHAINS, ATT_TILE), F32),
            pltpu.VMEM((2, N_CHAINS, ATT_TILE), F32),
            pltpu.VMEM((2, N_CHAINS, width + SUM_ROWS, ATT_TILE), F32)]


def _attn_body(*refs, lam_init, half, static_steps):
    s = pl.program_id(1)
    for k in static_steps:
        @pl.when(s == k)
        def _(k=k):
            _attn_step(k, *refs, lam_init=lam_init, half=half)

    generic = [k for k in range(half) if k not in static_steps]
    if generic:
        @pl.when(s <= max(generic))
        def _():
            _attn_step(s, *refs, lam_init=lam_init, half=half)


def _attn_step(s, dqa, dqb, dk, dvt, fqa, fqb, fk, fvt, qauga, qaugb, kaug, lq1_ref, lk1_ref,
               lq2_ref, lk2_ref, subw_ref, od_ref, of_ref, *scratch, lam_init, half):
    def group(fox, qt, k, vt, qaug, kaug_ref, out, scr):
        rhs, s0, s1, bmax, m, acc = scr
        return _Group(fox, qt, k, vt, qaug, kaug_ref, out, rhs, (s0, s1), bmax, m, acc)

    groups = (group(False, (dqa, dqb), dk, dvt, None, None, od_ref, scratch[:6]),
              group(True, (fqa, fqb), fk, fvt, (qauga, qaugb), kaug, of_ref, scratch[6:]))
    static = isinstance(s, int)
    last = 2 * s + half + 1
    ones_rows = jnp.ones((SUM_ROWS, ATT_TILE), BF16)

    half_zero = jnp.zeros((HEAD_DIM, ATT_TILE), BF16)
    for gr in groups:
        for tile, qt_ref in enumerate(gr.qt):
            for g in range(GROUP_W // LANES):
                r0 = g * LANES
                gr.rhs[tile, 2 * g, :HEAD_DIM, :] = qt_ref[r0:r0 + HEAD_DIM, :]
                gr.rhs[tile, 2 * g, HEAD_DIM:LANES, :] = half_zero
                gr.rhs[tile, 2 * g + 1, :HEAD_DIM, :] = half_zero
                gr.rhs[tile, 2 * g + 1, HEAD_DIM:LANES, :] = qt_ref[r0 + HEAD_DIM:r0 + LANES, :]
            if gr.fox:
                for c in range(N_CHAINS):
                    b0 = c * BIAS_SLOT
                    gr.rhs[tile, c, LANES:, :] = jnp.zeros((LANES, ATT_TILE), BF16)
                    gr.rhs[tile, c, LANES + b0:LANES + b0 + BIAS_SLOT, :] = (
                        gr.qaug[tile][b0:b0 + BIAS_SLOT, :])
        gr.m[...] = jnp.full(gr.m.shape, NEG_INF, F32)
        gr.acc[...] = jnp.zeros(gr.acc.shape, F32)

    def locate(p):
        if static:
            return (1, p - s - 1) if p > s else (0, s - p)
        in_b = p > s
        return in_b.astype(jnp.int32), jnp.where(in_b, p - s - 1, s - p)

    def diagonal_mask(gr):
        div = 1 if gr.fox else CHUNK
        k_id = lax.broadcasted_iota(jnp.int32, (ATT_TILE, ATT_TILE), 0) // div
        q_id = lax.broadcasted_iota(jnp.int32, (ATT_TILE, ATT_TILE), 1) // div
        return k_id <= q_id

    def scores(gr, p, slot, c, masked):
        tile, blk = locate(p)
        g = c // 2
        start = blk * ATT_TILE
        rows = pl.ds(start if static else pl.multiple_of(start, ATT_TILE), ATT_TILE)
        lhs = gr.k[rows, g * LANES:(g + 1) * LANES]
        if gr.fox:
            lhs = jnp.concatenate([lhs, gr.kaug[rows, :]], axis=-1)
        sc = _dot(lhs, gr.rhs[tile, c])
        if masked:
            sc = jnp.where(diagonal_mask(gr), sc, NEG_INF)
        gr.s[slot][c] = sc
        gr.bmax[slot, c:c + 1, :] = jnp.max(sc, axis=0, keepdims=True)

    def softmax_pv(gr, p, slot, c):
        tile, blk = locate(p)
        sc = gr.s[slot][c]
        m_old = gr.m[tile, c:c + 1, :]
        m_new = jnp.maximum(m_old, gr.bmax[slot, c:c + 1, :])
        alpha = jnp.exp2(m_old - m_new)
        prob = jnp.exp2(sc - m_new)
        gr.m[tile, c:c + 1, :] = m_new
        r0 = c * HEAD_DIM if gr.fox else (c // 2) * LANES
        vt = jnp.concatenate([gr.vt[blk, r0:r0 + gr.width, :], ones_rows], axis=0)
        gr.acc[tile, c] = alpha * gr.acc[tile, c] + _dot(vt, prob.astype(BF16))

    def step(score_args, soft_args, masked=False):
        for c in range(N_CHAINS):
            for gr in groups:
                if soft_args is not None:
                    softmax_pv(gr, *soft_args, c)
                if score_args is not None:
                    scores(gr, *score_args, c, masked)

    def pair(u, carry):
        p = 2 * u + 1
        step((p + 1, 0), (p, 1))
        step((p + 2, 1), (p + 1, 0))
        return carry

    step((0, 0), None, masked=True)
    step((1, 1), (0, 0))
    if static:
        for u in range((last - 3) // 2):
            pair(u, 0)
    else:
        lax.fori_loop(0, (last - 3) // 2, pair, 0)
    step((last - 1, 0), (last - 2, 1))
    step((last, 1), (last - 1, 0), masked=True)
    step(None, (last, 1))

    lam = (jnp.exp(jnp.sum(lq1_ref[...] * lk1_ref[...], axis=-1, keepdims=True))
           - jnp.exp(jnp.sum(lq2_ref[...] * lk2_ref[...], axis=-1, keepdims=True))
           + lam_init)
    for gr in groups:
        width = gr.width
        for tile in range(2):
            for g in range(GROUP_W // LANES):
                a1, a2 = gr.acc[tile, 2 * g], gr.acc[tile, 2 * g + 1]
                o1 = a1[:width] * (1.0 / a1[width:width + 1])
                o2 = a2[:width] * (1.0 / a2[width:width + 1])
                if gr.fox:
                    o_t = jnp.concatenate([o1, o2], axis=0)
                else:
                    o_t = o1 - lam * o2
                    ms = jnp.mean(o_t * o_t, axis=0, keepdims=True)
                    o_t = ((o_t * lax.rsqrt(ms + EPS)) * subw_ref[...]) * (1.0 - lam_init)
                gr.out[0, tile, 0, :, g * LANES:(g + 1) * LANES] = o_t.T.astype(BF16)


def _attention(dqt, dk, dvt, fqt, fk, fvt, qaugt, kaug, lam_params, batch, lam_init):
    n_tok = dk.shape[0]
    seq = n_tok // batch
    tiles = seq // ATT_TILE
    assert tiles % 2 == 0
    half = tiles // 2
    qt_spec = lambda r, off: pl.BlockSpec((r, ATT_TILE), lambda b, s: (0, b * tiles + s + off))
    kv_spec = lambda w: pl.BlockSpec((seq, w), lambda b, s: (b, 0))
    vt_spec = pl.BlockSpec((tiles, GROUP_W, ATT_TILE), lambda b, s: (b, 0, 0))
    qkv_specs = [qt_spec(GROUP_W, 0), qt_spec(GROUP_W, half), kv_spec(GROUP_W), vt_spec]
    out_spec = pl.BlockSpec((1, 2, 1, ATT_TILE, GROUP_W), lambda b, s: (b, 0, s, 0, 0))
    out_shape = jax.ShapeDtypeStruct((batch, 2, half, ATT_TILE, GROUP_W), BF16)
    od, of = pl.pallas_call(
        functools.partial(_attn_body, lam_init=lam_init, half=half, static_steps=(half - 1,)),
        grid=(batch, half),
        in_specs=qkv_specs + qkv_specs
        + [qt_spec(LANES, 0), qt_spec(LANES, half), kv_spec(LANES)]
        + [_resident(p.shape) for p in lam_params],
        out_specs=[out_spec, out_spec],
        out_shape=[out_shape, out_shape],
        scratch_shapes=_attn_scratch(False) + _attn_scratch(True),
        compiler_params=_params(("arbitrary", "arbitrary")),
        name="attention",
    )(dqt, dqt, dk, dvt, fqt, fqt, fk, fvt, qaugt, qaugt, kaug, *lam_params)
    return od.reshape(n_tok, GROUP_W), of.reshape(n_tok, GROUP_W)


def _mem_kv_body(m_ref, nw_ref, w_ref, kw_ref, mk_ref, mv_ref):
    hm = _rms(m_ref[...], nw_ref[...]).astype(BF16)
    kv = _dot(hm, w_ref[...])
    kw = kw_ref[...]
    for c in range(0, D_MODEL, MEM_HEAD_DIM):
        mk_ref[:, c:c + MEM_HEAD_DIM] = _rms(kv[:, c:c + MEM_HEAD_DIM], kw).astype(BF16)
    mv_ref[...] = kv[:, D_MODEL:].astype(BF16)


def _mem_kv(mem2d, norm_w, w_mem_kv, mem_k_norm_w):
    n_mem = mem2d.shape[0]
    tok = pl.BlockSpec((TOK_TILE, D_MODEL), lambda i: (i, 0))
    out = jax.ShapeDtypeStruct((n_mem, D_MODEL), BF16)
    return pl.pallas_call(
        _mem_kv_body,
        grid=(n_mem // TOK_TILE,),
        in_specs=[tok, _resident((1, D_MODEL)), _resident((D_MODEL, 2 * D_MODEL)),
                  _resident((1, MEM_HEAD_DIM))],
        out_specs=[tok, tok],
        out_shape=[out, out],
        compiler_params=_params(("arbitrary",)),
        name="mem_kv",
    )(mem2d, norm_w.reshape(1, -1), w_mem_kv.astype(BF16), mem_k_norm_w.reshape(1, -1))


POST_TILE = 1024
FF_CHUNK = 1024


def _post_attn_body(x_ref, od_ref, of_ref, mk_ref, mv_ref, wo_ref, nq_ref, wq_ref, qw_ref,
                    wmo_ref, nm_ref, wu_ref, wd_ref, o_ref):
    x1 = x_ref[...] + (_dot(od_ref[...], wo_ref[:GROUP_W, :])
                       + _dot(of_ref[...], wo_ref[GROUP_W:, :]))

    mq = _dot(_rms(x1, nq_ref[...]).astype(BF16), wq_ref[...])
    qw = qw_ref[...]
    heads = []
    for c in range(0, D_MODEL, MEM_HEAD_DIM):
        q = _rms(mq[:, c:c + MEM_HEAD_DIM], qw).astype(BF16)
        s = _dot_nt(q, mk_ref[:, c:c + MEM_HEAD_DIM])
        p = jnp.exp(s - jnp.max(s, axis=-1, keepdims=True))
        l = jnp.sum(p, axis=-1, keepdims=True)
        heads.append((_dot(p.astype(BF16), mv_ref[:, c:c + MEM_HEAD_DIM]) / l).astype(BF16))
    x2 = x1 + _dot(jnp.concatenate(heads, axis=-1), wmo_ref[...])

    h = _rms(x2, nm_ref[...]).astype(BF16)
    acc = x2
    for c in range(0, D_FF, FF_CHUNK):
        u = jnp.maximum(_dot(h, wu_ref[:, c:c + FF_CHUNK]), 0.0)
        acc = acc + _dot((u * u).astype(BF16), wd_ref[c:c + FF_CHUNK, :])
    o_ref[...] = acc


def _post_attn(x2d, od, of, mk, mv, batch, w_out, norm_q, w_mem_q, mem_q_norm_w, w_mem_o,
               norm_mlp, w_up, w_down):
    n_tok = x2d.shape[0]
    tiles = n_tok // batch // POST_TILE
    mem_len = mk.shape[0] // batch
    tok = lambda width: pl.BlockSpec((POST_TILE, width), lambda b, i: (b * tiles + i, 0))
    mem = pl.BlockSpec((mem_len, D_MODEL), lambda b, i: (b, 0))
    scale = MEM_HEAD_DIM ** -0.5
    square = _resident((D_MODEL, D_MODEL))
    vec = _resident((1, D_MODEL))
    return pl.pallas_call(
        _post_attn_body,
        grid=(batch, tiles),
        in_specs=[tok(D_MODEL), tok(GROUP_W), tok(GROUP_W), mem, mem, square, vec, square,
                  _resident((1, MEM_HEAD_DIM)), square, vec, _resident((D_MODEL, D_FF)),
                  _resident((D_FF, D_MODEL))],
        out_specs=tok(D_MODEL),
        out_shape=jax.ShapeDtypeStruct((n_tok, D_MODEL), F32),
        compiler_params=_params(("arbitrary", "arbitrary")),
        name="post_attn",
    )(x2d, od, of, mk, mv, w_out.astype(BF16), norm_q.reshape(1, -1), w_mem_q.astype(BF16),
      (mem_q_norm_w * scale).reshape(1, -1), w_mem_o.astype(BF16), norm_mlp.reshape(1, -1),
      w_up.astype(BF16), w_down.astype(BF16))


def kernel(x, mem, positions, norm_mix_w, w_in, b_forget, diff_q_norm_w, diff_k_norm_w,
           lambda_q1, lambda_k1, lambda_q2, lambda_k2, diff_subln_w, fox_q_norm_w,
           fox_k_norm_w, w_out, norm_mem_q_w, norm_mem_kv_w, w_mem_q, w_mem_kv,
           mem_q_norm_w, mem_k_norm_w, w_mem_o, norm_mlp_w, w_up, w_down):
    batch, seq, d = x.shape
    depth = w_in.shape[0]
    assert d == D_MODEL and seq % TOK_TILE == 0 and seq % POST_TILE == 0
    assert TOK_TILE % ATT_TILE == 0
    assert (batch * mem.shape[1]) % TOK_TILE == 0

    cos_t, sin_t = _rope_tables(positions)
    xc = x.reshape(batch * seq, d)
    mem2d = mem.reshape(-1, d)
    for l in range(depth):
        lam_init = 0.8 - 0.6 * math.exp(-0.3 * l)
        dqt, dk, dvt, fqt, fk, fvt, qaugt, kaug = _in_proj(
            xc, batch, norm_mix_w[l], w_in[l], cos_t, sin_t, diff_q_norm_w[l],
            diff_k_norm_w[l], fox_q_norm_w[l], fox_k_norm_w[l], b_forget[l])
        lam_params = [p[l].reshape(1, -1) for p in (lambda_q1, lambda_k1, lambda_q2, lambda_k2)]
        lam_params.append(diff_subln_w[l].reshape(-1, 1))
        od, of = _attention(dqt, dk, dvt, fqt, fk, fvt, qaugt, kaug, lam_params, batch, lam_init)
        mk, mv = _mem_kv(mem2d, norm_mem_kv_w[l], w_mem_kv[l], mem_k_norm_w[l])
        xc = _post_attn(xc, od, of, mk, mv, batch, w_out[l], norm_mem_q_w[l], w_mem_q[l],
                        mem_q_norm_w[l], w_mem_o[l], norm_mlp_w[l], w_up[l], w_down[l])
    return xc.reshape(batch, seq, d)
```

```python
import functools
import math
from typing import Any, NamedTuple

import jax
import jax.numpy as jnp
from jax import lax
from jax.experimental import pallas as pl
from jax.experimental.pallas import tpu as pltpu

F32 = jnp.float32
BF16 = jnp.bfloat16

D_MODEL = 1024
CHUNK = 64
HEAD_DIM = 64
DIFF_V_DIM = 2 * HEAD_DIM
N_FOX_HEADS = 8
GROUP_W = 512
ROPE_DIM = HEAD_DIM // 4
ROPE_THETA = 500000.0
N_MEM_HEADS = 4
MEM_HEAD_DIM = D_MODEL // N_MEM_HEADS
D_FF = 4 * D_MODEL
EPS = 1e-6
NEG_INF = -1e30
LOG2E = math.log2(math.e)

LANES = 128
MXU_DIM = 256
VMEM_LIMIT = 58 * 1024 * 1024

TOK_TILE = 1024
ATT_TILE = 256


def _params(sem):
    return pltpu.CompilerParams(dimension_semantics=sem, vmem_limit_bytes=VMEM_LIMIT)


def _resident(shape):
    return pl.BlockSpec(shape, lambda *_: (0,) * len(shape), pipeline_mode=pl.Buffered(1))


def _rms(x, w):
    ms = jnp.mean(x * x, axis=-1, keepdims=True)
    return (x * lax.rsqrt(ms + EPS)) * w


def _dot(a, b):
    return jnp.dot(a, b, preferred_element_type=F32)


def _dot_nt(a, b):
    return lax.dot_general(a, b, (((1,), (1,)), ((), ())), preferred_element_type=F32)


def _rope_body(pos_ref, freq_ref, cos_ref, sin_ref):
    ang = pos_ref[...] * freq_ref[...]
    cos_ref[...] = jnp.cos(ang)
    sin_ref[...] = jnp.sin(ang)


def _rope_tables(positions):
    n_tok = positions.size
    half = ROPE_DIM // 2
    inv_freq = ROPE_THETA ** (-jnp.arange(0, ROPE_DIM, 2, dtype=F32) / ROPE_DIM)
    out = jax.ShapeDtypeStruct((half, n_tok), F32)
    return pl.pallas_call(_rope_body, out_shape=(out, out), name="rope_tables")(
        positions.reshape(1, n_tok).astype(F32), inv_freq.reshape(half, 1))


def _log_sigmoid(x):
    return jnp.minimum(x, 0.0) - jnp.log1p(jnp.exp(-jnp.abs(x)))


def _split3(x):
    hi = x.astype(BF16)
    rem = x - hi.astype(F32)
    mid = rem.astype(BF16)
    lo = (rem - mid.astype(F32)).astype(BF16)
    return hi, mid, lo


BIAS_SLOT = LANES // N_FOX_HEADS


def _head_norm_t(t, gain, rope):
    half = ROPE_DIM // 2
    outs = []
    for r0 in range(0, t.shape[0], HEAD_DIM):
        th = t[r0:r0 + HEAD_DIM]
        ms = jnp.sum(th * th, axis=0, keepdims=True) * (1.0 / HEAD_DIM)
        th = (th * lax.rsqrt(ms + EPS)) * gain[r0:r0 + HEAD_DIM]
        if rope is not None:
            cos, sin = rope
            t1, t2 = th[:half], th[half:ROPE_DIM]
            th = jnp.concatenate(
                [t1 * cos - t2 * sin, t2 * cos + t1 * sin, th[ROPE_DIM:]], axis=0)
        outs.append(th)
    return jnp.concatenate(outs, axis=0)


def _in_proj_body(x_ref, nw_ref, wt_ref, wft_ref, cos_ref, sin_ref, gain_ref, bf_ref, dqt_ref,
                  dk_ref, dvt_ref, fqt_ref, fk_ref, fvt_ref, qaugt_ref, kaug_ref, carry_ref):
    @pl.when(pl.program_id(1) == 0)
    def _():
        carry_ref[...] = jnp.zeros_like(carry_ref)

    h = _rms(x_ref[...], nw_ref[...]).astype(BF16)
    rope = (cos_ref[...], sin_ref[...])
    g = GROUP_W

    def proj_t(group):
        r0 = group * GROUP_W
        return _dot_nt(wt_ref[r0:r0 + GROUP_W, :], h)

    def store_keys(k_ref, kt):
        for c in range(0, g, LANES):
            k_ref[:, c:c + LANES] = kt[c:c + LANES].T.astype(BF16)

    def store_values(vt_ref, vt):
        for t in range(TOK_TILE // ATT_TILE):
            vt_ref[t] = vt[:, t * ATT_TILE:(t + 1) * ATT_TILE].astype(BF16)

    dqt_ref[...] = _head_norm_t(proj_t(0), gain_ref[0:g], rope).astype(BF16)
    store_keys(dk_ref, _head_norm_t(proj_t(1), gain_ref[g:2 * g], rope))
    store_values(dvt_ref, proj_t(2))
    fqt_ref[...] = _head_norm_t(proj_t(3), gain_ref[2 * g:3 * g], None).astype(BF16)
    store_keys(fk_ref, _head_norm_t(proj_t(4), gain_ref[3 * g:4 * g], None))
    store_values(fvt_ref, proj_t(5))

    log_f = _log_sigmoid(_dot_nt(wft_ref[...], h) + bf_ref[...])
    r = lax.broadcasted_iota(jnp.int32, (MXU_DIM, MXU_DIM), 0)
    c = lax.broadcasted_iota(jnp.int32, (MXU_DIM, MXU_DIM), 1)
    triu = jnp.where(r <= c, 1.0, 0.0).astype(BF16)
    part = lax.broadcasted_iota(jnp.int32, (LANES, MXU_DIM), 0) % BIAS_SLOT
    carry = carry_ref[:, 0:1]
    for c0 in range(0, TOK_TILE, MXU_DIM):
        hi, mid, lo = _split3(log_f[:, c0:c0 + MXU_DIM])
        cum = (_dot(hi, triu) + _dot(mid, triu)) + _dot(lo, triu) + carry
        carry = cum[:, MXU_DIM - 1:MXU_DIM]
        hi, mid, lo = (t.astype(F32) for t in _split3(cum * LOG2E))
        kaug_t = jnp.where(part == 0, -hi, jnp.where(part == 1, -mid, jnp.where(
            part == 2, -lo, jnp.where(part < 6, 1.0, 0.0))))
        qaug_t = jnp.where(part < 3, 1.0, jnp.where(part == 3, hi, jnp.where(
            part == 4, mid, jnp.where(part == 5, lo, 0.0))))
        kaug_ref[c0:c0 + MXU_DIM, :] = kaug_t.T.astype(BF16)
        qaugt_ref[:, c0:c0 + MXU_DIM] = qaug_t.astype(BF16)
    carry_ref[...] = jnp.broadcast_to(carry, carry_ref.shape)


def _in_proj(x2d, batch, norm_w, w_in, cos_t, sin_t, dqw, dkw, fqw, fkw, b_forget):
    n_tok = x2d.shape[0]
    tiles = n_tok // batch // TOK_TILE
    g = GROUP_W
    wt = w_in[:, :6 * g].T.astype(BF16)
    wft = jnp.repeat(w_in[:, 6 * g:6 * g + N_FOX_HEADS].T, BIAS_SLOT, axis=0).astype(BF16)
    bf = jnp.repeat(b_forget, BIAS_SLOT).reshape(LANES, 1)
    qscale = HEAD_DIM ** -0.5 * LOG2E
    gains = jnp.concatenate([jnp.tile(v, g // HEAD_DIM) for v in
                             (dqw * qscale, dkw, fqw * qscale, fkw)]).reshape(4 * g, 1)

    row = lambda b, i: (b * tiles + i, 0)
    col = lambda b, i: (0, b * tiles + i)
    tok = lambda width: pl.BlockSpec((TOK_TILE, width), row)
    tok_t = lambda rows: pl.BlockSpec((rows, TOK_TILE), col)
    slabs = TOK_TILE // ATT_TILE
    vt_spec = pl.BlockSpec((slabs, g, ATT_TILE), lambda b, i: (b * tiles + i, 0, 0))
    k_shape = jax.ShapeDtypeStruct((n_tok, g), BF16)
    qt_shape = jax.ShapeDtypeStruct((g, n_tok), BF16)
    vt_shape = jax.ShapeDtypeStruct((n_tok // ATT_TILE, g, ATT_TILE), BF16)
    half = ROPE_DIM // 2
    return pl.pallas_call(
        _in_proj_body,
        grid=(batch, tiles),
        in_specs=[tok(D_MODEL), _resident((1, D_MODEL)), _resident(wt.shape),
                  _resident(wft.shape), tok_t(half), tok_t(half), _resident(gains.shape),
                  _resident(bf.shape)],
        out_specs=[tok_t(g), tok(g), vt_spec, tok_t(g), tok(g), vt_spec, tok_t(LANES),
                   tok(LANES)],
        out_shape=[qt_shape, k_shape, vt_shape, qt_shape, k_shape, vt_shape,
                   jax.ShapeDtypeStruct((LANES, n_tok), BF16),
                   jax.ShapeDtypeStruct((n_tok, LANES), BF16)],
        scratch_shapes=[pltpu.VMEM((LANES, LANES), F32)],
        compiler_params=_params(("arbitrary", "arbitrary")),
        name="in_proj",
    )(x2d, norm_w.reshape(1, -1), wt, wft, cos_t, sin_t, gains, bf)


N_CHAINS = 2 * (GROUP_W // LANES)
SUM_ROWS = 16


class _Group(NamedTuple):
    fox: bool
    qt: tuple
    k: Any
    vt: Any
    qaug: Any
    kaug: Any
    out: Any
    rhs: Any
    s: tuple
    bmax: Any
    m: Any
    acc: Any

    @property
    def width(self):
        return HEAD_DIM if self.fox else DIFF_V_DIM


def _attn_scratch(fox):
    width = HEAD_DIM if fox else DIFF_V_DIM
    rhs_w = 2 * LANES if fox else LANES
    return [pltpu.VMEM((2, N_CHAINS, rhs_w, ATT_TILE), BF16),
            pltpu.VMEM((N_CHAINS, ATT_TILE, ATT_TILE), F32),
            pltpu.VMEM((N_CHAINS, ATT_TILE, ATT_TILE), F32),
            pltpu.VMEM((2, N_CHAINS, ATT_TILE), F32),
            pltpu.VMEM((2, N_CHAINS, ATT_TILE), F32),
            pltpu.VMEM((2, N_CHAINS, width + SUM_ROWS, ATT_TILE), F32)]


def _attn_body(*refs, lam_init, half, static_steps):
    s = pl.program_id(1)
    for k in static_steps:
        @pl.when(s == k)
        def _(k=k):
            _attn_step(k, *refs, lam_init=lam_init, half=half)

    generic = [k for k in range(half) if k not in static_steps]
    if generic:
        @pl.when(s <= max(generic))
        def _():
            _attn_step(s, *refs, lam_init=lam_init, half=half)


def _attn_step(s, dqa, dqb, dk, dvt, fqa, fqb, fk, fvt, qauga, qaugb, kaug, lq1_ref, lk1_ref,
               lq2_ref, lk2_ref, subw_ref, od_ref, of_ref, *scratch, lam_init, half):
    def group(fox, qt, k, vt, qaug, kaug_ref, out, scr):
        rhs, s0, s1, bmax, m, acc = scr
        return _Group(fox, qt, k, vt, qaug, kaug_ref, out, rhs, (s0, s1), bmax, m, acc)

    groups = (group(False, (dqa, dqb), dk, dvt, None, None, od_ref, scratch[:6]),
              group(True, (fqa, fqb), fk, fvt, (qauga, qaugb), kaug, of_ref, scratch[6:]))
    static = isinstance(s, int)
    last = 2 * s + half + 1
    ones_rows = jnp.ones((SUM_ROWS, ATT_TILE), BF16)

    half_zero = jnp.zeros((HEAD_DIM, ATT_TILE), BF16)
    for gr in groups:
        for tile, qt_ref in enumerate(gr.qt):
            for g in range(GROUP_W // LANES):
                r0 = g * LANES
                gr.rhs[tile, 2 * g, :HEAD_DIM, :] = qt_ref[r0:r0 + HEAD_DIM, :]
                gr.rhs[tile, 2 * g, HEAD_DIM:LANES, :] = half_zero
                gr.rhs[tile, 2 * g + 1, :HEAD_DIM, :] = half_zero
                gr.rhs[tile, 2 * g + 1, HEAD_DIM:LANES, :] = qt_ref[r0 + HEAD_DIM:r0 + LANES, :]
            if gr.fox:
                for c in range(N_CHAINS):
                    b0 = c * BIAS_SLOT
                    gr.rhs[tile, c, LANES:, :] = jnp.zeros((LANES, ATT_TILE), BF16)
                    gr.rhs[tile, c, LANES + b0:LANES + b0 + BIAS_SLOT, :] = (
                        gr.qaug[tile][b0:b0 + BIAS_SLOT, :])
        gr.m[...] = jnp.full(gr.m.shape, NEG_INF, F32)
        gr.acc[...] = jnp.zeros(gr.acc.shape, F32)

    def locate(p):
        if static:
            return (1, p - s - 1) if p > s else (0, s - p)
        in_b = p > s
        return in_b.astype(jnp.int32), jnp.where(in_b, p - s - 1, s - p)

    def diagonal_mask(gr):
        div = 1 if gr.fox else CHUNK
        k_id = lax.broadcasted_iota(jnp.int32, (ATT_TILE, ATT_TILE), 0) // div
        q_id = lax.broadcasted_iota(jnp.int32, (ATT_TILE, ATT_TILE), 1) // div
        return k_id <= q_id

    def scores(gr, p, slot, g, masked):
        tile, blk = locate(p)
        start = blk * ATT_TILE
        rows = pl.ds(start if static else pl.multiple_of(start, ATT_TILE), ATT_TILE)
        lhs = gr.k[rows, g * LANES:(g + 1) * LANES]
        if gr.fox:
            lhs = jnp.concatenate([lhs, gr.kaug[rows, :]], axis=-1)
        for c in (2 * g, 2 * g + 1):
            sc = _dot(lhs, gr.rhs[tile, c])
            if masked:
                sc = jnp.where(diagonal_mask(gr), sc, NEG_INF)
            gr.s[slot][c] = sc
            gr.bmax[slot, c:c + 1, :] = jnp.max(sc, axis=0, keepdims=True)

    def softmax_pv(gr, p, slot, c):
        tile, blk = locate(p)
        sc = gr.s[slot][c]
        m_old = gr.m[tile, c:c + 1, :]
        m_new = jnp.maximum(m_old, gr.bmax[slot, c:c + 1, :])
        alpha = jnp.exp2(m_old - m_new)
        prob = jnp.exp2(sc - m_new)
        gr.m[tile, c:c + 1, :] = m_new
        r0 = c * HEAD_DIM if gr.fox else (c // 2) * LANES
        vt = jnp.concatenate([gr.vt[blk, r0:r0 + gr.width, :], ones_rows], axis=0)
        gr.acc[tile, c] = alpha * gr.acc[tile, c] + _dot(vt, prob.astype(BF16))

    def step(score_args, soft_args, masked=False):
        for g in range(GROUP_W // LANES):
            for gr in groups:
                if soft_args is not None:
                    softmax_pv(gr, *soft_args, 2 * g)
                if score_args is not None:
                    scores(gr, *score_args, g, masked)
                if soft_args is not None:
                    softmax_pv(gr, *soft_args, 2 * g + 1)

    def pair(u, carry):
        p = 2 * u + 1
        step((p + 1, 0), (p, 1))
        step((p + 2, 1), (p + 1, 0))
        return carry

    step((0, 0), None, masked=True)
    step((1, 1), (0, 0))
    if static:
        for u in range((last - 3) // 2):
            pair(u, 0)
    else:
        lax.fori_loop(0, (last - 3) // 2, pair, 0)
    step((last - 1, 0), (last - 2, 1))
    step((last, 1), (last - 1, 0), masked=True)
    step(None, (last, 1))

    lam = (jnp.exp(jnp.sum(lq1_ref[...] * lk1_ref[...], axis=-1, keepdims=True))
           - jnp.exp(jnp.sum(lq2_ref[...] * lk2_ref[...], axis=-1, keepdims=True))
           + lam_init)
    for gr in groups:
        width = gr.width
        for tile in range(2):
            for g in range(GROUP_W // LANES):
                a1, a2 = gr.acc[tile, 2 * g], gr.acc[tile, 2 * g + 1]
                o1 = a1[:width] * (1.0 / a1[width:width + 1])
                o2 = a2[:width] * (1.0 / a2[width:width + 1])
                if gr.fox:
                    o_t = jnp.concatenate([o1, o2], axis=0)
                else:
                    o_t = o1 - lam * o2
                    ms = jnp.mean(o_t * o_t, axis=0, keepdims=True)
                    o_t = ((o_t * lax.rsqrt(ms + EPS)) * subw_ref[...]) * (1.0 - lam_init)
                gr.out[0, tile, 0, :, g * LANES:(g + 1) * LANES] = o_t.T.astype(BF16)


def _attention(dqt, dk, dvt, fqt, fk, fvt, qaugt, kaug, lam_params, batch, lam_init):
    n_tok = dk.shape[0]
    seq = n_tok // batch
    tiles = seq // ATT_TILE
    assert tiles % 2 == 0
    half = tiles // 2
    qt_spec = lambda r, off: pl.BlockSpec((r, ATT_TILE), lambda b, s: (0, b * tiles + s + off))
    kv_spec = lambda w: pl.BlockSpec((seq, w), lambda b, s: (b, 0))
    vt_spec = pl.BlockSpec((tiles, GROUP_W, ATT_TILE), lambda b, s: (b, 0, 0))
    qkv_specs = [qt_spec(GROUP_W, 0), qt_spec(GROUP_W, half), kv_spec(GROUP_W), vt_spec]
    out_spec = pl.BlockSpec((1, 2, 1, ATT_TILE, GROUP_W), lambda b, s: (b, 0, s, 0, 0))
    out_shape = jax.ShapeDtypeStruct((batch, 2, half, ATT_TILE, GROUP_W), BF16)
    od, of = pl.pallas_call(
        functools.partial(_attn_body, lam_init=lam_init, half=half, static_steps=()),
        grid=(batch, half),
        in_specs=qkv_specs + qkv_specs
        + [qt_spec(LANES, 0), qt_spec(LANES, half), kv_spec(LANES)]
        + [_resident(p.shape) for p in lam_params],
        out_specs=[out_spec, out_spec],
        out_shape=[out_shape, out_shape],
        scratch_shapes=_attn_scratch(False) + _attn_scratch(True),
        compiler_params=_params(("arbitrary", "arbitrary")),
        name="attention",
    )(dqt, dqt, dk, dvt, fqt, fqt, fk, fvt, qaugt, qaugt, kaug, *lam_params)
    return od.reshape(n_tok, GROUP_W), of.reshape(n_tok, GROUP_W)


def _mem_kv_body(m_ref, nw_ref, w_ref, kw_ref, mk_ref, mv_ref):
    hm = _rms(m_ref[...], nw_ref[...]).astype(BF16)
    kv = _dot(hm, w_ref[...])
    kw = kw_ref[...]
    for c in range(0, D_MODEL, MEM_HEAD_DIM):
        mk_ref[:, c:c + MEM_HEAD_DIM] = _rms(kv[:, c:c + MEM_HEAD_DIM], kw).astype(BF16)
    mv_ref[...] = kv[:, D_MODEL:].astype(BF16)


def _mem_kv(mem2d, norm_w, w_mem_kv, mem_k_norm_w):
    n_mem = mem2d.shape[0]
    tok = pl.BlockSpec((TOK_TILE, D_MODEL), lambda i: (i, 0))
    out = jax.ShapeDtypeStruct((n_mem, D_MODEL), BF16)
    return pl.pallas_call(
        _mem_kv_body,
        grid=(n_mem // TOK_TILE,),
        in_specs=[tok, _resident((1, D_MODEL)), _resident((D_MODEL, 2 * D_MODEL)),
                  _resident((1, MEM_HEAD_DIM))],
        out_specs=[tok, tok],
        out_shape=[out, out],
        compiler_params=_params(("arbitrary",)),
        name="mem_kv",
    )(mem2d, norm_w.reshape(1, -1), w_mem_kv.astype(BF16), mem_k_norm_w.reshape(1, -1))


POST_TILE = 1024
FF_CHUNK = 1024


def _post_attn_body(x_ref, od_ref, of_ref, mk_ref, mv_ref, wo_ref, nq_ref, wq_ref, qw_ref,
                    wmo_ref, nm_ref, wu_ref, wd_ref, o_ref):
    x1 = x_ref[...] + (_dot(od_ref[...], wo_ref[:GROUP_W, :])
                       + _dot(of_ref[...], wo_ref[GROUP_W:, :]))

    mq = _dot(_rms(x1, nq_ref[...]).astype(BF16), wq_ref[...])
    qw = qw_ref[...]
    heads = []
    for c in range(0, D_MODEL, MEM_HEAD_DIM):
        q = _rms(mq[:, c:c + MEM_HEAD_DIM], qw).astype(BF16)
        s = _dot_nt(q, mk_ref[:, c:c + MEM_HEAD_DIM])
        p = jnp.exp(s - jnp.max(s, axis=-1, keepdims=True))
        l = jnp.sum(p, axis=-1, keepdims=True)
        heads.append((_dot(p.astype(BF16), mv_ref[:, c:c + MEM_HEAD_DIM]) / l).astype(BF16))
    x2 = x1 + _dot(jnp.concatenate(heads, axis=-1), wmo_ref[...])

    h = _rms(x2, nm_ref[...]).astype(BF16)
    acc = x2
    for c in range(0, D_FF, FF_CHUNK):
        u = jnp.maximum(_dot(h, wu_ref[:, c:c + FF_CHUNK]), 0.0)
        acc = acc + _dot((u * u).astype(BF16), wd_ref[c:c + FF_CHUNK, :])
    o_ref[...] = acc


def _post_attn(x2d, od, of, mk, mv, batch, w_out, norm_q, w_mem_q, mem_q_norm_w, w_mem_o,
               norm_mlp, w_up, w_down):
    n_tok = x2d.shape[0]
    tiles = n_tok // batch // POST_TILE
    mem_len = mk.shape[0] // batch
    tok = lambda width: pl.BlockSpec((POST_TILE, width), lambda b, i: (b * tiles + i, 0))
    mem = pl.BlockSpec((mem_len, D_MODEL), lambda b, i: (b, 0))
    scale = MEM_HEAD_DIM ** -0.5
    square = _resident((D_MODEL, D_MODEL))
    vec = _resident((1, D_MODEL))
    return pl.pallas_call(
        _post_attn_body,
        grid=(batch, tiles),
        in_specs=[tok(D_MODEL), tok(GROUP_W), tok(GROUP_W), mem, mem, square, vec, square,
                  _resident((1, MEM_HEAD_DIM)), square, vec, _resident((D_MODEL, D_FF)),
                  _resident((D_FF, D_MODEL))],
        out_specs=tok(D_MODEL),
        out_shape=jax.ShapeDtypeStruct((n_tok, D_MODEL), F32),
        compiler_params=_params(("arbitrary", "arbitrary")),
        name="post_attn",
    )(x2d, od, of, mk, mv, w_out.astype(BF16), norm_q.reshape(1, -1), w_mem_q.astype(BF16),
      (mem_q_norm_w * scale).reshape(1, -1), w_mem_o.astype(BF16), norm_mlp.reshape(1, -1),
      w_up.astype(BF16), w_down.astype(BF16))


def kernel(x, mem, positions, norm_mix_w, w_in, b_forget, diff_q_norm_w, diff_k_norm_w,
           lambda_q1, lambda_k1, lambda_q2, lambda_k2, diff_subln_w, fox_q_norm_w,
           fox_k_norm_w, w_out, norm_mem_q_w, norm_mem_kv_w, w_mem_q, w_mem_kv,
           mem_q_norm_w, mem_k_norm_w, w_mem_o, norm_mlp_w, w_up, w_down):
    batch, seq, d = x.shape
    depth = w_in.shape[0]
    assert d == D_MODEL and seq % TOK_TILE == 0 and seq % POST_TILE == 0
    assert TOK_TILE % ATT_TILE == 0
    assert (batch * mem.shape[1]) % TOK_TILE == 0

    cos_t, sin_t = _rope_tables(positions)
    xc = x.reshape(batch * seq, d)
    mem2d = mem.reshape(-1, d)
    for l in range(depth):
        lam_init = 0.8 - 0.6 * math.exp(-0.3 * l)
        dqt, dk, dvt, fqt, fk, fvt, qaugt, kaug = _in_proj(
            xc, batch, norm_mix_w[l], w_in[l], cos_t, sin_t, diff_q_norm_w[l],
            diff_k_norm_w[l], fox_q_norm_w[l], fox_k_norm_w[l], b_forget[l])
        lam_params = [p[l].reshape(1, -1) for p in (lambda_q1, lambda_k1, lambda_q2, lambda_k2)]
        lam_params.append(diff_subln_w[l].reshape(-1, 1))
        od, of = _attention(dqt, dk, dvt, fqt, fk, fvt, qaugt, kaug, lam_params, batch, lam_init)
        mk, mv = _mem_kv(mem2d, norm_mem_kv_w[l], w_mem_kv[l], mem_k_norm_w[l])
        xc = _post_attn(xc, od, of, mk, mv, batch, w_out[l], norm_mem_q_w[l], w_mem_q[l],
                        mem_q_norm_w[l], w_mem_o[l], norm_mlp_w[l], w_up[l], w_down[l])
    return xc.reshape(batch, seq, d)
```

```python
import functools
import math
from typing import Any, NamedTuple

import jax
import jax.numpy as jnp
from jax import lax
from jax.experimental import pallas as pl
from jax.experimental.pallas import tpu as pltpu

F32 = jnp.float32
BF16 = jnp.bfloat16

D_MODEL = 1024
CHUNK = 64
HEAD_DIM = 64
DIFF_V_DIM = 2 * HEAD_DIM
N_FOX_HEADS = 8
GROUP_W = 512
ROPE_DIM = HEAD_DIM // 4
ROPE_THETA = 500000.0
N_MEM_HEADS = 4
MEM_HEAD_DIM = D_MODEL // N_MEM_HEADS
D_FF = 4 * D_MODEL
EPS = 1e-6
NEG_INF = -1e30
LOG2E = math.log2(math.e)

LANES = 128
MXU_DIM = 256
VMEM_LIMIT = 56 * 1024 * 1024

TOK_TILE = 1024
ATT_TILE = 256


def _params(sem):
    return pltpu.CompilerParams(dimension_semantics=sem, vmem_limit_bytes=VMEM_LIMIT)


def _resident(shape):
    return pl.BlockSpec(shape, lambda *_: (0,) * len(shape), pipeline_mode=pl.Buffered(1))


def _rms(x, w):
    ms = jnp.mean(x * x, axis=-1, keepdims=True)
    return (x * lax.rsqrt(ms + EPS)) * w


def _dot(a, b):
    return jnp.dot(a, b, preferred_element_type=F32)


def _dot_nt(a, b):
    return lax.dot_general(a, b, (((1,), (1,)), ((), ())), preferred_element_type=F32)


def _rope_body(pos_ref, freq_ref, cos_ref, sin_ref):
    ang = pos_ref[...] * freq_ref[...]
    cos_ref[...] = jnp.cos(ang)
    sin_ref[...] = jnp.sin(ang)


def _rope_tables(positions):
    n_tok = positions.size
    half = ROPE_DIM // 2
    inv_freq = ROPE_THETA ** (-jnp.arange(0, ROPE_DIM, 2, dtype=F32) / ROPE_DIM)
    out = jax.ShapeDtypeStruct((half, n_tok), F32)
    return pl.pallas_call(_rope_body, out_shape=(out, out), name="rope_tables")(
        positions.reshape(1, n_tok).astype(F32), inv_freq.reshape(half, 1))


def _log_sigmoid(x):
    return jnp.minimum(x, 0.0) - jnp.log1p(jnp.exp(-jnp.abs(x)))


def _split3(x):
    hi = x.astype(BF16)
    rem = x - hi.astype(F32)
    mid = rem.astype(BF16)
    lo = (rem - mid.astype(F32)).astype(BF16)
    return hi, mid, lo


BIAS_SLOT = LANES // N_FOX_HEADS


def _head_norm_t(t, gain, rope):
    half = ROPE_DIM // 2
    outs = []
    for r0 in range(0, t.shape[0], HEAD_DIM):
        th = t[r0:r0 + HEAD_DIM]
        ms = jnp.sum(th * th, axis=0, keepdims=True) * (1.0 / HEAD_DIM)
        th = (th * lax.rsqrt(ms + EPS)) * gain[r0:r0 + HEAD_DIM]
        if rope is not None:
            cos, sin = rope
            t1, t2 = th[:half], th[half:ROPE_DIM]
            th = jnp.concatenate(
                [t1 * cos - t2 * sin, t2 * cos + t1 * sin, th[ROPE_DIM:]], axis=0)
        outs.append(th)
    return jnp.concatenate(outs, axis=0)


def _in_proj_body(x_ref, nw_ref, wt_ref, wft_ref, cos_ref, sin_ref, gain_ref, bf_ref, dqt_ref,
                  dk_ref, dvt_ref, fqt_ref, fk_ref, fvt_ref, qaugt_ref, kaug_ref, carry_ref):
    @pl.when(pl.program_id(1) == 0)
    def _():
        carry_ref[...] = jnp.zeros_like(carry_ref)

    h = _rms(x_ref[...], nw_ref[...]).astype(BF16)
    rope = (cos_ref[...], sin_ref[...])
    g = GROUP_W

    def proj_t(group):
        r0 = group * GROUP_W
        return _dot_nt(wt_ref[r0:r0 + GROUP_W, :], h)

    def store_keys(k_ref, kt):
        for c in range(0, g, LANES):
            k_ref[:, c:c + LANES] = kt[c:c + LANES].T.astype(BF16)

    def store_values(vt_ref, vt):
        for t in range(TOK_TILE // ATT_TILE):
            vt_ref[t] = vt[:, t * ATT_TILE:(t + 1) * ATT_TILE].astype(BF16)

    dqt_ref[...] = _head_norm_t(proj_t(0), gain_ref[0:g], rope).astype(BF16)
    store_keys(dk_ref, _head_norm_t(proj_t(1), gain_ref[g:2 * g], rope))
    store_values(dvt_ref, proj_t(2))
    fqt_ref[...] = _head_norm_t(proj_t(3), gain_ref[2 * g:3 * g], None).astype(BF16)
    store_keys(fk_ref, _head_norm_t(proj_t(4), gain_ref[3 * g:4 * g], None))
    store_values(fvt_ref, proj_t(5))

    log_f = _log_sigmoid(_dot_nt(wft_ref[...], h) + bf_ref[...])
    r = lax.broadcasted_iota(jnp.int32, (MXU_DIM, MXU_DIM), 0)
    c = lax.broadcasted_iota(jnp.int32, (MXU_DIM, MXU_DIM), 1)
    triu = jnp.where(r <= c, 1.0, 0.0).astype(BF16)
    part = lax.broadcasted_iota(jnp.int32, (LANES, MXU_DIM), 0) % BIAS_SLOT
    carry = carry_ref[:, 0:1]
    for c0 in range(0, TOK_TILE, MXU_DIM):
        hi, mid, lo = _split3(log_f[:, c0:c0 + MXU_DIM])
        cum = (_dot(hi, triu) + _dot(mid, triu)) + _dot(lo, triu) + carry
        carry = cum[:, MXU_DIM - 1:MXU_DIM]
        hi, mid, lo = (t.astype(F32) for t in _split3(cum * LOG2E))
        kaug_t = jnp.where(part == 0, -hi, jnp.where(part == 1, -mid, jnp.where(
            part == 2, -lo, jnp.where(part < 6, 1.0, 0.0))))
        qaug_t = jnp.where(part < 3, 1.0, jnp.where(part == 3, hi, jnp.where(
            part == 4, mid, jnp.where(part == 5, lo, 0.0))))
        kaug_ref[c0:c0 + MXU_DIM, :] = kaug_t.T.astype(BF16)
        qaugt_ref[:, c0:c0 + MXU_DIM] = qaug_t.astype(BF16)
    carry_ref[...] = jnp.broadcast_to(carry, carry_ref.shape)


def _in_proj(x2d, batch, norm_w, w_in, cos_t, sin_t, dqw, dkw, fqw, fkw, b_forget):
    n_tok = x2d.shape[0]
    tiles = n_tok // batch // TOK_TILE
    g = GROUP_W
    wt = w_in[:, :6 * g].T.astype(BF16)
    wft = jnp.repeat(w_in[:, 6 * g:6 * g + N_FOX_HEADS].T, BIAS_SLOT, axis=0).astype(BF16)
    bf = jnp.repeat(b_forget, BIAS_SLOT).reshape(LANES, 1)
    qscale = HEAD_DIM ** -0.5 * LOG2E
    gains = jnp.concatenate([jnp.tile(v, g // HEAD_DIM) for v in
                             (dqw * qscale, dkw, fqw * qscale, fkw)]).reshape(4 * g, 1)

    row = lambda b, i: (b * tiles + i, 0)
    col = lambda b, i: (0, b * tiles + i)
    tok = lambda width: pl.BlockSpec((TOK_TILE, width), row)
    tok_t = lambda rows: pl.BlockSpec((rows, TOK_TILE), col)
    slabs = TOK_TILE // ATT_TILE
    vt_spec = pl.BlockSpec((slabs, g, ATT_TILE), lambda b, i: (b * tiles + i, 0, 0))
    k_shape = jax.ShapeDtypeStruct((n_tok, g), BF16)
    qt_shape = jax.ShapeDtypeStruct((g, n_tok), BF16)
    vt_shape = jax.ShapeDtypeStruct((n_tok // ATT_TILE, g, ATT_TILE), BF16)
    half = ROPE_DIM // 2
    return pl.pallas_call(
        _in_proj_body,
        grid=(batch, tiles),
        in_specs=[tok(D_MODEL), _resident((1, D_MODEL)), _resident(wt.shape),
                  _resident(wft.shape), tok_t(half), tok_t(half), _resident(gains.shape),
                  _resident(bf.shape)],
        out_specs=[tok_t(g), tok(g), vt_spec, tok_t(g), tok(g), vt_spec, tok_t(LANES),
                   tok(LANES)],
        out_shape=[qt_shape, k_shape, vt_shape, qt_shape, k_shape, vt_shape,
                   jax.ShapeDtypeStruct((LANES, n_tok), BF16),
                   jax.ShapeDtypeStruct((n_tok, LANES), BF16)],
        scratch_shapes=[pltpu.VMEM((LANES, LANES), F32)],
        compiler_params=_params(("arbitrary", "arbitrary")),
        name="in_proj",
    )(x2d, norm_w.reshape(1, -1), wt, wft, cos_t, sin_t, gains, bf)


N_CHAINS = 2 * (GROUP_W // LANES)
SUM_ROWS = 16


class _Group(NamedTuple):
    fox: bool
    qt: tuple
    k: Any
    vt: Any
    qaug: Any
    kaug: Any
    out: Any
    rhs: Any
    s: tuple
    bmax: Any
    m: Any
    acc: Any

    @property
    def width(self):
        return HEAD_DIM if self.fox else DIFF_V_DIM


def _attn_scratch(fox):
    width = HEAD_DIM if fox else DIFF_V_DIM
    rhs_w = 2 * LANES if fox else LANES
    return [pltpu.VMEM((2, N_CHAINS, rhs_w, ATT_TILE), BF16),
            pltpu.VMEM((N_CHAINS, ATT_TILE, ATT_TILE), F32),
            pltpu.VMEM((N_CHAINS, ATT_TILE, ATT_TILE), F32),
            pltpu.VMEM((2, N_CHAINS, ATT_TILE), F32),
            pltpu.VMEM((2, N_CHAINS, ATT_TILE), F32),
            pltpu.VMEM((2, N_CHAINS, width + SUM_ROWS, ATT_TILE), F32)]


def _attn_body(*refs, lam_init, half, static_steps):
    s = pl.program_id(1)
    for k in static_steps:
        @pl.when(s == k)
        def _(k=k):
            _attn_step(k, *refs, lam_init=lam_init, half=half)

    generic = [k for k in range(half) if k not in static_steps]
    if generic:
        @pl.when(s <= max(generic))
        def _():
            _attn_step(s, *refs, lam_init=lam_init, half=half)


def _attn_step(s, dqa, dqb, dk, dvt, fqa, fqb, fk, fvt, qauga, qaugb, kaug, lq1_ref, lk1_ref,
               lq2_ref, lk2_ref, subw_ref, od_ref, of_ref, *scratch, lam_init, half):
    def group(fox, qt, k, vt, qaug, kaug_ref, out, scr):
        rhs, s0, s1, bmax, m, acc = scr
        return _Group(fox, qt, k, vt, qaug, kaug_ref, out, rhs, (s0, s1), bmax, m, acc)

    groups = (group(False, (dqa, dqb), dk, dvt, None, None, od_ref, scratch[:6]),
              group(True, (fqa, fqb), fk, fvt, (qauga, qaugb), kaug, of_ref, scratch[6:]))
    static = isinstance(s, int)
    last = 2 * s + half + 1
    ones_rows = jnp.ones((SUM_ROWS, ATT_TILE), BF16)

    half_zero = jnp.zeros((HEAD_DIM, ATT_TILE), BF16)
    for gr in groups:
        for tile, qt_ref in enumerate(gr.qt):
            for g in range(GROUP_W // LANES):
                r0 = g * LANES
                gr.rhs[tile, 2 * g, :HEAD_DIM, :] = qt_ref[r0:r0 + HEAD_DIM, :]
                gr.rhs[tile, 2 * g, HEAD_DIM:LANES, :] = half_zero
                gr.rhs[tile, 2 * g + 1, :HEAD_DIM, :] = half_zero
                gr.rhs[tile, 2 * g + 1, HEAD_DIM:LANES, :] = qt_ref[r0 + HEAD_DIM:r0 + LANES, :]
            if gr.fox:
                for c in range(N_CHAINS):
                    b0 = c * BIAS_SLOT
                    gr.rhs[tile, c, LANES:, :] = jnp.zeros((LANES, ATT_TILE), BF16)
                    gr.rhs[tile, c, LANES + b0:LANES + b0 + BIAS_SLOT, :] = (
                        gr.qaug[tile][b0:b0 + BIAS_SLOT, :])
        gr.m[...] = jnp.full(gr.m.shape, NEG_INF, F32)
        gr.acc[...] = jnp.zeros(gr.acc.shape, F32)

    def locate(p):
        if static:
            return (1, p - s - 1) if p > s else (0, s - p)
        in_b = p > s
        return in_b.astype(jnp.int32), jnp.where(in_b, p - s - 1, s - p)

    def diagonal_mask(gr):
        div = 1 if gr.fox else CHUNK
        k_id = lax.broadcasted_iota(jnp.int32, (ATT_TILE, ATT_TILE), 0) // div
        q_id = lax.broadcasted_iota(jnp.int32, (ATT_TILE, ATT_TILE), 1) // div
        return k_id <= q_id

    def scores(gr, p, slot, g, masked):
        tile, blk = locate(p)
        start = blk * ATT_TILE
        rows = pl.ds(start if static else pl.multiple_of(start, ATT_TILE), ATT_TILE)
        lhs = gr.k[rows, g * LANES:(g + 1) * LANES]
        if gr.fox:
            lhs = jnp.concatenate([lhs, gr.kaug[rows, :]], axis=-1)
        for c in (2 * g, 2 * g + 1):
            sc = _dot(lhs, gr.rhs[tile, c])
            if masked:
                sc = jnp.where(diagonal_mask(gr), sc, NEG_INF)
            gr.s[slot][c] = sc
            gr.bmax[slot, c:c + 1, :] = jnp.max(sc, axis=0, keepdims=True)

    def softmax_pv(gr, p, slot, c):
        tile, blk = locate(p)
        sc = gr.s[slot][c]
        m_old = gr.m[tile, c:c + 1, :]
        m_new = jnp.maximum(m_old, gr.bmax[slot, c:c + 1, :])
        alpha = jnp.exp2(m_old - m_new)
        prob = jnp.exp2(sc - m_new)
        gr.m[tile, c:c + 1, :] = m_new
        r0 = c * HEAD_DIM if gr.fox else (c // 2) * LANES
        vt = jnp.concatenate([gr.vt[blk, r0:r0 + gr.width, :], ones_rows], axis=0)
        gr.acc[tile, c] = alpha * gr.acc[tile, c] + _dot(vt, prob.astype(BF16))

    def step(score_args, soft_args, masked=False):
        for g in range(GROUP_W // LANES):
            for gr in groups:
                if soft_args is not None:
                    softmax_pv(gr, *soft_args, 2 * g)
                if score_args is not None:
                    scores(gr, *score_args, g, masked)
                if soft_args is not None:
                    softmax_pv(gr, *soft_args, 2 * g + 1)

    def pair(u, carry):
        p = 2 * u + 1
        step((p + 1, 0), (p, 1))
        step((p + 2, 1), (p + 1, 0))
        return carry

    step((0, 0), None, masked=True)
    step((1, 1), (0, 0))
    if static:
        for u in range((last - 3) // 2):
            pair(u, 0)
    else:
        lax.fori_loop(0, (last - 3) // 2, pair, 0)
    step((last - 1, 0), (last - 2, 1))
    step((last, 1), (last - 1, 0), masked=True)
    step(None, (last, 1))

    lam = (jnp.exp(jnp.sum(lq1_ref[...] * lk1_ref[...], axis=-1, keepdims=True))
           - jnp.exp(jnp.sum(lq2_ref[...] * lk2_ref[...], axis=-1, keepdims=True))
           + lam_init)
    for gr in groups:
        width = gr.width
        for tile in range(2):
            for g in range(GROUP_W // LANES):
                a1, a2 = gr.acc[tile, 2 * g], gr.acc[tile, 2 * g + 1]
                o1 = a1[:width] * (1.0 / a1[width:width + 1])
                o2 = a2[:width] * (1.0 / a2[width:width + 1])
                if gr.fox:
                    o_t = jnp.concatenate([o1, o2], axis=0)
                else:
                    o_t = o1 - lam * o2
                    ms = jnp.mean(o_t * o_t, axis=0, keepdims=True)
                    o_t = ((o_t * lax.rsqrt(ms + EPS)) * subw_ref[...]) * (1.0 - lam_init)
                gr.out[0, tile, 0, :, g * LANES:(g + 1) * LANES] = o_t.T.astype(BF16)


def _attention(dqt, dk, dvt, fqt, fk, fvt, qaugt, kaug, lam_params, batch, lam_init):
    n_tok = dk.shape[0]
    seq = n_tok // batch
    tiles = seq // ATT_TILE
    assert tiles % 2 == 0
    half = tiles // 2
    qt_spec = lambda r, off: pl.BlockSpec((r, ATT_TILE), lambda b, s: (0, b * tiles + s + off))
    kv_spec = lambda w: pl.BlockSpec((seq, w), lambda b, s: (b, 0))
    vt_spec = pl.BlockSpec((tiles, GROUP_W, ATT_TILE), lambda b, s: (b, 0, 0))
    qkv_specs = [qt_spec(GROUP_W, 0), qt_spec(GROUP_W, half), kv_spec(GROUP_W), vt_spec]
    out_spec = pl.BlockSpec((1, 2, 1, ATT_TILE, GROUP_W), lambda b, s: (b, 0, s, 0, 0))
    out_shape = jax.ShapeDtypeStruct((batch, 2, half, ATT_TILE, GROUP_W), BF16)
    od, of = pl.pallas_call(
        functools.partial(_attn_body, lam_init=lam_init, half=half, static_steps=(half - 1,)),
        grid=(batch, half),
        in_specs=qkv_specs + qkv_specs
        + [qt_spec(LANES, 0), qt_spec(LANES, half), kv_spec(LANES)]
        + [_resident(p.shape) for p in lam_params],
        out_specs=[out_spec, out_spec],
        out_shape=[out_shape, out_shape],
        scratch_shapes=_attn_scratch(False) + _attn_scratch(True),
        compiler_params=_params(("arbitrary", "arbitrary")),
        name="attention",
    )(dqt, dqt, dk, dvt, fqt, fqt, fk, fvt, qaugt, qaugt, kaug, *lam_params)
    return od.reshape(n_tok, GROUP_W), of.reshape(n_tok, GROUP_W)


def _mem_kv_body(m_ref, nw_ref, w_ref, kw_ref, mk_ref, mv_ref):
    hm = _rms(m_ref[...], nw_ref[...]).astype(BF16)
    kv = _dot(hm, w_ref[...])
    kw = kw_ref[...]
    for c in range(0, D_MODEL, MEM_HEAD_DIM):
        mk_ref[:, c:c + MEM_HEAD_DIM] = _rms(kv[:, c:c + MEM_HEAD_DIM], kw).astype(BF16)
    mv_ref[...] = kv[:, D_MODEL:].astype(BF16)


def _mem_kv(mem2d, norm_w, w_mem_kv, mem_k_norm_w):
    n_mem = mem2d.shape[0]
    tok = pl.BlockSpec((TOK_TILE, D_MODEL), lambda i: (i, 0))
    out = jax.ShapeDtypeStruct((n_mem, D_MODEL), BF16)
    return pl.pallas_call(
        _mem_kv_body,
        grid=(n_mem // TOK_TILE,),
        in_specs=[tok, _resident((1, D_MODEL)), _resident((D_MODEL, 2 * D_MODEL)),
                  _resident((1, MEM_HEAD_DIM))],
        out_specs=[tok, tok],
        out_shape=[out, out],
        compiler_params=_params(("arbitrary",)),
        name="mem_kv",
    )(mem2d, norm_w.reshape(1, -1), w_mem_kv.astype(BF16), mem_k_norm_w.reshape(1, -1))


POST_TILE = 512
FF_CHUNK = 1024


def _post_attn_body(x_ref, od_ref, of_ref, mk_ref, mv_ref, wo_ref, nq_ref, wq_ref, qw_ref,
                    wmo_ref, nm_ref, wu_ref, wd_ref, o_ref):
    x1 = x_ref[...] + (_dot(od_ref[...], wo_ref[:GROUP_W, :])
                       + _dot(of_ref[...], wo_ref[GROUP_W:, :]))

    mq = _dot(_rms(x1, nq_ref[...]).astype(BF16), wq_ref[...])
    qw = qw_ref[...]
    heads = []
    for c in range(0, D_MODEL, MEM_HEAD_DIM):
        q = _rms(mq[:, c:c + MEM_HEAD_DIM], qw).astype(BF16)
        s = _dot_nt(q, mk_ref[:, c:c + MEM_HEAD_DIM])
        p = jnp.exp(s - jnp.max(s, axis=-1, keepdims=True))
        l = jnp.sum(p, axis=-1, keepdims=True)
        heads.append((_dot(p.astype(BF16), mv_ref[:, c:c + MEM_HEAD_DIM]) / l).astype(BF16))
    x2 = x1 + _dot(jnp.concatenate(heads, axis=-1), wmo_ref[...])

    h = _rms(x2, nm_ref[...]).astype(BF16)
    acc = x2
    for c in range(0, D_FF, FF_CHUNK):
        u = jnp.maximum(_dot(h, wu_ref[:, c:c + FF_CHUNK]), 0.0)
        acc = acc + _dot((u * u).astype(BF16), wd_ref[c:c + FF_CHUNK, :])
    o_ref[...] = acc


def _post_attn(x2d, od, of, mk, mv, batch, w_out, norm_q, w_mem_q, mem_q_norm_w, w_mem_o,
               norm_mlp, w_up, w_down):
    n_tok = x2d.shape[0]
    tiles = n_tok // batch // POST_TILE
    mem_len = mk.shape[0] // batch
    tok = lambda width: pl.BlockSpec((POST_TILE, width), lambda b, i: (b * tiles + i, 0))
    mem = pl.BlockSpec((mem_len, D_MODEL), lambda b, i: (b, 0))
    scale = MEM_HEAD_DIM ** -0.5
    square = _resident((D_MODEL, D_MODEL))
    vec = _resident((1, D_MODEL))
    return pl.pallas_call(
        _post_attn_body,
        grid=(batch, tiles),
        in_specs=[tok(D_MODEL), tok(GROUP_W), tok(GROUP_W), mem, mem, square, vec, square,
                  _resident((1, MEM_HEAD_DIM)), square, vec, _resident((D_MODEL, D_FF)),
                  _resident((D_FF, D_MODEL))],
        out_specs=tok(D_MODEL),
        out_shape=jax.ShapeDtypeStruct((n_tok, D_MODEL), F32),
        compiler_params=_params(("arbitrary", "arbitrary")),
        name="post_attn",
    )(x2d, od, of, mk, mv, w_out.astype(BF16), norm_q.reshape(1, -1), w_mem_q.astype(BF16),
      (mem_q_norm_w * scale).reshape(1, -1), w_mem_o.astype(BF16), norm_mlp.reshape(1, -1),
      w_up.astype(BF16), w_down.astype(BF16))


def kernel(x, mem, positions, norm_mix_w, w_in, b_forget, diff_q_norm_w, diff_k_norm_w,
           lambda_q1, lambda_k1, lambda_q2, lambda_k2, diff_subln_w, fox_q_norm_w,
           fox_k_norm_w, w_out, norm_mem_q_w, norm_mem_kv_w, w_mem_q, w_mem_kv,
           mem_q_norm_w, mem_k_norm_w, w_mem_o, norm_mlp_w, w_up, w_down):
    batch, seq, d = x.shape
    depth = w_in.shape[0]
    assert d == D_MODEL and seq % TOK_TILE == 0 and seq % POST_TILE == 0
    assert TOK_TILE % ATT_TILE == 0
    assert (batch * mem.shape[1]) % TOK_TILE == 0

    cos_t, sin_t = _rope_tables(positions)
    xc = x.reshape(batch * seq, d)
    mem2d = mem.reshape(-1, d)
    for l in range(depth):
        lam_init = 0.8 - 0.6 * math.exp(-0.3 * l)
        dqt, dk, dvt, fqt, fk, fvt, qaugt, kaug = _in_proj(
            xc, batch, norm_mix_w[l], w_in[l], cos_t, sin_t, diff_q_norm_w[l],
            diff_k_norm_w[l], fox_q_norm_w[l], fox_k_norm_w[l], b_forget[l])
        lam_params = [p[l].reshape(1, -1) for p in (lambda_q1, lambda_k1, lambda_q2, lambda_k2)]
        lam_params.append(diff_subln_w[l].reshape(-1, 1))
        od, of = _attention(dqt, dk, dvt, fqt, fk, fvt, qaugt, kaug, lam_params, batch, lam_init)
        mk, mv = _mem_kv(mem2d, norm_mem_kv_w[l], w_mem_kv[l], mem_k_norm_w[l])
        xc = _post_attn(xc, od, of, mk, mv, batch, w_out[l], norm_mem_q_w[l], w_mem_q[l],
                        mem_q_norm_w[l], w_mem_o[l], norm_mlp_w[l], w_up[l], w_down[l])
    return xc.reshape(batch, seq, d)
```

```python
import functools
import math
from typing import Any, NamedTuple

import jax
import jax.numpy as jnp
from jax import lax
from jax.experimental import pallas as pl
from jax.experimental.pallas import tpu as pltpu

F32 = jnp.float32
BF16 = jnp.bfloat16

D_MODEL = 1024
CHUNK = 64
HEAD_DIM = 64
DIFF_V_DIM = 2 * HEAD_DIM
N_FOX_HEADS = 8
GROUP_W = 512
ROPE_DIM = HEAD_DIM // 4
ROPE_THETA = 500000.0
N_MEM_HEADS = 4
MEM_HEAD_DIM = D_MODEL // N_MEM_HEADS
D_FF = 4 * D_MODEL
EPS = 1e-6
NEG_INF = -1e30
LOG2E = math.log2(math.e)

LANES = 128
MXU_DIM = 256
VMEM_LIMIT = 58 * 1024 * 1024

TOK_TILE = 1024
ATT_TILE = 256


def _params(sem):
    return pltpu.CompilerParams(dimension_semantics=sem, vmem_limit_bytes=VMEM_LIMIT)


def _resident(shape):
    return pl.BlockSpec(shape, lambda *_: (0,) * len(shape), pipeline_mode=pl.Buffered(1))


def _rms(x, w):
    ms = jnp.mean(x * x, axis=-1, keepdims=True)
    return (x * lax.rsqrt(ms + EPS)) * w


def _dot(a, b):
    return jnp.dot(a, b, preferred_element_type=F32)


def _dot_nt(a, b):
    return lax.dot_general(a, b, (((1,), (1,)), ((), ())), preferred_element_type=F32)


def _log_sigmoid(x):
    return jnp.minimum(x, 0.0) - jnp.log1p(jnp.exp(-jnp.abs(x)))


def _split3(x):
    hi = x.astype(BF16)
    rem = x - hi.astype(F32)
    mid = rem.astype(BF16)
    lo = (rem - mid.astype(F32)).astype(BF16)
    return hi, mid, lo


BIAS_SLOT = LANES // N_FOX_HEADS


def _head_norm_t(t, gain, rope):
    half = ROPE_DIM // 2
    outs = []
    for r0 in range(0, t.shape[0], HEAD_DIM):
        th = t[r0:r0 + HEAD_DIM]
        ms = jnp.sum(th * th, axis=0, keepdims=True) * (1.0 / HEAD_DIM)
        th = (th * lax.rsqrt(ms + EPS)) * gain[r0:r0 + HEAD_DIM]
        if rope is not None:
            cos, sin = rope
            t1, t2 = th[:half], th[half:ROPE_DIM]
            th = jnp.concatenate(
                [t1 * cos - t2 * sin, t2 * cos + t1 * sin, th[ROPE_DIM:]], axis=0)
        outs.append(th)
    return jnp.concatenate(outs, axis=0)


def _in_proj_body(x_ref, nw_ref, wt_ref, wft_ref, pos_ref, freq_ref, gain_ref, bf_ref, dqt_ref,
                  dk_ref, dvt_ref, fqt_ref, fk_ref, fvt_ref, qaugt_ref, kaug_ref, carry_ref):
    @pl.when(pl.program_id(1) == 0)
    def _():
        carry_ref[...] = jnp.zeros_like(carry_ref)

    h = _rms(x_ref[...], nw_ref[...]).astype(BF16)
    ang = pos_ref[...] * freq_ref[...]
    rope = (jnp.cos(ang), jnp.sin(ang))
    g = GROUP_W

    def proj_t(group):
        r0 = group * GROUP_W
        return _dot_nt(wt_ref[r0:r0 + GROUP_W, :], h)

    def store_keys(k_ref, kt):
        for c in range(0, g, LANES):
            k_ref[:, c:c + LANES] = kt[c:c + LANES].T.astype(BF16)

    def store_values(vt_ref, vt):
        for t in range(TOK_TILE // ATT_TILE):
            vt_ref[t] = vt[:, t * ATT_TILE:(t + 1) * ATT_TILE].astype(BF16)

    dqt_ref[...] = _head_norm_t(proj_t(0), gain_ref[0:g], rope).astype(BF16)
    store_keys(dk_ref, _head_norm_t(proj_t(1), gain_ref[g:2 * g], rope))
    store_values(dvt_ref, proj_t(2))
    fqt_ref[...] = _head_norm_t(proj_t(3), gain_ref[2 * g:3 * g], None).astype(BF16)
    store_keys(fk_ref, _head_norm_t(proj_t(4), gain_ref[3 * g:4 * g], None))
    store_values(fvt_ref, proj_t(5))

    log_f = _log_sigmoid(_dot_nt(wft_ref[...], h) + bf_ref[...])
    r = lax.broadcasted_iota(jnp.int32, (MXU_DIM, MXU_DIM), 0)
    c = lax.broadcasted_iota(jnp.int32, (MXU_DIM, MXU_DIM), 1)
    triu = jnp.where(r <= c, 1.0, 0.0).astype(BF16)
    part = lax.broadcasted_iota(jnp.int32, (LANES, MXU_DIM), 0) % BIAS_SLOT
    carry = carry_ref[:, 0:1]
    for c0 in range(0, TOK_TILE, MXU_DIM):
        hi, mid, lo = _split3(log_f[:, c0:c0 + MXU_DIM])
        cum = (_dot(hi, triu) + _dot(mid, triu)) + _dot(lo, triu) + carry
        carry = cum[:, MXU_DIM - 1:MXU_DIM]
        hi, mid, lo = (t.astype(F32) for t in _split3(cum * LOG2E))
        kaug_t = jnp.where(part == 0, -hi, jnp.where(part == 1, -mid, jnp.where(
            part == 2, -lo, jnp.where(part < 6, 1.0, 0.0))))
        qaug_t = jnp.where(part < 3, 1.0, jnp.where(part == 3, hi, jnp.where(
            part == 4, mid, jnp.where(part == 5, lo, 0.0))))
        kaug_ref[c0:c0 + MXU_DIM, :] = kaug_t.T.astype(BF16)
        qaugt_ref[:, c0:c0 + MXU_DIM] = qaug_t.astype(BF16)
    carry_ref[...] = jnp.broadcast_to(carry, carry_ref.shape)


def _in_proj(x2d, batch, norm_w, w_in, positions, dqw, dkw, fqw, fkw, b_forget):
    n_tok = x2d.shape[0]
    tiles = n_tok // batch // TOK_TILE
    g = GROUP_W
    wt = w_in[:, :6 * g].T.astype(BF16)
    wft = jnp.repeat(w_in[:, 6 * g:6 * g + N_FOX_HEADS].T, BIAS_SLOT, axis=0).astype(BF16)
    bf = jnp.repeat(b_forget, BIAS_SLOT).reshape(LANES, 1)
    qscale = HEAD_DIM ** -0.5 * LOG2E
    gains = jnp.concatenate([jnp.tile(v, g // HEAD_DIM) for v in
                             (dqw * qscale, dkw, fqw * qscale, fkw)]).reshape(4 * g, 1)

    row = lambda b, i: (b * tiles + i, 0)
    col = lambda b, i: (0, b * tiles + i)
    tok = lambda width: pl.BlockSpec((TOK_TILE, width), row)
    tok_t = lambda rows: pl.BlockSpec((rows, TOK_TILE), col)
    slabs = TOK_TILE // ATT_TILE
    vt_spec = pl.BlockSpec((slabs, g, ATT_TILE), lambda b, i: (b * tiles + i, 0, 0))
    k_shape = jax.ShapeDtypeStruct((n_tok, g), BF16)
    qt_shape = jax.ShapeDtypeStruct((g, n_tok), BF16)
    vt_shape = jax.ShapeDtypeStruct((n_tok // ATT_TILE, g, ATT_TILE), BF16)
    half = ROPE_DIM // 2
    inv_freq = ROPE_THETA ** (-jnp.arange(0, ROPE_DIM, 2, dtype=F32) / ROPE_DIM)
    pos_row = positions.reshape(1, n_tok).astype(F32)
    return pl.pallas_call(
        _in_proj_body,
        grid=(batch, tiles),
        in_specs=[tok(D_MODEL), _resident((1, D_MODEL)), _resident(wt.shape),
                  _resident(wft.shape), tok_t(1), _resident((half, 1)), _resident(gains.shape),
                  _resident(bf.shape)],
        out_specs=[tok_t(g), tok(g), vt_spec, tok_t(g), tok(g), vt_spec, tok_t(LANES),
                   tok(LANES)],
        out_shape=[qt_shape, k_shape, vt_shape, qt_shape, k_shape, vt_shape,
                   jax.ShapeDtypeStruct((LANES, n_tok), BF16),
                   jax.ShapeDtypeStruct((n_tok, LANES), BF16)],
        scratch_shapes=[pltpu.VMEM((LANES, LANES), F32)],
        compiler_params=_params(("arbitrary", "arbitrary")),
        name="in_proj",
    )(x2d, norm_w.reshape(1, -1), wt, wft, pos_row, inv_freq.reshape(half, 1), gains, bf)


N_CHAINS = 2 * (GROUP_W // LANES)
SUM_ROWS = 16


class _Group(NamedTuple):
    fox: bool
    qt: tuple
    k: Any
    vt: Any
    qaug: Any
    kaug: Any
    out: Any
    rhs: Any
    s: tuple
    bmax: Any
    m: Any
    acc: Any

    @property
    def width(self):
        return HEAD_DIM if self.fox else DIFF_V_DIM


def _attn_scratch(fox):
    width = HEAD_DIM if fox else DIFF_V_DIM
    rhs_w = 2 * LANES if fox else LANES
    return [pltpu.VMEM((2, N_CHAINS, rhs_w, ATT_TILE), BF16),
            pltpu.VMEM((N_CHAINS, ATT_TILE, ATT_TILE), F32),
            pltpu.VMEM((N_CHAINS, ATT_TILE, ATT_TILE), F32),
            pltpu.VMEM((2, N_CHAINS, ATT_TILE), F32),
            pltpu.VMEM((2, N_CHAINS, ATT_TILE), F32),
            pltpu.VMEM((2, N_CHAINS, width + SUM_ROWS, ATT_TILE), F32)]


def _attn_body(*refs, lam_init, half, static_steps):
    s = pl.program_id(1)
    for k in static_steps:
        @pl.when(s == k)
        def _(k=k):
            _attn_step(k, *refs, lam_init=lam_init, half=half)

    generic = [k for k in range(half) if k not in static_steps]
    if generic:
        @pl.when(s <= max(generic))
        def _():
            _attn_step(s, *refs, lam_init=lam_init, half=half)


def _attn_step(s, dqa, dqb, dk, dvt, fqa, fqb, fk, fvt, qauga, qaugb, kaug, lq1_ref, lk1_ref,
               lq2_ref, lk2_ref, subw_ref, od_ref, of_ref, *scratch, lam_init, half):
    def group(fox, qt, k, vt, qaug, kaug_ref, out, scr):
        rhs, s0, s1, bmax, m, acc = scr
        return _Group(fox, qt, k, vt, qaug, kaug_ref, out, rhs, (s0, s1), bmax, m, acc)

    groups = (group(False, (dqa, dqb), dk, dvt, None, None, od_ref, scratch[:6]),
              group(True, (fqa, fqb), fk, fvt, (qauga, qaugb), kaug, of_ref, scratch[6:]))
    static = isinstance(s, int)
    last = 2 * s + half + 1
    ones_rows = jnp.ones((SUM_ROWS, ATT_TILE), BF16)

    half_zero = jnp.zeros((HEAD_DIM, ATT_TILE), BF16)
    for gr in groups:
        for tile, qt_ref in enumerate(gr.qt):
            for g in range(GROUP_W // LANES):
                r0 = g * LANES
                gr.rhs[tile, 2 * g, :HEAD_DIM, :] = qt_ref[r0:r0 + HEAD_DIM, :]
                gr.rhs[tile, 2 * g, HEAD_DIM:LANES, :] = half_zero
                gr.rhs[tile, 2 * g + 1, :HEAD_DIM, :] = half_zero
                gr.rhs[tile, 2 * g + 1, HEAD_DIM:LANES, :] = qt_ref[r0 + HEAD_DIM:r0 + LANES, :]
            if gr.fox:
                for c in range(N_CHAINS):
                    b0 = c * BIAS_SLOT
                    gr.rhs[tile, c, LANES:, :] = jnp.zeros((LANES, ATT_TILE), BF16)
                    gr.rhs[tile, c, LANES + b0:LANES + b0 + BIAS_SLOT, :] = (
                        gr.qaug[tile][b0:b0 + BIAS_SLOT, :])
        gr.m[...] = jnp.full(gr.m.shape, NEG_INF, F32)
        gr.acc[...] = jnp.zeros(gr.acc.shape, F32)

    def locate(p):
        if static:
            return (1, p - s - 1) if p > s else (0, s - p)
        in_b = p > s
        return in_b.astype(jnp.int32), jnp.where(in_b, p - s - 1, s - p)

    def diagonal_mask(gr):
        div = 1 if gr.fox else CHUNK
        k_id = lax.broadcasted_iota(jnp.int32, (ATT_TILE, ATT_TILE), 0) // div
        q_id = lax.broadcasted_iota(jnp.int32, (ATT_TILE, ATT_TILE), 1) // div
        return k_id <= q_id

    def scores(gr, p, slot, g, masked):
        tile, blk = locate(p)
        start = blk * ATT_TILE
        rows = pl.ds(start if static else pl.multiple_of(start, ATT_TILE), ATT_TILE)
        lhs = gr.k[rows, g * LANES:(g + 1) * LANES]
        if gr.fox:
            lhs = jnp.concatenate([lhs, gr.kaug[rows, :]], axis=-1)
        for c in (2 * g, 2 * g + 1):
            sc = _dot(lhs, gr.rhs[tile, c])
            if masked:
                sc = jnp.where(diagonal_mask(gr), sc, NEG_INF)
            gr.s[slot][c] = sc
            gr.bmax[slot, c:c + 1, :] = jnp.max(sc, axis=0, keepdims=True)

    def softmax_pv(gr, p, slot, c):
        tile, blk = locate(p)
        sc = gr.s[slot][c]
        m_old = gr.m[tile, c:c + 1, :]
        m_new = jnp.maximum(m_old, gr.bmax[slot, c:c + 1, :])
        alpha = jnp.exp2(m_old - m_new)
        prob = jnp.exp2(sc - m_new)
        gr.m[tile, c:c + 1, :] = m_new
        r0 = c * HEAD_DIM if gr.fox else (c // 2) * LANES
        vt = jnp.concatenate([gr.vt[blk, r0:r0 + gr.width, :], ones_rows], axis=0)
        gr.acc[tile, c] = alpha * gr.acc[tile, c] + _dot(vt, prob.astype(BF16))

    def step(score_args, soft_args, masked=False):
        for g in range(GROUP_W // LANES):
            for gr in groups:
                if soft_args is not None:
                    softmax_pv(gr, *soft_args, 2 * g)
                if score_args is not None:
                    scores(gr, *score_args, g, masked)
                if soft_args is not None:
                    softmax_pv(gr, *soft_args, 2 * g + 1)

    def pair(u, carry):
        p = 2 * u + 1
        step((p + 1, 0), (p, 1))
        step((p + 2, 1), (p + 1, 0))
        return carry

    step((0, 0), None, masked=True)
    step((1, 1), (0, 0))
    if static:
        for u in range((last - 3) // 2):
            pair(u, 0)
    else:
        lax.fori_loop(0, (last - 3) // 2, pair, 0)
    step((last - 1, 0), (last - 2, 1))
    step((last, 1), (last - 1, 0), masked=True)
    step(None, (last, 1))

    lam = (jnp.exp(jnp.sum(lq1_ref[...] * lk1_ref[...], axis=-1, keepdims=True))
           - jnp.exp(jnp.sum(lq2_ref[...] * lk2_ref[...], axis=-1, keepdims=True))
           + lam_init)
    for gr in groups:
        width = gr.width
        for tile in range(2):
            for g in range(GROUP_W // LANES):
                a1, a2 = gr.acc[tile, 2 * g], gr.acc[tile, 2 * g + 1]
                o1 = a1[:width] * (1.0 / a1[width:width + 1])
                o2 = a2[:width] * (1.0 / a2[width:width + 1])
                if gr.fox:
                    o_t = jnp.concatenate([o1, o2], axis=0)
                else:
                    o_t = o1 - lam * o2
                    ms = jnp.mean(o_t * o_t, axis=0, keepdims=True)
                    o_t = ((o_t * lax.rsqrt(ms + EPS)) * subw_ref[...]) * (1.0 - lam_init)
                gr.out[0, tile, 0, :, g * LANES:(g + 1) * LANES] = o_t.T.astype(BF16)


def _attention(dqt, dk, dvt, fqt, fk, fvt, qaugt, kaug, lam_params, batch, lam_init):
    n_tok = dk.shape[0]
    seq = n_tok // batch
    tiles = seq // ATT_TILE
    assert tiles % 2 == 0
    half = tiles // 2
    qt_spec = lambda r, off: pl.BlockSpec((r, ATT_TILE), lambda b, s: (0, b * tiles + s + off))
    kv_spec = lambda w: pl.BlockSpec((seq, w), lambda b, s: (b, 0))
    vt_spec = pl.BlockSpec((tiles, GROUP_W, ATT_TILE), lambda b, s: (b, 0, 0))
    qkv_specs = [qt_spec(GROUP_W, 0), qt_spec(GROUP_W, half), kv_spec(GROUP_W), vt_spec]
    out_spec = pl.BlockSpec((1, 2, 1, ATT_TILE, GROUP_W), lambda b, s: (b, 0, s, 0, 0))
    out_shape = jax.ShapeDtypeStruct((batch, 2, half, ATT_TILE, GROUP_W), BF16)
    od, of = pl.pallas_call(
        functools.partial(_attn_body, lam_init=lam_init, half=half, static_steps=(half - 1,)),
        grid=(batch, half),
        in_specs=qkv_specs + qkv_specs
        + [qt_spec(LANES, 0), qt_spec(LANES, half), kv_spec(LANES)]
        + [_resident(p.shape) for p in lam_params],
        out_specs=[out_spec, out_spec],
        out_shape=[out_shape, out_shape],
        scratch_shapes=_attn_scratch(False) + _attn_scratch(True),
        compiler_params=_params(("arbitrary", "arbitrary")),
        name="attention",
    )(dqt, dqt, dk, dvt, fqt, fqt, fk, fvt, qaugt, qaugt, kaug, *lam_params)
    return od.reshape(n_tok, GROUP_W), of.reshape(n_tok, GROUP_W)


def _mem_kv_body(m_ref, nw_ref, w_ref, kw_ref, mk_ref, mv_ref):
    hm = _rms(m_ref[...], nw_ref[...]).astype(BF16)
    kv = _dot(hm, w_ref[...])
    kw = kw_ref[...]
    for c in range(0, D_MODEL, MEM_HEAD_DIM):
        mk_ref[:, c:c + MEM_HEAD_DIM] = _rms(kv[:, c:c + MEM_HEAD_DIM], kw).astype(BF16)
    mv_ref[...] = kv[:, D_MODEL:].astype(BF16)


def _mem_kv(mem2d, norm_w, w_mem_kv, mem_k_norm_w):
    n_mem = mem2d.shape[0]
    tok = pl.BlockSpec((TOK_TILE, D_MODEL), lambda i: (i, 0))
    out = jax.ShapeDtypeStruct((n_mem, D_MODEL), BF16)
    return pl.pallas_call(
        _mem_kv_body,
        grid=(n_mem // TOK_TILE,),
        in_specs=[tok, _resident((1, D_MODEL)), _resident((D_MODEL, 2 * D_MODEL)),
                  _resident((1, MEM_HEAD_DIM))],
        out_specs=[tok, tok],
        out_shape=[out, out],
        compiler_params=_params(("arbitrary",)),
        name="mem_kv",
    )(mem2d, norm_w.reshape(1, -1), w_mem_kv.astype(BF16), mem_k_norm_w.reshape(1, -1))


POST_TILE = 1024
FF_CHUNK = 1024


def _post_attn_body(x_ref, od_ref, of_ref, mk_ref, mv_ref, wo_ref, nq_ref, wq_ref, qw_ref,
                    wmo_ref, nm_ref, wu_ref, wd_ref, o_ref):
    x1 = x_ref[...] + (_dot(od_ref[...], wo_ref[:GROUP_W, :])
                       + _dot(of_ref[...], wo_ref[GROUP_W:, :]))

    mq = _dot(_rms(x1, nq_ref[...]).astype(BF16), wq_ref[...])
    qw = qw_ref[...]
    heads = []
    for c in range(0, D_MODEL, MEM_HEAD_DIM):
        q = _rms(mq[:, c:c + MEM_HEAD_DIM], qw).astype(BF16)
        s = _dot_nt(q, mk_ref[:, c:c + MEM_HEAD_DIM])
        p = jnp.exp(s - jnp.max(s, axis=-1, keepdims=True))
        l = jnp.sum(p, axis=-1, keepdims=True)
        heads.append((_dot(p.astype(BF16), mv_ref[:, c:c + MEM_HEAD_DIM]) / l).astype(BF16))
    x2 = x1 + _dot(jnp.concatenate(heads, axis=-1), wmo_ref[...])

    h = _rms(x2, nm_ref[...]).astype(BF16)
    acc = x2
    for c in range(0, D_FF, FF_CHUNK):
        u = jnp.maximum(_dot(h, wu_ref[:, c:c + FF_CHUNK]), 0.0)
        acc = acc + _dot((u * u).astype(BF16), wd_ref[c:c + FF_CHUNK, :])
    o_ref[...] = acc


def _post_attn(x2d, od, of, mk, mv, batch, w_out, norm_q, w_mem_q, mem_q_norm_w, w_mem_o,
               norm_mlp, w_up, w_down):
    n_tok = x2d.shape[0]
    tiles = n_tok // batch // POST_TILE
    mem_len = mk.shape[0] // batch
    tok = lambda width: pl.BlockSpec((POST_TILE, width), lambda b, i: (b * tiles + i, 0))
    mem = pl.BlockSpec((mem_len, D_MODEL), lambda b, i: (b, 0))
    scale = MEM_HEAD_DIM ** -0.5
    square = _resident((D_MODEL, D_MODEL))
    vec = _resident((1, D_MODEL))
    return pl.pallas_call(
        _post_attn_body,
        grid=(batch, tiles),
        in_specs=[tok(D_MODEL), tok(GROUP_W), tok(GROUP_W), mem, mem, square, vec, square,
                  _resident((1, MEM_HEAD_DIM)), square, vec, _resident((D_MODEL, D_FF)),
                  _resident((D_FF, D_MODEL))],
        out_specs=tok(D_MODEL),
        out_shape=jax.ShapeDtypeStruct((n_tok, D_MODEL), F32),
        compiler_params=_params(("arbitrary", "arbitrary")),
        name="post_attn",
    )(x2d, od, of, mk, mv, w_out.astype(BF16), norm_q.reshape(1, -1), w_mem_q.astype(BF16),
      (mem_q_norm_w * scale).reshape(1, -1), w_mem_o.astype(BF16), norm_mlp.reshape(1, -1),
      w_up.astype(BF16), w_down.astype(BF16))


def kernel(x, mem, positions, norm_mix_w, w_in, b_forget, diff_q_norm_w, diff_k_norm_w,
           lambda_q1, lambda_k1, lambda_q2, lambda_k2, diff_subln_w, fox_q_norm_w,
           fox_k_norm_w, w_out, norm_mem_q_w, norm_mem_kv_w, w_mem_q, w_mem_kv,
           mem_q_norm_w, mem_k_norm_w, w_mem_o, norm_mlp_w, w_up, w_down):
    batch, seq, d = x.shape
    depth = w_in.shape[0]
    assert d == D_MODEL and seq % TOK_TILE == 0 and seq % POST_TILE == 0
    assert TOK_TILE % ATT_TILE == 0
    assert (batch * mem.shape[1]) % TOK_TILE == 0

    xc = x.reshape(batch * seq, d)
    mem2d = mem.reshape(-1, d)
    for l in range(depth):
        lam_init = 0.8 - 0.6 * math.exp(-0.3 * l)
        dqt, dk, dvt, fqt, fk, fvt, qaugt, kaug = _in_proj(
            xc, batch, norm_mix_w[l], w_in[l], positions, diff_q_norm_w[l],
            diff_k_norm_w[l], fox_q_norm_w[l], fox_k_norm_w[l], b_forget[l])
        lam_params = [p[l].reshape(1, -1) for p in (lambda_q1, lambda_k1, lambda_q2, lambda_k2)]
        lam_params.append(diff_subln_w[l].reshape(-1, 1))
        od, of = _attention(dqt, dk, dvt, fqt, fk, fvt, qaugt, kaug, lam_params, batch, lam_init)
        mk, mv = _mem_kv(mem2d, norm_mem_kv_w[l], w_mem_kv[l], mem_k_norm_w[l])
        xc = _post_attn(xc, od, of, mk, mv, batch, w_out[l], norm_mem_q_w[l], w_mem_q[l],
                        mem_q_norm_w[l], w_mem_o[l], norm_mlp_w[l], w_up[l], w_down[l])
    return xc.reshape(batch, seq, d)
```
